```python
import jax, jax.numpy as jnp
from jax import lax
import numpy as np

D_MODEL = 1024
BATCH = 8
SEQ = 4096
DEPTH = 2

CHUNK = 64
Q_BLOCK = 128
MIX_WIDTH = D_MODEL
FOX_HEADS = 8
FOX_HEAD_DIM = 64
FOX_WIDTH = FOX_HEADS * FOX_HEAD_DIM
RET_HEADS = 4
RET_HEAD_DIM = 128
RET_WIDTH = RET_HEADS * RET_HEAD_DIM
IN_WIDTH = 3 * FOX_WIDTH + FOX_HEADS + 4 * RET_WIDTH
ROPE_BASE = 10000.0
N_GROUPS = 4
EXPERTS_PER_GROUP = 8
N_EXPERTS = N_GROUPS * EXPERTS_PER_GROUP
TOP_K_INNER = 2
D_EXPERT = D_MODEL // 2
DISPATCH_BLOCK = 128
RMS_EPS = 1e-6

kernel_name = 'hybrid_fox_retnet_hmoe_trunk'


def rmsnorm(x, w):
    xf = x.astype(jnp.float32)
    y = xf * lax.rsqrt(jnp.mean(xf * xf, axis=-1, keepdims=True) + RMS_EPS)
    return y.astype(x.dtype) * w


def head_rmsnorm(x):
    xf = x.astype(jnp.float32)
    y = xf * lax.rsqrt(jnp.mean(xf * xf, axis=-1, keepdims=True) + RMS_EPS)
    return y.astype(x.dtype)


def rotary(x):
    S, D = x.shape[1], x.shape[-1]
    half = D // 2
    inv_freq = 1.0 / (ROPE_BASE ** (jnp.arange(half, dtype=jnp.float32) / half))
    ang = jnp.arange(S, dtype=jnp.float32)[:, None] * inv_freq[None, :]
    cos = jnp.cos(ang)[None, :, None, :].astype(x.dtype)
    sin = jnp.sin(ang)[None, :, None, :].astype(x.dtype)
    x1, x2 = x[..., :half], x[..., half:]
    return jnp.concatenate([x1 * cos - x2 * sin, x1 * sin + x2 * cos], axis=-1)


def forgetting_attention(q, k, v, log_f):
    B, S, H, Dh = q.shape
    c = jnp.cumsum(log_f, axis=1).transpose(0, 2, 1)
    scale = Dh ** -0.5
    outs = []
    for i in range(S // Q_BLOCK):
        q0 = i * Q_BLOCK
        kend = q0 + Q_BLOCK
        s = jnp.einsum('bqhd,bkhd->bhqk', q[:, q0:kend], k[:, :kend]).astype(jnp.float32) * scale
        s = s + (c[:, :, q0:kend, None] - c[:, :, None, :kend])
        q_pos = q0 + jnp.arange(Q_BLOCK)
        mask = q_pos[:, None] >= jnp.arange(kend)[None, :]
        s = jnp.where(mask[None, None], s, -jnp.inf)
        p = jax.nn.softmax(s, axis=-1).astype(v.dtype)
        outs.append(jnp.einsum('bhqk,bkhd->bqhd', p, v[:, :kend]))
    return jnp.concatenate(outs, axis=1)


def retention(q, k, v):
    B, S, H, Dk = q.shape
    Dv = v.shape[-1]
    NC = S // CHUNK
    out_dtype = v.dtype
    qc = q.astype(jnp.float32).reshape(B, NC, CHUNK, H, Dk)
    kc = k.astype(jnp.float32).reshape(B, NC, CHUNK, H, Dk) * (Dk ** -0.5)
    vc = v.astype(jnp.float32).reshape(B, NC, CHUNK, H, Dv)
    log_gamma = jnp.log(1.0 - 2.0 ** (-5.0 - jnp.arange(H, dtype=jnp.float32)))
    idx = jnp.arange(CHUNK, dtype=jnp.float32)
    dist = jnp.abs(idx[:, None] - idx[None, :])
    intra_decay = jnp.exp(log_gamma[:, None, None] * dist)
    scores = jnp.einsum('bnjhd,bnlhd->bnhjl', qc, kc) * intra_decay
    intra = jnp.einsum('bnhjl,bnlhe->bnjhe', scores, vc)
    k_decay = jnp.exp(log_gamma[None, :] * (CHUNK - idx)[:, None])
    chunk_kv = jnp.einsum('bnlhd,bnlhe->nbhde', kc * k_decay[:, :, None], vc)
    chunk_decay = jnp.exp(log_gamma * CHUNK)[:, None, None]

    def step(state, kv):
        return chunk_decay * state + kv, state

    _, states = lax.scan(step, jnp.zeros((B, H, Dk, Dv), jnp.float32), chunk_kv)
    q_decay = jnp.exp(log_gamma[None, :] * idx[:, None])
    cross = jnp.einsum('bnjhd,nbhde->bnjhe', qc * q_decay[:, :, None], states)
    return (intra + cross).reshape(B, S, H, Dv).astype(out_dtype)


def mixer(h, w_in, fox_forget_b, w_out):
    B, S, _ = h.shape
    proj = h @ w_in
    splits = [int(i) for i in np.cumsum([FOX_WIDTH] * 3 + [FOX_HEADS] + [RET_WIDTH] * 3)]
    fq, fk, fv, ff, rq, rk, rv, rg = jnp.split(proj, splits, axis=-1)
    log_f = jax.nn.log_sigmoid((ff + fox_forget_b).astype(jnp.float32))
    fox = forgetting_attention(fq.reshape(B, S, FOX_HEADS, FOX_HEAD_DIM),
                               fk.reshape(B, S, FOX_HEADS, FOX_HEAD_DIM),
                               fv.reshape(B, S, FOX_HEADS, FOX_HEAD_DIM), log_f)
    ret = retention(rotary(rq.reshape(B, S, RET_HEADS, RET_HEAD_DIM)),
                    rotary(rk.reshape(B, S, RET_HEADS, RET_HEAD_DIM)),
                    rv.reshape(B, S, RET_HEADS, RET_HEAD_DIM))
    ret = head_rmsnorm(ret) * jax.nn.silu(rg).reshape(B, S, RET_HEADS, RET_HEAD_DIM)
    mixed = jnp.concatenate([fox.reshape(B, S, FOX_WIDTH), ret.reshape(B, S, RET_WIDTH)], axis=-1)
    return mixed @ w_out


def hierarchical_moe(h, w_router_group, b_router_group, w_router_expert, b_router_expert,
                     w_expert_gate, w_expert_up, w_expert_down):
    B, S, D = h.shape
    N = B * S
    K = TOP_K_INNER
    hf = h.reshape(N, D)
    g_logits = (hf @ w_router_group + b_router_group).astype(jnp.float32)
    g_prob = jax.nn.softmax(g_logits, axis=-1)
    g_idx = jnp.argmax(g_logits, axis=-1)
    g_w = jnp.take_along_axis(g_prob, g_idx[:, None], axis=-1)
    e_logits = (hf @ w_router_expert + b_router_expert).astype(jnp.float32)
    e_logits = e_logits.reshape(N, N_GROUPS, EXPERTS_PER_GROUP)
    e_in_group = jnp.take_along_axis(e_logits, g_idx[:, None, None], axis=1)[:, 0]
    top_v, top_i = lax.top_k(e_in_group, K)
    weights = (g_w * jax.nn.softmax(top_v, axis=-1)).reshape(-1)
    eid = (g_idx[:, None] * EXPERTS_PER_GROUP + top_i).reshape(-1).astype(jnp.int32)
    tok = jnp.repeat(jnp.arange(N, dtype=jnp.int32), K)
    order = jnp.argsort(eid)
    eid_s, tok_s, w_s = eid[order], tok[order], weights[order]
    counts = jnp.bincount(eid, length=N_EXPERTS)
    starts = jnp.cumsum(counts) - counts
    padded = (counts + DISPATCH_BLOCK - 1) // DISPATCH_BLOCK * DISPATCH_BLOCK
    pend = jnp.cumsum(padded)
    pstarts = pend - padded
    dest = pstarts[eid_s] + (jnp.arange(N * K, dtype=jnp.int32) - starts[eid_s])
    P = N * K + N_EXPERTS * DISPATCH_BLOCK
    tok_buf = jnp.zeros((P,), jnp.int32).at[dest].set(tok_s)
    w_buf = jnp.zeros((P,), jnp.float32).at[dest].set(w_s)
    nb = P // DISPATCH_BLOCK
    blk_e = jnp.searchsorted(pend, jnp.arange(nb, dtype=pend.dtype) * DISPATCH_BLOCK, side='right')
    blk_e = jnp.minimum(blk_e, N_EXPERTS - 1)
    xb = hf[tok_buf].reshape(nb, DISPATCH_BLOCK, D)

    def expert_block(args):
        xblk, e = args
        a = jax.nn.silu(xblk @ w_expert_gate[e]) * (xblk @ w_expert_up[e])
        return a @ w_expert_down[e]

    yb = lax.map(expert_block, (xb, blk_e)).reshape(P, D)
    out = jax.ops.segment_sum(yb * w_buf.astype(yb.dtype)[:, None], tok_buf, num_segments=N)
    return out.reshape(B, S, D)


def setup_inputs(seed: int = 0) -> dict:
    key = jax.random.key(seed)
    ks = jax.random.split(key, 14)
    f32 = jnp.float32

    def nrm(k, shape, scale):
        return jax.random.normal(k, shape, f32) * scale

    return {
        'x': nrm(ks[0], (BATCH, SEQ, D_MODEL), 1.0),
        'norm_mix_w': 1.0 + nrm(ks[1], (DEPTH, D_MODEL), 0.1),
        'w_in': nrm(ks[2], (DEPTH, D_MODEL, IN_WIDTH), D_MODEL ** -0.5),
        'fox_forget_b': jax.random.uniform(ks[3], (DEPTH, FOX_HEADS), f32, 1.0, 3.0),
        'w_out': nrm(ks[4], (DEPTH, MIX_WIDTH, D_MODEL), MIX_WIDTH ** -0.5),
        'norm_ffn_w': 1.0 + nrm(ks[5], (DEPTH, D_MODEL), 0.1),
        'w_router_group': nrm(ks[6], (DEPTH, D_MODEL, N_GROUPS), D_MODEL ** -0.5),
        'b_router_group': nrm(ks[7], (DEPTH, N_GROUPS), 0.01),
        'w_router_expert': nrm(ks[8], (DEPTH, D_MODEL, N_EXPERTS), D_MODEL ** -0.5),
        'b_router_expert': nrm(ks[9], (DEPTH, N_EXPERTS), 0.01),
        'w_expert_gate': nrm(ks[10], (DEPTH, N_EXPERTS, D_MODEL, D_EXPERT), D_MODEL ** -0.5),
        'w_expert_up': nrm(ks[11], (DEPTH, N_EXPERTS, D_MODEL, D_EXPERT), D_MODEL ** -0.5),
        'w_expert_down': nrm(ks[12], (DEPTH, N_EXPERTS, D_EXPERT, D_MODEL), D_EXPERT ** -0.5),
        'norm_final_w': 1.0 + nrm(ks[13], (D_MODEL,), 0.1),
    }


def reference(x, norm_mix_w, w_in, fox_forget_b, w_out, norm_ffn_w, w_router_group, b_router_group,
              w_router_expert, b_router_expert, w_expert_gate, w_expert_up, w_expert_down, norm_final_w):
    for layer in range(DEPTH):
        x = x + mixer(rmsnorm(x, norm_mix_w[layer]), w_in[layer], fox_forget_b[layer], w_out[layer])
        x = x + hierarchical_moe(rmsnorm(x, norm_ffn_w[layer]), w_router_group[layer], b_router_group[layer],
                                 w_router_expert[layer], b_router_expert[layer], w_expert_gate[layer],
                                 w_expert_up[layer], w_expert_down[layer])
    return rmsnorm(x, norm_final_w)
```

```python
import functools

import jax
import jax.numpy as jnp
import numpy as np
from jax import lax
from jax.experimental import pallas as pl
from jax.experimental.pallas import tpu as pltpu

F32 = jnp.float32
BF16 = jnp.bfloat16

D_MODEL = 1024
FOX_HEADS = 8
FOX_HEAD_DIM = 64
FOX_WIDTH = 512
RET_HEADS = 4
RET_HEAD_DIM = 128
RET_WIDTH = 512
CHUNK = 64
ROPE_BASE = 10000.0
N_GROUPS = 4
EXPERTS_PER_GROUP = 8
N_EXPERTS = 32
TOP_K = 2
D_EXPERT = 512
RMS_EPS = 1e-6

LANES = 128
VMEM_LIMIT = 56 * 1024 * 1024

ROW_TILE = 512
ATT_TILE = 256
RET_TILE = 256
MOE_TILE = 256
N_MAIN = 7 * 512


def _rms(xf, w):
    return xf * lax.rsqrt(jnp.mean(xf * xf, axis=-1, keepdims=True) + RMS_EPS) * w


def _dot(a, b):
    return jnp.dot(a, b, preferred_element_type=F32)


def _dot_nt(a, b):
    return lax.dot_general(a, b, (((1,), (1,)), ((), ())), preferred_element_type=F32)


def _dot_tn(a, b):
    return lax.dot_general(a, b, (((0,), (0,)), ((), ())), preferred_element_type=F32)


def _inproj_kernel(*refs, has_y, tiles_per_seq):
    if has_y:
        x_ref, y0_ref, y1_ref = refs[:3]
        refs = refs[3:]
    else:
        x_ref = refs[0]
        refs = refs[1:]
    nw_ref, w_ref, wff_ref, bff_ref, tri_ref = refs[:5]
    refs = refs[5:]
    if has_y:
        xres_ref = refs[0]
        refs = refs[1:]
    fq_ref, fk_ref, fv_ref, rq_ref, rk_ref, rv_ref, rg_ref, ct_ref, carry_sc = refs

    i = pl.program_id(0)
    x = x_ref[...]
    if has_y:
        x = x + (y0_ref[...] + y1_ref[...])
        xres_ref[...] = x
    h = _rms(x, nw_ref[...]).astype(BF16)

    outs = (fq_ref, fk_ref, fv_ref, rq_ref, rk_ref, rv_ref, rg_ref)
    for j, o_ref in enumerate(outs):
        acc = _dot(h, w_ref[:, j * 512:(j + 1) * 512])
        if j == 0:
            acc = acc * (FOX_HEAD_DIM ** -0.5)
        o_ref[...] = acc.astype(BF16)

    z = _dot(h, wff_ref[...]) + bff_ref[...]
    lf = jnp.minimum(z, 0.0) - jnp.log1p(jnp.exp(-jnp.abs(z)))
    lane = lax.broadcasted_iota(jnp.int32, lf.shape, 1)
    lf = jnp.where(lane < FOX_HEADS, lf, 0.0)
    hi = lf.astype(BF16)
    r1 = lf - hi.astype(F32)
    mid = r1.astype(BF16)
    lo = (r1 - mid.astype(F32)).astype(BF16)
    tri = tri_ref[...]
    cs = _dot(tri, hi) + _dot(tri, mid) + _dot(tri, lo)

    @pl.when(i % tiles_per_seq == 0)
    def _():
        carry_sc[...] = jnp.zeros_like(carry_sc)

    c = cs + carry_sc[0:1, :]
    carry_sc[...] = jnp.broadcast_to(c[-1:, :], carry_sc.shape)
    ct_ref[0] = c.T[:8, :]


def _inproj(x, y, nw, w_main, w_ff, b_ff, tri, seq):
    n = x.shape[0]
    tm = ROW_TILE
    nt = n // tm
    tps = seq // tm
    has_y = y is not None
    row_spec = pl.BlockSpec((tm, D_MODEL), lambda i: (i, 0))
    in_specs = [row_spec]
    args = [x]
    if has_y:
        in_specs += [pl.BlockSpec((tm, D_MODEL), lambda i: (i, 0)),
                     pl.BlockSpec((tm, D_MODEL), lambda i: (i + nt, 0))]
        args += [y, y]
    in_specs += [
        pl.BlockSpec((1, D_MODEL), lambda i: (0, 0)),
        pl.BlockSpec((D_MODEL, N_MAIN), lambda i: (0, 0)),
        pl.BlockSpec((D_MODEL, LANES), lambda i: (0, 0)),
        pl.BlockSpec((1, LANES), lambda i: (0, 0)),
        pl.BlockSpec((tm, tm), lambda i: (0, 0)),
    ]
    args += [nw, w_main, w_ff, b_ff, tri]
    half_spec = pl.BlockSpec((tm, 512), lambda i: (i, 0))
    out_shape = []
    out_specs = []
    if has_y:
        out_shape.append(jax.ShapeDtypeStruct((n, D_MODEL), F32))
        out_specs.append(row_spec)
    out_shape += [jax.ShapeDtypeStruct((n, 512), BF16)] * 7
    out_specs += [half_spec] * 7
    out_shape.append(jax.ShapeDtypeStruct((n // seq, 8, seq), F32))
    out_specs.append(pl.BlockSpec((1, 8, tm), lambda i: (i // tps, 0, i % tps)))
    outs = pl.pallas_call(
        functools.partial(_inproj_kernel, has_y=has_y, tiles_per_seq=tps),
        grid=(nt,),
        in_specs=in_specs,
        out_specs=out_specs,
        out_shape=out_shape,
        scratch_shapes=[pltpu.VMEM((8, LANES), F32)],
        compiler_params=pltpu.CompilerParams(
            dimension_semantics=("arbitrary",), vmem_limit_bytes=VMEM_LIMIT),
        name="inproj_y" if has_y else "inproj",
    )(*args)
    if has_y:
        return outs[0], outs[1:]
    return x, outs


def _fox_kernel(q_ref, k_ref, v_ref, ct_ref, o_ref, m_sc, l_sc, acc_sc, *, tile):
    hp = pl.program_id(1)
    qi = pl.program_id(2)
    q2 = q_ref[0]
    lane = lax.broadcasted_iota(jnp.int32, q2.shape, 1)
    zero = jnp.zeros_like(q2)
    qh = (jnp.where(lane < FOX_HEAD_DIM, q2, zero), jnp.where(lane >= FOX_HEAD_DIM, q2, zero))
    row = lax.broadcasted_iota(jnp.int32, (tile, tile), 0)
    col = lax.broadcasted_iota(jnp.int32, (tile, tile), 1)
    causal = col <= row

    m_sc[...] = jnp.full(m_sc.shape, -jnp.inf, F32)
    l_sc[...] = jnp.zeros_like(l_sc)
    acc_sc[...] = jnp.zeros_like(acc_sc)

    def block(kb, masked):
        start = pl.multiple_of(kb * tile, tile)
        k_blk = k_ref[0, pl.ds(start, tile), :]
        v_blk = v_ref[0, pl.ds(start, tile), :]
        for h in range(2):
            ck = ct_ref[0, 2 * hp + h, pl.ds(kb, 1), :]
            s = _dot_nt(qh[h], k_blk) - ck
            if masked:
                s = jnp.where(causal, s, -jnp.inf)
            m_old = m_sc[h]
            m_new = jnp.maximum(m_old, jnp.max(s, axis=1, keepdims=True))
            alpha = jnp.exp(m_old - m_new)
            p = jnp.exp(s - m_new)
            l_sc[h] = alpha * l_sc[h] + jnp.sum(p, axis=1, keepdims=True)
            acc_sc[h] = alpha * acc_sc[h] + _dot(p.astype(BF16), v_blk)
            m_sc[h] = m_new

    block(qi, True)

    def body(kb, carry):
        block(kb, False)
        return carry

    lax.fori_loop(0, qi, body, 0)

    o0 = acc_sc[0] / l_sc[0]
    o1 = acc_sc[1] / l_sc[1]
    o_ref[0] = jnp.where(lane < FOX_HEAD_DIM, o0, o1).astype(o_ref.dtype)


def _fox_attention(fq, fk, fv, ct4, batch, seq):
    t = ATT_TILE
    nq = seq // t
    q3 = fq.reshape(batch, seq, FOX_WIDTH)
    k3 = fk.reshape(batch, seq, FOX_WIDTH)
    v3 = fv.reshape(batch, seq, FOX_WIDTH)
    out = pl.pallas_call(
        functools.partial(_fox_kernel, tile=t),
        grid=(batch, FOX_HEADS // 2, nq),
        in_specs=[
            pl.BlockSpec((1, t, LANES), lambda b, j, i: (b, i, j)),
            pl.BlockSpec((1, seq, LANES), lambda b, j, i: (b, 0, j)),
            pl.BlockSpec((1, seq, LANES), lambda b, j, i: (b, 0, j)),
            pl.BlockSpec((1, 8, nq, t), lambda b, j, i: (b, 0, 0, 0)),
        ],
        out_specs=pl.BlockSpec((1, t, LANES), lambda b, j, i: (b, i, j)),
        out_shape=jax.ShapeDtypeStruct((batch, seq, FOX_WIDTH), BF16),
        scratch_shapes=[pltpu.VMEM((2, t, 1), F32), pltpu.VMEM((2, t, 1), F32),
                        pltpu.VMEM((2, t, LANES), F32)],
        compiler_params=pltpu.CompilerParams(
            dimension_semantics=("parallel", "parallel", "parallel"),
            vmem_limit_bytes=VMEM_LIMIT),
        name="fox_attention",
    )(q3, k3, v3, ct4)
    return out.reshape(batch * seq, FOX_WIDTH)


def _ret_kernel(q_ref, k_ref, v_ref, g_ref, cos_ref, sin_ref, dmat_ref, qdec_ref, kdec_ref,
                sdec_ref, o_ref, state_sc):
    si = pl.program_id(2)

    @pl.when(si == 0)
    def _():
        state_sc[...] = jnp.zeros_like(state_sc)

    cos2 = cos_ref[...]
    sin2 = sin_ref[...]

    def rot(ref):
        xf = ref[0].astype(F32)
        return xf * cos2 + pltpu.roll(xf, RET_HEAD_DIM // 2, 1) * sin2

    q = rot(q_ref)
    k = rot(k_ref) * (RET_HEAD_DIM ** -0.5)
    v = v_ref[0]
    scores = _dot_nt(q.astype(BF16), k.astype(BF16)) * dmat_ref[0]
    intra = _dot(scores.astype(BF16), v)
    state = state_sc[...]
    cross = _dot((q * qdec_ref[0]).astype(BF16), state.astype(BF16))
    out = intra + cross
    state_sc[...] = state * sdec_ref[0, 0:1, :] + _dot_tn((k * kdec_ref[0]).astype(BF16), v)

    y = out * lax.rsqrt(jnp.mean(out * out, axis=-1, keepdims=True) + RMS_EPS)
    g = g_ref[0].astype(F32)
    o_ref[0] = (y * (g * jax.nn.sigmoid(g))).astype(o_ref.dtype)


def _ret_tables(seq):
    half = RET_HEAD_DIM // 2
    inv_freq = 1.0 / (ROPE_BASE ** (jnp.arange(half, dtype=F32) / half))
    ang = jnp.arange(seq, dtype=F32)[:, None] * inv_freq[None, :]
    cos = jnp.cos(ang)
    sin = jnp.sin(ang)
    cos2 = jnp.concatenate([cos, cos], axis=1)
    sin2 = jnp.concatenate([-sin, sin], axis=1)
    lt = RET_TILE
    log_gamma = jnp.log(1.0 - 2.0 ** (-5.0 - jnp.arange(RET_HEADS, dtype=F32)))
    idx = jnp.arange(lt)
    t = idx[:, None]
    s = idx[None, :]
    same = (t // CHUNK) == (s // CHUNK)
    earlier = (s // CHUNK) < (t // CHUNK)
    dist = jnp.where(same, jnp.abs(t - s), t - s).astype(F32)
    dmat = jnp.where((same | earlier)[None], jnp.exp(log_gamma[:, None, None] * dist[None]), 0.0)
    idxf = idx.astype(F32)
    qdec = jnp.exp(log_gamma[:, None] * idxf[None, :])
    kdec = jnp.exp(log_gamma[:, None] * (lt - idxf)[None, :])
    sdec = jnp.exp(log_gamma * lt)
    qdec = jnp.broadcast_to(qdec[:, :, None], (RET_HEADS, lt, LANES))
    kdec = jnp.broadcast_to(kdec[:, :, None], (RET_HEADS, lt, LANES))
    sdec = jnp.broadcast_to(sdec[:, None, None], (RET_HEADS, 8, LANES))
    return cos2, sin2, dmat, qdec, kdec, sdec


def _retention(rq, rk, rv, rg, tables, batch, seq):
    lt = RET_TILE
    ns = seq // lt
    cos2, sin2, dmat, qdec, kdec, sdec = tables
    blk = pl.BlockSpec((1, lt, LANES), lambda b, h, i: (b, i, h))
    tab = pl.BlockSpec((lt, LANES), lambda b, h, i: (i, 0))
    args = [a.reshape(batch, seq, RET_WIDTH) for a in (rq, rk, rv, rg)]
    out = pl.pallas_call(
        _ret_kernel,
        grid=(batch, RET_HEADS, ns),
        in_specs=[blk, blk, blk, blk, tab, tab,
                  pl.BlockSpec((1, lt, lt), lambda b, h, i: (h, 0, 0)),
                  pl.BlockSpec((1, lt, LANES), lambda b, h, i: (h, 0, 0)),
                  pl.BlockSpec((1, lt, LANES), lambda b, h, i: (h, 0, 0)),
                  pl.BlockSpec((1, 8, LANES), lambda b, h, i: (h, 0, 0))],
        out_specs=blk,
        out_shape=jax.ShapeDtypeStruct((batch, seq, RET_WIDTH), BF16),
        scratch_shapes=[pltpu.VMEM((RET_HEAD_DIM, RET_HEAD_DIM), F32)],
        compiler_params=pltpu.CompilerParams(
            dimension_semantics=("parallel", "parallel", "arbitrary"),
            vmem_limit_bytes=VMEM_LIMIT),
        name="retention",
    )(*args, cos2, sin2, dmat, qdec, kdec, sdec)
    return out.reshape(batch * seq, RET_WIDTH)


def _outproj_kernel(fox_ref, ret_ref, x_ref, wo_ref, nw_ref, wr_ref, br_ref, xo_ref, lg_ref):
    mixed = jnp.concatenate([fox_ref[...], ret_ref[...]], axis=1)
    x = x_ref[...] + _dot(mixed, wo_ref[...])
    xo_ref[...] = x
    h = _rms(x, nw_ref[...]).astype(BF16)
    lg_ref[...] = _dot(h, wr_ref[...]) + br_ref[...]


def _outproj(fox, ret, x, wo, nw, wr, br):
    n = x.shape[0]
    tm = ROW_TILE
    row = pl.BlockSpec((tm, D_MODEL), lambda i: (i, 0))
    half = pl.BlockSpec((tm, 512), lambda i: (i, 0))
    return pl.pallas_call(
        _outproj_kernel,
        grid=(n // tm,),
        in_specs=[half, half, row,
                  pl.BlockSpec((D_MODEL, D_MODEL), lambda i: (0, 0)),
                  pl.BlockSpec((1, D_MODEL), lambda i: (0, 0)),
                  pl.BlockSpec((D_MODEL, LANES), lambda i: (0, 0)),
                  pl.BlockSpec((1, LANES), lambda i: (0, 0))],
        out_specs=[row, pl.BlockSpec((tm, LANES), lambda i: (i, 0))],
        out_shape=[jax.ShapeDtypeStruct((n, D_MODEL), F32), jax.ShapeDtypeStruct((n, LANES), F32)],
        compiler_params=pltpu.CompilerParams(
            dimension_semantics=("parallel",), vmem_limit_bytes=VMEM_LIMIT),
        name="outproj",
    )(fox, ret, x, wo, nw, wr, br)


def _moe_kernel(be_ref, tokc_ref, tokn_ref, dst_ref, x_hbm, wrow_ref, nw_ref, wg_ref, wu_ref,
                wd_ref, y_hbm, xbuf, ybuf, gsem, ssem, *, tb, nb):
    del be_ref
    i = pl.program_id(0)
    slot = i % 2
    nslot = 1 - slot

    def gather_row(tok_ref, s, r):
        return pltpu.make_async_copy(
            x_hbm.at[pl.ds(tok_ref[0, 0, r], 1), :], xbuf.at[s, pl.ds(r, 1), :], gsem.at[s])

    def scatter_row(s, r):
        return pltpu.make_async_copy(
            ybuf.at[s, pl.ds(r, 1), :], y_hbm.at[pl.ds(dst_ref[0, 0, r], 1), :], ssem.at[s])

    def start_gather(tok_ref, s):
        def body(r, c):
            gather_row(tok_ref, s, r).start()
            return c
        lax.fori_loop(0, tb, body, 0, unroll=8)

    def wait_rows(sem, s):
        pltpu.make_async_copy(xbuf.at[s], ybuf.at[s], sem.at[s]).wait()

    @pl.when(i == 0)
    def _():
        start_gather(tokc_ref, 0)

    @pl.when(i + 1 < nb)
    def _():
        start_gather(tokn_ref, nslot)

    wait_rows(gsem, slot)

    x = xbuf[slot]
    h = _rms(x, nw_ref[...]).astype(BF16)
    g = _dot(h, wg_ref[0])
    u = _dot(h, wu_ref[0])
    a = (g * jax.nn.sigmoid(g) * u).astype(BF16)
    y = _dot(a, wd_ref[0]) * wrow_ref[...]

    @pl.when(i >= 2)
    def _():
        wait_rows(ssem, slot)

    ybuf[slot] = y

    def body(r, c):
        scatter_row(slot, r).start()
        return c
    lax.fori_loop(0, tb, body, 0, unroll=8)

    @pl.when(i == nb - 1)
    def _():
        if nb >= 2:
            wait_rows(ssem, nslot)
        wait_rows(ssem, slot)


def _moe(x, tok_buf, dst_buf, w_buf, blk_e, nw, wg, wu, wd, tb=MOE_TILE):
    p = tok_buf.shape[0]
    nb = p // tb
    d_model, d_expert = wg.shape[1], wg.shape[2]
    tok3 = tok_buf.reshape(nb, 1, tb)
    dst3 = dst_buf.reshape(nb, 1, tb)
    smem_blk = lambda f: pl.BlockSpec((1, 1, tb), f, memory_space=pltpu.SMEM)
    grid_spec = pltpu.PrefetchScalarGridSpec(
        num_scalar_prefetch=1,
        grid=(nb,),
        in_specs=[
            smem_blk(lambda i, be: (i, 0, 0)),
            smem_blk(lambda i, be: (jnp.minimum(i + 1, nb - 1), 0, 0)),
            smem_blk(lambda i, be: (i, 0, 0)),
            pl.BlockSpec(memory_space=pl.ANY),
            pl.BlockSpec((tb, 1), lambda i, be: (i, 0)),
            pl.BlockSpec((1, d_model), lambda i, be: (0, 0)),
            pl.BlockSpec((1, d_model, d_expert), lambda i, be: (be[i], 0, 0)),
            pl.BlockSpec((1, d_model, d_expert), lambda i, be: (be[i], 0, 0)),
            pl.BlockSpec((1, d_expert, d_model), lambda i, be: (be[i], 0, 0)),
        ],
        out_specs=pl.BlockSpec(memory_space=pl.ANY),
        scratch_shapes=[pltpu.VMEM((2, tb, d_model), F32), pltpu.VMEM((2, tb, d_model), F32),
                        pltpu.SemaphoreType.DMA((2,)), pltpu.SemaphoreType.DMA((2,))],
    )
    return pl.pallas_call(
        functools.partial(_moe_kernel, tb=tb, nb=nb),
        grid_spec=grid_spec,
        out_shape=jax.ShapeDtypeStruct((p, d_model), F32),
        compiler_params=pltpu.CompilerParams(
            dimension_semantics=("arbitrary",), vmem_limit_bytes=VMEM_LIMIT),
        name="moe_experts",
    )(blk_e, tok3, tok3, dst3, x, w_buf.reshape(p, 1), nw, wg, wu, wd)


def _route(logits, n):
    tb = MOE_TILE
    g_logits = logits[:, :N_GROUPS]
    e_logits = logits[:, N_GROUPS:N_GROUPS + N_EXPERTS].reshape(n, N_GROUPS, EXPERTS_PER_GROUP)
    g_prob = jax.nn.softmax(g_logits, axis=-1)
    g_idx = jnp.argmax(g_logits, axis=-1)
    g_w = jnp.take_along_axis(g_prob, g_idx[:, None], axis=-1)
    e_in = jnp.take_along_axis(e_logits, g_idx[:, None, None], axis=1)[:, 0]
    top_v, top_i = lax.top_k(e_in, TOP_K)
    weights = (g_w * jax.nn.softmax(top_v, axis=-1)).reshape(-1)
    eid = (g_idx[:, None] * EXPERTS_PER_GROUP + top_i).reshape(-1).astype(jnp.int32)

    na = n * TOP_K
    p = na + N_EXPERTS * tb
    nb = p // tb
    onehot = (eid[:, None] == jnp.arange(N_EXPERTS, dtype=jnp.int32)[None, :]).astype(jnp.int32)
    csum = jnp.cumsum(onehot, axis=0)
    rank = jnp.take_along_axis(csum, eid[:, None], axis=1)[:, 0] - 1
    counts = csum[-1]
    padded = (counts + tb - 1) // tb * tb
    pend = jnp.cumsum(padded)
    pstart = pend - padded
    dest = pstart[eid] + rank
    a_idx = jnp.arange(na, dtype=jnp.int32)
    tok = a_idx // TOP_K
    slot_major = (a_idx % TOP_K) * n + tok
    tok_buf = jnp.zeros((p,), jnp.int32).at[dest].set(tok)
    w_buf = jnp.zeros((p,), F32).at[dest].set(weights)
    real = jnp.zeros((p,), jnp.int32).at[dest].set(1)
    pad_rank = jnp.cumsum(1 - real) - 1
    dst_buf = jnp.where(real == 1, jnp.zeros((p,), jnp.int32).at[dest].set(slot_major),
                        na + pad_rank).astype(jnp.int32)
    blk_e = jnp.searchsorted(pend, jnp.arange(nb, dtype=pend.dtype) * tb, side='right')
    blk_e = jnp.minimum(blk_e, N_EXPERTS - 1).astype(jnp.int32)
    return tok_buf, dst_buf, w_buf, blk_e


def _final_kernel(x_ref, y0_ref, y1_ref, nw_ref, o_ref):
    x = x_ref[...] + (y0_ref[...] + y1_ref[...])
    o_ref[...] = _rms(x, nw_ref[...])


def _final(x, y, nw):
    n = x.shape[0]
    tm = ROW_TILE
    nt = n // tm
    row = pl.BlockSpec((tm, D_MODEL), lambda i: (i, 0))
    return pl.pallas_call(
        _final_kernel,
        grid=(nt,),
        in_specs=[row, row, pl.BlockSpec((tm, D_MODEL), lambda i: (i + nt, 0)),
                  pl.BlockSpec((1, D_MODEL), lambda i: (0, 0))],
        out_specs=row,
        out_shape=jax.ShapeDtypeStruct((n, D_MODEL), F32),
        compiler_params=pltpu.CompilerParams(
            dimension_semantics=("parallel",), vmem_limit_bytes=VMEM_LIMIT),
        name="final_norm",
    )(x, y, y, nw)


def kernel(x, norm_mix_w, w_in, fox_forget_b, w_out, norm_ffn_w, w_router_group, b_router_group,
           w_router_expert, b_router_expert, w_expert_gate, w_expert_up, w_expert_down,
           norm_final_w):
    batch, seq, d = x.shape
    n = batch * seq
    depth = w_in.shape[0]
    xf = x.reshape(n, d)
    tables = _ret_tables(seq)
    tri = jnp.tril(jnp.ones((ROW_TILE, ROW_TILE), F32)).astype(BF16)
    nq = seq // ATT_TILE

    y = None
    for layer in range(depth):
        wl = w_in[layer]
        c0 = 3 * FOX_WIDTH
        w_main = jnp.concatenate([wl[:, :c0], wl[:, c0 + FOX_HEADS:]], axis=1).astype(BF16)
        w_ff = jnp.pad(wl[:, c0:c0 + FOX_HEADS], ((0, 0), (0, LANES - FOX_HEADS))).astype(BF16)
        b_ff = jnp.pad(fox_forget_b[layer], (0, LANES - FOX_HEADS)).reshape(1, LANES)
        xf, (fq, fk, fv, rq, rk, rv, rg, ct) = _inproj(
            xf, y, norm_mix_w[layer].reshape(1, d), w_main, w_ff, b_ff, tri, seq)
        ct4 = ct.reshape(batch, 8, nq, ATT_TILE)
        fox = _fox_attention(fq, fk, fv, ct4, batch, seq)
        ret = _retention(rq, rk, rv, rg, tables, batch, seq)

        w_r = jnp.concatenate([w_router_group[layer], w_router_expert[layer]], axis=1)
        nr = N_GROUPS + N_EXPERTS
        w_r = jnp.pad(w_r, ((0, 0), (0, LANES - nr))).astype(BF16)
        b_r = jnp.pad(jnp.concatenate([b_router_group[layer], b_router_expert[layer]]),
                      (0, LANES - nr)).reshape(1, LANES)
        xf, logits = _outproj(fox, ret, xf, w_out[layer].astype(BF16),
                              norm_ffn_w[layer].reshape(1, d), w_r, b_r)
        tok_buf, dst_buf, w_buf, blk_e = _route(logits, n)
        y = _moe(xf, tok_buf, dst_buf, w_buf, blk_e, norm_ffn_w[layer].reshape(1, d),
                 w_expert_gate[layer].astype(BF16), w_expert_up[layer].astype(BF16),
                 w_expert_down[layer].astype(BF16))
    out = _final(xf, y, norm_final_w.reshape(1, d))
    return out.reshape(batch, seq, d)
```

```python
import functools

import jax
import jax.numpy as jnp
import numpy as np
from jax import lax
from jax.experimental import pallas as pl
from jax.experimental.pallas import tpu as pltpu

F32 = jnp.float32
BF16 = jnp.bfloat16

D_MODEL = 1024
FOX_HEADS = 8
FOX_HEAD_DIM = 64
FOX_WIDTH = 512
RET_HEADS = 4
RET_HEAD_DIM = 128
RET_WIDTH = 512
CHUNK = 64
ROPE_BASE = 10000.0
N_GROUPS = 4
EXPERTS_PER_GROUP = 8
N_EXPERTS = 32
TOP_K = 2
D_EXPERT = 512
RMS_EPS = 1e-6

LANES = 128
VMEM_LIMIT = 56 * 1024 * 1024

ROW_TILE = 512
ATT_TILE = 512
RET_TILE = 256
MOE_TILE = 256
N_MAIN = 7 * 512


def _rms(xf, w):
    return xf * lax.rsqrt(jnp.mean(xf * xf, axis=-1, keepdims=True) + RMS_EPS) * w


def _dot(a, b):
    return jnp.dot(a, b, preferred_element_type=F32)


def _dot_nt(a, b):
    return lax.dot_general(a, b, (((1,), (1,)), ((), ())), preferred_element_type=F32)


def _dot_tn(a, b):
    return lax.dot_general(a, b, (((0,), (0,)), ((), ())), preferred_element_type=F32)


def _inproj_kernel(*refs, has_y, tiles_per_seq):
    if has_y:
        x_ref, y0_ref, y1_ref = refs[:3]
        refs = refs[3:]
    else:
        x_ref = refs[0]
        refs = refs[1:]
    nw_ref, w_ref, wff_ref, bff_ref, tri_ref = refs[:5]
    refs = refs[5:]
    if has_y:
        xres_ref = refs[0]
        refs = refs[1:]
    fq_ref, fk_ref, fv_ref, rq_ref, rk_ref, rv_ref, rg_ref, ct_ref, carry_sc = refs

    i = pl.program_id(0)
    x = x_ref[...]
    if has_y:
        x = x + (y0_ref[...] + y1_ref[...])
        xres_ref[...] = x
    h = _rms(x, nw_ref[...]).astype(BF16)

    outs = (fq_ref, fk_ref, fv_ref, rq_ref, rk_ref, rv_ref, rg_ref)
    for j, o_ref in enumerate(outs):
        acc = _dot(h, w_ref[:, j * 512:(j + 1) * 512])
        if j == 0:
            acc = acc * (FOX_HEAD_DIM ** -0.5)
        o_ref[...] = acc.astype(BF16)

    z = _dot(h, wff_ref[...]) + bff_ref[...]
    lf = jnp.minimum(z, 0.0) - jnp.log1p(jnp.exp(-jnp.abs(z)))
    lane = lax.broadcasted_iota(jnp.int32, lf.shape, 1)
    lf = jnp.where(lane < FOX_HEADS, lf, 0.0)
    hi = lf.astype(BF16)
    r1 = lf - hi.astype(F32)
    mid = r1.astype(BF16)
    lo = (r1 - mid.astype(F32)).astype(BF16)
    tri = tri_ref[...]
    cs = _dot(tri, hi) + _dot(tri, mid) + _dot(tri, lo)

    @pl.when(i % tiles_per_seq == 0)
    def _():
        carry_sc[...] = jnp.zeros_like(carry_sc)

    c = cs + carry_sc[0:1, :]
    carry_sc[...] = jnp.broadcast_to(c[-1:, :], carry_sc.shape)
    ct_ref[0] = c.T[:8, :]


def _inproj(x, y, nw, w_main, w_ff, b_ff, tri, seq):
    n = x.shape[0]
    tm = ROW_TILE
    nt = n // tm
    tps = seq // tm
    has_y = y is not None
    row_spec = pl.BlockSpec((tm, D_MODEL), lambda i: (i, 0))
    in_specs = [row_spec]
    args = [x]
    if has_y:
        in_specs += [pl.BlockSpec((tm, D_MODEL), lambda i: (i, 0)),
                     pl.BlockSpec((tm, D_MODEL), lambda i: (i + nt, 0))]
        args += [y, y]
    in_specs += [
        pl.BlockSpec((1, D_MODEL), lambda i: (0, 0)),
        pl.BlockSpec((D_MODEL, N_MAIN), lambda i: (0, 0)),
        pl.BlockSpec((D_MODEL, LANES), lambda i: (0, 0)),
        pl.BlockSpec((1, LANES), lambda i: (0, 0)),
        pl.BlockSpec((tm, tm), lambda i: (0, 0)),
    ]
    args += [nw, w_main, w_ff, b_ff, tri]
    half_spec = pl.BlockSpec((tm, 512), lambda i: (i, 0))
    out_shape = []
    out_specs = []
    if has_y:
        out_shape.append(jax.ShapeDtypeStruct((n, D_MODEL), F32))
        out_specs.append(row_spec)
    out_shape += [jax.ShapeDtypeStruct((n, 512), BF16)] * 7
    out_specs += [half_spec] * 7
    out_shape.append(jax.ShapeDtypeStruct((n // seq, 8, seq), F32))
    out_specs.append(pl.BlockSpec((1, 8, tm), lambda i: (i // tps, 0, i % tps)))
    outs = pl.pallas_call(
        functools.partial(_inproj_kernel, has_y=has_y, tiles_per_seq=tps),
        grid=(nt,),
        in_specs=in_specs,
        out_specs=out_specs,
        out_shape=out_shape,
        scratch_shapes=[pltpu.VMEM((8, LANES), F32)],
        compiler_params=pltpu.CompilerParams(
            dimension_semantics=("arbitrary",), vmem_limit_bytes=VMEM_LIMIT),
        name="inproj_y" if has_y else "inproj",
    )(*args)
    if has_y:
        return outs[0], outs[1:]
    return x, outs


def _fox_kernel(q_ref, k_ref, v_ref, ct_ref, o_ref, m_sc, acc_sc, *, tile):
    hp = pl.program_id(1)
    qi = pl.program_id(2)
    q2 = q_ref[0]
    lane = lax.broadcasted_iota(jnp.int32, q2.shape, 1)
    first = lane < FOX_HEAD_DIM
    zero = jnp.zeros_like(q2)
    qh = (jnp.where(first, q2, zero), jnp.where(first, zero, q2))
    reps = tile // LANES

    def head_step(h, kb, k_blk, v_blk, mask, m_old, acc_old):
        one = jnp.ones_like(v_blk)
        va = jnp.where(first, v_blk, one) if h == 0 else jnp.where(first, one, v_blk)
        s = _dot_nt(qh[h], k_blk) - ct_ref[0, 2 * hp + h, pl.ds(kb, 1), :]
        if mask is not None:
            s = jnp.where(mask, s, -jnp.inf)
        m_cur = jnp.max(s, axis=1, keepdims=True)
        if m_old is None:
            m_new = jnp.broadcast_to(m_cur, (tile, LANES))
            p = jnp.exp(s - jnp.concatenate([m_new] * reps, axis=1))
            acc = _dot(p.astype(BF16), va)
        else:
            m_new = jnp.maximum(m_old, m_cur)
            alpha = jnp.exp(m_old - m_new)
            p = jnp.exp(s - jnp.concatenate([m_new] * reps, axis=1))
            acc = alpha * acc_old + _dot(p.astype(BF16), va)
        return m_new, acc

    row = lax.broadcasted_iota(jnp.int32, (tile, tile), 0)
    col = lax.broadcasted_iota(jnp.int32, (tile, tile), 1)
    start = pl.multiple_of(qi * tile, tile)
    k_blk = k_ref[0, pl.ds(start, tile), :]
    v_blk = v_ref[0, pl.ds(start, tile), :]
    for h in range(2):
        m_new, acc = head_step(h, qi, k_blk, v_blk, col <= row, None, None)
        m_sc[h] = m_new
        acc_sc[h] = acc

    def body(kb, carry):
        start = pl.multiple_of(kb * tile, tile)
        k_blk = k_ref[0, pl.ds(start, tile), :]
        v_blk = v_ref[0, pl.ds(start, tile), :]
        old = [(m_sc[h], acc_sc[h]) for h in range(2)]
        new = [head_step(h, kb, k_blk, v_blk, None, *old[h]) for h in range(2)]
        for h in range(2):
            m_sc[h] = new[h][0]
            acc_sc[h] = new[h][1]
        return carry

    lax.fori_loop(0, qi, body, 0)

    a0 = acc_sc[0]
    a1 = acc_sc[1]
    half = FOX_HEAD_DIM
    o = jnp.where(first, a0 / pltpu.roll(a0, half, 1), a1 / pltpu.roll(a1, half, 1))
    o_ref[0] = o.astype(o_ref.dtype)


def _fox_attention(fq, fk, fv, ct4, batch, seq):
    t = ATT_TILE
    nq = seq // t
    q3 = fq.reshape(batch, seq, FOX_WIDTH)
    k3 = fk.reshape(batch, seq, FOX_WIDTH)
    v3 = fv.reshape(batch, seq, FOX_WIDTH)
    out = pl.pallas_call(
        functools.partial(_fox_kernel, tile=t),
        grid=(batch, FOX_HEADS // 2, nq),
        in_specs=[
            pl.BlockSpec((1, t, LANES), lambda b, j, i: (b, i, j)),
            pl.BlockSpec((1, seq, LANES), lambda b, j, i: (b, 0, j)),
            pl.BlockSpec((1, seq, LANES), lambda b, j, i: (b, 0, j)),
            pl.BlockSpec((1, 8, nq, t), lambda b, j, i: (b, 0, 0, 0)),
        ],
        out_specs=pl.BlockSpec((1, t, LANES), lambda b, j, i: (b, i, j)),
        out_shape=jax.ShapeDtypeStruct((batch, seq, FOX_WIDTH), BF16),
        scratch_shapes=[pltpu.VMEM((2, t, LANES), F32), pltpu.VMEM((2, t, LANES), F32)],
        compiler_params=pltpu.CompilerParams(
            dimension_semantics=("parallel", "parallel", "parallel"),
            vmem_limit_bytes=VMEM_LIMIT),
        name="fox_attention",
    )(q3, k3, v3, ct4)
    return out.reshape(batch * seq, FOX_WIDTH)


def _ret_kernel(q_ref, k_ref, v_ref, g_ref, cos_ref, sin_ref, dmat_ref, qdec_ref, kdec_ref,
                sdec_ref, o_ref, state_sc):
    si = pl.program_id(2)

    @pl.when(si == 0)
    def _():
        state_sc[...] = jnp.zeros_like(state_sc)

    cos2 = cos_ref[...]
    sin2 = sin_ref[...]

    def rot(ref):
        xf = ref[0].astype(F32)
        return xf * cos2 + pltpu.roll(xf, RET_HEAD_DIM // 2, 1) * sin2

    q = rot(q_ref)
    k = rot(k_ref) * (RET_HEAD_DIM ** -0.5)
    v = v_ref[0]
    scores = _dot_nt(q.astype(BF16), k.astype(BF16)) * dmat_ref[0]
    intra = _dot(scores.astype(BF16), v)
    state = state_sc[...]
    cross = _dot((q * qdec_ref[0]).astype(BF16), state.astype(BF16))
    out = intra + cross
    state_sc[...] = state * sdec_ref[0, 0:1, :] + _dot_tn((k * kdec_ref[0]).astype(BF16), v)

    y = out * lax.rsqrt(jnp.mean(out * out, axis=-1, keepdims=True) + RMS_EPS)
    g = g_ref[0].astype(F32)
    o_ref[0] = (y * (g * jax.nn.sigmoid(g))).astype(o_ref.dtype)


def _ret_tables(seq):
    half = RET_HEAD_DIM // 2
    inv_freq = 1.0 / (ROPE_BASE ** (jnp.arange(half, dtype=F32) / half))
    ang = jnp.arange(seq, dtype=F32)[:, None] * inv_freq[None, :]
    cos = jnp.cos(ang)
    sin = jnp.sin(ang)
    cos2 = jnp.concatenate([cos, cos], axis=1)
    sin2 = jnp.concatenate([-sin, sin], axis=1)
    lt = RET_TILE
    log_gamma = jnp.log(1.0 - 2.0 ** (-5.0 - jnp.arange(RET_HEADS, dtype=F32)))
    idx = jnp.arange(lt)
    t = idx[:, None]
    s = idx[None, :]
    same = (t // CHUNK) == (s // CHUNK)
    earlier = (s // CHUNK) < (t // CHUNK)
    dist = jnp.where(same, jnp.abs(t - s), t - s).astype(F32)
    dmat = jnp.where((same | earlier)[None], jnp.exp(log_gamma[:, None, None] * dist[None]), 0.0)
    idxf = idx.astype(F32)
    qdec = jnp.exp(log_gamma[:, None] * idxf[None, :])
    kdec = jnp.exp(log_gamma[:, None] * (lt - idxf)[None, :])
    sdec = jnp.exp(log_gamma * lt)
    qdec = jnp.broadcast_to(qdec[:, :, None], (RET_HEADS, lt, LANES))
    kdec = jnp.broadcast_to(kdec[:, :, None], (RET_HEADS, lt, LANES))
    sdec = jnp.broadcast_to(sdec[:, None, None], (RET_HEADS, 8, LANES))
    return cos2, sin2, dmat, qdec, kdec, sdec


def _retention(rq, rk, rv, rg, tables, batch, seq):
    lt = RET_TILE
    ns = seq // lt
    cos2, sin2, dmat, qdec, kdec, sdec = tables
    blk = pl.BlockSpec((1, lt, LANES), lambda b, h, i: (b, i, h))
    tab = pl.BlockSpec((lt, LANES), lambda b, h, i: (i, 0))
    args = [a.reshape(batch, seq, RET_WIDTH) for a in (rq, rk, rv, rg)]
    out = pl.pallas_call(
        _ret_kernel,
        grid=(batch, RET_HEADS, ns),
        in_specs=[blk, blk, blk, blk, tab, tab,
                  pl.BlockSpec((1, lt, lt), lambda b, h, i: (h, 0, 0)),
                  pl.BlockSpec((1, lt, LANES), lambda b, h, i: (h, 0, 0)),
                  pl.BlockSpec((1, lt, LANES), lambda b, h, i: (h, 0, 0)),
                  pl.BlockSpec((1, 8, LANES), lambda b, h, i: (h, 0, 0))],
        out_specs=blk,
        out_shape=jax.ShapeDtypeStruct((batch, seq, RET_WIDTH), BF16),
        scratch_shapes=[pltpu.VMEM((RET_HEAD_DIM, RET_HEAD_DIM), F32)],
        compiler_params=pltpu.CompilerParams(
            dimension_semantics=("parallel", "parallel", "arbitrary"),
            vmem_limit_bytes=VMEM_LIMIT),
        name="retention",
    )(*args, cos2, sin2, dmat, qdec, kdec, sdec)
    return out.reshape(batch * seq, RET_WIDTH)


def _outproj_kernel(fox_ref, ret_ref, x_ref, wo_ref, nw_ref, wr_ref, br_ref, xo_ref, lg_ref):
    mixed = jnp.concatenate([fox_ref[...], ret_ref[...]], axis=1)
    x = x_ref[...] + _dot(mixed, wo_ref[...])
    xo_ref[...] = x
    h = _rms(x, nw_ref[...]).astype(BF16)
    lg_ref[...] = _dot(h, wr_ref[...]) + br_ref[...]


def _outproj(fox, ret, x, wo, nw, wr, br):
    n = x.shape[0]
    tm = ROW_TILE
    row = pl.BlockSpec((tm, D_MODEL), lambda i: (i, 0))
    half = pl.BlockSpec((tm, 512), lambda i: (i, 0))
    return pl.pallas_call(
        _outproj_kernel,
        grid=(n // tm,),
        in_specs=[half, half, row,
                  pl.BlockSpec((D_MODEL, D_MODEL), lambda i: (0, 0)),
                  pl.BlockSpec((1, D_MODEL), lambda i: (0, 0)),
                  pl.BlockSpec((D_MODEL, LANES), lambda i: (0, 0)),
                  pl.BlockSpec((1, LANES), lambda i: (0, 0))],
        out_specs=[row, pl.BlockSpec((tm, LANES), lambda i: (i, 0))],
        out_shape=[jax.ShapeDtypeStruct((n, D_MODEL), F32), jax.ShapeDtypeStruct((n, LANES), F32)],
        compiler_params=pltpu.CompilerParams(
            dimension_semantics=("parallel",), vmem_limit_bytes=VMEM_LIMIT),
        name="outproj",
    )(fox, ret, x, wo, nw, wr, br)


def _moe_kernel(be_ref, tokc_ref, tokn_ref, dst_ref, x_hbm, wrow_ref, nw_ref, wg_ref, wu_ref,
                wd_ref, y_hbm, xbuf, ybuf, gsem, ssem, *, tb, nb):
    del be_ref
    i = pl.program_id(0)
    slot = i % 2
    nslot = 1 - slot

    def gather_row(tok_ref, s, r):
        return pltpu.make_async_copy(
            x_hbm.at[pl.ds(tok_ref[0, 0, r], 1), :], xbuf.at[s, pl.ds(r, 1), :], gsem.at[s])

    def scatter_row(s, r):
        return pltpu.make_async_copy(
            ybuf.at[s, pl.ds(r, 1), :], y_hbm.at[pl.ds(dst_ref[0, 0, r], 1), :], ssem.at[s])

    def start_gather(tok_ref, s):
        def body(r, c):
            gather_row(tok_ref, s, r).start()
            return c
        lax.fori_loop(0, tb, body, 0, unroll=8)

    def wait_rows(sem, s):
        pltpu.make_async_copy(xbuf.at[s], ybuf.at[s], sem.at[s]).wait()

    @pl.when(i == 0)
    def _():
        start_gather(tokc_ref, 0)

    @pl.when(i + 1 < nb)
    def _():
        start_gather(tokn_ref, nslot)

    wait_rows(gsem, slot)

    x = xbuf[slot]
    h = _rms(x, nw_ref[...]).astype(BF16)
    g = _dot(h, wg_ref[0])
    u = _dot(h, wu_ref[0])
    a = (g * jax.nn.sigmoid(g) * u).astype(BF16)
    y = _dot(a, wd_ref[0]) * wrow_ref[...]

    @pl.when(i >= 2)
    def _():
        wait_rows(ssem, slot)

    ybuf[slot] = y

    def body(r, c):
        scatter_row(slot, r).start()
        return c
    lax.fori_loop(0, tb, body, 0, unroll=8)

    @pl.when(i == nb - 1)
    def _():
        if nb >= 2:
            wait_rows(ssem, nslot)
        wait_rows(ssem, slot)


def _moe(x, tok_buf, dst_buf, w_buf, blk_e, nw, wg, wu, wd, tb=MOE_TILE):
    p = tok_buf.shape[0]
    nb = p // tb
    d_model, d_expert = wg.shape[1], wg.shape[2]
    tok3 = tok_buf.reshape(nb, 1, tb)
    dst3 = dst_buf.reshape(nb, 1, tb)
    smem_blk = lambda f: pl.BlockSpec((1, 1, tb), f, memory_space=pltpu.SMEM)
    grid_spec = pltpu.PrefetchScalarGridSpec(
        num_scalar_prefetch=1,
        grid=(nb,),
        in_specs=[
            smem_blk(lambda i, be: (i, 0, 0)),
            smem_blk(lambda i, be: (jnp.minimum(i + 1, nb - 1), 0, 0)),
            smem_blk(lambda i, be: (i, 0, 0)),
            pl.BlockSpec(memory_space=pl.ANY),
            pl.BlockSpec((tb, 1), lambda i, be: (i, 0)),
            pl.BlockSpec((1, d_model), lambda i, be: (0, 0)),
            pl.BlockSpec((1, d_model, d_expert), lambda i, be: (be[i], 0, 0)),
            pl.BlockSpec((1, d_model, d_expert), lambda i, be: (be[i], 0, 0)),
            pl.BlockSpec((1, d_expert, d_model), lambda i, be: (be[i], 0, 0)),
        ],
        out_specs=pl.BlockSpec(memory_space=pl.ANY),
        scratch_shapes=[pltpu.VMEM((2, tb, d_model), F32), pltpu.VMEM((2, tb, d_model), F32),
                        pltpu.SemaphoreType.DMA((2,)), pltpu.SemaphoreType.DMA((2,))],
    )
    return pl.pallas_call(
        functools.partial(_moe_kernel, tb=tb, nb=nb),
        grid_spec=grid_spec,
        out_shape=jax.ShapeDtypeStruct((p, d_model), F32),
        compiler_params=pltpu.CompilerParams(
            dimension_semantics=("arbitrary",), vmem_limit_bytes=VMEM_LIMIT),
        name="moe_experts",
    )(blk_e, tok3, tok3, dst3, x, w_buf.reshape(p, 1), nw, wg, wu, wd)


def _route(logits, n):
    tb = MOE_TILE
    g_logits = logits[:, :N_GROUPS]
    e_logits = logits[:, N_GROUPS:N_GROUPS + N_EXPERTS].reshape(n, N_GROUPS, EXPERTS_PER_GROUP)
    g_prob = jax.nn.softmax(g_logits, axis=-1)
    g_idx = jnp.argmax(g_logits, axis=-1)
    g_w = jnp.take_along_axis(g_prob, g_idx[:, None], axis=-1)
    e_in = jnp.take_along_axis(e_logits, g_idx[:, None, None], axis=1)[:, 0]
    top_v, top_i = lax.top_k(e_in, TOP_K)
    weights = (g_w * jax.nn.softmax(top_v, axis=-1)).reshape(-1)
    eid = (g_idx[:, None] * EXPERTS_PER_GROUP + top_i).reshape(-1).astype(jnp.int32)

    na = n * TOP_K
    p = na + N_EXPERTS * tb
    nb = p // tb
    onehot = (eid[:, None] == jnp.arange(N_EXPERTS, dtype=jnp.int32)[None, :]).astype(jnp.int32)
    csum = jnp.cumsum(onehot, axis=0)
    rank = jnp.take_along_axis(csum, eid[:, None], axis=1)[:, 0] - 1
    counts = csum[-1]
    padded = (counts + tb - 1) // tb * tb
    pend = jnp.cumsum(padded)
    pstart = pend - padded
    dest = pstart[eid] + rank
    a_idx = jnp.arange(na, dtype=jnp.int32)
    tok = a_idx // TOP_K
    slot_major = (a_idx % TOP_K) * n + tok
    tok_buf = jnp.zeros((p,), jnp.int32).at[dest].set(tok)
    w_buf = jnp.zeros((p,), F32).at[dest].set(weights)
    real = jnp.zeros((p,), jnp.int32).at[dest].set(1)
    pad_rank = jnp.cumsum(1 - real) - 1
    dst_buf = jnp.where(real == 1, jnp.zeros((p,), jnp.int32).at[dest].set(slot_major),
                        na + pad_rank).astype(jnp.int32)
    blk_e = jnp.searchsorted(pend, jnp.arange(nb, dtype=pend.dtype) * tb, side='right')
    blk_e = jnp.minimum(blk_e, N_EXPERTS - 1).astype(jnp.int32)
    return tok_buf, dst_buf, w_buf, blk_e


def _final_kernel(x_ref, y0_ref, y1_ref, nw_ref, o_ref):
    x = x_ref[...] + (y0_ref[...] + y1_ref[...])
    o_ref[...] = _rms(x, nw_ref[...])


def _final(x, y, nw):
    n = x.shape[0]
    tm = ROW_TILE
    nt = n // tm
    row = pl.BlockSpec((tm, D_MODEL), lambda i: (i, 0))
    return pl.pallas_call(
        _final_kernel,
        grid=(nt,),
        in_specs=[row, row, pl.BlockSpec((tm, D_MODEL), lambda i: (i + nt, 0)),
                  pl.BlockSpec((1, D_MODEL), lambda i: (0, 0))],
        out_specs=row,
        out_shape=jax.ShapeDtypeStruct((n, D_MODEL), F32),
        compiler_params=pltpu.CompilerParams(
            dimension_semantics=("parallel",), vmem_limit_bytes=VMEM_LIMIT),
        name="final_norm",
    )(x, y, y, nw)


def kernel(x, norm_mix_w, w_in, fox_forget_b, w_out, norm_ffn_w, w_router_group, b_router_group,
           w_router_expert, b_router_expert, w_expert_gate, w_expert_up, w_expert_down,
           norm_final_w):
    batch, seq, d = x.shape
    n = batch * seq
    depth = w_in.shape[0]
    xf = x.reshape(n, d)
    tables = _ret_tables(seq)
    tri = jnp.tril(jnp.ones((ROW_TILE, ROW_TILE), F32)).astype(BF16)
    nq = seq // ATT_TILE

    y = None
    for layer in range(depth):
        wl = w_in[layer]
        c0 = 3 * FOX_WIDTH
        w_main = jnp.concatenate([wl[:, :c0], wl[:, c0 + FOX_HEADS:]], axis=1).astype(BF16)
        w_ff = jnp.pad(wl[:, c0:c0 + FOX_HEADS], ((0, 0), (0, LANES - FOX_HEADS))).astype(BF16)
        b_ff = jnp.pad(fox_forget_b[layer], (0, LANES - FOX_HEADS)).reshape(1, LANES)
        xf, (fq, fk, fv, rq, rk, rv, rg, ct) = _inproj(
            xf, y, norm_mix_w[layer].reshape(1, d), w_main, w_ff, b_ff, tri, seq)
        ct4 = ct.reshape(batch, 8, nq, ATT_TILE)
        fox = _fox_attention(fq, fk, fv, ct4, batch, seq)
        ret = _retention(rq, rk, rv, rg, tables, batch, seq)

        w_r = jnp.concatenate([w_router_group[layer], w_router_expert[layer]], axis=1)
        nr = N_GROUPS + N_EXPERTS
        w_r = jnp.pad(w_r, ((0, 0), (0, LANES - nr))).astype(BF16)
        b_r = jnp.pad(jnp.concatenate([b_router_group[layer], b_router_expert[layer]]),
                      (0, LANES - nr)).reshape(1, LANES)
        xf, logits = _outproj(fox, ret, xf, w_out[layer].astype(BF16),
                              norm_ffn_w[layer].reshape(1, d), w_r, b_r)
        tok_buf, dst_buf, w_buf, blk_e = _route(logits, n)
        y = _moe(xf, tok_buf, dst_buf, w_buf, blk_e, norm_ffn_w[layer].reshape(1, d),
                 w_expert_gate[layer].astype(BF16), w_expert_up[layer].astype(BF16),
                 w_expert_down[layer].astype(BF16))
    out = _final(xf, y, norm_final_w.reshape(1, d))
    return out.reshape(batch, seq, d)
```

```python
import functools

import jax
import jax.numpy as jnp
import numpy as np
from jax import lax
from jax.experimental import pallas as pl
from jax.experimental.pallas import tpu as pltpu

F32 = jnp.float32
BF16 = jnp.bfloat16

D_MODEL = 1024
FOX_HEADS = 8
FOX_HEAD_DIM = 64
FOX_WIDTH = 512
RET_HEADS = 4
RET_HEAD_DIM = 128
RET_WIDTH = 512
CHUNK = 64
ROPE_BASE = 10000.0
N_GROUPS = 4
EXPERTS_PER_GROUP = 8
N_EXPERTS = 32
TOP_K = 2
D_EXPERT = 512
RMS_EPS = 1e-6

LANES = 128
VMEM_LIMIT = 56 * 1024 * 1024

ROW_TILE = 512
ATT_TILE = 512
RET_TILE = 256
MOE_TILE = 256
N_MAIN = 7 * 512


def _rms(xf, w):
    return xf * lax.rsqrt(jnp.mean(xf * xf, axis=-1, keepdims=True) + RMS_EPS) * w


def _dot(a, b):
    return jnp.dot(a, b, preferred_element_type=F32)


def _dot_nt(a, b):
    return lax.dot_general(a, b, (((1,), (1,)), ((), ())), preferred_element_type=F32)


def _dot_tn(a, b):
    return lax.dot_general(a, b, (((0,), (0,)), ((), ())), preferred_element_type=F32)


def _inproj_kernel(*refs, has_y, tiles_per_seq):
    if has_y:
        x_ref, y0_ref, y1_ref = refs[:3]
        refs = refs[3:]
    else:
        x_ref = refs[0]
        refs = refs[1:]
    nw_ref, w_ref, wff_ref, bff_ref, tri_ref = refs[:5]
    refs = refs[5:]
    if has_y:
        xres_ref = refs[0]
        refs = refs[1:]
    fq_ref, fk_ref, fv_ref, rq_ref, rk_ref, rv_ref, rg_ref, ct_ref, carry_sc = refs

    i = pl.program_id(0)
    x = x_ref[...]
    if has_y:
        x = x + (y0_ref[...] + y1_ref[...])
        xres_ref[...] = x
    h = _rms(x, nw_ref[...]).astype(BF16)

    outs = (fq_ref, fk_ref, fv_ref, rq_ref, rk_ref, rv_ref, rg_ref)
    for j, o_ref in enumerate(outs):
        acc = _dot(h, w_ref[:, j * 512:(j + 1) * 512])
        if j == 0:
            acc = acc * (FOX_HEAD_DIM ** -0.5)
        o_ref[...] = acc.astype(BF16)

    z = _dot(h, wff_ref[...]) + bff_ref[...]
    lf = jnp.minimum(z, 0.0) - jnp.log1p(jnp.exp(-jnp.abs(z)))
    lane = lax.broadcasted_iota(jnp.int32, lf.shape, 1)
    lf = jnp.where(lane < FOX_HEADS, lf, 0.0)
    hi = lf.astype(BF16)
    r1 = lf - hi.astype(F32)
    mid = r1.astype(BF16)
    lo = (r1 - mid.astype(F32)).astype(BF16)
    tri = tri_ref[...]
    cs = _dot(tri, hi) + _dot(tri, mid) + _dot(tri, lo)

    @pl.when(i % tiles_per_seq == 0)
    def _():
        carry_sc[...] = jnp.zeros_like(carry_sc)

    c = cs + carry_sc[0:1, :]
    carry_sc[...] = jnp.broadcast_to(c[-1:, :], carry_sc.shape)
    ct_ref[0] = c.T[:8, :]


def _inproj(x, y, nw, w_main, w_ff, b_ff, tri, seq):
    n = x.shape[0]
    tm = ROW_TILE
    nt = n // tm
    tps = seq // tm
    has_y = y is not None
    row_spec = pl.BlockSpec((tm, D_MODEL), lambda i: (i, 0))
    in_specs = [row_spec]
    args = [x]
    if has_y:
        in_specs += [pl.BlockSpec((tm, D_MODEL), lambda i: (i, 0)),
                     pl.BlockSpec((tm, D_MODEL), lambda i: (i + nt, 0))]
        args += [y, y]
    in_specs += [
        pl.BlockSpec((1, D_MODEL), lambda i: (0, 0)),
        pl.BlockSpec((D_MODEL, N_MAIN), lambda i: (0, 0)),
        pl.BlockSpec((D_MODEL, LANES), lambda i: (0, 0)),
        pl.BlockSpec((1, LANES), lambda i: (0, 0)),
        pl.BlockSpec((tm, tm), lambda i: (0, 0)),
    ]
    args += [nw, w_main, w_ff, b_ff, tri]
    half_spec = pl.BlockSpec((tm, 512), lambda i: (i, 0))
    out_shape = []
    out_specs = []
    if has_y:
        out_shape.append(jax.ShapeDtypeStruct((n, D_MODEL), F32))
        out_specs.append(row_spec)
    out_shape += [jax.ShapeDtypeStruct((n, 512), BF16)] * 7
    out_specs += [half_spec] * 7
    out_shape.append(jax.ShapeDtypeStruct((n // seq, 8, seq), F32))
    out_specs.append(pl.BlockSpec((1, 8, tm), lambda i: (i // tps, 0, i % tps)))
    outs = pl.pallas_call(
        functools.partial(_inproj_kernel, has_y=has_y, tiles_per_seq=tps),
        grid=(nt,),
        in_specs=in_specs,
        out_specs=out_specs,
        out_shape=out_shape,
        scratch_shapes=[pltpu.VMEM((8, LANES), F32)],
        compiler_params=pltpu.CompilerParams(
            dimension_semantics=("arbitrary",), vmem_limit_bytes=VMEM_LIMIT),
        name="inproj_y" if has_y else "inproj",
    )(*args)
    if has_y:
        return outs[0], outs[1:]
    return x, outs


def _fox_kernel(q_ref, k_ref, v_ref, ct_ref, o_ref, m_sc, acc_sc, *, tile):
    hp = pl.program_id(1)
    qi = pl.program_id(2)
    q2 = q_ref[0]
    lane = lax.broadcasted_iota(jnp.int32, q2.shape, 1)
    first = lane < FOX_HEAD_DIM
    zero = jnp.zeros_like(q2)
    qh = (jnp.where(first, q2, zero), jnp.where(first, zero, q2))
    reps = tile // LANES

    def head_step(h, kb, k_blk, v_blk, mask, m_old, acc_old):
        one = jnp.ones_like(v_blk)
        va = jnp.where(first, v_blk, one) if h == 0 else jnp.where(first, one, v_blk)
        s = _dot_nt(qh[h], k_blk) - ct_ref[0, 2 * hp + h, pl.ds(kb, 1), :]
        if mask is not None:
            s = jnp.where(mask, s, -jnp.inf)
        m_cur = jnp.max(s, axis=1, keepdims=True)
        if m_old is None:
            m_new = jnp.broadcast_to(m_cur, (tile, LANES))
            p = jnp.exp(s - jnp.concatenate([m_new] * reps, axis=1))
            acc = _dot(p.astype(BF16), va)
        else:
            m_new = jnp.maximum(m_old, m_cur)
            alpha = jnp.exp(m_old - m_new)
            p = jnp.exp(s - jnp.concatenate([m_new] * reps, axis=1))
            acc = alpha * acc_old + _dot(p.astype(BF16), va)
        return m_new, acc

    row = lax.broadcasted_iota(jnp.int32, (tile, tile), 0)
    col = lax.broadcasted_iota(jnp.int32, (tile, tile), 1)
    start = pl.multiple_of(qi * tile, tile)
    k_blk = k_ref[0, pl.ds(start, tile), :]
    v_blk = v_ref[0, pl.ds(start, tile), :]
    for h in range(2):
        m_new, acc = head_step(h, qi, k_blk, v_blk, col <= row, None, None)
        m_sc[h] = m_new
        acc_sc[h] = acc

    def body(kb, carry):
        start = pl.multiple_of(kb * tile, tile)
        k_blk = k_ref[0, pl.ds(start, tile), :]
        v_blk = v_ref[0, pl.ds(start, tile), :]
        old = [(m_sc[h], acc_sc[h]) for h in range(2)]
        new = [head_step(h, kb, k_blk, v_blk, None, *old[h]) for h in range(2)]
        for h in range(2):
            m_sc[h] = new[h][0]
            acc_sc[h] = new[h][1]
        return carry

    lax.fori_loop(0, qi, body, 0)

    a0 = acc_sc[0]
    a1 = acc_sc[1]
    half = FOX_HEAD_DIM
    o = jnp.where(first, a0 / pltpu.roll(a0, half, 1), a1 / pltpu.roll(a1, half, 1))
    o_ref[0] = o.astype(o_ref.dtype)


def _fox_attention(fq, fk, fv, ct4, batch, seq):
    t = ATT_TILE
    nq = seq // t
    q3 = fq.reshape(batch, seq, FOX_WIDTH)
    k3 = fk.reshape(batch, seq, FOX_WIDTH)
    v3 = fv.reshape(batch, seq, FOX_WIDTH)
    out = pl.pallas_call(
        functools.partial(_fox_kernel, tile=t),
        grid=(batch, FOX_HEADS // 2, nq),
        in_specs=[
            pl.BlockSpec((1, t, LANES), lambda b, j, i: (b, i, j)),
            pl.BlockSpec((1, seq, LANES), lambda b, j, i: (b, 0, j)),
            pl.BlockSpec((1, seq, LANES), lambda b, j, i: (b, 0, j)),
            pl.BlockSpec((1, 8, nq, t), lambda b, j, i: (b, 0, 0, 0)),
        ],
        out_specs=pl.BlockSpec((1, t, LANES), lambda b, j, i: (b, i, j)),
        out_shape=jax.ShapeDtypeStruct((batch, seq, FOX_WIDTH), BF16),
        scratch_shapes=[pltpu.VMEM((2, t, LANES), F32), pltpu.VMEM((2, t, LANES), F32)],
        compiler_params=pltpu.CompilerParams(
            dimension_semantics=("parallel", "parallel", "parallel"),
            vmem_limit_bytes=VMEM_LIMIT),
        name="fox_attention",
    )(q3, k3, v3, ct4)
    return out.reshape(batch * seq, FOX_WIDTH)


def _ret_kernel(q_ref, k_ref, v_ref, g_ref, cos_ref, sin_ref, dmat_ref, qdec_ref, kdec_ref,
                sdec_ref, o_ref, state_sc):
    si = pl.program_id(2)

    @pl.when(si == 0)
    def _():
        state_sc[...] = jnp.zeros_like(state_sc)

    cos2 = cos_ref[...]
    sin2 = sin_ref[...]

    def rot(ref):
        xf = ref[0].astype(F32)
        return xf * cos2 + pltpu.roll(xf, RET_HEAD_DIM // 2, 1) * sin2

    q = rot(q_ref)
    k = rot(k_ref) * (RET_HEAD_DIM ** -0.5)
    v = v_ref[0]
    scores = _dot_nt(q.astype(BF16), k.astype(BF16)) * dmat_ref[0]
    intra = _dot(scores.astype(BF16), v)
    state = state_sc[...]
    cross = _dot((q * qdec_ref[0]).astype(BF16), state.astype(BF16))
    out = intra + cross
    state_sc[...] = state * sdec_ref[0, 0:1, :] + _dot_tn((k * kdec_ref[0]).astype(BF16), v)

    y = out * lax.rsqrt(jnp.mean(out * out, axis=-1, keepdims=True) + RMS_EPS)
    g = g_ref[0].astype(F32)
    o_ref[0] = (y * (g * jax.nn.sigmoid(g))).astype(o_ref.dtype)


def _ret_tables(seq):
    half = RET_HEAD_DIM // 2
    inv_freq = 1.0 / (ROPE_BASE ** (jnp.arange(half, dtype=F32) / half))
    ang = jnp.arange(seq, dtype=F32)[:, None] * inv_freq[None, :]
    cos = jnp.cos(ang)
    sin = jnp.sin(ang)
    cos2 = jnp.concatenate([cos, cos], axis=1)
    sin2 = jnp.concatenate([-sin, sin], axis=1)
    lt = RET_TILE
    log_gamma = jnp.log(1.0 - 2.0 ** (-5.0 - jnp.arange(RET_HEADS, dtype=F32)))
    idx = jnp.arange(lt)
    t = idx[:, None]
    s = idx[None, :]
    same = (t // CHUNK) == (s // CHUNK)
    earlier = (s // CHUNK) < (t // CHUNK)
    dist = jnp.where(same, jnp.abs(t - s), t - s).astype(F32)
    dmat = jnp.where((same | earlier)[None], jnp.exp(log_gamma[:, None, None] * dist[None]), 0.0)
    idxf = idx.astype(F32)
    qdec = jnp.exp(log_gamma[:, None] * idxf[None, :])
    kdec = jnp.exp(log_gamma[:, None] * (lt - idxf)[None, :])
    sdec = jnp.exp(log_gamma * lt)
    qdec = jnp.broadcast_to(qdec[:, :, None], (RET_HEADS, lt, LANES))
    kdec = jnp.broadcast_to(kdec[:, :, None], (RET_HEADS, lt, LANES))
    sdec = jnp.broadcast_to(sdec[:, None, None], (RET_HEADS, 8, LANES))
    return cos2, sin2, dmat, qdec, kdec, sdec


def _retention(rq, rk, rv, rg, tables, batch, seq):
    lt = RET_TILE
    ns = seq // lt
    cos2, sin2, dmat, qdec, kdec, sdec = tables
    blk = pl.BlockSpec((1, lt, LANES), lambda b, h, i: (b, i, h))
    tab = pl.BlockSpec((lt, LANES), lambda b, h, i: (i, 0))
    args = [a.reshape(batch, seq, RET_WIDTH) for a in (rq, rk, rv, rg)]
    out = pl.pallas_call(
        _ret_kernel,
        grid=(batch, RET_HEADS, ns),
        in_specs=[blk, blk, blk, blk, tab, tab,
                  pl.BlockSpec((1, lt, lt), lambda b, h, i: (h, 0, 0)),
                  pl.BlockSpec((1, lt, LANES), lambda b, h, i: (h, 0, 0)),
                  pl.BlockSpec((1, lt, LANES), lambda b, h, i: (h, 0, 0)),
                  pl.BlockSpec((1, 8, LANES), lambda b, h, i: (h, 0, 0))],
        out_specs=blk,
        out_shape=jax.ShapeDtypeStruct((batch, seq, RET_WIDTH), BF16),
        scratch_shapes=[pltpu.VMEM((RET_HEAD_DIM, RET_HEAD_DIM), F32)],
        compiler_params=pltpu.CompilerParams(
            dimension_semantics=("parallel", "parallel", "arbitrary"),
            vmem_limit_bytes=VMEM_LIMIT),
        name="retention",
    )(*args, cos2, sin2, dmat, qdec, kdec, sdec)
    return out.reshape(batch * seq, RET_WIDTH)


def _outproj_kernel(fox_ref, ret_ref, x_ref, wo_ref, nw_ref, wr_ref, br_ref, xo_ref, eid_ref,
                    rw_ref):
    mixed = jnp.concatenate([fox_ref[...], ret_ref[...]], axis=1)
    x = x_ref[...] + _dot(mixed, wo_ref[...])
    xo_ref[...] = x
    h = _rms(x, nw_ref[...]).astype(BF16)
    lt = (_dot(h, wr_ref[...]) + br_ref[...]).T
    tm = lt.shape[1]
    rowid = lax.broadcasted_iota(jnp.int32, (8, tm), 0)
    neg = -jnp.inf

    def top1(v):
        vmax = jnp.max(v, axis=0, keepdims=True)
        idx = jnp.min(jnp.where(v == vmax, rowid, 8), axis=0, keepdims=True)
        return vmax, idx

    gl = jnp.where(rowid < N_GROUPS, lt[0:8], neg)
    gmax, gidx = top1(gl)
    g_w = 1.0 / jnp.sum(jnp.exp(gl - gmax), axis=0, keepdims=True)
    e_in = jnp.zeros((8, tm), F32)
    for g in range(N_GROUPS):
        e_in = jnp.where(gidx == g, lt[8 + 8 * g:16 + 8 * g], e_in)
    v1, i1 = top1(e_in)
    rest = jnp.where(rowid == i1, neg, e_in)
    v2, i2 = top1(rest)
    t = jnp.exp(v2 - v1)
    w1 = g_w / (1.0 + t)
    eid_ref[0:1, :] = gidx * EXPERTS_PER_GROUP + i1
    eid_ref[1:2, :] = gidx * EXPERTS_PER_GROUP + i2
    rw_ref[0:1, :] = w1
    rw_ref[1:2, :] = w1 * t


def _outproj(fox, ret, x, wo, nw, wr, br):
    n = x.shape[0]
    tm = ROW_TILE
    row = pl.BlockSpec((tm, D_MODEL), lambda i: (i, 0))
    half = pl.BlockSpec((tm, 512), lambda i: (i, 0))
    pair = pl.BlockSpec((TOP_K, tm), lambda i: (0, i))
    return pl.pallas_call(
        _outproj_kernel,
        grid=(n // tm,),
        in_specs=[half, half, row,
                  pl.BlockSpec((D_MODEL, D_MODEL), lambda i: (0, 0)),
                  pl.BlockSpec((1, D_MODEL), lambda i: (0, 0)),
                  pl.BlockSpec((D_MODEL, LANES), lambda i: (0, 0)),
                  pl.BlockSpec((1, LANES), lambda i: (0, 0))],
        out_specs=[row, pair, pair],
        out_shape=[jax.ShapeDtypeStruct((n, D_MODEL), F32),
                   jax.ShapeDtypeStruct((TOP_K, n), jnp.int32),
                   jax.ShapeDtypeStruct((TOP_K, n), F32)],
        compiler_params=pltpu.CompilerParams(
            dimension_semantics=("parallel",), vmem_limit_bytes=VMEM_LIMIT),
        name="outproj",
    )(fox, ret, x, wo, nw, wr, br)


_FIRST, _LAST, _VALID = 1, 2, 4


def _moe_kernel(blk_ref, exp_ref, flag_ref, starts_ref,
                tokc_ref, tokn_ref, dstp_ref, dstc_ref, x_hbm, wcol_ref, nw_ref, wg_ref, wu_ref,
                wd_ref, y_hbm, xbuf, ybuf, hbuf, gsem, ssem, *, tb, nb):
    w = pl.program_id(0)
    b = blk_ref[w]
    e = exp_ref[w]
    flags = flag_ref[w]
    slot = b % 2
    nslot = 1 - slot

    def start_gather(tok_ref, s):
        for r in range(tb):
            pltpu.make_async_copy(x_hbm.at[pl.ds(tok_ref[0, 0, r], 1), :],
                                  xbuf.at[s, pl.ds(r, 1), :], gsem.at[s]).start()

    def start_scatter(dst_ref, s):
        for r in range(tb):
            pltpu.make_async_copy(ybuf.at[s, pl.ds(r, 1), :],
                                  y_hbm.at[pl.ds(dst_ref[0, 0, r], 1), :], ssem.at[s]).start()

    def wait_rows(sem, s):
        pltpu.make_async_copy(xbuf.at[s], ybuf.at[s], sem.at[s]).wait()

    @pl.when((flags & _FIRST) != 0)
    def _():
        @pl.when(w == 0)
        def _():
            start_gather(tokc_ref, 0)

        wait_rows(gsem, slot)

        @pl.when(b >= 2)
        def _():
            wait_rows(ssem, slot)

        @pl.when(b + 1 < nb)
        def _():
            start_gather(tokn_ref, nslot)

        @pl.when(b >= 1)
        def _():
            start_scatter(dstp_ref, nslot)

        hbuf[...] = _rms(xbuf[slot], nw_ref[...]).astype(BF16)
        ybuf[slot] = jnp.zeros((tb, ybuf.shape[2]), F32)

    @pl.when((flags & _VALID) != 0)
    def _():
        h = hbuf[...]
        g = _dot(h, wg_ref[0])
        u = _dot(h, wu_ref[0])
        a = (g * jax.nn.sigmoid(g) * u).astype(BF16)
        y = _dot(a, wd_ref[0]) * wcol_ref[...]
        q = b * tb + lax.broadcasted_iota(jnp.int32, y.shape, 0)
        mine = (q >= starts_ref[e]) & (q < starts_ref[e + 1])
        ybuf[slot] = jnp.where(mine, y, ybuf[slot])

    @pl.when(((flags & _LAST) != 0) & (b == nb - 1))
    def _():
        start_scatter(dstc_ref, slot)
        if nb >= 2:
            wait_rows(ssem, nslot)
        wait_rows(ssem, slot)


def _moe(x, s_tok, s_dst, s_w, blk, exp, flags, starts, nw, wg, wu, wd, tb=MOE_TILE):
    na = s_tok.shape[0]
    nb = na // tb
    d_model, d_expert = wg.shape[1], wg.shape[2]
    tok3 = s_tok.reshape(nb, 1, tb)
    dst3 = s_dst.reshape(nb, 1, tb)
    smem_blk = lambda f: pl.BlockSpec((1, 1, tb), f, memory_space=pltpu.SMEM)
    grid_spec = pltpu.PrefetchScalarGridSpec(
        num_scalar_prefetch=4,
        grid=(blk.shape[0],),
        in_specs=[
            smem_blk(lambda w, bl, ex, fl, st: (bl[w], 0, 0)),
            smem_blk(lambda w, bl, ex, fl, st: (jnp.minimum(bl[w] + 1, nb - 1), 0, 0)),
            smem_blk(lambda w, bl, ex, fl, st: (jnp.maximum(bl[w] - 1, 0), 0, 0)),
            smem_blk(lambda w, bl, ex, fl, st: (bl[w], 0, 0)),
            pl.BlockSpec(memory_space=pl.ANY),
            pl.BlockSpec((tb, 1), lambda w, bl, ex, fl, st: (bl[w], 0)),
            pl.BlockSpec((1, d_model), lambda w, bl, ex, fl, st: (0, 0)),
            pl.BlockSpec((1, d_model, d_expert), lambda w, bl, ex, fl, st: (ex[w], 0, 0)),
            pl.BlockSpec((1, d_model, d_expert), lambda w, bl, ex, fl, st: (ex[w], 0, 0)),
            pl.BlockSpec((1, d_expert, d_model), lambda w, bl, ex, fl, st: (ex[w], 0, 0)),
        ],
        out_specs=pl.BlockSpec(memory_space=pl.ANY),
        scratch_shapes=[pltpu.VMEM((2, tb, d_model), F32), pltpu.VMEM((2, tb, d_model), F32),
                        pltpu.VMEM((tb, d_model), BF16),
                        pltpu.SemaphoreType.DMA((2,)), pltpu.SemaphoreType.DMA((2,))],
    )
    return pl.pallas_call(
        functools.partial(_moe_kernel, tb=tb, nb=nb),
        grid_spec=grid_spec,
        out_shape=jax.ShapeDtypeStruct((na, d_model), F32),
        compiler_params=pltpu.CompilerParams(
            dimension_semantics=("arbitrary",), vmem_limit_bytes=VMEM_LIMIT),
        name="moe_experts",
    )(blk, exp, flags, starts, tok3, tok3, dst3, dst3, x, s_w.reshape(na, 1), nw, wg, wu, wd)


def _dispatch_plan(eid, rw, n, tb=MOE_TILE, n_experts=N_EXPERTS):
    na = TOP_K * n
    nb = na // tb
    s_eid, s_a, s_w = lax.sort((eid.reshape(na), jnp.arange(na, dtype=jnp.int32), rw.reshape(na)),
                               num_keys=1)
    s_tok = s_a % n
    starts = jnp.searchsorted(s_eid, jnp.arange(n_experts + 1, dtype=jnp.int32),
                              side='left').astype(jnp.int32)
    lo, hi = starts[:-1], starts[1:]
    nonempty = hi > lo
    first_blk = lo // tb
    npass = jnp.where(nonempty, (hi - 1) // tb - first_blk + 1, 0)
    cum = jnp.cumsum(npass)
    total = cum[-1]
    n_pass = nb + n_experts
    w = jnp.arange(n_pass, dtype=jnp.int32)
    wc = jnp.minimum(w, total - 1)
    ex = jnp.searchsorted(cum, wc, side='right').astype(jnp.int32)
    blk = (first_blk[ex] + (wc - (cum[ex] - npass[ex]))).astype(jnp.int32)
    valid = w < total
    prev_blk = jnp.concatenate([jnp.full((1,), -1, jnp.int32), blk[:-1]])
    next_blk = jnp.concatenate([blk[1:], jnp.full((1,), -1, jnp.int32)])
    first = valid & (blk != prev_blk)
    last = valid & ((blk != next_blk) | (w == total - 1))
    flags = (first * _FIRST + last * _LAST + valid * _VALID).astype(jnp.int32)
    return s_tok, s_a, s_w, blk, ex, flags, starts


def _final_kernel(x_ref, y0_ref, y1_ref, nw_ref, o_ref):
    x = x_ref[...] + (y0_ref[...] + y1_ref[...])
    o_ref[...] = _rms(x, nw_ref[...])


def _final(x, y, nw):
    n = x.shape[0]
    tm = ROW_TILE
    nt = n // tm
    row = pl.BlockSpec((tm, D_MODEL), lambda i: (i, 0))
    return pl.pallas_call(
        _final_kernel,
        grid=(nt,),
        in_specs=[row, row, pl.BlockSpec((tm, D_MODEL), lambda i: (i + nt, 0)),
                  pl.BlockSpec((1, D_MODEL), lambda i: (0, 0))],
        out_specs=row,
        out_shape=jax.ShapeDtypeStruct((n, D_MODEL), F32),
        compiler_params=pltpu.CompilerParams(
            dimension_semantics=("parallel",), vmem_limit_bytes=VMEM_LIMIT),
        name="final_norm",
    )(x, y, y, nw)


def kernel(x, norm_mix_w, w_in, fox_forget_b, w_out, norm_ffn_w, w_router_group, b_router_group,
           w_router_expert, b_router_expert, w_expert_gate, w_expert_up, w_expert_down,
           norm_final_w):
    batch, seq, d = x.shape
    n = batch * seq
    depth = w_in.shape[0]
    xf = x.reshape(n, d)
    tables = _ret_tables(seq)
    tri = jnp.tril(jnp.ones((ROW_TILE, ROW_TILE), F32)).astype(BF16)
    nq = seq // ATT_TILE

    y = None
    for layer in range(depth):
        wl = w_in[layer]
        c0 = 3 * FOX_WIDTH
        w_main = jnp.concatenate([wl[:, :c0], wl[:, c0 + FOX_HEADS:]], axis=1).astype(BF16)
        w_ff = jnp.pad(wl[:, c0:c0 + FOX_HEADS], ((0, 0), (0, LANES - FOX_HEADS))).astype(BF16)
        b_ff = jnp.pad(fox_forget_b[layer], (0, LANES - FOX_HEADS)).reshape(1, LANES)
        xf, (fq, fk, fv, rq, rk, rv, rg, ct) = _inproj(
            xf, y, norm_mix_w[layer].reshape(1, d), w_main, w_ff, b_ff, tri, seq)
        ct4 = ct.reshape(batch, 8, nq, ATT_TILE)
        fox = _fox_attention(fq, fk, fv, ct4, batch, seq)
        ret = _retention(rq, rk, rv, rg, tables, batch, seq)

        zpad = jnp.zeros((d, N_GROUPS), F32)
        w_r = jnp.concatenate([w_router_group[layer], zpad, w_router_expert[layer]], axis=1)
        nr = 2 * N_GROUPS + N_EXPERTS
        w_r = jnp.pad(w_r, ((0, 0), (0, LANES - nr))).astype(BF16)
        b_r = jnp.concatenate([b_router_group[layer], jnp.zeros((N_GROUPS,), F32),
                               b_router_expert[layer]])
        b_r = jnp.pad(b_r, (0, LANES - nr)).reshape(1, LANES)
        xf, eid, rw = _outproj(fox, ret, xf, w_out[layer].astype(BF16),
                               norm_ffn_w[layer].reshape(1, d), w_r, b_r)
        s_tok, s_dst, s_w, blk, ex, flags, starts = _dispatch_plan(eid, rw, n)
        y = _moe(xf, s_tok, s_dst, s_w, blk, ex, flags, starts, norm_ffn_w[layer].reshape(1, d),
                 w_expert_gate[layer].astype(BF16), w_expert_up[layer].astype(BF16),
                 w_expert_down[layer].astype(BF16))
    out = _final(xf, y, norm_final_w.reshape(1, d))
    return out.reshape(batch, seq, d)
```

```python
import functools

import jax
import jax.numpy as jnp
import numpy as np
from jax import lax
from jax.experimental import pallas as pl
from jax.experimental.pallas import tpu as pltpu

F32 = jnp.float32
BF16 = jnp.bfloat16

D_MODEL = 1024
FOX_HEADS = 8
FOX_HEAD_DIM = 64
FOX_WIDTH = 512
RET_HEADS = 4
RET_HEAD_DIM = 128
RET_WIDTH = 512
CHUNK = 64
ROPE_BASE = 10000.0
N_GROUPS = 4
EXPERTS_PER_GROUP = 8
N_EXPERTS = 32
TOP_K = 2
D_EXPERT = 512
RMS_EPS = 1e-6

LANES = 128
VMEM_LIMIT = 56 * 1024 * 1024

ROW_TILE = 512
ATT_TILE = 512
RET_TILE = 256
MOE_TILE = 256
N_MAIN = 7 * 512


def _rms(xf, w):
    return xf * lax.rsqrt(jnp.mean(xf * xf, axis=-1, keepdims=True) + RMS_EPS) * w


def _dot(a, b):
    return jnp.dot(a, b, preferred_element_type=F32)


def _dot_nt(a, b):
    return lax.dot_general(a, b, (((1,), (1,)), ((), ())), preferred_element_type=F32)


def _dot_tn(a, b):
    return lax.dot_general(a, b, (((0,), (0,)), ((), ())), preferred_element_type=F32)


CHUNKS = D_MODEL // LANES


def _tm_load(ref, base, rows):
    return jnp.concatenate([ref[pl.ds(base + c, rows, stride=CHUNKS), :] for c in range(CHUNKS)],
                           axis=1)


def _tm_store(ref, base, rows, val):
    for c in range(CHUNKS):
        ref[pl.ds(base + c, rows, stride=CHUNKS), :] = val[:, c * LANES:(c + 1) * LANES]


def _inproj_kernel(*refs, has_y, tiles_per_seq):
    if has_y:
        x_ref, y0_ref, y1_ref, rw_ref = refs[:4]
        refs = refs[4:]
    else:
        x_ref = refs[0]
        refs = refs[1:]
    nw_ref, w_ref, wff_ref, bff_ref, tri_ref = refs[:5]
    refs = refs[5:]
    if has_y:
        xres_ref = refs[0]
        refs = refs[1:]
    fq_ref, fk_ref, fv_ref, rq_ref, rk_ref, rv_ref, rg_ref, ct_ref, carry_sc = refs

    i = pl.program_id(0)
    x = x_ref[...]
    if has_y:
        rw = rw_ref[...]
        tm = x.shape[0]
        x = x + (rw[:, 0:1] * _tm_load(y0_ref, 0, tm) + rw[:, 1:2] * _tm_load(y1_ref, 0, tm))
        xres_ref[...] = x
    h = _rms(x, nw_ref[...]).astype(BF16)

    outs = (fq_ref, fk_ref, fv_ref, rq_ref, rk_ref, rv_ref, rg_ref)
    for j, o_ref in enumerate(outs):
        acc = _dot(h, w_ref[:, j * 512:(j + 1) * 512])
        if j == 0:
            acc = acc * (FOX_HEAD_DIM ** -0.5)
        o_ref[...] = acc.astype(BF16)

    z = _dot(h, wff_ref[...]) + bff_ref[...]
    lf = jnp.minimum(z, 0.0) - jnp.log1p(jnp.exp(-jnp.abs(z)))
    lane = lax.broadcasted_iota(jnp.int32, lf.shape, 1)
    lf = jnp.where(lane < FOX_HEADS, lf, 0.0)
    hi = lf.astype(BF16)
    r1 = lf - hi.astype(F32)
    mid = r1.astype(BF16)
    lo = (r1 - mid.astype(F32)).astype(BF16)
    tri = tri_ref[...]
    cs = _dot(tri, hi) + _dot(tri, mid) + _dot(tri, lo)

    @pl.when(i % tiles_per_seq == 0)
    def _():
        carry_sc[...] = jnp.zeros_like(carry_sc)

    c = cs + carry_sc[0:1, :]
    carry_sc[...] = jnp.broadcast_to(c[-1:, :], carry_sc.shape)
    ct_ref[0] = c.T[:8, :]


def _inproj(x, y, rw, nw, w_main, w_ff, b_ff, tri, seq):
    n = x.shape[0]
    tm = ROW_TILE
    nt = n // tm
    tps = seq // tm
    has_y = y is not None
    row_spec = pl.BlockSpec((tm, D_MODEL), lambda i: (i, 0))
    in_specs = [row_spec]
    args = [x]
    if has_y:
        in_specs += [pl.BlockSpec((tm * CHUNKS, LANES), lambda i: (i, 0)),
                     pl.BlockSpec((tm * CHUNKS, LANES), lambda i: (i + nt, 0)),
                     pl.BlockSpec((tm, LANES), lambda i: (i, 0))]
        args += [y, y, rw]
    in_specs += [
        pl.BlockSpec((1, D_MODEL), lambda i: (0, 0)),
        pl.BlockSpec((D_MODEL, N_MAIN), lambda i: (0, 0)),
        pl.BlockSpec((D_MODEL, LANES), lambda i: (0, 0)),
        pl.BlockSpec((1, LANES), lambda i: (0, 0)),
        pl.BlockSpec((tm, tm), lambda i: (0, 0)),
    ]
    args += [nw, w_main, w_ff, b_ff, tri]
    half_spec = pl.BlockSpec((tm, 512), lambda i: (i, 0))
    out_shape = []
    out_specs = []
    if has_y:
        out_shape.append(jax.ShapeDtypeStruct((n, D_MODEL), F32))
        out_specs.append(row_spec)
    out_shape += [jax.ShapeDtypeStruct((n, 512), BF16)] * 7
    out_specs += [half_spec] * 7
    out_shape.append(jax.ShapeDtypeStruct((n // seq, 8, seq), F32))
    out_specs.append(pl.BlockSpec((1, 8, tm), lambda i: (i // tps, 0, i % tps)))
    outs = pl.pallas_call(
        functools.partial(_inproj_kernel, has_y=has_y, tiles_per_seq=tps),
        grid=(nt,),
        in_specs=in_specs,
        out_specs=out_specs,
        out_shape=out_shape,
        scratch_shapes=[pltpu.VMEM((8, LANES), F32)],
        compiler_params=pltpu.CompilerParams(
            dimension_semantics=("arbitrary",), vmem_limit_bytes=VMEM_LIMIT),
        name="inproj_y" if has_y else "inproj",
    )(*args)
    if has_y:
        return outs[0], outs[1:]
    return x, outs


def _fox_kernel(q_ref, k_ref, v_ref, ct_ref, o_ref, m_sc, acc_sc, *, tile):
    hp = pl.program_id(1)
    qi = pl.program_id(2)
    q2 = q_ref[0]
    lane = lax.broadcasted_iota(jnp.int32, q2.shape, 1)
    first = lane < FOX_HEAD_DIM
    zero = jnp.zeros_like(q2)
    qh = (jnp.where(first, q2, zero), jnp.where(first, zero, q2))
    reps = tile // LANES

    def head_step(h, kb, k_blk, v_blk, mask, m_old, acc_old):
        one = jnp.ones_like(v_blk)
        va = jnp.where(first, v_blk, one) if h == 0 else jnp.where(first, one, v_blk)
        s = _dot_nt(qh[h], k_blk) - ct_ref[0, 2 * hp + h, pl.ds(kb, 1), :]
        if mask is not None:
            s = jnp.where(mask, s, -jnp.inf)
        m_cur = jnp.max(s, axis=1, keepdims=True)
        if m_old is None:
            m_new = jnp.broadcast_to(m_cur, (tile, LANES))
            p = jnp.exp(s - jnp.concatenate([m_new] * reps, axis=1))
            acc = _dot(p.astype(BF16), va)
        else:
            m_new = jnp.maximum(m_old, m_cur)
            alpha = jnp.exp(m_old - m_new)
            p = jnp.exp(s - jnp.concatenate([m_new] * reps, axis=1))
            acc = alpha * acc_old + _dot(p.astype(BF16), va)
        return m_new, acc

    row = lax.broadcasted_iota(jnp.int32, (tile, tile), 0)
    col = lax.broadcasted_iota(jnp.int32, (tile, tile), 1)
    start = pl.multiple_of(qi * tile, tile)
    k_blk = k_ref[0, pl.ds(start, tile), :]
    v_blk = v_ref[0, pl.ds(start, tile), :]
    for h in range(2):
        m_new, acc = head_step(h, qi, k_blk, v_blk, col <= row, None, None)
        m_sc[h] = m_new
        acc_sc[h] = acc

    def body(kb, carry):
        start = pl.multiple_of(kb * tile, tile)
        k_blk = k_ref[0, pl.ds(start, tile), :]
        v_blk = v_ref[0, pl.ds(start, tile), :]
        old = [(m_sc[h], acc_sc[h]) for h in range(2)]
        new = [head_step(h, kb, k_blk, v_blk, None, *old[h]) for h in range(2)]
        for h in range(2):
            m_sc[h] = new[h][0]
            acc_sc[h] = new[h][1]
        return carry

    lax.fori_loop(0, qi, body, 0)

    a0 = acc_sc[0]
    a1 = acc_sc[1]
    half = FOX_HEAD_DIM
    o = jnp.where(first, a0 / pltpu.roll(a0, half, 1), a1 / pltpu.roll(a1, half, 1))
    o_ref[0] = o.astype(o_ref.dtype)


def _fox_attention(fq, fk, fv, ct4, batch, seq):
    t = ATT_TILE
    nq = seq // t
    q3 = fq.reshape(batch, seq, FOX_WIDTH)
    k3 = fk.reshape(batch, seq, FOX_WIDTH)
    v3 = fv.reshape(batch, seq, FOX_WIDTH)
    out = pl.pallas_call(
        functools.partial(_fox_kernel, tile=t),
        grid=(batch, FOX_HEADS // 2, nq),
        in_specs=[
            pl.BlockSpec((1, t, LANES), lambda b, j, i: (b, i, j)),
            pl.BlockSpec((1, seq, LANES), lambda b, j, i: (b, 0, j)),
            pl.BlockSpec((1, seq, LANES), lambda b, j, i: (b, 0, j)),
            pl.BlockSpec((1, 8, nq, t), lambda b, j, i: (b, 0, 0, 0)),
        ],
        out_specs=pl.BlockSpec((1, t, LANES), lambda b, j, i: (b, i, j)),
        out_shape=jax.ShapeDtypeStruct((batch, seq, FOX_WIDTH), BF16),
        scratch_shapes=[pltpu.VMEM((2, t, LANES), F32), pltpu.VMEM((2, t, LANES), F32)],
        compiler_params=pltpu.CompilerParams(
            dimension_semantics=("parallel", "parallel", "parallel"),
            vmem_limit_bytes=VMEM_LIMIT),
        name="fox_attention",
    )(q3, k3, v3, ct4)
    return out.reshape(batch * seq, FOX_WIDTH)


def _ret_kernel(q_ref, k_ref, v_ref, g_ref, cos_ref, sin_ref, dmat_ref, qdec_ref, kdec_ref,
                sdec_ref, o_ref, state_sc):
    si = pl.program_id(1)

    @pl.when(si == 0)
    def _():
        state_sc[...] = jnp.zeros_like(state_sc)

    cos2 = cos_ref[...]
    sin2 = sin_ref[...]
    dk = RET_HEAD_DIM

    def rot(xf):
        return xf * cos2 + pltpu.roll(xf, dk // 2, 1) * sin2

    for h in range(RET_HEADS):
        cols = slice(h * dk, (h + 1) * dk)
        q = rot(q_ref[0, :, cols].astype(F32))
        k = rot(k_ref[0, :, cols].astype(F32)) * (dk ** -0.5)
        v = v_ref[0, :, cols]
        scores = _dot_nt(q.astype(BF16), k.astype(BF16)) * dmat_ref[h]
        intra = _dot(scores.astype(BF16), v)
        state = state_sc[h]
        cross = _dot((q * qdec_ref[h]).astype(BF16), state.astype(BF16))
        out = intra + cross
        state_sc[h] = state * sdec_ref[h, 0:1, :] + _dot_tn((k * kdec_ref[h]).astype(BF16), v)

        y = out * lax.rsqrt(jnp.mean(out * out, axis=-1, keepdims=True) + RMS_EPS)
        g = g_ref[0, :, cols].astype(F32)
        o_ref[0, :, cols] = (y * (g * jax.nn.sigmoid(g))).astype(o_ref.dtype)


def _ret_tables(seq):
    half = RET_HEAD_DIM // 2
    inv_freq = 1.0 / (ROPE_BASE ** (jnp.arange(half, dtype=F32) / half))
    ang = jnp.arange(seq, dtype=F32)[:, None] * inv_freq[None, :]
    cos = jnp.cos(ang)
    sin = jnp.sin(ang)
    cos2 = jnp.concatenate([cos, cos], axis=1)
    sin2 = jnp.concatenate([-sin, sin], axis=1)
    lt = RET_TILE
    log_gamma = jnp.log(1.0 - 2.0 ** (-5.0 - jnp.arange(RET_HEADS, dtype=F32)))
    idx = jnp.arange(lt)
    t = idx[:, None]
    s = idx[None, :]
    same = (t // CHUNK) == (s // CHUNK)
    earlier = (s // CHUNK) < (t // CHUNK)
    dist = jnp.where(same, jnp.abs(t - s), t - s).astype(F32)
    dmat = jnp.where((same | earlier)[None], jnp.exp(log_gamma[:, None, None] * dist[None]), 0.0)
    idxf = idx.astype(F32)
    qdec = jnp.exp(log_gamma[:, None] * idxf[None, :])
    kdec = jnp.exp(log_gamma[:, None] * (lt - idxf)[None, :])
    sdec = jnp.exp(log_gamma * lt)
    qdec = jnp.broadcast_to(qdec[:, :, None], (RET_HEADS, lt, LANES))
    kdec = jnp.broadcast_to(kdec[:, :, None], (RET_HEADS, lt, LANES))
    sdec = jnp.broadcast_to(sdec[:, None, None], (RET_HEADS, 8, LANES))
    return cos2, sin2, dmat, qdec, kdec, sdec


def _retention(rq, rk, rv, rg, tables, batch, seq):
    lt = RET_TILE
    ns = seq // lt
    cos2, sin2, dmat, qdec, kdec, sdec = tables
    blk = pl.BlockSpec((1, lt, RET_WIDTH), lambda b, i: (b, i, 0))
    tab = pl.BlockSpec((lt, LANES), lambda b, i: (i, 0))
    args = [a.reshape(batch, seq, RET_WIDTH) for a in (rq, rk, rv, rg)]
    out = pl.pallas_call(
        _ret_kernel,
        grid=(batch, ns),
        in_specs=[blk, blk, blk, blk, tab, tab,
                  pl.BlockSpec((RET_HEADS, lt, lt), lambda b, i: (0, 0, 0)),
                  pl.BlockSpec((RET_HEADS, lt, LANES), lambda b, i: (0, 0, 0)),
                  pl.BlockSpec((RET_HEADS, lt, LANES), lambda b, i: (0, 0, 0)),
                  pl.BlockSpec((RET_HEADS, 8, LANES), lambda b, i: (0, 0, 0))],
        out_specs=blk,
        out_shape=jax.ShapeDtypeStruct((batch, seq, RET_WIDTH), BF16),
        scratch_shapes=[pltpu.VMEM((RET_HEADS, RET_HEAD_DIM, RET_HEAD_DIM), F32)],
        compiler_params=pltpu.CompilerParams(
            dimension_semantics=("parallel", "arbitrary"),
            vmem_limit_bytes=VMEM_LIMIT),
        name="retention",
    )(*args, cos2, sin2, dmat, qdec, kdec, sdec)
    return out.reshape(batch * seq, RET_WIDTH)


def _outproj_kernel(fox_ref, ret_ref, x_ref, wo_ref, nw_ref, wr_ref, br_ref, xo_ref, xg_ref,
                    eid_ref, rw_ref):
    mixed = jnp.concatenate([fox_ref[...], ret_ref[...]], axis=1)
    x = x_ref[...] + _dot(mixed, wo_ref[...])
    xo_ref[...] = x
    _tm_store(xg_ref, 0, x.shape[0], x)
    h = _rms(x, nw_ref[...]).astype(BF16)
    lt = (_dot(h, wr_ref[...]) + br_ref[...]).T
    tm = lt.shape[1]
    rowid = lax.broadcasted_iota(jnp.int32, (8, tm), 0)
    neg = -jnp.inf

    def top1(v):
        vmax = jnp.max(v, axis=0, keepdims=True)
        idx = jnp.min(jnp.where(v == vmax, rowid, 8), axis=0, keepdims=True)
        return vmax, idx

    gl = jnp.where(rowid < N_GROUPS, lt[0:8], neg)
    gmax, gidx = top1(gl)
    g_w = 1.0 / jnp.sum(jnp.exp(gl - gmax), axis=0, keepdims=True)
    e_in = jnp.zeros((8, tm), F32)
    for g in range(N_GROUPS):
        e_in = jnp.where(gidx == g, lt[8 + 8 * g:16 + 8 * g], e_in)
    v1, i1 = top1(e_in)
    rest = jnp.where(rowid == i1, neg, e_in)
    v2, i2 = top1(rest)
    t = jnp.exp(v2 - v1)
    w1 = g_w / (1.0 + t)
    eid_ref[0:1, :] = gidx * EXPERTS_PER_GROUP + i1
    eid_ref[1:2, :] = gidx * EXPERTS_PER_GROUP + i2
    wslab = jnp.concatenate([w1, w1 * t, jnp.zeros((LANES - TOP_K, tm), F32)], axis=0)
    rw_ref[...] = wslab.T


def _outproj(fox, ret, x, wo, nw, wr, br):
    n = x.shape[0]
    tm = ROW_TILE
    row = pl.BlockSpec((tm, D_MODEL), lambda i: (i, 0))
    half = pl.BlockSpec((tm, 512), lambda i: (i, 0))
    pair = pl.BlockSpec((TOP_K, tm), lambda i: (0, i))
    wts = pl.BlockSpec((tm, LANES), lambda i: (i, 0))
    return pl.pallas_call(
        _outproj_kernel,
        grid=(n // tm,),
        in_specs=[half, half, row,
                  pl.BlockSpec((D_MODEL, D_MODEL), lambda i: (0, 0)),
                  pl.BlockSpec((1, D_MODEL), lambda i: (0, 0)),
                  pl.BlockSpec((D_MODEL, LANES), lambda i: (0, 0)),
                  pl.BlockSpec((1, LANES), lambda i: (0, 0))],
        out_specs=[row, pl.BlockSpec((tm * CHUNKS, LANES), lambda i: (i, 0)), pair, wts],
        out_shape=[jax.ShapeDtypeStruct((n, D_MODEL), F32),
                   jax.ShapeDtypeStruct((n * CHUNKS, LANES), F32),
                   jax.ShapeDtypeStruct((TOP_K, n), jnp.int32),
                   jax.ShapeDtypeStruct((n, LANES), F32)],
        compiler_params=pltpu.CompilerParams(
            dimension_semantics=("parallel",), vmem_limit_bytes=VMEM_LIMIT),
        name="outproj",
    )(fox, ret, x, wo, nw, wr, br)


_FIRST, _LAST, _VALID, _NEWEXP = 1, 2, 4, 8


def _moe_kernel(blk_ref, exp_ref, flag_ref, starts_ref,
                tokc_ref, tokn_ref, dstp_ref, dstc_ref, x_hbm, nw_ref, wg_ref, wu_ref, wd_ref,
                y_hbm, xbuf, ybuf, hbuf, wgb, wub, wdb, gsem, ssem, *, tb, nb):
    w = pl.program_id(0)
    b = blk_ref[w]
    e = exp_ref[w]
    flags = flag_ref[w]
    slot = b % 2
    nslot = 1 - slot
    span = tb * CHUNKS

    def hbm_row(ref, idx):
        return ref.at[pl.ds(pl.multiple_of(idx * CHUNKS, CHUNKS), CHUNKS), :]

    def buf_row(buf, s, r):
        return buf.at[pl.ds(pl.multiple_of(s * span + r * CHUNKS, CHUNKS), CHUNKS), :]

    def start_gather(tok_ref, s):
        for r in range(tb):
            pltpu.make_async_copy(hbm_row(x_hbm, tok_ref[0, 0, r]), buf_row(xbuf, s, r),
                                  gsem.at[s]).start()

    def start_scatter(dst_ref, s):
        for r in range(tb):
            pltpu.make_async_copy(buf_row(ybuf, s, r), hbm_row(y_hbm, dst_ref[0, 0, r]),
                                  ssem.at[s]).start()

    def wait_rows(sem, s):
        whole = pl.ds(pl.multiple_of(s * span, span), span)
        pltpu.make_async_copy(xbuf.at[whole, :], ybuf.at[whole, :], sem.at[s]).wait()

    @pl.when((flags & _FIRST) != 0)
    def _():
        @pl.when(w == 0)
        def _():
            start_gather(tokc_ref, 0)

        wait_rows(gsem, slot)

        @pl.when(b >= 2)
        def _():
            wait_rows(ssem, slot)

        @pl.when(b + 1 < nb)
        def _():
            start_gather(tokn_ref, nslot)

        @pl.when(b >= 1)
        def _():
            start_scatter(dstp_ref, nslot)

        hbuf[...] = _rms(_tm_load(xbuf, slot * span, tb), nw_ref[...]).astype(BF16)
        ybuf[pl.ds(pl.multiple_of(slot * span, span), span), :] = jnp.zeros((span, LANES), F32)

    @pl.when((flags & _NEWEXP) != 0)
    def _():
        wgb[...] = wg_ref[0].astype(BF16)
        wub[...] = wu_ref[0].astype(BF16)
        wdb[...] = wd_ref[0].astype(BF16)

    @pl.when((flags & _VALID) != 0)
    def _():
        h = hbuf[...]
        g = _dot(h, wgb[...])
        u = _dot(h, wub[...])
        a = (g * jax.nn.sigmoid(g) * u).astype(BF16)
        y = _dot(a, wdb[...])
        q = b * tb + lax.broadcasted_iota(jnp.int32, y.shape, 0)
        mine = (q >= starts_ref[e]) & (q < starts_ref[e + 1])
        _tm_store(ybuf, slot * span, tb, jnp.where(mine, y, _tm_load(ybuf, slot * span, tb)))

    @pl.when(((flags & _LAST) != 0) & (b == nb - 1))
    def _():
        start_scatter(dstc_ref, slot)
        if nb >= 2:
            wait_rows(ssem, nslot)
        wait_rows(ssem, slot)


def _moe(x, s_tok, s_dst, blk, exp, flags, starts, nw, wg, wu, wd, tb=MOE_TILE):
    na = s_tok.shape[0]
    nb = na // tb
    d_model, d_expert = wg.shape[1], wg.shape[2]
    tok3 = s_tok.reshape(nb, 1, tb)
    dst3 = s_dst.reshape(nb, 1, tb)
    smem_blk = lambda f: pl.BlockSpec((1, 1, tb), f, memory_space=pltpu.SMEM)
    grid_spec = pltpu.PrefetchScalarGridSpec(
        num_scalar_prefetch=4,
        grid=(blk.shape[0],),
        in_specs=[
            smem_blk(lambda w, bl, ex, fl, st: (bl[w], 0, 0)),
            smem_blk(lambda w, bl, ex, fl, st: (jnp.minimum(bl[w] + 1, nb - 1), 0, 0)),
            smem_blk(lambda w, bl, ex, fl, st: (jnp.maximum(bl[w] - 1, 0), 0, 0)),
            smem_blk(lambda w, bl, ex, fl, st: (bl[w], 0, 0)),
            pl.BlockSpec(memory_space=pl.ANY),
            pl.BlockSpec((1, d_model), lambda w, bl, ex, fl, st: (0, 0)),
            pl.BlockSpec((1, d_model, d_expert), lambda w, bl, ex, fl, st: (ex[w], 0, 0)),
            pl.BlockSpec((1, d_model, d_expert), lambda w, bl, ex, fl, st: (ex[w], 0, 0)),
            pl.BlockSpec((1, d_expert, d_model), lambda w, bl, ex, fl, st: (ex[w], 0, 0)),
        ],
        out_specs=pl.BlockSpec(memory_space=pl.ANY),
        scratch_shapes=[pltpu.VMEM((2 * tb * CHUNKS, LANES), F32),
                        pltpu.VMEM((2 * tb * CHUNKS, LANES), F32),
                        pltpu.VMEM((tb, d_model), BF16),
                        pltpu.VMEM((d_model, d_expert), BF16), pltpu.VMEM((d_model, d_expert), BF16),
                        pltpu.VMEM((d_expert, d_model), BF16),
                        pltpu.SemaphoreType.DMA((2,)), pltpu.SemaphoreType.DMA((2,))],
    )
    return pl.pallas_call(
        functools.partial(_moe_kernel, tb=tb, nb=nb),
        grid_spec=grid_spec,
        out_shape=jax.ShapeDtypeStruct((na * CHUNKS, LANES), F32),
        compiler_params=pltpu.CompilerParams(
            dimension_semantics=("arbitrary",), vmem_limit_bytes=VMEM_LIMIT),
        name="moe_experts",
    )(blk, exp, flags, starts, tok3, tok3, dst3, dst3, x, nw, wg, wu, wd)


def _dispatch_plan(eid, n, tb=MOE_TILE, n_experts=N_EXPERTS):
    na = TOP_K * n
    nb = na // tb
    eid_flat = eid.reshape(na)
    _, s_a = lax.sort((eid_flat, jnp.arange(na, dtype=jnp.int32)), num_keys=1)
    s_tok = s_a % n
    counts = jnp.sum(eid_flat[None, :] == jnp.arange(n_experts, dtype=jnp.int32)[:, None], axis=1)
    starts = jnp.concatenate([jnp.zeros((1,), jnp.int32),
                              jnp.cumsum(counts).astype(jnp.int32)])
    lo, hi = starts[:-1], starts[1:]
    nonempty = hi > lo
    first_blk = lo // tb
    npass = jnp.where(nonempty, (hi - 1) // tb - first_blk + 1, 0)
    cum = jnp.cumsum(npass)
    total = cum[-1]
    n_pass = nb + n_experts
    w = jnp.arange(n_pass, dtype=jnp.int32)
    wc = jnp.minimum(w, total - 1)
    ex = jnp.searchsorted(cum, wc, side='right').astype(jnp.int32)
    blk = (first_blk[ex] + (wc - (cum[ex] - npass[ex]))).astype(jnp.int32)
    valid = w < total
    prev_blk = jnp.concatenate([jnp.full((1,), -1, jnp.int32), blk[:-1]])
    next_blk = jnp.concatenate([blk[1:], jnp.full((1,), -1, jnp.int32)])
    first = valid & (blk != prev_blk)
    last = valid & ((blk != next_blk) | (w == total - 1))
    prev_ex = jnp.concatenate([jnp.full((1,), -1, jnp.int32), ex[:-1]])
    newexp = valid & (ex != prev_ex)
    flags = (first * _FIRST + last * _LAST + valid * _VALID + newexp * _NEWEXP).astype(jnp.int32)
    return s_tok, s_a, blk, ex, flags, starts


def _final_kernel(x_ref, y0_ref, y1_ref, rw_ref, nw_ref, o_ref):
    rw = rw_ref[...]
    tm = x_ref.shape[0]
    x = x_ref[...] + (rw[:, 0:1] * _tm_load(y0_ref, 0, tm) + rw[:, 1:2] * _tm_load(y1_ref, 0, tm))
    o_ref[...] = _rms(x, nw_ref[...])


def _final(x, y, rw, nw):
    n = x.shape[0]
    tm = ROW_TILE
    nt = n // tm
    row = pl.BlockSpec((tm, D_MODEL), lambda i: (i, 0))
    return pl.pallas_call(
        _final_kernel,
        grid=(nt,),
        in_specs=[row, pl.BlockSpec((tm * CHUNKS, LANES), lambda i: (i, 0)),
                  pl.BlockSpec((tm * CHUNKS, LANES), lambda i: (i + nt, 0)),
                  pl.BlockSpec((tm, LANES), lambda i: (i, 0)),
                  pl.BlockSpec((1, D_MODEL), lambda i: (0, 0))],
        out_specs=row,
        out_shape=jax.ShapeDtypeStruct((n, D_MODEL), F32),
        compiler_params=pltpu.CompilerParams(
            dimension_semantics=("parallel",), vmem_limit_bytes=VMEM_LIMIT),
        name="final_norm",
    )(x, y, y, rw, nw)


def kernel(x, norm_mix_w, w_in, fox_forget_b, w_out, norm_ffn_w, w_router_group, b_router_group,
           w_router_expert, b_router_expert, w_expert_gate, w_expert_up, w_expert_down,
           norm_final_w):
    batch, seq, d = x.shape
    n = batch * seq
    depth = w_in.shape[0]
    xf = x.reshape(n, d)
    tables = _ret_tables(seq)
    tri = jnp.tril(jnp.ones((ROW_TILE, ROW_TILE), F32)).astype(BF16)
    nq = seq // ATT_TILE

    y = rw = None
    for layer in range(depth):
        wl = w_in[layer]
        c0 = 3 * FOX_WIDTH
        w_main = jnp.concatenate([wl[:, :c0], wl[:, c0 + FOX_HEADS:]], axis=1).astype(BF16)
        w_ff = jnp.pad(wl[:, c0:c0 + FOX_HEADS], ((0, 0), (0, LANES - FOX_HEADS))).astype(BF16)
        b_ff = jnp.pad(fox_forget_b[layer], (0, LANES - FOX_HEADS)).reshape(1, LANES)
        xf, (fq, fk, fv, rq, rk, rv, rg, ct) = _inproj(
            xf, y, rw, norm_mix_w[layer].reshape(1, d), w_main, w_ff, b_ff, tri, seq)
        ct4 = ct.reshape(batch, 8, nq, ATT_TILE)
        fox = _fox_attention(fq, fk, fv, ct4, batch, seq)
        ret = _retention(rq, rk, rv, rg, tables, batch, seq)

        zpad = jnp.zeros((d, N_GROUPS), F32)
        w_r = jnp.concatenate([w_router_group[layer], zpad, w_router_expert[layer]], axis=1)
        nr = 2 * N_GROUPS + N_EXPERTS
        w_r = jnp.pad(w_r, ((0, 0), (0, LANES - nr))).astype(BF16)
        b_r = jnp.concatenate([b_router_group[layer], jnp.zeros((N_GROUPS,), F32),
                               b_router_expert[layer]])
        b_r = jnp.pad(b_r, (0, LANES - nr)).reshape(1, LANES)
        xf, xg, eid, rw = _outproj(fox, ret, xf, w_out[layer].astype(BF16),
                                   norm_ffn_w[layer].reshape(1, d), w_r, b_r)
        s_tok, s_dst, blk, ex, flags, starts = _dispatch_plan(eid, n)
        y = _moe(xg, s_tok, s_dst, blk, ex, flags, starts, norm_ffn_w[layer].reshape(1, d),
                 w_expert_gate[layer], w_expert_up[layer], w_expert_down[layer])
    out = _final(xf, y, rw, norm_final_w.reshape(1, d))
    return out.reshape(batch, seq, d)
```

```python
import functools

import jax
import jax.numpy as jnp
import numpy as np
from jax import lax
from jax.experimental import pallas as pl
from jax.experimental.pallas import tpu as pltpu

F32 = jnp.float32
BF16 = jnp.bfloat16

D_MODEL = 1024
FOX_HEADS = 8
FOX_HEAD_DIM = 64
FOX_WIDTH = 512
RET_HEADS = 4
RET_HEAD_DIM = 128
RET_WIDTH = 512
CHUNK = 64
ROPE_BASE = 10000.0
N_GROUPS = 4
EXPERTS_PER_GROUP = 8
N_EXPERTS = 32
TOP_K = 2
D_EXPERT = 512
RMS_EPS = 1e-6

LANES = 128
VMEM_LIMIT = 56 * 1024 * 1024

ROW_TILE = 512
ATT_TILE = 512
RET_TILE = 256
MOE_TILE = 256
N_MAIN = 7 * 512


def _rms(xf, w):
    return xf * lax.rsqrt(jnp.mean(xf * xf, axis=-1, keepdims=True) + RMS_EPS) * w


def _dot(a, b):
    return jnp.dot(a, b, preferred_element_type=F32)


def _dot_nt(a, b):
    return lax.dot_general(a, b, (((1,), (1,)), ((), ())), preferred_element_type=F32)


def _dot_tn(a, b):
    return lax.dot_general(a, b, (((0,), (0,)), ((), ())), preferred_element_type=F32)


CHUNKS = D_MODEL // LANES


def _tm_load(ref, base, rows):
    return jnp.concatenate([ref[pl.ds(base + c, rows, stride=CHUNKS), :] for c in range(CHUNKS)],
                           axis=1)


def _tm_store(ref, base, rows, val):
    for c in range(CHUNKS):
        ref[pl.ds(base + c, rows, stride=CHUNKS), :] = val[:, c * LANES:(c + 1) * LANES]


def _inproj_kernel(*refs, has_y, tiles_per_seq):
    if has_y:
        x_ref, y0_ref, y1_ref, rw_ref = refs[:4]
        refs = refs[4:]
    else:
        x_ref = refs[0]
        refs = refs[1:]
    nw_ref, w_ref, wff_ref, bff_ref, tri_ref = refs[:5]
    refs = refs[5:]
    if has_y:
        xres_ref = refs[0]
        refs = refs[1:]
    fq_ref, fk_ref, fv_ref, rq_ref, rk_ref, rv_ref, rg_ref, ct_ref, carry_sc = refs

    i = pl.program_id(0)
    x = x_ref[...]
    if has_y:
        rw = rw_ref[...]
        tm = x.shape[0]
        x = x + (rw[:, 0:1] * _tm_load(y0_ref, 0, tm) + rw[:, 1:2] * _tm_load(y1_ref, 0, tm))
        xres_ref[...] = x
    h = _rms(x, nw_ref[...]).astype(BF16)

    outs = (fq_ref, fk_ref, fv_ref, rq_ref, rk_ref, rv_ref, rg_ref)
    for j, o_ref in enumerate(outs):
        acc = _dot(h, w_ref[:, j * 512:(j + 1) * 512])
        if j == 0:
            acc = acc * (FOX_HEAD_DIM ** -0.5)
        o_ref[...] = acc.astype(BF16)

    z = _dot(h, wff_ref[...]) + bff_ref[...]
    lf = jnp.minimum(z, 0.0) - jnp.log1p(jnp.exp(-jnp.abs(z)))
    lane = lax.broadcasted_iota(jnp.int32, lf.shape, 1)
    lf = jnp.where(lane < FOX_HEADS, lf, 0.0)
    hi = lf.astype(BF16)
    r1 = lf - hi.astype(F32)
    mid = r1.astype(BF16)
    lo = (r1 - mid.astype(F32)).astype(BF16)
    tri = tri_ref[...]
    cs = _dot(tri, hi) + _dot(tri, mid) + _dot(tri, lo)

    @pl.when(i % tiles_per_seq == 0)
    def _():
        carry_sc[...] = jnp.zeros_like(carry_sc)

    c = cs + carry_sc[0:1, :]
    carry_sc[...] = jnp.broadcast_to(c[-1:, :], carry_sc.shape)
    ct_ref[0] = c.T[:8, :]


def _inproj(x, y, rw, nw, w_main, w_ff, b_ff, tri, seq):
    n = x.shape[0]
    tm = ROW_TILE
    nt = n // tm
    tps = seq // tm
    has_y = y is not None
    row_spec = pl.BlockSpec((tm, D_MODEL), lambda i: (i, 0))
    in_specs = [row_spec]
    args = [x]
    if has_y:
        in_specs += [pl.BlockSpec((tm * CHUNKS, LANES), lambda i: (i, 0)),
                     pl.BlockSpec((tm * CHUNKS, LANES), lambda i: (i + nt, 0)),
                     pl.BlockSpec((tm, LANES), lambda i: (i, 0))]
        args += [y, y, rw]
    in_specs += [
        pl.BlockSpec((1, D_MODEL), lambda i: (0, 0)),
        pl.BlockSpec((D_MODEL, N_MAIN), lambda i: (0, 0)),
        pl.BlockSpec((D_MODEL, LANES), lambda i: (0, 0)),
        pl.BlockSpec((1, LANES), lambda i: (0, 0)),
        pl.BlockSpec((tm, tm), lambda i: (0, 0)),
    ]
    args += [nw, w_main, w_ff, b_ff, tri]
    half_spec = pl.BlockSpec((tm, 512), lambda i: (i, 0))
    out_shape = []
    out_specs = []
    if has_y:
        out_shape.append(jax.ShapeDtypeStruct((n, D_MODEL), F32))
        out_specs.append(row_spec)
    out_shape += [jax.ShapeDtypeStruct((n, 512), BF16)] * 7
    out_specs += [half_spec] * 7
    out_shape.append(jax.ShapeDtypeStruct((n // seq, 8, seq), F32))
    out_specs.append(pl.BlockSpec((1, 8, tm), lambda i: (i // tps, 0, i % tps)))
    outs = pl.pallas_call(
        functools.partial(_inproj_kernel, has_y=has_y, tiles_per_seq=tps),
        grid=(nt,),
        in_specs=in_specs,
        out_specs=out_specs,
        out_shape=out_shape,
        scratch_shapes=[pltpu.VMEM((8, LANES), F32)],
        compiler_params=pltpu.CompilerParams(
            dimension_semantics=("arbitrary",), vmem_limit_bytes=VMEM_LIMIT),
        name="inproj_y" if has_y else "inproj",
    )(*args)
    if has_y:
        return outs[0], outs[1:]
    return x, outs


def _fox_kernel(q_ref, k_ref, v_ref, ct_ref, o_ref, m_sc, acc_sc, *, tile):
    hp = pl.program_id(1)
    qi = pl.program_id(2)
    q2 = q_ref[0]
    lane = lax.broadcasted_iota(jnp.int32, q2.shape, 1)
    first = lane < FOX_HEAD_DIM
    zero = jnp.zeros_like(q2)
    qh = (jnp.where(first, q2, zero), jnp.where(first, zero, q2))
    reps = tile // LANES

    def head_step(h, kb, k_blk, v_blk, mask, m_old, acc_old):
        one = jnp.ones_like(v_blk)
        va = jnp.where(first, v_blk, one) if h == 0 else jnp.where(first, one, v_blk)
        s = _dot_nt(qh[h], k_blk) - ct_ref[0, 2 * hp + h, pl.ds(kb, 1), :]
        if mask is not None:
            s = jnp.where(mask, s, -jnp.inf)
        m_cur = jnp.max(s, axis=1, keepdims=True)
        if m_old is None:
            m_new = jnp.broadcast_to(m_cur, (tile, LANES))
            p = jnp.exp(s - jnp.concatenate([m_new] * reps, axis=1))
            acc = _dot(p.astype(BF16), va)
        else:
            m_new = jnp.maximum(m_old, m_cur)
            alpha = jnp.exp(m_old - m_new)
            p = jnp.exp(s - jnp.concatenate([m_new] * reps, axis=1))
            acc = alpha * acc_old + _dot(p.astype(BF16), va)
        return m_new, acc

    row = lax.broadcasted_iota(jnp.int32, (tile, tile), 0)
    col = lax.broadcasted_iota(jnp.int32, (tile, tile), 1)
    start = pl.multiple_of(qi * tile, tile)
    k_blk = k_ref[0, pl.ds(start, tile), :]
    v_blk = v_ref[0, pl.ds(start, tile), :]
    for h in range(2):
        m_new, acc = head_step(h, qi, k_blk, v_blk, col <= row, None, None)
        m_sc[h] = m_new
        acc_sc[h] = acc

    def off_diagonal(kbs):
        state = [(m_sc[h], acc_sc[h]) for h in range(2)]
        for kb in kbs:
            start = pl.multiple_of(kb * tile, tile)
            k_blk = k_ref[0, pl.ds(start, tile), :]
            v_blk = v_ref[0, pl.ds(start, tile), :]
            state = [head_step(h, kb, k_blk, v_blk, None, *state[h]) for h in range(2)]
        for h in range(2):
            m_sc[h] = state[h][0]
            acc_sc[h] = state[h][1]

    def pair(j, carry):
        off_diagonal((2 * j, 2 * j + 1))
        return carry

    lax.fori_loop(0, qi // 2, pair, 0)

    @pl.when(qi % 2 == 1)
    def _():
        off_diagonal((qi - 1,))

    a0 = acc_sc[0]
    a1 = acc_sc[1]
    half = FOX_HEAD_DIM
    o = jnp.where(first, a0 / pltpu.roll(a0, half, 1), a1 / pltpu.roll(a1, half, 1))
    o_ref[0] = o.astype(o_ref.dtype)


def _fox_attention(fq, fk, fv, ct4, batch, seq):
    t = ATT_TILE
    nq = seq // t
    q3 = fq.reshape(batch, seq, FOX_WIDTH)
    k3 = fk.reshape(batch, seq, FOX_WIDTH)
    v3 = fv.reshape(batch, seq, FOX_WIDTH)
    out = pl.pallas_call(
        functools.partial(_fox_kernel, tile=t),
        grid=(batch, FOX_HEADS // 2, nq),
        in_specs=[
            pl.BlockSpec((1, t, LANES), lambda b, j, i: (b, i, j)),
            pl.BlockSpec((1, seq, LANES), lambda b, j, i: (b, 0, j)),
            pl.BlockSpec((1, seq, LANES), lambda b, j, i: (b, 0, j)),
            pl.BlockSpec((1, 8, nq, t), lambda b, j, i: (b, 0, 0, 0)),
        ],
        out_specs=pl.BlockSpec((1, t, LANES), lambda b, j, i: (b, i, j)),
        out_shape=jax.ShapeDtypeStruct((batch, seq, FOX_WIDTH), BF16),
        scratch_shapes=[pltpu.VMEM((2, t, LANES), F32), pltpu.VMEM((2, t, LANES), F32)],
        compiler_params=pltpu.CompilerParams(
            dimension_semantics=("parallel", "parallel", "parallel"),
            vmem_limit_bytes=VMEM_LIMIT),
        name="fox_attention",
    )(q3, k3, v3, ct4)
    return out.reshape(batch * seq, FOX_WIDTH)


def _ret_kernel(q_ref, k_ref, v_ref, g_ref, cos_ref, sin_ref, dmat_ref, qdec_ref, kdec_ref,
                sdec_ref, o_ref, state_sc):
    si = pl.program_id(1)

    @pl.when(si == 0)
    def _():
        state_sc[...] = jnp.zeros_like(state_sc)

    cos2 = cos_ref[...]
    sin2 = sin_ref[...]
    dk = RET_HEAD_DIM

    def rot(xf):
        return xf * cos2 + pltpu.roll(xf, dk // 2, 1) * sin2

    for h in range(RET_HEADS):
        cols = slice(h * dk, (h + 1) * dk)
        q = rot(q_ref[0, :, cols].astype(F32))
        k = rot(k_ref[0, :, cols].astype(F32)) * (dk ** -0.5)
        v = v_ref[0, :, cols]
        scores = _dot_nt(q.astype(BF16), k.astype(BF16)) * dmat_ref[h]
        intra = _dot(scores.astype(BF16), v)
        state = state_sc[h]
        cross = _dot((q * qdec_ref[h]).astype(BF16), state.astype(BF16))
        out = intra + cross
        state_sc[h] = state * sdec_ref[h, 0:1, :] + _dot_tn((k * kdec_ref[h]).astype(BF16), v)

        y = out * lax.rsqrt(jnp.mean(out * out, axis=-1, keepdims=True) + RMS_EPS)
        g = g_ref[0, :, cols].astype(F32)
        o_ref[0, :, cols] = (y * (g * jax.nn.sigmoid(g))).astype(o_ref.dtype)


def _ret_tables(seq):
    half = RET_HEAD_DIM // 2
    inv_freq = 1.0 / (ROPE_BASE ** (jnp.arange(half, dtype=F32) / half))
    ang = jnp.arange(seq, dtype=F32)[:, None] * inv_freq[None, :]
    cos = jnp.cos(ang)
    sin = jnp.sin(ang)
    cos2 = jnp.concatenate([cos, cos], axis=1)
    sin2 = jnp.concatenate([-sin, sin], axis=1)
    lt = RET_TILE
    log_gamma = jnp.log(1.0 - 2.0 ** (-5.0 - jnp.arange(RET_HEADS, dtype=F32)))
    idx = jnp.arange(lt)
    t = idx[:, None]
    s = idx[None, :]
    same = (t // CHUNK) == (s // CHUNK)
    earlier = (s // CHUNK) < (t // CHUNK)
    dist = jnp.where(same, jnp.abs(t - s), t - s).astype(F32)
    dmat = jnp.where((same | earlier)[None], jnp.exp(log_gamma[:, None, None] * dist[None]), 0.0)
    idxf = idx.astype(F32)
    qdec = jnp.exp(log_gamma[:, None] * idxf[None, :])
    kdec = jnp.exp(log_gamma[:, None] * (lt - idxf)[None, :])
    sdec = jnp.exp(log_gamma * lt)
    qdec = jnp.broadcast_to(qdec[:, :, None], (RET_HEADS, lt, LANES))
    kdec = jnp.broadcast_to(kdec[:, :, None], (RET_HEADS, lt, LANES))
    sdec = jnp.broadcast_to(sdec[:, None, None], (RET_HEADS, 8, LANES))
    return cos2, sin2, dmat, qdec, kdec, sdec


def _retention(rq, rk, rv, rg, tables, batch, seq):
    lt = RET_TILE
    ns = seq // lt
    cos2, sin2, dmat, qdec, kdec, sdec = tables
    blk = pl.BlockSpec((1, lt, RET_WIDTH), lambda b, i: (b, i, 0))
    tab = pl.BlockSpec((lt, LANES), lambda b, i: (i, 0))
    args = [a.reshape(batch, seq, RET_WIDTH) for a in (rq, rk, rv, rg)]
    out = pl.pallas_call(
        _ret_kernel,
        grid=(batch, ns),
        in_specs=[blk, blk, blk, blk, tab, tab,
                  pl.BlockSpec((RET_HEADS, lt, lt), lambda b, i: (0, 0, 0)),
                  pl.BlockSpec((RET_HEADS, lt, LANES), lambda b, i: (0, 0, 0)),
                  pl.BlockSpec((RET_HEADS, lt, LANES), lambda b, i: (0, 0, 0)),
                  pl.BlockSpec((RET_HEADS, 8, LANES), lambda b, i: (0, 0, 0))],
        out_specs=blk,
        out_shape=jax.ShapeDtypeStruct((batch, seq, RET_WIDTH), BF16),
        scratch_shapes=[pltpu.VMEM((RET_HEADS, RET_HEAD_DIM, RET_HEAD_DIM), F32)],
        compiler_params=pltpu.CompilerParams(
            dimension_semantics=("parallel", "arbitrary"),
            vmem_limit_bytes=VMEM_LIMIT),
        name="retention",
    )(*args, cos2, sin2, dmat, qdec, kdec, sdec)
    return out.reshape(batch * seq, RET_WIDTH)


def _outproj_kernel(fox_ref, ret_ref, x_ref, wo_ref, nw_ref, wr_ref, br_ref, xo_ref, xg_ref,
                    eid_ref, rw_ref):
    mixed = jnp.concatenate([fox_ref[...], ret_ref[...]], axis=1)
    x = x_ref[...] + _dot(mixed, wo_ref[...])
    xo_ref[...] = x
    _tm_store(xg_ref, 0, x.shape[0], x)
    h = _rms(x, nw_ref[...]).astype(BF16)
    lt = (_dot(h, wr_ref[...]) + br_ref[...]).T
    tm = lt.shape[1]
    rowid = lax.broadcasted_iota(jnp.int32, (8, tm), 0)
    neg = -jnp.inf

    def top1(v):
        vmax = jnp.max(v, axis=0, keepdims=True)
        idx = jnp.min(jnp.where(v == vmax, rowid, 8), axis=0, keepdims=True)
        return vmax, idx

    gl = jnp.where(rowid < N_GROUPS, lt[0:8], neg)
    gmax, gidx = top1(gl)
    g_w = 1.0 / jnp.sum(jnp.exp(gl - gmax), axis=0, keepdims=True)
    e_in = jnp.zeros((8, tm), F32)
    for g in range(N_GROUPS):
        e_in = jnp.where(gidx == g, lt[8 + 8 * g:16 + 8 * g], e_in)
    v1, i1 = top1(e_in)
    rest = jnp.where(rowid == i1, neg, e_in)
    v2, i2 = top1(rest)
    t = jnp.exp(v2 - v1)
    w1 = g_w / (1.0 + t)
    eid_ref[0:1, :] = gidx * EXPERTS_PER_GROUP + i1
    eid_ref[1:2, :] = gidx * EXPERTS_PER_GROUP + i2
    wslab = jnp.concatenate([w1, w1 * t, jnp.zeros((LANES - TOP_K, tm), F32)], axis=0)
    rw_ref[...] = wslab.T


def _outproj(fox, ret, x, wo, nw, wr, br):
    n = x.shape[0]
    tm = ROW_TILE
    row = pl.BlockSpec((tm, D_MODEL), lambda i: (i, 0))
    half = pl.BlockSpec((tm, 512), lambda i: (i, 0))
    pair = pl.BlockSpec((TOP_K, tm), lambda i: (0, i))
    wts = pl.BlockSpec((tm, LANES), lambda i: (i, 0))
    return pl.pallas_call(
        _outproj_kernel,
        grid=(n // tm,),
        in_specs=[half, half, row,
                  pl.BlockSpec((D_MODEL, D_MODEL), lambda i: (0, 0)),
                  pl.BlockSpec((1, D_MODEL), lambda i: (0, 0)),
                  pl.BlockSpec((D_MODEL, LANES), lambda i: (0, 0)),
                  pl.BlockSpec((1, LANES), lambda i: (0, 0))],
        out_specs=[row, pl.BlockSpec((tm * CHUNKS, LANES), lambda i: (i, 0)), pair, wts],
        out_shape=[jax.ShapeDtypeStruct((n, D_MODEL), F32),
                   jax.ShapeDtypeStruct((n * CHUNKS, LANES), F32),
                   jax.ShapeDtypeStruct((TOP_K, n), jnp.int32),
                   jax.ShapeDtypeStruct((n, LANES), F32)],
        compiler_params=pltpu.CompilerParams(
            dimension_semantics=("parallel",), vmem_limit_bytes=VMEM_LIMIT),
        name="outproj",
    )(fox, ret, x, wo, nw, wr, br)


_FIRST, _LAST, _VALID, _NEWEXP = 1, 2, 4, 8


def _moe_kernel(blk_ref, exp_ref, flag_ref, starts_ref,
                tokc_ref, tokn_ref, dstp_ref, dstc_ref, x_hbm, nw_ref, wg_ref, wu_ref, wd_ref,
                y_hbm, xbuf, ybuf, hbuf, wgb, wub, wdb, gsem, ssem, *, tb, nb):
    w = pl.program_id(0)
    b = blk_ref[w]
    e = exp_ref[w]
    flags = flag_ref[w]
    slot = b % 2
    nslot = 1 - slot
    span = tb * CHUNKS

    def hbm_row(ref, idx):
        return ref.at[pl.ds(pl.multiple_of(idx * CHUNKS, CHUNKS), CHUNKS), :]

    def buf_row(buf, s, r):
        return buf.at[pl.ds(pl.multiple_of(s * span + r * CHUNKS, CHUNKS), CHUNKS), :]

    def start_gather(tok_ref, s):
        for r in range(tb):
            pltpu.make_async_copy(hbm_row(x_hbm, tok_ref[0, 0, r]), buf_row(xbuf, s, r),
                                  gsem.at[s]).start()

    def start_scatter(dst_ref, s):
        for r in range(tb):
            pltpu.make_async_copy(buf_row(ybuf, s, r), hbm_row(y_hbm, dst_ref[0, 0, r]),
                                  ssem.at[s]).start()

    def wait_rows(sem, s):
        whole = pl.ds(pl.multiple_of(s * span, span), span)
        pltpu.make_async_copy(xbuf.at[whole, :], ybuf.at[whole, :], sem.at[s]).wait()

    @pl.when((flags & _FIRST) != 0)
    def _():
        @pl.when(w == 0)
        def _():
            start_gather(tokc_ref, 0)

        wait_rows(gsem, slot)

        @pl.when(b >= 2)
        def _():
            wait_rows(ssem, slot)

        @pl.when(b + 1 < nb)
        def _():
            start_gather(tokn_ref, nslot)

        @pl.when(b >= 1)
        def _():
            start_scatter(dstp_ref, nslot)

        hbuf[...] = _rms(_tm_load(xbuf, slot * span, tb), nw_ref[...]).astype(BF16)
        ybuf[pl.ds(pl.multiple_of(slot * span, span), span), :] = jnp.zeros((span, LANES), F32)

    @pl.when((flags & _NEWEXP) != 0)
    def _():
        wgb[...] = wg_ref[0, 0].astype(BF16)
        wub[...] = wu_ref[0, 0].astype(BF16)
        wdb[...] = wd_ref[0, 0].astype(BF16)

    @pl.when((flags & _VALID) != 0)
    def _():
        h = hbuf[...]
        g = _dot(h, wgb[...])
        u = _dot(h, wub[...])
        a = (g * jax.nn.sigmoid(g) * u).astype(BF16)
        y = _dot(a, wdb[...])
        q = b * tb + lax.broadcasted_iota(jnp.int32, y.shape, 0)
        mine = (q >= starts_ref[e]) & (q < starts_ref[e + 1])
        _tm_store(ybuf, slot * span, tb, jnp.where(mine, y, _tm_load(ybuf, slot * span, tb)))

    @pl.when(((flags & _LAST) != 0) & (b == nb - 1))
    def _():
        start_scatter(dstc_ref, slot)
        if nb >= 2:
            wait_rows(ssem, nslot)
        wait_rows(ssem, slot)


def _moe(x, s_tok, s_dst, blk, exp, flags, starts, nw, wg, wu, wd, layer, tb=MOE_TILE):
    na = s_tok.shape[0]
    nb = na // tb
    d_model, d_expert = wg.shape[2], wg.shape[3]
    tok3 = s_tok.reshape(nb, 1, tb)
    dst3 = s_dst.reshape(nb, 1, tb)
    smem_blk = lambda f: pl.BlockSpec((1, 1, tb), f, memory_space=pltpu.SMEM)
    grid_spec = pltpu.PrefetchScalarGridSpec(
        num_scalar_prefetch=4,
        grid=(blk.shape[0],),
        in_specs=[
            smem_blk(lambda w, bl, ex, fl, st: (bl[w], 0, 0)),
            smem_blk(lambda w, bl, ex, fl, st: (jnp.minimum(bl[w] + 1, nb - 1), 0, 0)),
            smem_blk(lambda w, bl, ex, fl, st: (jnp.maximum(bl[w] - 1, 0), 0, 0)),
            smem_blk(lambda w, bl, ex, fl, st: (bl[w], 0, 0)),
            pl.BlockSpec(memory_space=pl.ANY),
            pl.BlockSpec((1, d_model), lambda w, bl, ex, fl, st: (0, 0)),
            pl.BlockSpec((1, 1, d_model, d_expert), lambda w, bl, ex, fl, st: (layer, ex[w], 0, 0)),
            pl.BlockSpec((1, 1, d_model, d_expert), lambda w, bl, ex, fl, st: (layer, ex[w], 0, 0)),
            pl.BlockSpec((1, 1, d_expert, d_model), lambda w, bl, ex, fl, st: (layer, ex[w], 0, 0)),
        ],
        out_specs=pl.BlockSpec(memory_space=pl.ANY),
        scratch_shapes=[pltpu.VMEM((2 * tb * CHUNKS, LANES), F32),
                        pltpu.VMEM((2 * tb * CHUNKS, LANES), F32),
                        pltpu.VMEM((tb, d_model), BF16),
                        pltpu.VMEM((d_model, d_expert), BF16), pltpu.VMEM((d_model, d_expert), BF16),
                        pltpu.VMEM((d_expert, d_model), BF16),
                        pltpu.SemaphoreType.DMA((2,)), pltpu.SemaphoreType.DMA((2,))],
    )
    return pl.pallas_call(
        functools.partial(_moe_kernel, tb=tb, nb=nb),
        grid_spec=grid_spec,
        out_shape=jax.ShapeDtypeStruct((na * CHUNKS, LANES), F32),
        compiler_params=pltpu.CompilerParams(
            dimension_semantics=("arbitrary",), vmem_limit_bytes=VMEM_LIMIT),
        name="moe_experts",
    )(blk, exp, flags, starts, tok3, tok3, dst3, dst3, x, nw, wg, wu, wd)


def _dispatch_plan(eid, n, tb=MOE_TILE, n_experts=N_EXPERTS):
    na = TOP_K * n
    nb = na // tb
    eid_flat = eid.reshape(na)
    _, s_a = lax.sort((eid_flat, jnp.arange(na, dtype=jnp.int32)), num_keys=1)
    s_tok = s_a % n
    counts = jnp.sum(eid_flat[None, :] == jnp.arange(n_experts, dtype=jnp.int32)[:, None], axis=1)
    starts = jnp.concatenate([jnp.zeros((1,), jnp.int32),
                              jnp.cumsum(counts).astype(jnp.int32)])
    lo, hi = starts[:-1], starts[1:]
    nonempty = hi > lo
    first_blk = lo // tb
    npass = jnp.where(nonempty, (hi - 1) // tb - first_blk + 1, 0)
    cum = jnp.cumsum(npass)
    total = cum[-1]
    n_pass = nb + n_experts
    w = jnp.arange(n_pass, dtype=jnp.int32)
    wc = jnp.minimum(w, total - 1)
    ex = jnp.sum(cum[None, :] <= wc[:, None], axis=1).astype(jnp.int32)
    sel = ex[:, None] == jnp.arange(n_experts, dtype=jnp.int32)[None, :]
    pick = lambda v: jnp.sum(jnp.where(sel, v[None, :], 0), axis=1)
    blk = (pick(first_blk) + (wc - pick(cum - npass))).astype(jnp.int32)
    valid = w < total
    prev_blk = jnp.concatenate([jnp.full((1,), -1, jnp.int32), blk[:-1]])
    next_blk = jnp.concatenate([blk[1:], jnp.full((1,), -1, jnp.int32)])
    first = valid & (blk != prev_blk)
    last = valid & ((blk != next_blk) | (w == total - 1))
    prev_ex = jnp.concatenate([jnp.full((1,), -1, jnp.int32), ex[:-1]])
    newexp = valid & (ex != prev_ex)
    flags = (first * _FIRST + last * _LAST + valid * _VALID + newexp * _NEWEXP).astype(jnp.int32)
    return s_tok, s_a, blk, ex, flags, starts


def _final_kernel(x_ref, y0_ref, y1_ref, rw_ref, nw_ref, o_ref):
    rw = rw_ref[...]
    tm = x_ref.shape[0]
    x = x_ref[...] + (rw[:, 0:1] * _tm_load(y0_ref, 0, tm) + rw[:, 1:2] * _tm_load(y1_ref, 0, tm))
    o_ref[...] = _rms(x, nw_ref[...])


def _final(x, y, rw, nw):
    n = x.shape[0]
    tm = ROW_TILE
    nt = n // tm
    row = pl.BlockSpec((tm, D_MODEL), lambda i: (i, 0))
    return pl.pallas_call(
        _final_kernel,
        grid=(nt,),
        in_specs=[row, pl.BlockSpec((tm * CHUNKS, LANES), lambda i: (i, 0)),
                  pl.BlockSpec((tm * CHUNKS, LANES), lambda i: (i + nt, 0)),
                  pl.BlockSpec((tm, LANES), lambda i: (i, 0)),
                  pl.BlockSpec((1, D_MODEL), lambda i: (0, 0))],
        out_specs=row,
        out_shape=jax.ShapeDtypeStruct((n, D_MODEL), F32),
        compiler_params=pltpu.CompilerParams(
            dimension_semantics=("parallel",), vmem_limit_bytes=VMEM_LIMIT),
        name="final_norm",
    )(x, y, y, rw, nw)


def kernel(x, norm_mix_w, w_in, fox_forget_b, w_out, norm_ffn_w, w_router_group, b_router_group,
           w_router_expert, b_router_expert, w_expert_gate, w_expert_up, w_expert_down,
           norm_final_w):
    batch, seq, d = x.shape
    n = batch * seq
    depth = w_in.shape[0]
    xf = x.reshape(n, d)
    tables = _ret_tables(seq)
    tri = jnp.tril(jnp.ones((ROW_TILE, ROW_TILE), F32)).astype(BF16)
    nq = seq // ATT_TILE

    y = rw = None
    for layer in range(depth):
        wl = w_in[layer]
        c0 = 3 * FOX_WIDTH
        w_main = jnp.concatenate([wl[:, :c0], wl[:, c0 + FOX_HEADS:]], axis=1).astype(BF16)
        w_ff = jnp.pad(wl[:, c0:c0 + FOX_HEADS], ((0, 0), (0, LANES - FOX_HEADS))).astype(BF16)
        b_ff = jnp.pad(fox_forget_b[layer], (0, LANES - FOX_HEADS)).reshape(1, LANES)
        xf, (fq, fk, fv, rq, rk, rv, rg, ct) = _inproj(
            xf, y, rw, norm_mix_w[layer].reshape(1, d), w_main, w_ff, b_ff, tri, seq)
        ct4 = ct.reshape(batch, 8, nq, ATT_TILE)
        fox = _fox_attention(fq, fk, fv, ct4, batch, seq)
        ret = _retention(rq, rk, rv, rg, tables, batch, seq)

        zpad = jnp.zeros((d, N_GROUPS), F32)
        w_r = jnp.concatenate([w_router_group[layer], zpad, w_router_expert[layer]], axis=1)
        nr = 2 * N_GROUPS + N_EXPERTS
        w_r = jnp.pad(w_r, ((0, 0), (0, LANES - nr))).astype(BF16)
        b_r = jnp.concatenate([b_router_group[layer], jnp.zeros((N_GROUPS,), F32),
                               b_router_expert[layer]])
        b_r = jnp.pad(b_r, (0, LANES - nr)).reshape(1, LANES)
        xf, xg, eid, rw = _outproj(fox, ret, xf, w_out[layer].astype(BF16),
                                   norm_ffn_w[layer].reshape(1, d), w_r, b_r)
        s_tok, s_dst, blk, ex, flags, starts = _dispatch_plan(eid, n)
        y = _moe(xg, s_tok, s_dst, blk, ex, flags, starts, norm_ffn_w[layer].reshape(1, d),
                 w_expert_gate, w_expert_up, w_expert_down, layer)
    out = _final(xf, y, rw, norm_final_w.reshape(1, d))
    return out.reshape(batch, seq, d)
```

```python
import functools

import jax
import jax.numpy as jnp
import numpy as np
from jax import lax
from jax.experimental import pallas as pl
from jax.experimental.pallas import tpu as pltpu

F32 = jnp.float32
BF16 = jnp.bfloat16

D_MODEL = 1024
FOX_HEADS = 8
FOX_HEAD_DIM = 64
FOX_WIDTH = 512
RET_HEADS = 4
RET_HEAD_DIM = 128
RET_WIDTH = 512
CHUNK = 64
ROPE_BASE = 10000.0
N_GROUPS = 4
EXPERTS_PER_GROUP = 8
N_EXPERTS = 32
TOP_K = 2
D_EXPERT = 512
RMS_EPS = 1e-6

LANES = 128
VMEM_LIMIT = 56 * 1024 * 1024

ROW_TILE = 512
ATT_TILE = 512
RET_TILE = 256
MOE_TILE = 256
N_MAIN = 7 * 512
EXP_UNDERFLOW = 110.0


def _rms(xf, w):
    return xf * lax.rsqrt(jnp.mean(xf * xf, axis=-1, keepdims=True) + RMS_EPS) * w


def _dot(a, b):
    return jnp.dot(a, b, preferred_element_type=F32)


def _dot_nt(a, b):
    return lax.dot_general(a, b, (((1,), (1,)), ((), ())), preferred_element_type=F32)


def _dot_tn(a, b):
    return lax.dot_general(a, b, (((0,), (0,)), ((), ())), preferred_element_type=F32)


CHUNKS = D_MODEL // LANES


def _tm_load(ref, base, rows):
    return jnp.concatenate([ref[pl.ds(base + c, rows, stride=CHUNKS), :] for c in range(CHUNKS)],
                           axis=1)


def _tm_store(ref, base, rows, val):
    for c in range(CHUNKS):
        ref[pl.ds(base + c, rows, stride=CHUNKS), :] = val[:, c * LANES:(c + 1) * LANES]


def _inproj_kernel(*refs, has_y, tiles_per_seq):
    if has_y:
        x_ref, y0_ref, y1_ref, rw_ref = refs[:4]
        refs = refs[4:]
    else:
        x_ref = refs[0]
        refs = refs[1:]
    nw_ref, w_ref, wff_ref, bff_ref, tri_ref = refs[:5]
    refs = refs[5:]
    if has_y:
        xres_ref = refs[0]
        refs = refs[1:]
    fq_ref, fk_ref, fv_ref, rq_ref, rk_ref, rv_ref, rg_ref, ct_ref, carry_sc = refs

    i = pl.program_id(0)
    x = x_ref[...]
    if has_y:
        rw = rw_ref[...]
        tm = x.shape[0]
        x = x + (rw[:, 0:1] * _tm_load(y0_ref, 0, tm) + rw[:, 1:2] * _tm_load(y1_ref, 0, tm))
        xres_ref[...] = x
    h = _rms(x, nw_ref[...]).astype(BF16)

    outs = (fq_ref, fk_ref, fv_ref, rq_ref, rk_ref, rv_ref, rg_ref)
    for j, o_ref in enumerate(outs):
        acc = _dot(h, w_ref[:, j * 512:(j + 1) * 512])
        if j == 0:
            acc = acc * (FOX_HEAD_DIM ** -0.5)
        o_ref[...] = acc.astype(BF16)

    z = _dot(h, wff_ref[...]) + bff_ref[...]
    lf = jnp.minimum(z, 0.0) - jnp.log1p(jnp.exp(-jnp.abs(z)))
    lane = lax.broadcasted_iota(jnp.int32, lf.shape, 1)
    lf = jnp.where(lane < FOX_HEADS, lf, 0.0)
    hi = lf.astype(BF16)
    r1 = lf - hi.astype(F32)
    mid = r1.astype(BF16)
    lo = (r1 - mid.astype(F32)).astype(BF16)
    tri = tri_ref[...]
    cs = _dot(tri, hi) + _dot(tri, mid) + _dot(tri, lo)

    @pl.when(i % tiles_per_seq == 0)
    def _():
        carry_sc[...] = jnp.zeros_like(carry_sc)

    c = cs + carry_sc[0:1, :]
    carry_sc[...] = jnp.broadcast_to(c[-1:, :], carry_sc.shape)
    ct_ref[0] = c.T[:8, :]


def _inproj(x, y, rw, nw, w_main, w_ff, b_ff, tri, seq):
    n = x.shape[0]
    tm = ROW_TILE
    nt = n // tm
    tps = seq // tm
    has_y = y is not None
    row_spec = pl.BlockSpec((tm, D_MODEL), lambda i: (i, 0))
    in_specs = [row_spec]
    args = [x]
    if has_y:
        in_specs += [pl.BlockSpec((tm * CHUNKS, LANES), lambda i: (i, 0)),
                     pl.BlockSpec((tm * CHUNKS, LANES), lambda i: (i + nt, 0)),
                     pl.BlockSpec((tm, LANES), lambda i: (i, 0))]
        args += [y, y, rw]
    in_specs += [
        pl.BlockSpec((1, D_MODEL), lambda i: (0, 0)),
        pl.BlockSpec((D_MODEL, N_MAIN), lambda i: (0, 0)),
        pl.BlockSpec((D_MODEL, LANES), lambda i: (0, 0)),
        pl.BlockSpec((1, LANES), lambda i: (0, 0)),
        pl.BlockSpec((tm, tm), lambda i: (0, 0)),
    ]
    args += [nw, w_main, w_ff, b_ff, tri]
    half_spec = pl.BlockSpec((tm, 512), lambda i: (i, 0))
    out_shape = []
    out_specs = []
    if has_y:
        out_shape.append(jax.ShapeDtypeStruct((n, D_MODEL), F32))
        out_specs.append(row_spec)
    out_shape += [jax.ShapeDtypeStruct((n, 512), BF16)] * 7
    out_specs += [half_spec] * 7
    out_shape.append(jax.ShapeDtypeStruct((n // seq, 8, seq), F32))
    out_specs.append(pl.BlockSpec((1, 8, tm), lambda i: (i // tps, 0, i % tps)))
    outs = pl.pallas_call(
        functools.partial(_inproj_kernel, has_y=has_y, tiles_per_seq=tps),
        grid=(nt,),
        in_specs=in_specs,
        out_specs=out_specs,
        out_shape=out_shape,
        scratch_shapes=[pltpu.VMEM((8, LANES), F32)],
        compiler_params=pltpu.CompilerParams(
            dimension_semantics=("arbitrary",), vmem_limit_bytes=VMEM_LIMIT),
        name="inproj_y" if has_y else "inproj",
    )(*args)
    if has_y:
        return outs[0], outs[1:]
    return x, outs


def _fox_kernel(q_ref, k_ref, v_ref, ct_ref, o_ref, m_sc, acc_sc, kmax_sc, *, tile, nq):
    hp = pl.program_id(1)
    qi = pl.program_id(2)
    q2 = q_ref[0]
    lane = lax.broadcasted_iota(jnp.int32, q2.shape, 1)
    first = lane < FOX_HEAD_DIM
    zero = jnp.zeros_like(q2)
    qh = (jnp.where(first, q2, zero), jnp.where(first, zero, q2))
    reps = tile // LANES

    def head_sqnorm_max(xf, h):
        sq = xf * xf
        sq = jnp.where(first, sq, 0.0) if h == 0 else jnp.where(first, 0.0, sq)
        return jnp.max(jnp.sum(sq, axis=1, keepdims=True), axis=0, keepdims=True)

    @pl.when(qi == 0)
    def _():
        for j in range(nq):
            kf = k_ref[0, j * tile:(j + 1) * tile, :].astype(F32)
            for h in range(2):
                kmax_sc[h, j:j + 1, :] = jnp.broadcast_to(head_sqnorm_max(kf, h), (1, LANES))

    def head_step(h, kb, k_blk, v_blk, mask, m_old, acc_old):
        one = jnp.ones_like(v_blk)
        va = jnp.where(first, v_blk, one) if h == 0 else jnp.where(first, one, v_blk)
        s = _dot_nt(qh[h], k_blk) - ct_ref[0, 2 * hp + h, pl.ds(kb, 1), :]
        if mask is not None:
            s = jnp.where(mask, s, -jnp.inf)
        m_cur = jnp.max(s, axis=1, keepdims=True)
        if m_old is None:
            m_new = jnp.broadcast_to(m_cur, (tile, LANES))
            p = jnp.exp(s - jnp.concatenate([m_new] * reps, axis=1))
            acc = _dot(p.astype(BF16), va)
        else:
            m_new = jnp.maximum(m_old, m_cur)
            alpha = jnp.exp(m_old - m_new)
            p = jnp.exp(s - jnp.concatenate([m_new] * reps, axis=1))
            acc = alpha * acc_old + _dot(p.astype(BF16), va)
        return m_new, acc

    row = lax.broadcasted_iota(jnp.int32, (tile, tile), 0)
    col = lax.broadcasted_iota(jnp.int32, (tile, tile), 1)
    start = pl.multiple_of(qi * tile, tile)
    k_blk = k_ref[0, pl.ds(start, tile), :]
    v_blk = v_ref[0, pl.ds(start, tile), :]
    for h in range(2):
        m_new, acc = head_step(h, qi, k_blk, v_blk, col <= row, None, None)
        m_sc[h] = m_new
        acc_sc[h] = acc

    def off_diagonal(kbs):
        state = [(m_sc[h], acc_sc[h]) for h in range(2)]
        for kb in kbs:
            start = pl.multiple_of(kb * tile, tile)
            k_blk = k_ref[0, pl.ds(start, tile), :]
            v_blk = v_ref[0, pl.ds(start, tile), :]
            state = [head_step(h, kb, k_blk, v_blk, None, *state[h]) for h in range(2)]
        for h in range(2):
            m_sc[h] = state[h][0]
            acc_sc[h] = state[h][1]

    jrow = lax.broadcasted_iota(jnp.int32, (nq, LANES), 0)
    qf = q2.astype(F32)
    jmin = qi
    for h in range(2):
        m_low = jnp.min(jnp.min(m_sc[h], axis=1, keepdims=True), axis=0, keepdims=True)
        c_last = ct_ref[0, 2 * hp + h, :, tile - 1:tile]
        bound = jnp.sqrt(head_sqnorm_max(qf, h) * kmax_sc[h]) - c_last
        need = (bound - m_low > -EXP_UNDERFLOW) & (jrow < qi)
        j_first = jnp.min(jnp.min(jnp.where(need, jrow, qi), axis=1, keepdims=True),
                          axis=0, keepdims=True)
        jmin = jnp.minimum(jmin, j_first[0, 0])
    count = qi - jmin

    def pair(j, carry):
        off_diagonal((jmin + 2 * j, jmin + 2 * j + 1))
        return carry

    lax.fori_loop(0, count // 2, pair, 0)

    @pl.when(count % 2 == 1)
    def _():
        off_diagonal((qi - 1,))

    a0 = acc_sc[0]
    a1 = acc_sc[1]
    half = FOX_HEAD_DIM
    o = jnp.where(first, a0 / pltpu.roll(a0, half, 1), a1 / pltpu.roll(a1, half, 1))
    o_ref[0] = o.astype(o_ref.dtype)


def _fox_attention(fq, fk, fv, ct4, batch, seq):
    t = ATT_TILE
    nq = seq // t
    q3 = fq.reshape(batch, seq, FOX_WIDTH)
    k3 = fk.reshape(batch, seq, FOX_WIDTH)
    v3 = fv.reshape(batch, seq, FOX_WIDTH)
    out = pl.pallas_call(
        functools.partial(_fox_kernel, tile=t, nq=nq),
        grid=(batch, FOX_HEADS // 2, nq),
        in_specs=[
            pl.BlockSpec((1, t, LANES), lambda b, j, i: (b, i, j)),
            pl.BlockSpec((1, seq, LANES), lambda b, j, i: (b, 0, j)),
            pl.BlockSpec((1, seq, LANES), lambda b, j, i: (b, 0, j)),
            pl.BlockSpec((1, 8, nq, t), lambda b, j, i: (b, 0, 0, 0)),
        ],
        out_specs=pl.BlockSpec((1, t, LANES), lambda b, j, i: (b, i, j)),
        out_shape=jax.ShapeDtypeStruct((batch, seq, FOX_WIDTH), BF16),
        scratch_shapes=[pltpu.VMEM((2, t, LANES), F32), pltpu.VMEM((2, t, LANES), F32),
                        pltpu.VMEM((2, nq, LANES), F32)],
        compiler_params=pltpu.CompilerParams(
            dimension_semantics=("parallel", "parallel", "arbitrary"),
            vmem_limit_bytes=VMEM_LIMIT),
        name="fox_attention",
    )(q3, k3, v3, ct4)
    return out.reshape(batch * seq, FOX_WIDTH)


def _ret_kernel(q_ref, k_ref, v_ref, g_ref, cos_ref, sin_ref, dmat_ref, qdec_ref, kdec_ref,
                sdec_ref, o_ref, state_sc):
    si = pl.program_id(1)

    @pl.when(si == 0)
    def _():
        state_sc[...] = jnp.zeros_like(state_sc)

    cos2 = cos_ref[...]
    sin2 = sin_ref[...]
    dk = RET_HEAD_DIM

    def rot(xf):
        return xf * cos2 + pltpu.roll(xf, dk // 2, 1) * sin2

    for h in range(RET_HEADS):
        cols = slice(h * dk, (h + 1) * dk)
        q = rot(q_ref[0, :, cols].astype(F32))
        k = rot(k_ref[0, :, cols].astype(F32)) * (dk ** -0.5)
        v = v_ref[0, :, cols]
        scores = _dot_nt(q.astype(BF16), k.astype(BF16)) * dmat_ref[h]
        intra = _dot(scores.astype(BF16), v)
        state = state_sc[h]
        cross = _dot((q * qdec_ref[h]).astype(BF16), state.astype(BF16))
        out = intra + cross
        state_sc[h] = state * sdec_ref[h, 0:1, :] + _dot_tn((k * kdec_ref[h]).astype(BF16), v)

        y = out * lax.rsqrt(jnp.mean(out * out, axis=-1, keepdims=True) + RMS_EPS)
        g = g_ref[0, :, cols].astype(F32)
        o_ref[0, :, cols] = (y * (g * jax.nn.sigmoid(g))).astype(o_ref.dtype)


def _ret_tables(seq):
    half = RET_HEAD_DIM // 2
    inv_freq = 1.0 / (ROPE_BASE ** (jnp.arange(half, dtype=F32) / half))
    ang = jnp.arange(seq, dtype=F32)[:, None] * inv_freq[None, :]
    cos = jnp.cos(ang)
    sin = jnp.sin(ang)
    cos2 = jnp.concatenate([cos, cos], axis=1)
    sin2 = jnp.concatenate([-sin, sin], axis=1)
    lt = RET_TILE
    log_gamma = jnp.log(1.0 - 2.0 ** (-5.0 - jnp.arange(RET_HEADS, dtype=F32)))
    idx = jnp.arange(lt)
    t = idx[:, None]
    s = idx[None, :]
    same = (t // CHUNK) == (s // CHUNK)
    earlier = (s // CHUNK) < (t // CHUNK)
    dist = jnp.where(same, jnp.abs(t - s), t - s).astype(F32)
    dmat = jnp.where((same | earlier)[None], jnp.exp(log_gamma[:, None, None] * dist[None]), 0.0)
    idxf = idx.astype(F32)
    qdec = jnp.exp(log_gamma[:, None] * idxf[None, :])
    kdec = jnp.exp(log_gamma[:, None] * (lt - idxf)[None, :])
    sdec = jnp.exp(log_gamma * lt)
    qdec = jnp.broadcast_to(qdec[:, :, None], (RET_HEADS, lt, LANES))
    kdec = jnp.broadcast_to(kdec[:, :, None], (RET_HEADS, lt, LANES))
    sdec = jnp.broadcast_to(sdec[:, None, None], (RET_HEADS, 8, LANES))
    return cos2, sin2, dmat, qdec, kdec, sdec


def _retention(rq, rk, rv, rg, tables, batch, seq):
    lt = RET_TILE
    ns = seq // lt
    cos2, sin2, dmat, qdec, kdec, sdec = tables
    blk = pl.BlockSpec((1, lt, RET_WIDTH), lambda b, i: (b, i, 0))
    tab = pl.BlockSpec((lt, LANES), lambda b, i: (i, 0))
    args = [a.reshape(batch, seq, RET_WIDTH) for a in (rq, rk, rv, rg)]
    out = pl.pallas_call(
        _ret_kernel,
        grid=(batch, ns),
        in_specs=[blk, blk, blk, blk, tab, tab,
                  pl.BlockSpec((RET_HEADS, lt, lt), lambda b, i: (0, 0, 0)),
                  pl.BlockSpec((RET_HEADS, lt, LANES), lambda b, i: (0, 0, 0)),
                  pl.BlockSpec((RET_HEADS, lt, LANES), lambda b, i: (0, 0, 0)),
                  pl.BlockSpec((RET_HEADS, 8, LANES), lambda b, i: (0, 0, 0))],
        out_specs=blk,
        out_shape=jax.ShapeDtypeStruct((batch, seq, RET_WIDTH), BF16),
        scratch_shapes=[pltpu.VMEM((RET_HEADS, RET_HEAD_DIM, RET_HEAD_DIM), F32)],
        compiler_params=pltpu.CompilerParams(
            dimension_semantics=("parallel", "arbitrary"),
            vmem_limit_bytes=VMEM_LIMIT),
        name="retention",
    )(*args, cos2, sin2, dmat, qdec, kdec, sdec)
    return out.reshape(batch * seq, RET_WIDTH)


def _outproj_kernel(fox_ref, ret_ref, x_ref, wo_ref, nw_ref, wr_ref, br_ref, xo_ref, xg_ref,
                    eid_ref, rw_ref):
    mixed = jnp.concatenate([fox_ref[...], ret_ref[...]], axis=1)
    x = x_ref[...] + _dot(mixed, wo_ref[...])
    xo_ref[...] = x
    _tm_store(xg_ref, 0, x.shape[0], x)
    h = _rms(x, nw_ref[...]).astype(BF16)
    lt = (_dot(h, wr_ref[...]) + br_ref[...]).T
    tm = lt.shape[1]
    rowid = lax.broadcasted_iota(jnp.int32, (8, tm), 0)
    neg = -jnp.inf

    def top1(v):
        vmax = jnp.max(v, axis=0, keepdims=True)
        idx = jnp.min(jnp.where(v == vmax, rowid, 8), axis=0, keepdims=True)
        return vmax, idx

    gl = jnp.where(rowid < N_GROUPS, lt[0:8], neg)
    gmax, gidx = top1(gl)
    g_w = 1.0 / jnp.sum(jnp.exp(gl - gmax), axis=0, keepdims=True)
    e_in = jnp.zeros((8, tm), F32)
    for g in range(N_GROUPS):
        e_in = jnp.where(gidx == g, lt[8 + 8 * g:16 + 8 * g], e_in)
    v1, i1 = top1(e_in)
    rest = jnp.where(rowid == i1, neg, e_in)
    v2, i2 = top1(rest)
    t = jnp.exp(v2 - v1)
    w1 = g_w / (1.0 + t)
    eid_ref[0:1, :] = gidx * EXPERTS_PER_GROUP + i1
    eid_ref[1:2, :] = gidx * EXPERTS_PER_GROUP + i2
    wslab = jnp.concatenate([w1, w1 * t, jnp.zeros((LANES - TOP_K, tm), F32)], axis=0)
    rw_ref[...] = wslab.T


def _outproj(fox, ret, x, wo, nw, wr, br):
    n = x.shape[0]
    tm = ROW_TILE
    row = pl.BlockSpec((tm, D_MODEL), lambda i: (i, 0))
    half = pl.BlockSpec((tm, 512), lambda i: (i, 0))
    pair = pl.BlockSpec((TOP_K, tm), lambda i: (0, i))
    wts = pl.BlockSpec((tm, LANES), lambda i: (i, 0))
    return pl.pallas_call(
        _outproj_kernel,
        grid=(n // tm,),
        in_specs=[half, half, row,
                  pl.BlockSpec((D_MODEL, D_MODEL), lambda i: (0, 0)),
                  pl.BlockSpec((1, D_MODEL), lambda i: (0, 0)),
                  pl.BlockSpec((D_MODEL, LANES), lambda i: (0, 0)),
                  pl.BlockSpec((1, LANES), lambda i: (0, 0))],
        out_specs=[row, pl.BlockSpec((tm * CHUNKS, LANES), lambda i: (i, 0)), pair, wts],
        out_shape=[jax.ShapeDtypeStruct((n, D_MODEL), F32),
                   jax.ShapeDtypeStruct((n * CHUNKS, LANES), F32),
                   jax.ShapeDtypeStruct((TOP_K, n), jnp.int32),
                   jax.ShapeDtypeStruct((n, LANES), F32)],
        compiler_params=pltpu.CompilerParams(
            dimension_semantics=("parallel",), vmem_limit_bytes=VMEM_LIMIT),
        name="outproj",
    )(fox, ret, x, wo, nw, wr, br)


_FIRST, _LAST, _VALID, _NEWEXP = 1, 2, 4, 8


def _moe_kernel(blk_ref, exp_ref, flag_ref, starts_ref,
                tokc_ref, tokn_ref, dstp_ref, dstc_ref, x_hbm, nw_ref, wg_ref, wu_ref, wd_ref,
                y_hbm, xbuf, ybuf, hbuf, wgb, wub, wdb, gsem, ssem, *, tb, nb):
    w = pl.program_id(0)
    b = blk_ref[w]
    e = exp_ref[w]
    flags = flag_ref[w]
    slot = b % 2
    nslot = 1 - slot
    span = tb * CHUNKS

    def hbm_row(ref, idx):
        return ref.at[pl.ds(pl.multiple_of(idx * CHUNKS, CHUNKS), CHUNKS), :]

    def buf_row(buf, s, r):
        return buf.at[pl.ds(pl.multiple_of(s * span + r * CHUNKS, CHUNKS), CHUNKS), :]

    def start_gather(tok_ref, s):
        for r in range(tb):
            pltpu.make_async_copy(hbm_row(x_hbm, tok_ref[0, 0, r]), buf_row(xbuf, s, r),
                                  gsem.at[s]).start()

    def start_scatter(dst_ref, s):
        for r in range(tb):
            pltpu.make_async_copy(buf_row(ybuf, s, r), hbm_row(y_hbm, dst_ref[0, 0, r]),
                                  ssem.at[s]).start()

    def wait_rows(sem, s):
        whole = pl.ds(pl.multiple_of(s * span, span), span)
        pltpu.make_async_copy(xbuf.at[whole, :], ybuf.at[whole, :], sem.at[s]).wait()

    @pl.when((flags & _FIRST) != 0)
    def _():
        @pl.when(w == 0)
        def _():
            start_gather(tokc_ref, 0)

        wait_rows(gsem, slot)

        @pl.when(b >= 2)
        def _():
            wait_rows(ssem, slot)

        @pl.when(b + 1 < nb)
        def _():
            start_gather(tokn_ref, nslot)

        @pl.when(b >= 1)
        def _():
            start_scatter(dstp_ref, nslot)

        hbuf[...] = _rms(_tm_load(xbuf, slot * span, tb), nw_ref[...]).astype(BF16)
        ybuf[pl.ds(pl.multiple_of(slot * span, span), span), :] = jnp.zeros((span, LANES), F32)

    @pl.when((flags & _NEWEXP) != 0)
    def _():
        wgb[...] = wg_ref[0, 0].astype(BF16)
        wub[...] = wu_ref[0, 0].astype(BF16)
        wdb[...] = wd_ref[0, 0].astype(BF16)

    @pl.when((flags & _VALID) != 0)
    def _():
        h = hbuf[...]
        g = _dot(h, wgb[...])
        u = _dot(h, wub[...])
        a = (g * jax.nn.sigmoid(g) * u).astype(BF16)
        y = _dot(a, wdb[...])
        q = b * tb + lax.broadcasted_iota(jnp.int32, y.shape, 0)
        mine = (q >= starts_ref[e]) & (q < starts_ref[e + 1])
        _tm_store(ybuf, slot * span, tb, jnp.where(mine, y, _tm_load(ybuf, slot * span, tb)))

    @pl.when(((flags & _LAST) != 0) & (b == nb - 1))
    def _():
        start_scatter(dstc_ref, slot)
        if nb >= 2:
            wait_rows(ssem, nslot)
        wait_rows(ssem, slot)


def _moe(x, s_tok, s_dst, blk, exp, flags, starts, nw, wg, wu, wd, layer, tb=MOE_TILE):
    na = s_tok.shape[0]
    nb = na // tb
    d_model, d_expert = wg.shape[2], wg.shape[3]
    tok3 = s_tok.reshape(nb, 1, tb)
    dst3 = s_dst.reshape(nb, 1, tb)
    smem_blk = lambda f: pl.BlockSpec((1, 1, tb), f, memory_space=pltpu.SMEM)
    grid_spec = pltpu.PrefetchScalarGridSpec(
        num_scalar_prefetch=4,
        grid=(blk.shape[0],),
        in_specs=[
            smem_blk(lambda w, bl, ex, fl, st: (bl[w], 0, 0)),
            smem_blk(lambda w, bl, ex, fl, st: (jnp.minimum(bl[w] + 1, nb - 1), 0, 0)),
            smem_blk(lambda w, bl, ex, fl, st: (jnp.maximum(bl[w] - 1, 0), 0, 0)),
            smem_blk(lambda w, bl, ex, fl, st: (bl[w], 0, 0)),
            pl.BlockSpec(memory_space=pl.ANY),
            pl.BlockSpec((1, d_model), lambda w, bl, ex, fl, st: (0, 0)),
            pl.BlockSpec((1, 1, d_model, d_expert), lambda w, bl, ex, fl, st: (layer, ex[w], 0, 0)),
            pl.BlockSpec((1, 1, d_model, d_expert), lambda w, bl, ex, fl, st: (layer, ex[w], 0, 0)),
            pl.BlockSpec((1, 1, d_expert, d_model), lambda w, bl, ex, fl, st: (layer, ex[w], 0, 0)),
        ],
        out_specs=pl.BlockSpec(memory_space=pl.ANY),
        scratch_shapes=[pltpu.VMEM((2 * tb * CHUNKS, LANES), F32),
                        pltpu.VMEM((2 * tb * CHUNKS, LANES), F32),
                        pltpu.VMEM((tb, d_model), BF16),
                        pltpu.VMEM((d_model, d_expert), BF16), pltpu.VMEM((d_model, d_expert), BF16),
                        pltpu.VMEM((d_expert, d_model), BF16),
                        pltpu.SemaphoreType.DMA((2,)), pltpu.SemaphoreType.DMA((2,))],
    )
    return pl.pallas_call(
        functools.partial(_moe_kernel, tb=tb, nb=nb),
        grid_spec=grid_spec,
        out_shape=jax.ShapeDtypeStruct((na * CHUNKS, LANES), F32),
        compiler_params=pltpu.CompilerParams(
            dimension_semantics=("arbitrary",), vmem_limit_bytes=VMEM_LIMIT),
        name="moe_experts",
    )(blk, exp, flags, starts, tok3, tok3, dst3, dst3, x, nw, wg, wu, wd)


def _dispatch_plan(eid, n, tb=MOE_TILE, n_experts=N_EXPERTS):
    na = TOP_K * n
    nb = na // tb
    eid_flat = eid.reshape(na)
    _, s_a = lax.sort((eid_flat, jnp.arange(na, dtype=jnp.int32)), num_keys=1)
    s_tok = s_a % n
    counts = jnp.sum(eid_flat[None, :] == jnp.arange(n_experts, dtype=jnp.int32)[:, None], axis=1)
    starts = jnp.concatenate([jnp.zeros((1,), jnp.int32),
                              jnp.cumsum(counts).astype(jnp.int32)])
    lo, hi = starts[:-1], starts[1:]
    nonempty = hi > lo
    first_blk = lo // tb
    npass = jnp.where(nonempty, (hi - 1) // tb - first_blk + 1, 0)
    cum = jnp.cumsum(npass)
    total = cum[-1]
    n_pass = nb + n_experts
    w = jnp.arange(n_pass, dtype=jnp.int32)
    wc = jnp.minimum(w, total - 1)
    ex = jnp.sum(cum[None, :] <= wc[:, None], axis=1).astype(jnp.int32)
    sel = ex[:, None] == jnp.arange(n_experts, dtype=jnp.int32)[None, :]
    pick = lambda v: jnp.sum(jnp.where(sel, v[None, :], 0), axis=1)
    blk = (pick(first_blk) + (wc - pick(cum - npass))).astype(jnp.int32)
    valid = w < total
    prev_blk = jnp.concatenate([jnp.full((1,), -1, jnp.int32), blk[:-1]])
    next_blk = jnp.concatenate([blk[1:], jnp.full((1,), -1, jnp.int32)])
    first = valid & (blk != prev_blk)
    last = valid & ((blk != next_blk) | (w == total - 1))
    prev_ex = jnp.concatenate([jnp.full((1,), -1, jnp.int32), ex[:-1]])
    newexp = valid & (ex != prev_ex)
    flags = (first * _FIRST + last * _LAST + valid * _VALID + newexp * _NEWEXP).astype(jnp.int32)
    return s_tok, s_a, blk, ex, flags, starts


def _final_kernel(x_ref, y0_ref, y1_ref, rw_ref, nw_ref, o_ref):
    rw = rw_ref[...]
    tm = x_ref.shape[0]
    x = x_ref[...] + (rw[:, 0:1] * _tm_load(y0_ref, 0, tm) + rw[:, 1:2] * _tm_load(y1_ref, 0, tm))
    o_ref[...] = _rms(x, nw_ref[...])


def _final(x, y, rw, nw):
    n = x.shape[0]
    tm = ROW_TILE
    nt = n // tm
    row = pl.BlockSpec((tm, D_MODEL), lambda i: (i, 0))
    return pl.pallas_call(
        _final_kernel,
        grid=(nt,),
        in_specs=[row, pl.BlockSpec((tm * CHUNKS, LANES), lambda i: (i, 0)),
                  pl.BlockSpec((tm * CHUNKS, LANES), lambda i: (i + nt, 0)),
                  pl.BlockSpec((tm, LANES), lambda i: (i, 0)),
                  pl.BlockSpec((1, D_MODEL), lambda i: (0, 0))],
        out_specs=row,
        out_shape=jax.ShapeDtypeStruct((n, D_MODEL), F32),
        compiler_params=pltpu.CompilerParams(
            dimension_semantics=("parallel",), vmem_limit_bytes=VMEM_LIMIT),
        name="final_norm",
    )(x, y, y, rw, nw)


def kernel(x, norm_mix_w, w_in, fox_forget_b, w_out, norm_ffn_w, w_router_group, b_router_group,
           w_router_expert, b_router_expert, w_expert_gate, w_expert_up, w_expert_down,
           norm_final_w):
    batch, seq, d = x.shape
    n = batch * seq
    depth = w_in.shape[0]
    xf = x.reshape(n, d)
    tables = _ret_tables(seq)
    tri = jnp.tril(jnp.ones((ROW_TILE, ROW_TILE), F32)).astype(BF16)
    nq = seq // ATT_TILE

    y = rw = None
    for layer in range(depth):
        wl = w_in[layer]
        c0 = 3 * FOX_WIDTH
        w_main = jnp.concatenate([wl[:, :c0], wl[:, c0 + FOX_HEADS:]], axis=1).astype(BF16)
        w_ff = jnp.pad(wl[:, c0:c0 + FOX_HEADS], ((0, 0), (0, LANES - FOX_HEADS))).astype(BF16)
        b_ff = jnp.pad(fox_forget_b[layer], (0, LANES - FOX_HEADS)).reshape(1, LANES)
        xf, (fq, fk, fv, rq, rk, rv, rg, ct) = _inproj(
            xf, y, rw, norm_mix_w[layer].reshape(1, d), w_main, w_ff, b_ff, tri, seq)
        ct4 = ct.reshape(batch, 8, nq, ATT_TILE)
        fox = _fox_attention(fq, fk, fv, ct4, batch, seq)
        ret = _retention(rq, rk, rv, rg, tables, batch, seq)

        zpad = jnp.zeros((d, N_GROUPS), F32)
        w_r = jnp.concatenate([w_router_group[layer], zpad, w_router_expert[layer]], axis=1)
        nr = 2 * N_GROUPS + N_EXPERTS
        w_r = jnp.pad(w_r, ((0, 0), (0, LANES - nr))).astype(BF16)
        b_r = jnp.concatenate([b_router_group[layer], jnp.zeros((N_GROUPS,), F32),
                               b_router_expert[layer]])
        b_r = jnp.pad(b_r, (0, LANES - nr)).reshape(1, LANES)
        xf, xg, eid, rw = _outproj(fox, ret, xf, w_out[layer].astype(BF16),
                                   norm_ffn_w[layer].reshape(1, d), w_r, b_r)
        s_tok, s_dst, blk, ex, flags, starts = _dispatch_plan(eid, n)
        y = _moe(xg, s_tok, s_dst, blk, ex, flags, starts, norm_ffn_w[layer].reshape(1, d),
                 w_expert_gate, w_expert_up, w_expert_down, layer)
    out = _final(xf, y, rw, norm_final_w.reshape(1, d))
    return out.reshape(batch, seq, d)
```

```python
import functools

import jax
import jax.numpy as jnp
import numpy as np
from jax import lax
from jax.experimental import pallas as pl
from jax.experimental.pallas import tpu as pltpu

F32 = jnp.float32
BF16 = jnp.bfloat16

D_MODEL = 1024
FOX_HEADS = 8
FOX_HEAD_DIM = 64
FOX_WIDTH = 512
RET_HEADS = 4
RET_HEAD_DIM = 128
RET_WIDTH = 512
CHUNK = 64
ROPE_BASE = 10000.0
N_GROUPS = 4
EXPERTS_PER_GROUP = 8
N_EXPERTS = 32
TOP_K = 2
D_EXPERT = 512
RMS_EPS = 1e-6

LANES = 128
VMEM_LIMIT = 56 * 1024 * 1024

ROW_TILE = 512
ATT_TILE = 512
RET_TILE = 256
MOE_TILE = 256
N_MAIN = 7 * 512
EXP_UNDERFLOW = 110.0


def _rms(xf, w):
    return xf * lax.rsqrt(jnp.mean(xf * xf, axis=-1, keepdims=True) + RMS_EPS) * w


def _dot(a, b):
    return jnp.dot(a, b, preferred_element_type=F32)


def _dot_nt(a, b):
    return lax.dot_general(a, b, (((1,), (1,)), ((), ())), preferred_element_type=F32)


def _dot_tn(a, b):
    return lax.dot_general(a, b, (((0,), (0,)), ((), ())), preferred_element_type=F32)


CHUNKS = D_MODEL // LANES


def _tm_load(ref, base, rows):
    return jnp.concatenate([ref[pl.ds(base + c, rows, stride=CHUNKS), :] for c in range(CHUNKS)],
                           axis=1)


def _tm_store(ref, base, rows, val):
    for c in range(CHUNKS):
        ref[pl.ds(base + c, rows, stride=CHUNKS), :] = val[:, c * LANES:(c + 1) * LANES]


def _inproj_kernel(*refs, has_y, tiles_per_seq):
    if has_y:
        x_ref, y0_ref, y1_ref, rw_ref = refs[:4]
        refs = refs[4:]
    else:
        x_ref = refs[0]
        refs = refs[1:]
    nw_ref, w_ref, wff_ref, bff_ref, tri_ref = refs[:5]
    refs = refs[5:]
    if has_y:
        xres_ref = refs[0]
        refs = refs[1:]
    fq_ref, fk_ref, fv_ref, rq_ref, rk_ref, rv_ref, rg_ref, ct_ref, carry_sc = refs

    i = pl.program_id(0)
    x = x_ref[...]
    if has_y:
        rw = rw_ref[...]
        tm = x.shape[0]
        x = x + (rw[:, 0:1] * _tm_load(y0_ref, 0, tm) + rw[:, 1:2] * _tm_load(y1_ref, 0, tm))
        xres_ref[...] = x
    h = _rms(x, nw_ref[...]).astype(BF16)

    outs = (fq_ref, fk_ref, fv_ref, rq_ref, rk_ref, rv_ref, rg_ref)
    for j, o_ref in enumerate(outs):
        acc = _dot(h, w_ref[:, j * 512:(j + 1) * 512])
        if j == 0:
            acc = acc * (FOX_HEAD_DIM ** -0.5)
        o_ref[...] = acc.astype(BF16)

    z = _dot(h, wff_ref[...]) + bff_ref[...]
    lf = jnp.minimum(z, 0.0) - jnp.log1p(jnp.exp(-jnp.abs(z)))
    lane = lax.broadcasted_iota(jnp.int32, lf.shape, 1)
    lf = jnp.where(lane < FOX_HEADS, lf, 0.0)
    hi = lf.astype(BF16)
    r1 = lf - hi.astype(F32)
    mid = r1.astype(BF16)
    lo = (r1 - mid.astype(F32)).astype(BF16)
    tri = tri_ref[...]
    cs = _dot(tri, hi) + _dot(tri, mid) + _dot(tri, lo)

    @pl.when(i % tiles_per_seq == 0)
    def _():
        carry_sc[...] = jnp.zeros_like(carry_sc)

    c = cs + carry_sc[0:1, :]
    carry_sc[...] = jnp.broadcast_to(c[-1:, :], carry_sc.shape)
    ct_ref[0] = c.T[:8, :]


def _inproj(x, y, rw, nw, w_main, w_ff, b_ff, tri, seq):
    n = x.shape[0]
    tm = ROW_TILE
    nt = n // tm
    tps = seq // tm
    has_y = y is not None
    row_spec = pl.BlockSpec((tm, D_MODEL), lambda i: (i, 0))
    in_specs = [row_spec]
    args = [x]
    if has_y:
        in_specs += [pl.BlockSpec((tm * CHUNKS, LANES), lambda i: (i, 0)),
                     pl.BlockSpec((tm * CHUNKS, LANES), lambda i: (i + nt, 0)),
                     pl.BlockSpec((tm, LANES), lambda i: (i, 0))]
        args += [y, y, rw]
    in_specs += [
        pl.BlockSpec((1, D_MODEL), lambda i: (0, 0)),
        pl.BlockSpec((D_MODEL, N_MAIN), lambda i: (0, 0)),
        pl.BlockSpec((D_MODEL, LANES), lambda i: (0, 0)),
        pl.BlockSpec((1, LANES), lambda i: (0, 0)),
        pl.BlockSpec((tm, tm), lambda i: (0, 0)),
    ]
    args += [nw, w_main, w_ff, b_ff, tri]
    half_spec = pl.BlockSpec((tm, 512), lambda i: (i, 0))
    out_shape = []
    out_specs = []
    if has_y:
        out_shape.append(jax.ShapeDtypeStruct((n, D_MODEL), F32))
        out_specs.append(row_spec)
    out_shape += [jax.ShapeDtypeStruct((n, 512), BF16)] * 7
    out_specs += [half_spec] * 7
    out_shape.append(jax.ShapeDtypeStruct((n // seq, 8, seq), F32))
    out_specs.append(pl.BlockSpec((1, 8, tm), lambda i: (i // tps, 0, i % tps)))
    outs = pl.pallas_call(
        functools.partial(_inproj_kernel, has_y=has_y, tiles_per_seq=tps),
        grid=(nt,),
        in_specs=in_specs,
        out_specs=out_specs,
        out_shape=out_shape,
        scratch_shapes=[pltpu.VMEM((8, LANES), F32)],
        compiler_params=pltpu.CompilerParams(
            dimension_semantics=("arbitrary",), vmem_limit_bytes=VMEM_LIMIT),
        name="inproj_y" if has_y else "inproj",
    )(*args)
    if has_y:
        return outs[0], outs[1:]
    return x, outs


def _fox_kernel(q_ref, k_ref, v_ref, ct_ref, o_ref, m_sc, acc_sc, kmax_sc, *, tile, nq):
    hp = pl.program_id(1)
    qi = pl.program_id(2)
    q2 = q_ref[0]
    lane = lax.broadcasted_iota(jnp.int32, q2.shape, 1)
    first = lane < FOX_HEAD_DIM
    zero = jnp.zeros_like(q2)
    qh = (jnp.where(first, q2, zero), jnp.where(first, zero, q2))
    reps = tile // LANES

    def head_sqnorm_max(xf, h):
        sq = xf * xf
        sq = jnp.where(first, sq, 0.0) if h == 0 else jnp.where(first, 0.0, sq)
        return jnp.max(jnp.sum(sq, axis=1, keepdims=True), axis=0, keepdims=True)

    @pl.when(qi == 0)
    def _():
        for j in range(nq):
            kf = k_ref[0, j * tile:(j + 1) * tile, :].astype(F32)
            for h in range(2):
                kmax_sc[h, j:j + 1, :] = jnp.broadcast_to(head_sqnorm_max(kf, h), (1, LANES))

    def head_step(h, kb, k_blk, v_blk, mask, m_old, acc_old):
        one = jnp.ones_like(v_blk)
        va = jnp.where(first, v_blk, one) if h == 0 else jnp.where(first, one, v_blk)
        s = _dot_nt(qh[h], k_blk) - ct_ref[0, 2 * hp + h, pl.ds(kb, 1), :]
        if mask is not None:
            s = jnp.where(mask, s, -jnp.inf)
        m_cur = jnp.max(s, axis=1, keepdims=True)
        if m_old is None:
            m_new = jnp.broadcast_to(m_cur, (tile, LANES))
            p = jnp.exp(s - jnp.concatenate([m_new] * reps, axis=1))
            acc = _dot(p.astype(BF16), va)
        else:
            m_new = jnp.maximum(m_old, m_cur)
            alpha = jnp.exp(m_old - m_new)
            p = jnp.exp(s - jnp.concatenate([m_new] * reps, axis=1))
            acc = alpha * acc_old + _dot(p.astype(BF16), va)
        return m_new, acc

    row = lax.broadcasted_iota(jnp.int32, (tile, tile), 0)
    col = lax.broadcasted_iota(jnp.int32, (tile, tile), 1)
    start = pl.multiple_of(qi * tile, tile)
    k_blk = k_ref[0, pl.ds(start, tile), :]
    v_blk = v_ref[0, pl.ds(start, tile), :]
    for h in range(2):
        m_new, acc = head_step(h, qi, k_blk, v_blk, col <= row, None, None)
        m_sc[h] = m_new
        acc_sc[h] = acc

    def off_diagonal(kbs):
        state = [(m_sc[h], acc_sc[h]) for h in range(2)]
        for kb in kbs:
            start = pl.multiple_of(kb * tile, tile)
            k_blk = k_ref[0, pl.ds(start, tile), :]
            v_blk = v_ref[0, pl.ds(start, tile), :]
            state = [head_step(h, kb, k_blk, v_blk, None, *state[h]) for h in range(2)]
        for h in range(2):
            m_sc[h] = state[h][0]
            acc_sc[h] = state[h][1]

    jrow = lax.broadcasted_iota(jnp.int32, (nq, LANES), 0)
    qf = q2.astype(F32)
    jmin = qi
    for h in range(2):
        m_low = jnp.min(jnp.min(m_sc[h], axis=1, keepdims=True), axis=0, keepdims=True)
        c_last = ct_ref[0, 2 * hp + h, :, tile - 1:tile]
        bound = jnp.sqrt(head_sqnorm_max(qf, h) * kmax_sc[h]) - c_last
        need = (bound - m_low > -EXP_UNDERFLOW) & (jrow < qi)
        j_first = jnp.min(jnp.min(jnp.where(need, jrow, qi), axis=1, keepdims=True),
                          axis=0, keepdims=True)
        jmin = jnp.minimum(jmin, j_first[0, 0])
    count = qi - jmin

    def pair(j, carry):
        off_diagonal((jmin + 2 * j, jmin + 2 * j + 1))
        return carry

    lax.fori_loop(0, count // 2, pair, 0)

    @pl.when(count % 2 == 1)
    def _():
        off_diagonal((qi - 1,))

    a0 = acc_sc[0]
    a1 = acc_sc[1]
    half = FOX_HEAD_DIM
    o = jnp.where(first, a0 / pltpu.roll(a0, half, 1), a1 / pltpu.roll(a1, half, 1))
    o_ref[0] = o.astype(o_ref.dtype)


def _fox_attention(fq, fk, fv, ct4, batch, seq):
    t = ATT_TILE
    nq = seq // t
    q3 = fq.reshape(batch, seq, FOX_WIDTH)
    k3 = fk.reshape(batch, seq, FOX_WIDTH)
    v3 = fv.reshape(batch, seq, FOX_WIDTH)
    out = pl.pallas_call(
        functools.partial(_fox_kernel, tile=t, nq=nq),
        grid=(batch, FOX_HEADS // 2, nq),
        in_specs=[
            pl.BlockSpec((1, t, LANES), lambda b, j, i: (b, i, j)),
            pl.BlockSpec((1, seq, LANES), lambda b, j, i: (b, 0, j)),
            pl.BlockSpec((1, seq, LANES), lambda b, j, i: (b, 0, j)),
            pl.BlockSpec((1, 8, nq, t), lambda b, j, i: (b, 0, 0, 0)),
        ],
        out_specs=pl.BlockSpec((1, t, LANES), lambda b, j, i: (b, i, j)),
        out_shape=jax.ShapeDtypeStruct((batch, seq, FOX_WIDTH), BF16),
        scratch_shapes=[pltpu.VMEM((2, t, LANES), F32), pltpu.VMEM((2, t, LANES), F32),
                        pltpu.VMEM((2, nq, LANES), F32)],
        compiler_params=pltpu.CompilerParams(
            dimension_semantics=("parallel", "parallel", "arbitrary"),
            vmem_limit_bytes=VMEM_LIMIT),
        name="fox_attention",
    )(q3, k3, v3, ct4)
    return out.reshape(batch * seq, FOX_WIDTH)


def _ret_kernel(q_ref, k_ref, v_ref, g_ref, cos_ref, sin_ref, dmat_ref, qdec_ref, kdec_ref,
                sdec_ref, o_ref, state_sc):
    si = pl.program_id(1)

    @pl.when(si == 0)
    def _():
        state_sc[...] = jnp.zeros_like(state_sc)

    cos2 = cos_ref[...]
    sin2 = sin_ref[...]
    dk = RET_HEAD_DIM

    def rot(xf):
        return xf * cos2 + pltpu.roll(xf, dk // 2, 1) * sin2

    for h in range(RET_HEADS):
        cols = slice(h * dk, (h + 1) * dk)
        q = rot(q_ref[0, :, cols].astype(F32))
        k = rot(k_ref[0, :, cols].astype(F32)) * (dk ** -0.5)
        v = v_ref[0, :, cols]
        scores = _dot_nt(q.astype(BF16), k.astype(BF16)) * dmat_ref[h]
        intra = _dot(scores.astype(BF16), v)
        state = state_sc[h]
        cross = _dot((q * qdec_ref[h]).astype(BF16), state.astype(BF16))
        out = intra + cross
        state_sc[h] = state * sdec_ref[h, 0:1, :] + _dot_tn((k * kdec_ref[h]).astype(BF16), v)

        y = out * lax.rsqrt(jnp.mean(out * out, axis=-1, keepdims=True) + RMS_EPS)
        g = g_ref[0, :, cols].astype(F32)
        o_ref[0, :, cols] = (y * (g * jax.nn.sigmoid(g))).astype(o_ref.dtype)


def _ret_tables(seq):
    half = RET_HEAD_DIM // 2
    inv_freq = 1.0 / (ROPE_BASE ** (jnp.arange(half, dtype=F32) / half))
    ang = jnp.arange(seq, dtype=F32)[:, None] * inv_freq[None, :]
    cos = jnp.cos(ang)
    sin = jnp.sin(ang)
    cos2 = jnp.concatenate([cos, cos], axis=1)
    sin2 = jnp.concatenate([-sin, sin], axis=1)
    lt = RET_TILE
    log_gamma = jnp.log(1.0 - 2.0 ** (-5.0 - jnp.arange(RET_HEADS, dtype=F32)))
    idx = jnp.arange(lt)
    t = idx[:, None]
    s = idx[None, :]
    same = (t // CHUNK) == (s // CHUNK)
    earlier = (s // CHUNK) < (t // CHUNK)
    dist = jnp.where(same, jnp.abs(t - s), t - s).astype(F32)
    dmat = jnp.where((same | earlier)[None], jnp.exp(log_gamma[:, None, None] * dist[None]), 0.0)
    idxf = idx.astype(F32)
    qdec = jnp.exp(log_gamma[:, None] * idxf[None, :])
    kdec = jnp.exp(log_gamma[:, None] * (lt - idxf)[None, :])
    sdec = jnp.exp(log_gamma * lt)
    qdec = jnp.broadcast_to(qdec[:, :, None], (RET_HEADS, lt, LANES))
    kdec = jnp.broadcast_to(kdec[:, :, None], (RET_HEADS, lt, LANES))
    sdec = jnp.broadcast_to(sdec[:, None, None], (RET_HEADS, 8, LANES))
    return cos2, sin2, dmat, qdec, kdec, sdec


def _retention(rq, rk, rv, rg, tables, batch, seq):
    lt = RET_TILE
    ns = seq // lt
    cos2, sin2, dmat, qdec, kdec, sdec = tables
    blk = pl.BlockSpec((1, lt, RET_WIDTH), lambda b, i: (b, i, 0))
    tab = pl.BlockSpec((lt, LANES), lambda b, i: (i, 0))
    args = [a.reshape(batch, seq, RET_WIDTH) for a in (rq, rk, rv, rg)]
    out = pl.pallas_call(
        _ret_kernel,
        grid=(batch, ns),
        in_specs=[blk, blk, blk, blk, tab, tab,
                  pl.BlockSpec((RET_HEADS, lt, lt), lambda b, i: (0, 0, 0)),
                  pl.BlockSpec((RET_HEADS, lt, LANES), lambda b, i: (0, 0, 0)),
                  pl.BlockSpec((RET_HEADS, lt, LANES), lambda b, i: (0, 0, 0)),
                  pl.BlockSpec((RET_HEADS, 8, LANES), lambda b, i: (0, 0, 0))],
        out_specs=blk,
        out_shape=jax.ShapeDtypeStruct((batch, seq, RET_WIDTH), BF16),
        scratch_shapes=[pltpu.VMEM((RET_HEADS, RET_HEAD_DIM, RET_HEAD_DIM), F32)],
        compiler_params=pltpu.CompilerParams(
            dimension_semantics=("parallel", "arbitrary"),
            vmem_limit_bytes=VMEM_LIMIT),
        name="retention",
    )(*args, cos2, sin2, dmat, qdec, kdec, sdec)
    return out.reshape(batch * seq, RET_WIDTH)


def _outproj_kernel(fox_ref, ret_ref, x_ref, wo_ref, nw_ref, wr_ref, br_ref, xo_ref, xg_ref,
                    eid_ref, rw_ref):
    mixed = jnp.concatenate([fox_ref[...], ret_ref[...]], axis=1)
    x = x_ref[...] + _dot(mixed, wo_ref[...])
    xo_ref[...] = x
    _tm_store(xg_ref, 0, x.shape[0], x)
    h = _rms(x, nw_ref[...]).astype(BF16)
    lt = (_dot(h, wr_ref[...]) + br_ref[...]).T
    tm = lt.shape[1]
    rowid = lax.broadcasted_iota(jnp.int32, (8, tm), 0)
    neg = -jnp.inf

    def top1(v):
        vmax = jnp.max(v, axis=0, keepdims=True)
        idx = jnp.min(jnp.where(v == vmax, rowid, 8), axis=0, keepdims=True)
        return vmax, idx

    gl = jnp.where(rowid < N_GROUPS, lt[0:8], neg)
    gmax, gidx = top1(gl)
    g_w = 1.0 / jnp.sum(jnp.exp(gl - gmax), axis=0, keepdims=True)
    e_in = jnp.zeros((8, tm), F32)
    for g in range(N_GROUPS):
        e_in = jnp.where(gidx == g, lt[8 + 8 * g:16 + 8 * g], e_in)
    v1, i1 = top1(e_in)
    rest = jnp.where(rowid == i1, neg, e_in)
    v2, i2 = top1(rest)
    t = jnp.exp(v2 - v1)
    w1 = g_w / (1.0 + t)
    eid_ref[0:1, :] = gidx * EXPERTS_PER_GROUP + i1
    eid_ref[1:2, :] = gidx * EXPERTS_PER_GROUP + i2
    wslab = jnp.concatenate([w1, w1 * t, jnp.zeros((LANES - TOP_K, tm), F32)], axis=0)
    rw_ref[...] = wslab.T


def _outproj(fox, ret, x, wo, nw, wr, br):
    n = x.shape[0]
    tm = ROW_TILE
    row = pl.BlockSpec((tm, D_MODEL), lambda i: (i, 0))
    half = pl.BlockSpec((tm, 512), lambda i: (i, 0))
    pair = pl.BlockSpec((TOP_K, tm), lambda i: (0, i))
    wts = pl.BlockSpec((tm, LANES), lambda i: (i, 0))
    return pl.pallas_call(
        _outproj_kernel,
        grid=(n // tm,),
        in_specs=[half, half, row,
                  pl.BlockSpec((D_MODEL, D_MODEL), lambda i: (0, 0)),
                  pl.BlockSpec((1, D_MODEL), lambda i: (0, 0)),
                  pl.BlockSpec((D_MODEL, LANES), lambda i: (0, 0)),
                  pl.BlockSpec((1, LANES), lambda i: (0, 0))],
        out_specs=[row, pl.BlockSpec((tm * CHUNKS, LANES), lambda i: (i, 0)), pair, wts],
        out_shape=[jax.ShapeDtypeStruct((n, D_MODEL), F32),
                   jax.ShapeDtypeStruct((n * CHUNKS, LANES), F32),
                   jax.ShapeDtypeStruct((TOP_K, n), jnp.int32),
                   jax.ShapeDtypeStruct((n, LANES), F32)],
        compiler_params=pltpu.CompilerParams(
            dimension_semantics=("parallel",), vmem_limit_bytes=VMEM_LIMIT),
        name="outproj",
    )(fox, ret, x, wo, nw, wr, br)


_FIRST, _LAST, _VALID, _NEWEXP = 1, 2, 4, 8


def _moe_kernel(blk_ref, exp_ref, flag_ref, starts_ref,
                tokc_ref, tokn_ref, dstp_ref, dstc_ref, x_hbm, nw_ref, wg_ref, wu_ref, wd_ref,
                y_hbm, xbuf, ybuf, hbuf, abuf, wgb, wub, wdb, gsem, ssem, *, tb, nb):
    w = pl.program_id(0)
    b = blk_ref[w]
    e = exp_ref[w]
    flags = flag_ref[w]
    slot = b % 2
    nslot = 1 - slot
    span = tb * CHUNKS
    d_expert = wgb.shape[1]
    d_model = wdb.shape[1]
    col_tile = 2 * LANES

    def hbm_row(ref, idx):
        return ref.at[pl.ds(pl.multiple_of(idx * CHUNKS, CHUNKS), CHUNKS), :]

    def buf_row(buf, s, r):
        return buf.at[pl.ds(pl.multiple_of(s * span + r * CHUNKS, CHUNKS), CHUNKS), :]

    def gather_row(tok_ref, s, r):
        return pltpu.make_async_copy(hbm_row(x_hbm, tok_ref[0, 0, r]), buf_row(xbuf, s, r),
                                     gsem.at[s])

    def scatter_row(dst_ref, s, r):
        return pltpu.make_async_copy(buf_row(ybuf, s, r), hbm_row(y_hbm, dst_ref[0, 0, r]),
                                     ssem.at[s])

    def wait_rows(sem, s):
        whole = pl.ds(pl.multiple_of(s * span, span), span)
        pltpu.make_async_copy(xbuf.at[whole, :], ybuf.at[whole, :], sem.at[s]).wait()

    def expert_pass(fresh, copies):
        n_phase = d_expert // col_tile + d_model // col_tile
        per = -(-len(copies) // n_phase) if copies else 0
        issued = [0]

        def issue_group():
            for c in copies[issued[0]:issued[0] + per]:
                c.start()
            issued[0] += per

        h = hbuf[...]
        for c in range(d_expert // col_tile):
            cols = slice(c * col_tile, (c + 1) * col_tile)
            g = _dot(h, wgb[:, cols])
            u = _dot(h, wub[:, cols])
            abuf[:, cols] = (g * jax.nn.sigmoid(g) * u).astype(BF16)
            issue_group()
        a = abuf[...]
        q = b * tb + lax.broadcasted_iota(jnp.int32, (tb, LANES), 0)
        mine = (q >= starts_ref[e]) & (q < starts_ref[e + 1])
        for c in range(d_model // col_tile):
            y = _dot(a, wdb[:, c * col_tile:(c + 1) * col_tile])
            for k in range(col_tile // LANES):
                rows = pl.ds(slot * span + c * (col_tile // LANES) + k, tb, stride=CHUNKS)
                prev = jnp.zeros((tb, LANES), F32) if fresh else ybuf[rows, :]
                ybuf[rows, :] = jnp.where(mine, y[:, k * LANES:(k + 1) * LANES], prev)
            issue_group()

    @pl.when(w == 0)
    def _():
        ybuf[...] = jnp.zeros_like(ybuf)
        for r in range(tb):
            gather_row(tokc_ref, 0, r).start()

    @pl.when((flags & _NEWEXP) != 0)
    def _():
        wgb[...] = wg_ref[0, 0].astype(BF16)
        wub[...] = wu_ref[0, 0].astype(BF16)
        wdb[...] = wd_ref[0, 0].astype(BF16)

    @pl.when((flags & _FIRST) != 0)
    def _():
        wait_rows(gsem, slot)

        @pl.when(b >= 1)
        def _():
            wait_rows(ssem, slot)

        copies = []
        for r in range(tb):
            copies.append(gather_row(tokn_ref, nslot, r))
            copies.append(scatter_row(dstp_ref, nslot, r))
        hbuf[...] = _rms(_tm_load(xbuf, slot * span, tb), nw_ref[...]).astype(BF16)
        expert_pass(True, copies)

    @pl.when(((flags & _VALID) != 0) & ((flags & _FIRST) == 0))
    def _():
        expert_pass(False, [])

    @pl.when(((flags & _LAST) != 0) & (b == nb - 1))
    def _():
        wait_rows(gsem, nslot)
        for r in range(tb):
            scatter_row(dstc_ref, slot, r).start()
        wait_rows(ssem, nslot)
        wait_rows(ssem, slot)


def _moe(x, s_tok, s_dst, blk, exp, flags, starts, nw, wg, wu, wd, layer, tb=MOE_TILE):
    na = s_tok.shape[0]
    nb = na // tb
    d_model, d_expert = wg.shape[2], wg.shape[3]
    tok3 = s_tok.reshape(nb, 1, tb)
    dst3 = s_dst.reshape(nb, 1, tb)
    smem_blk = lambda f: pl.BlockSpec((1, 1, tb), f, memory_space=pltpu.SMEM)
    grid_spec = pltpu.PrefetchScalarGridSpec(
        num_scalar_prefetch=4,
        grid=(blk.shape[0],),
        in_specs=[
            smem_blk(lambda w, bl, ex, fl, st: (bl[w], 0, 0)),
            smem_blk(lambda w, bl, ex, fl, st: (jnp.minimum(bl[w] + 1, nb - 1), 0, 0)),
            smem_blk(lambda w, bl, ex, fl, st: (jnp.maximum(bl[w] - 1, 0), 0, 0)),
            smem_blk(lambda w, bl, ex, fl, st: (bl[w], 0, 0)),
            pl.BlockSpec(memory_space=pl.ANY),
            pl.BlockSpec((1, d_model), lambda w, bl, ex, fl, st: (0, 0)),
            pl.BlockSpec((1, 1, d_model, d_expert), lambda w, bl, ex, fl, st: (layer, ex[w], 0, 0)),
            pl.BlockSpec((1, 1, d_model, d_expert), lambda w, bl, ex, fl, st: (layer, ex[w], 0, 0)),
            pl.BlockSpec((1, 1, d_expert, d_model), lambda w, bl, ex, fl, st: (layer, ex[w], 0, 0)),
        ],
        out_specs=pl.BlockSpec(memory_space=pl.ANY),
        scratch_shapes=[pltpu.VMEM((2 * tb * CHUNKS, LANES), F32),
                        pltpu.VMEM((2 * tb * CHUNKS, LANES), F32),
                        pltpu.VMEM((tb, d_model), BF16), pltpu.VMEM((tb, d_expert), BF16),
                        pltpu.VMEM((d_model, d_expert), BF16), pltpu.VMEM((d_model, d_expert), BF16),
                        pltpu.VMEM((d_expert, d_model), BF16),
                        pltpu.SemaphoreType.DMA((2,)), pltpu.SemaphoreType.DMA((2,))],
    )
    return pl.pallas_call(
        functools.partial(_moe_kernel, tb=tb, nb=nb),
        grid_spec=grid_spec,
        out_shape=jax.ShapeDtypeStruct((na * CHUNKS, LANES), F32),
        compiler_params=pltpu.CompilerParams(
            dimension_semantics=("arbitrary",), vmem_limit_bytes=VMEM_LIMIT),
        name="moe_experts",
    )(blk, exp, flags, starts, tok3, tok3, dst3, dst3, x, nw, wg, wu, wd)


def _dispatch_plan(eid, n, tb=MOE_TILE, n_experts=N_EXPERTS):
    na = TOP_K * n
    nb = na // tb
    eid_flat = eid.reshape(na)
    _, s_a = lax.sort((eid_flat, jnp.arange(na, dtype=jnp.int32)), num_keys=1)
    s_tok = s_a % n
    counts = jnp.sum(eid_flat[None, :] == jnp.arange(n_experts, dtype=jnp.int32)[:, None], axis=1)
    starts = jnp.concatenate([jnp.zeros((1,), jnp.int32),
                              jnp.cumsum(counts).astype(jnp.int32)])
    lo, hi = starts[:-1], starts[1:]
    nonempty = hi > lo
    first_blk = lo // tb
    npass = jnp.where(nonempty, (hi - 1) // tb - first_blk + 1, 0)
    cum = jnp.cumsum(npass)
    total = cum[-1]
    n_pass = nb + n_experts
    w = jnp.arange(n_pass, dtype=jnp.int32)
    wc = jnp.minimum(w, total - 1)
    ex = jnp.sum(cum[None, :] <= wc[:, None], axis=1).astype(jnp.int32)
    sel = ex[:, None] == jnp.arange(n_experts, dtype=jnp.int32)[None, :]
    pick = lambda v: jnp.sum(jnp.where(sel, v[None, :], 0), axis=1)
    blk = (pick(first_blk) + (wc - pick(cum - npass))).astype(jnp.int32)
    valid = w < total
    prev_blk = jnp.concatenate([jnp.full((1,), -1, jnp.int32), blk[:-1]])
    next_blk = jnp.concatenate([blk[1:], jnp.full((1,), -1, jnp.int32)])
    first = valid & (blk != prev_blk)
    last = valid & ((blk != next_blk) | (w == total - 1))
    prev_ex = jnp.concatenate([jnp.full((1,), -1, jnp.int32), ex[:-1]])
    newexp = valid & (ex != prev_ex)
    flags = (first * _FIRST + last * _LAST + valid * _VALID + newexp * _NEWEXP).astype(jnp.int32)
    return s_tok, s_a, blk, ex, flags, starts


def _final_kernel(x_ref, y0_ref, y1_ref, rw_ref, nw_ref, o_ref):
    rw = rw_ref[...]
    tm = x_ref.shape[0]
    x = x_ref[...] + (rw[:, 0:1] * _tm_load(y0_ref, 0, tm) + rw[:, 1:2] * _tm_load(y1_ref, 0, tm))
    o_ref[...] = _rms(x, nw_ref[...])


def _final(x, y, rw, nw):
    n = x.shape[0]
    tm = ROW_TILE
    nt = n // tm
    row = pl.BlockSpec((tm, D_MODEL), lambda i: (i, 0))
    return pl.pallas_call(
        _final_kernel,
        grid=(nt,),
        in_specs=[row, pl.BlockSpec((tm * CHUNKS, LANES), lambda i: (i, 0)),
                  pl.BlockSpec((tm * CHUNKS, LANES), lambda i: (i + nt, 0)),
                  pl.BlockSpec((tm, LANES), lambda i: (i, 0)),
                  pl.BlockSpec((1, D_MODEL), lambda i: (0, 0))],
        out_specs=row,
        out_shape=jax.ShapeDtypeStruct((n, D_MODEL), F32),
        compiler_params=pltpu.CompilerParams(
            dimension_semantics=("parallel",), vmem_limit_bytes=VMEM_LIMIT),
        name="final_norm",
    )(x, y, y, rw, nw)


def kernel(x, norm_mix_w, w_in, fox_forget_b, w_out, norm_ffn_w, w_router_group, b_router_group,
           w_router_expert, b_router_expert, w_expert_gate, w_expert_up, w_expert_down,
           norm_final_w):
    batch, seq, d = x.shape
    n = batch * seq
    depth = w_in.shape[0]
    xf = x.reshape(n, d)
    tables = _ret_tables(seq)
    tri = jnp.tril(jnp.ones((ROW_TILE, ROW_TILE), F32)).astype(BF16)
    nq = seq // ATT_TILE

    y = rw = None
    for layer in range(depth):
        wl = w_in[layer]
        c0 = 3 * FOX_WIDTH
        w_main = jnp.concatenate([wl[:, :c0], wl[:, c0 + FOX_HEADS:]], axis=1).astype(BF16)
        w_ff = jnp.pad(wl[:, c0:c0 + FOX_HEADS], ((0, 0), (0, LANES - FOX_HEADS))).astype(BF16)
        b_ff = jnp.pad(fox_forget_b[layer], (0, LANES - FOX_HEADS)).reshape(1, LANES)
        xf, (fq, fk, fv, rq, rk, rv, rg, ct) = _inproj(
            xf, y, rw, norm_mix_w[layer].reshape(1, d), w_main, w_ff, b_ff, tri, seq)
        ct4 = ct.reshape(batch, 8, nq, ATT_TILE)
        fox = _fox_attention(fq, fk, fv, ct4, batch, seq)
        ret = _retention(rq, rk, rv, rg, tables, batch, seq)

        zpad = jnp.zeros((d, N_GROUPS), F32)
        w_r = jnp.concatenate([w_router_group[layer], zpad, w_router_expert[layer]], axis=1)
        nr = 2 * N_GROUPS + N_EXPERTS
        w_r = jnp.pad(w_r, ((0, 0), (0, LANES - nr))).astype(BF16)
        b_r = jnp.concatenate([b_router_group[layer], jnp.zeros((N_GROUPS,), F32),
                               b_router_expert[layer]])
        b_r = jnp.pad(b_r, (0, LANES - nr)).reshape(1, LANES)
        xf, xg, eid, rw = _outproj(fox, ret, xf, w_out[layer].astype(BF16),
                                   norm_ffn_w[layer].reshape(1, d), w_r, b_r)
        s_tok, s_dst, blk, ex, flags, starts = _dispatch_plan(eid, n)
        y = _moe(xg, s_tok, s_dst, blk, ex, flags, starts, norm_ffn_w[layer].reshape(1, d),
                 w_expert_gate, w_expert_up, w_expert_down, layer)
    out = _final(xf, y, rw, norm_final_w.reshape(1, d))
    return out.reshape(batch, seq, d)
```

```python
import functools

import jax
import jax.numpy as jnp
import numpy as np
from jax import lax
from jax.experimental import pallas as pl
from jax.experimental.pallas import tpu as pltpu

F32 = jnp.float32
BF16 = jnp.bfloat16

D_MODEL = 1024
FOX_HEADS = 8
FOX_HEAD_DIM = 64
FOX_WIDTH = 512
RET_HEADS = 4
RET_HEAD_DIM = 128
RET_WIDTH = 512
CHUNK = 64
ROPE_BASE = 10000.0
N_GROUPS = 4
EXPERTS_PER_GROUP = 8
N_EXPERTS = 32
TOP_K = 2
D_EXPERT = 512
RMS_EPS = 1e-6

LANES = 128
VMEM_LIMIT = 56 * 1024 * 1024
DMA_QUEUES = 2

ROW_TILE = 512
ATT_TILE = 512
RET_TILE = 256
MOE_TILE = 256
N_MAIN = 7 * 512
EXP_UNDERFLOW = 110.0


def _rms(xf, w):
    return xf * lax.rsqrt(jnp.mean(xf * xf, axis=-1, keepdims=True) + RMS_EPS) * w


def _dot(a, b):
    return jnp.dot(a, b, preferred_element_type=F32)


def _dot_nt(a, b):
    return lax.dot_general(a, b, (((1,), (1,)), ((), ())), preferred_element_type=F32)


def _dot_tn(a, b):
    return lax.dot_general(a, b, (((0,), (0,)), ((), ())), preferred_element_type=F32)


CHUNKS = D_MODEL // LANES


def _tm_load(ref, base, rows):
    return jnp.concatenate([ref[pl.ds(base + c, rows, stride=CHUNKS), :] for c in range(CHUNKS)],
                           axis=1)


def _tm_store(ref, base, rows, val):
    for c in range(CHUNKS):
        ref[pl.ds(base + c, rows, stride=CHUNKS), :] = val[:, c * LANES:(c + 1) * LANES]


def _inproj_kernel(*refs, has_y, tiles_per_seq):
    if has_y:
        x_ref, y0_ref, y1_ref, rw_ref = refs[:4]
        refs = refs[4:]
    else:
        x_ref = refs[0]
        refs = refs[1:]
    nw_ref, w_ref, wff_ref, bff_ref, tri_ref = refs[:5]
    refs = refs[5:]
    if has_y:
        xres_ref = refs[0]
        refs = refs[1:]
    fq_ref, fk_ref, fv_ref, rq_ref, rk_ref, rv_ref, rg_ref, ct_ref, carry_sc = refs

    i = pl.program_id(0)
    x = x_ref[...]
    if has_y:
        rw = rw_ref[...]
        tm = x.shape[0]
        x = x + (rw[:, 0:1] * _tm_load(y0_ref, 0, tm) + rw[:, 1:2] * _tm_load(y1_ref, 0, tm))
        xres_ref[...] = x
    h = _rms(x, nw_ref[...]).astype(BF16)

    outs = (fq_ref, fk_ref, fv_ref, rq_ref, rk_ref, rv_ref, rg_ref)
    for j, o_ref in enumerate(outs):
        acc = _dot(h, w_ref[:, j * 512:(j + 1) * 512])
        if j == 0:
            acc = acc * (FOX_HEAD_DIM ** -0.5)
        o_ref[...] = acc.astype(BF16)

    z = _dot(h, wff_ref[...]) + bff_ref[...]
    lf = jnp.minimum(z, 0.0) - jnp.log1p(jnp.exp(-jnp.abs(z)))
    lane = lax.broadcasted_iota(jnp.int32, lf.shape, 1)
    lf = jnp.where(lane < FOX_HEADS, lf, 0.0)
    hi = lf.astype(BF16)
    r1 = lf - hi.astype(F32)
    mid = r1.astype(BF16)
    lo = (r1 - mid.astype(F32)).astype(BF16)
    tri = tri_ref[...]
    cs = _dot(tri, hi) + _dot(tri, mid) + _dot(tri, lo)

    @pl.when(i % tiles_per_seq == 0)
    def _():
        carry_sc[...] = jnp.zeros_like(carry_sc)

    c = cs + carry_sc[0:1, :]
    carry_sc[...] = jnp.broadcast_to(c[-1:, :], carry_sc.shape)
    ct_ref[0] = c.T[:8, :]


def _inproj(x, y, rw, nw, w_main, w_ff, b_ff, tri, seq):
    n = x.shape[0]
    tm = ROW_TILE
    nt = n // tm
    tps = seq // tm
    has_y = y is not None
    row_spec = pl.BlockSpec((tm, D_MODEL), lambda i: (i, 0))
    in_specs = [row_spec]
    args = [x]
    if has_y:
        in_specs += [pl.BlockSpec((tm * CHUNKS, LANES), lambda i: (i, 0)),
                     pl.BlockSpec((tm * CHUNKS, LANES), lambda i: (i + nt, 0)),
                     pl.BlockSpec((tm, LANES), lambda i: (i, 0))]
        args += [y, y, rw]
    in_specs += [
        pl.BlockSpec((1, D_MODEL), lambda i: (0, 0)),
        pl.BlockSpec((D_MODEL, N_MAIN), lambda i: (0, 0)),
        pl.BlockSpec((D_MODEL, LANES), lambda i: (0, 0)),
        pl.BlockSpec((1, LANES), lambda i: (0, 0)),
        pl.BlockSpec((tm, tm), lambda i: (0, 0)),
    ]
    args += [nw, w_main, w_ff, b_ff, tri]
    half_spec = pl.BlockSpec((tm, 512), lambda i: (i, 0))
    out_shape = []
    out_specs = []
    if has_y:
        out_shape.append(jax.ShapeDtypeStruct((n, D_MODEL), F32))
        out_specs.append(row_spec)
    out_shape += [jax.ShapeDtypeStruct((n, 512), BF16)] * 7
    out_specs += [half_spec] * 7
    out_shape.append(jax.ShapeDtypeStruct((n // seq, 8, seq), F32))
    out_specs.append(pl.BlockSpec((1, 8, tm), lambda i: (i // tps, 0, i % tps)))
    outs = pl.pallas_call(
        functools.partial(_inproj_kernel, has_y=has_y, tiles_per_seq=tps),
        grid=(nt,),
        in_specs=in_specs,
        out_specs=out_specs,
        out_shape=out_shape,
        scratch_shapes=[pltpu.VMEM((8, LANES), F32)],
        compiler_params=pltpu.CompilerParams(
            dimension_semantics=("arbitrary",), vmem_limit_bytes=VMEM_LIMIT),
        name="inproj_y" if has_y else "inproj",
    )(*args)
    if has_y:
        return outs[0], outs[1:]
    return x, outs


def _fox_kernel(q_ref, k_ref, v_ref, ct_ref, o_ref, m_sc, acc_sc, kmax_sc, *, tile, nq):
    hp = pl.program_id(1)
    qi = pl.program_id(2)
    q2 = q_ref[0]
    lane = lax.broadcasted_iota(jnp.int32, q2.shape, 1)
    first = lane < FOX_HEAD_DIM
    zero = jnp.zeros_like(q2)
    qh = (jnp.where(first, q2, zero), jnp.where(first, zero, q2))
    reps = tile // LANES

    def head_sqnorm_max(xf, h):
        sq = xf * xf
        sq = jnp.where(first, sq, 0.0) if h == 0 else jnp.where(first, 0.0, sq)
        return jnp.max(jnp.sum(sq, axis=1, keepdims=True), axis=0, keepdims=True)

    @pl.when(qi == 0)
    def _():
        for j in range(nq):
            kf = k_ref[0, j * tile:(j + 1) * tile, :].astype(F32)
            for h in range(2):
                kmax_sc[h, j:j + 1, :] = jnp.broadcast_to(head_sqnorm_max(kf, h), (1, LANES))

    def head_step(h, kb, k_blk, v_blk, mask, m_old, acc_old):
        one = jnp.ones_like(v_blk)
        va = jnp.where(first, v_blk, one) if h == 0 else jnp.where(first, one, v_blk)
        s = _dot_nt(qh[h], k_blk) - ct_ref[0, 2 * hp + h, pl.ds(kb, 1), :]
        if mask is not None:
            s = jnp.where(mask, s, -jnp.inf)
        m_cur = jnp.max(s, axis=1, keepdims=True)
        if m_old is None:
            m_new = jnp.broadcast_to(m_cur, (tile, LANES))
            p = jnp.exp(s - jnp.concatenate([m_new] * reps, axis=1))
            acc = _dot(p.astype(BF16), va)
        else:
            m_new = jnp.maximum(m_old, m_cur)
            alpha = jnp.exp(m_old - m_new)
            p = jnp.exp(s - jnp.concatenate([m_new] * reps, axis=1))
            acc = alpha * acc_old + _dot(p.astype(BF16), va)
        return m_new, acc

    row = lax.broadcasted_iota(jnp.int32, (tile, tile), 0)
    col = lax.broadcasted_iota(jnp.int32, (tile, tile), 1)
    start = pl.multiple_of(qi * tile, tile)
    k_blk = k_ref[0, pl.ds(start, tile), :]
    v_blk = v_ref[0, pl.ds(start, tile), :]
    for h in range(2):
        m_new, acc = head_step(h, qi, k_blk, v_blk, col <= row, None, None)
        m_sc[h] = m_new
        acc_sc[h] = acc

    def off_diagonal(kbs):
        state = [(m_sc[h], acc_sc[h]) for h in range(2)]
        for kb in kbs:
            start = pl.multiple_of(kb * tile, tile)
            k_blk = k_ref[0, pl.ds(start, tile), :]
            v_blk = v_ref[0, pl.ds(start, tile), :]
            state = [head_step(h, kb, k_blk, v_blk, None, *state[h]) for h in range(2)]
        for h in range(2):
            m_sc[h] = state[h][0]
            acc_sc[h] = state[h][1]

    jrow = lax.broadcasted_iota(jnp.int32, (nq, LANES), 0)
    qf = q2.astype(F32)
    jmin = qi
    for h in range(2):
        m_low = jnp.min(jnp.min(m_sc[h], axis=1, keepdims=True), axis=0, keepdims=True)
        c_last = ct_ref[0, 2 * hp + h, :, tile - 1:tile]
        bound = jnp.sqrt(head_sqnorm_max(qf, h) * kmax_sc[h]) - c_last
        need = (bound - m_low > -EXP_UNDERFLOW) & (jrow < qi)
        j_first = jnp.min(jnp.min(jnp.where(need, jrow, qi), axis=1, keepdims=True),
                          axis=0, keepdims=True)
        jmin = jnp.minimum(jmin, j_first[0, 0])
    count = qi - jmin

    def pair(j, carry):
        off_diagonal((jmin + 2 * j, jmin + 2 * j + 1))
        return carry

    lax.fori_loop(0, count // 2, pair, 0)

    @pl.when(count % 2 == 1)
    def _():
        off_diagonal((qi - 1,))

    a0 = acc_sc[0]
    a1 = acc_sc[1]
    half = FOX_HEAD_DIM
    o = jnp.where(first, a0 / pltpu.roll(a0, half, 1), a1 / pltpu.roll(a1, half, 1))
    o_ref[0] = o.astype(o_ref.dtype)


def _fox_attention(fq, fk, fv, ct4, batch, seq):
    t = ATT_TILE
    nq = seq // t
    q3 = fq.reshape(batch, seq, FOX_WIDTH)
    k3 = fk.reshape(batch, seq, FOX_WIDTH)
    v3 = fv.reshape(batch, seq, FOX_WIDTH)
    out = pl.pallas_call(
        functools.partial(_fox_kernel, tile=t, nq=nq),
        grid=(batch, FOX_HEADS // 2, nq),
        in_specs=[
            pl.BlockSpec((1, t, LANES), lambda b, j, i: (b, i, j)),
            pl.BlockSpec((1, seq, LANES), lambda b, j, i: (b, 0, j)),
            pl.BlockSpec((1, seq, LANES), lambda b, j, i: (b, 0, j)),
            pl.BlockSpec((1, 8, nq, t), lambda b, j, i: (b, 0, 0, 0)),
        ],
        out_specs=pl.BlockSpec((1, t, LANES), lambda b, j, i: (b, i, j)),
        out_shape=jax.ShapeDtypeStruct((batch, seq, FOX_WIDTH), BF16),
        scratch_shapes=[pltpu.VMEM((2, t, LANES), F32), pltpu.VMEM((2, t, LANES), F32),
                        pltpu.VMEM((2, nq, LANES), F32)],
        compiler_params=pltpu.CompilerParams(
            dimension_semantics=("parallel", "parallel", "arbitrary"),
            vmem_limit_bytes=VMEM_LIMIT),
        name="fox_attention",
    )(q3, k3, v3, ct4)
    return out.reshape(batch * seq, FOX_WIDTH)


def _ret_kernel(q_ref, k_ref, v_ref, g_ref, cos_ref, sin_ref, dmat_ref, qdec_ref, kdec_ref,
                sdec_ref, o_ref, state_sc):
    si = pl.program_id(1)

    @pl.when(si == 0)
    def _():
        state_sc[...] = jnp.zeros_like(state_sc)

    cos2 = cos_ref[...]
    sin2 = sin_ref[...]
    dk = RET_HEAD_DIM

    def rot(xf):
        return xf * cos2 + pltpu.roll(xf, dk // 2, 1) * sin2

    for h in range(RET_HEADS):
        cols = slice(h * dk, (h + 1) * dk)
        q = rot(q_ref[0, :, cols].astype(F32))
        k = rot(k_ref[0, :, cols].astype(F32)) * (dk ** -0.5)
        v = v_ref[0, :, cols]
        scores = _dot_nt(q.astype(BF16), k.astype(BF16)) * dmat_ref[h]
        intra = _dot(scores.astype(BF16), v)
        state = state_sc[h]
        cross = _dot((q * qdec_ref[h]).astype(BF16), state.astype(BF16))
        out = intra + cross
        state_sc[h] = state * sdec_ref[h, 0:1, :] + _dot_tn((k * kdec_ref[h]).astype(BF16), v)

        y = out * lax.rsqrt(jnp.mean(out * out, axis=-1, keepdims=True) + RMS_EPS)
        g = g_ref[0, :, cols].astype(F32)
        o_ref[0, :, cols] = (y * (g * jax.nn.sigmoid(g))).astype(o_ref.dtype)


def _ret_tables(seq):
    half = RET_HEAD_DIM // 2
    inv_freq = 1.0 / (ROPE_BASE ** (jnp.arange(half, dtype=F32) / half))
    ang = jnp.arange(seq, dtype=F32)[:, None] * inv_freq[None, :]
    cos = jnp.cos(ang)
    sin = jnp.sin(ang)
    cos2 = jnp.concatenate([cos, cos], axis=1)
    sin2 = jnp.concatenate([-sin, sin], axis=1)
    lt = RET_TILE
    log_gamma = jnp.log(1.0 - 2.0 ** (-5.0 - jnp.arange(RET_HEADS, dtype=F32)))
    idx = jnp.arange(lt)
    t = idx[:, None]
    s = idx[None, :]
    same = (t // CHUNK) == (s // CHUNK)
    earlier = (s // CHUNK) < (t // CHUNK)
    dist = jnp.where(same, jnp.abs(t - s), t - s).astype(F32)
    dmat = jnp.where((same | earlier)[None], jnp.exp(log_gamma[:, None, None] * dist[None]), 0.0)
    idxf = idx.astype(F32)
    qdec = jnp.exp(log_gamma[:, None] * idxf[None, :])
    kdec = jnp.exp(log_gamma[:, None] * (lt - idxf)[None, :])
    sdec = jnp.exp(log_gamma * lt)
    qdec = jnp.broadcast_to(qdec[:, :, None], (RET_HEADS, lt, LANES))
    kdec = jnp.broadcast_to(kdec[:, :, None], (RET_HEADS, lt, LANES))
    sdec = jnp.broadcast_to(sdec[:, None, None], (RET_HEADS, 8, LANES))
    return cos2, sin2, dmat, qdec, kdec, sdec


def _retention(rq, rk, rv, rg, tables, batch, seq):
    lt = RET_TILE
    ns = seq // lt
    cos2, sin2, dmat, qdec, kdec, sdec = tables
    blk = pl.BlockSpec((1, lt, RET_WIDTH), lambda b, i: (b, i, 0))
    tab = pl.BlockSpec((lt, LANES), lambda b, i: (i, 0))
    args = [a.reshape(batch, seq, RET_WIDTH) for a in (rq, rk, rv, rg)]
    out = pl.pallas_call(
        _ret_kernel,
        grid=(batch, ns),
        in_specs=[blk, blk, blk, blk, tab, tab,
                  pl.BlockSpec((RET_HEADS, lt, lt), lambda b, i: (0, 0, 0)),
                  pl.BlockSpec((RET_HEADS, lt, LANES), lambda b, i: (0, 0, 0)),
                  pl.BlockSpec((RET_HEADS, lt, LANES), lambda b, i: (0, 0, 0)),
                  pl.BlockSpec((RET_HEADS, 8, LANES), lambda b, i: (0, 0, 0))],
        out_specs=blk,
        out_shape=jax.ShapeDtypeStruct((batch, seq, RET_WIDTH), BF16),
        scratch_shapes=[pltpu.VMEM((RET_HEADS, RET_HEAD_DIM, RET_HEAD_DIM), F32)],
        compiler_params=pltpu.CompilerParams(
            dimension_semantics=("parallel", "arbitrary"),
            vmem_limit_bytes=VMEM_LIMIT),
        name="retention",
    )(*args, cos2, sin2, dmat, qdec, kdec, sdec)
    return out.reshape(batch * seq, RET_WIDTH)


def _outproj_kernel(fox_ref, ret_ref, x_ref, wo_ref, nw_ref, wr_ref, br_ref, xo_ref, xg_ref,
                    eid_ref, rw_ref):
    mixed = jnp.concatenate([fox_ref[...], ret_ref[...]], axis=1)
    x = x_ref[...] + _dot(mixed, wo_ref[...])
    xo_ref[...] = x
    _tm_store(xg_ref, 0, x.shape[0], x)
    h = _rms(x, nw_ref[...]).astype(BF16)
    lt = (_dot(h, wr_ref[...]) + br_ref[...]).T
    tm = lt.shape[1]
    rowid = lax.broadcasted_iota(jnp.int32, (8, tm), 0)
    neg = -jnp.inf

    def top1(v):
        vmax = jnp.max(v, axis=0, keepdims=True)
        idx = jnp.min(jnp.where(v == vmax, rowid, 8), axis=0, keepdims=True)
        return vmax, idx

    gl = jnp.where(rowid < N_GROUPS, lt[0:8], neg)
    gmax, gidx = top1(gl)
    g_w = 1.0 / jnp.sum(jnp.exp(gl - gmax), axis=0, keepdims=True)
    e_in = jnp.zeros((8, tm), F32)
    for g in range(N_GROUPS):
        e_in = jnp.where(gidx == g, lt[8 + 8 * g:16 + 8 * g], e_in)
    v1, i1 = top1(e_in)
    rest = jnp.where(rowid == i1, neg, e_in)
    v2, i2 = top1(rest)
    t = jnp.exp(v2 - v1)
    w1 = g_w / (1.0 + t)
    eid_ref[0:1, :] = gidx * EXPERTS_PER_GROUP + i1
    eid_ref[1:2, :] = gidx * EXPERTS_PER_GROUP + i2
    wslab = jnp.concatenate([w1, w1 * t, jnp.zeros((LANES - TOP_K, tm), F32)], axis=0)
    rw_ref[...] = wslab.T


def _outproj(fox, ret, x, wo, nw, wr, br):
    n = x.shape[0]
    tm = ROW_TILE
    row = pl.BlockSpec((tm, D_MODEL), lambda i: (i, 0))
    half = pl.BlockSpec((tm, 512), lambda i: (i, 0))
    pair = pl.BlockSpec((TOP_K, tm), lambda i: (0, i))
    wts = pl.BlockSpec((tm, LANES), lambda i: (i, 0))
    return pl.pallas_call(
        _outproj_kernel,
        grid=(n // tm,),
        in_specs=[half, half, row,
                  pl.BlockSpec((D_MODEL, D_MODEL), lambda i: (0, 0)),
                  pl.BlockSpec((1, D_MODEL), lambda i: (0, 0)),
                  pl.BlockSpec((D_MODEL, LANES), lambda i: (0, 0)),
                  pl.BlockSpec((1, LANES), lambda i: (0, 0))],
        out_specs=[row, pl.BlockSpec((tm * CHUNKS, LANES), lambda i: (i, 0)), pair, wts],
        out_shape=[jax.ShapeDtypeStruct((n, D_MODEL), F32),
                   jax.ShapeDtypeStruct((n * CHUNKS, LANES), F32),
                   jax.ShapeDtypeStruct((TOP_K, n), jnp.int32),
                   jax.ShapeDtypeStruct((n, LANES), F32)],
        compiler_params=pltpu.CompilerParams(
            dimension_semantics=("parallel",), vmem_limit_bytes=VMEM_LIMIT),
        name="outproj",
    )(fox, ret, x, wo, nw, wr, br)


_FIRST, _LAST, _VALID, _NEWEXP = 1, 2, 4, 8


def _moe_kernel(blk_ref, exp_ref, flag_ref, starts_ref,
                tokc_ref, tokn_ref, dstp_ref, dstc_ref, x_hbm, nw_ref, wg_ref, wu_ref, wd_ref,
                y_hbm, xbuf, ybuf, hbuf, wgb, wub, wdb, gsem, ssem, *, tb, nb):
    w = pl.program_id(0)
    b = blk_ref[w]
    e = exp_ref[w]
    flags = flag_ref[w]
    slot = b % 2
    nslot = 1 - slot
    span = tb * CHUNKS

    def hbm_row(ref, idx):
        return ref.at[pl.ds(pl.multiple_of(idx * CHUNKS, CHUNKS), CHUNKS), :]

    def buf_row(buf, s, r):
        return buf.at[pl.ds(pl.multiple_of(s * span + r * CHUNKS, CHUNKS), CHUNKS), :]

    def start_gather(tok_ref, s):
        for r in range(tb):
            pltpu.make_async_copy(hbm_row(x_hbm, tok_ref[0, 0, r]), buf_row(xbuf, s, r),
                                  gsem.at[s]).start(priority=r % DMA_QUEUES)

    def start_scatter(dst_ref, s):
        for r in range(tb):
            pltpu.make_async_copy(buf_row(ybuf, s, r), hbm_row(y_hbm, dst_ref[0, 0, r]),
                                  ssem.at[s]).start(priority=r % DMA_QUEUES)

    def wait_rows(sem, s):
        whole = pl.ds(pl.multiple_of(s * span, span), span)
        pltpu.make_async_copy(xbuf.at[whole, :], ybuf.at[whole, :], sem.at[s]).wait()

    @pl.when((flags & _FIRST) != 0)
    def _():
        @pl.when(w == 0)
        def _():
            start_gather(tokc_ref, 0)

        wait_rows(gsem, slot)

        @pl.when(b >= 2)
        def _():
            wait_rows(ssem, slot)

        @pl.when(b + 1 < nb)
        def _():
            start_gather(tokn_ref, nslot)

        @pl.when(b >= 1)
        def _():
            start_scatter(dstp_ref, nslot)

        hbuf[...] = _rms(_tm_load(xbuf, slot * span, tb), nw_ref[...]).astype(BF16)
        ybuf[pl.ds(pl.multiple_of(slot * span, span), span), :] = jnp.zeros((span, LANES), F32)

    @pl.when((flags & _NEWEXP) != 0)
    def _():
        wgb[...] = wg_ref[0, 0].astype(BF16)
        wub[...] = wu_ref[0, 0].astype(BF16)
        wdb[...] = wd_ref[0, 0].astype(BF16)

    @pl.when((flags & _VALID) != 0)
    def _():
        h = hbuf[...]
        g = _dot(h, wgb[...])
        u = _dot(h, wub[...])
        a = (g * jax.nn.sigmoid(g) * u).astype(BF16)
        y = _dot(a, wdb[...])
        q = b * tb + lax.broadcasted_iota(jnp.int32, y.shape, 0)
        mine = (q >= starts_ref[e]) & (q < starts_ref[e + 1])
        _tm_store(ybuf, slot * span, tb, jnp.where(mine, y, _tm_load(ybuf, slot * span, tb)))

    @pl.when(((flags & _LAST) != 0) & (b == nb - 1))
    def _():
        start_scatter(dstc_ref, slot)
        if nb >= 2:
            wait_rows(ssem, nslot)
        wait_rows(ssem, slot)


def _moe(x, s_tok, s_dst, blk, exp, flags, starts, nw, wg, wu, wd, layer, tb=MOE_TILE):
    na = s_tok.shape[0]
    nb = na // tb
    d_model, d_expert = wg.shape[2], wg.shape[3]
    tok3 = s_tok.reshape(nb, 1, tb)
    dst3 = s_dst.reshape(nb, 1, tb)
    smem_blk = lambda f: pl.BlockSpec((1, 1, tb), f, memory_space=pltpu.SMEM)
    grid_spec = pltpu.PrefetchScalarGridSpec(
        num_scalar_prefetch=4,
        grid=(blk.shape[0],),
        in_specs=[
            smem_blk(lambda w, bl, ex, fl, st: (bl[w], 0, 0)),
            smem_blk(lambda w, bl, ex, fl, st: (jnp.minimum(bl[w] + 1, nb - 1), 0, 0)),
            smem_blk(lambda w, bl, ex, fl, st: (jnp.maximum(bl[w] - 1, 0), 0, 0)),
            smem_blk(lambda w, bl, ex, fl, st: (bl[w], 0, 0)),
            pl.BlockSpec(memory_space=pl.ANY),
            pl.BlockSpec((1, d_model), lambda w, bl, ex, fl, st: (0, 0)),
            pl.BlockSpec((1, 1, d_model, d_expert), lambda w, bl, ex, fl, st: (layer, ex[w], 0, 0)),
            pl.BlockSpec((1, 1, d_model, d_expert), lambda w, bl, ex, fl, st: (layer, ex[w], 0, 0)),
            pl.BlockSpec((1, 1, d_expert, d_model), lambda w, bl, ex, fl, st: (layer, ex[w], 0, 0)),
        ],
        out_specs=pl.BlockSpec(memory_space=pl.ANY),
        scratch_shapes=[pltpu.VMEM((2 * tb * CHUNKS, LANES), F32),
                        pltpu.VMEM((2 * tb * CHUNKS, LANES), F32),
                        pltpu.VMEM((tb, d_model), BF16),
                        pltpu.VMEM((d_model, d_expert), BF16), pltpu.VMEM((d_model, d_expert), BF16),
                        pltpu.VMEM((d_expert, d_model), BF16),
                        pltpu.SemaphoreType.DMA((2,)), pltpu.SemaphoreType.DMA((2,))],
    )
    return pl.pallas_call(
        functools.partial(_moe_kernel, tb=tb, nb=nb),
        grid_spec=grid_spec,
        out_shape=jax.ShapeDtypeStruct((na * CHUNKS, LANES), F32),
        compiler_params=pltpu.CompilerParams(
            dimension_semantics=("arbitrary",), vmem_limit_bytes=VMEM_LIMIT),
        name="moe_experts",
    )(blk, exp, flags, starts, tok3, tok3, dst3, dst3, x, nw, wg, wu, wd)


def _dispatch_plan(eid, n, tb=MOE_TILE, n_experts=N_EXPERTS):
    na = TOP_K * n
    nb = na // tb
    eid_flat = eid.reshape(na)
    _, s_a = lax.sort((eid_flat, jnp.arange(na, dtype=jnp.int32)), num_keys=1)
    s_tok = s_a % n
    counts = jnp.sum(eid_flat[None, :] == jnp.arange(n_experts, dtype=jnp.int32)[:, None], axis=1)
    starts = jnp.concatenate([jnp.zeros((1,), jnp.int32),
                              jnp.cumsum(counts).astype(jnp.int32)])
    lo, hi = starts[:-1], starts[1:]
    nonempty = hi > lo
    first_blk = lo // tb
    npass = jnp.where(nonempty, (hi - 1) // tb - first_blk + 1, 0)
    cum = jnp.cumsum(npass)
    total = cum[-1]
    n_pass = nb + n_experts
    w = jnp.arange(n_pass, dtype=jnp.int32)
    wc = jnp.minimum(w, total - 1)
    ex = jnp.sum(cum[None, :] <= wc[:, None], axis=1).astype(jnp.int32)
    sel = ex[:, None] == jnp.arange(n_experts, dtype=jnp.int32)[None, :]
    pick = lambda v: jnp.sum(jnp.where(sel, v[None, :], 0), axis=1)
    blk = (pick(first_blk) + (wc - pick(cum - npass))).astype(jnp.int32)
    valid = w < total
    prev_blk = jnp.concatenate([jnp.full((1,), -1, jnp.int32), blk[:-1]])
    next_blk = jnp.concatenate([blk[1:], jnp.full((1,), -1, jnp.int32)])
    first = valid & (blk != prev_blk)
    last = valid & ((blk != next_blk) | (w == total - 1))
    prev_ex = jnp.concatenate([jnp.full((1,), -1, jnp.int32), ex[:-1]])
    newexp = valid & (ex != prev_ex)
    flags = (first * _FIRST + last * _LAST + valid * _VALID + newexp * _NEWEXP).astype(jnp.int32)
    return s_tok, s_a, blk, ex, flags, starts


def _final_kernel(x_ref, y0_ref, y1_ref, rw_ref, nw_ref, o_ref):
    rw = rw_ref[...]
    tm = x_ref.shape[0]
    x = x_ref[...] + (rw[:, 0:1] * _tm_load(y0_ref, 0, tm) + rw[:, 1:2] * _tm_load(y1_ref, 0, tm))
    o_ref[...] = _rms(x, nw_ref[...])


def _final(x, y, rw, nw):
    n = x.shape[0]
    tm = ROW_TILE
    nt = n // tm
    row = pl.BlockSpec((tm, D_MODEL), lambda i: (i, 0))
    return pl.pallas_call(
        _final_kernel,
        grid=(nt,),
        in_specs=[row, pl.BlockSpec((tm * CHUNKS, LANES), lambda i: (i, 0)),
                  pl.BlockSpec((tm * CHUNKS, LANES), lambda i: (i + nt, 0)),
                  pl.BlockSpec((tm, LANES), lambda i: (i, 0)),
                  pl.BlockSpec((1, D_MODEL), lambda i: (0, 0))],
        out_specs=row,
        out_shape=jax.ShapeDtypeStruct((n, D_MODEL), F32),
        compiler_params=pltpu.CompilerParams(
            dimension_semantics=("parallel",), vmem_limit_bytes=VMEM_LIMIT),
        name="final_norm",
    )(x, y, y, rw, nw)


def kernel(x, norm_mix_w, w_in, fox_forget_b, w_out, norm_ffn_w, w_router_group, b_router_group,
           w_router_expert, b_router_expert, w_expert_gate, w_expert_up, w_expert_down,
           norm_final_w):
    batch, seq, d = x.shape
    n = batch * seq
    depth = w_in.shape[0]
    xf = x.reshape(n, d)
    tables = _ret_tables(seq)
    tri = jnp.tril(jnp.ones((ROW_TILE, ROW_TILE), F32)).astype(BF16)
    nq = seq // ATT_TILE

    y = rw = None
    for layer in range(depth):
        wl = w_in[layer]
        c0 = 3 * FOX_WIDTH
        w_main = jnp.concatenate([wl[:, :c0], wl[:, c0 + FOX_HEADS:]], axis=1).astype(BF16)
        w_ff = jnp.pad(wl[:, c0:c0 + FOX_HEADS], ((0, 0), (0, LANES - FOX_HEADS))).astype(BF16)
        b_ff = jnp.pad(fox_forget_b[layer], (0, LANES - FOX_HEADS)).reshape(1, LANES)
        xf, (fq, fk, fv, rq, rk, rv, rg, ct) = _inproj(
            xf, y, rw, norm_mix_w[layer].reshape(1, d), w_main, w_ff, b_ff, tri, seq)
        ct4 = ct.reshape(batch, 8, nq, ATT_TILE)
        fox = _fox_attention(fq, fk, fv, ct4, batch, seq)
        ret = _retention(rq, rk, rv, rg, tables, batch, seq)

        zpad = jnp.zeros((d, N_GROUPS), F32)
        w_r = jnp.concatenate([w_router_group[layer], zpad, w_router_expert[layer]], axis=1)
        nr = 2 * N_GROUPS + N_EXPERTS
        w_r = jnp.pad(w_r, ((0, 0), (0, LANES - nr))).astype(BF16)
        b_r = jnp.concatenate([b_router_group[layer], jnp.zeros((N_GROUPS,), F32),
                               b_router_expert[layer]])
        b_r = jnp.pad(b_r, (0, LANES - nr)).reshape(1, LANES)
        xf, xg, eid, rw = _outproj(fox, ret, xf, w_out[layer].astype(BF16),
                                   norm_ffn_w[layer].reshape(1, d), w_r, b_r)
        s_tok, s_dst, blk, ex, flags, starts = _dispatch_plan(eid, n)
        y = _moe(xg, s_tok, s_dst, blk, ex, flags, starts, norm_ffn_w[layer].reshape(1, d),
                 w_expert_gate, w_expert_up, w_expert_down, layer)
    out = _final(xf, y, rw, norm_final_w.reshape(1, d))
    return out.reshape(batch, seq, d)
```

```python
import functools

import jax
import jax.numpy as jnp
import numpy as np
from jax import lax
from jax.experimental import pallas as pl
from jax.experimental.pallas import tpu as pltpu

F32 = jnp.float32
BF16 = jnp.bfloat16

D_MODEL = 1024
FOX_HEADS = 8
FOX_HEAD_DIM = 64
FOX_WIDTH = 512
RET_HEADS = 4
RET_HEAD_DIM = 128
RET_WIDTH = 512
CHUNK = 64
ROPE_BASE = 10000.0
N_GROUPS = 4
EXPERTS_PER_GROUP = 8
N_EXPERTS = 32
TOP_K = 2
D_EXPERT = 512
RMS_EPS = 1e-6

LANES = 128
VMEM_LIMIT = 56 * 1024 * 1024

ROW_TILE = 512
ATT_TILE = 512
RET_TILE = 256
MOE_TILE = 256
N_MAIN = 7 * 512
EXP_UNDERFLOW = 110.0


def _rms(xf, w):
    return xf * lax.rsqrt(jnp.mean(xf * xf, axis=-1, keepdims=True) + RMS_EPS) * w


def _dot(a, b):
    return jnp.dot(a, b, preferred_element_type=F32)


def _dot_nt(a, b):
    return lax.dot_general(a, b, (((1,), (1,)), ((), ())), preferred_element_type=F32)


def _dot_tn(a, b):
    return lax.dot_general(a, b, (((0,), (0,)), ((), ())), preferred_element_type=F32)


CHUNKS = D_MODEL // LANES


def _tm_load(ref, base, rows):
    return jnp.concatenate([ref[pl.ds(base + c, rows, stride=CHUNKS), :] for c in range(CHUNKS)],
                           axis=1)


def _tm_store(ref, base, rows, val):
    for c in range(CHUNKS):
        ref[pl.ds(base + c, rows, stride=CHUNKS), :] = val[:, c * LANES:(c + 1) * LANES]


def _inproj_kernel(*refs, has_y, tiles_per_seq):
    if has_y:
        x_ref, y0_ref, y1_ref, rw_ref = refs[:4]
        refs = refs[4:]
    else:
        x_ref = refs[0]
        refs = refs[1:]
    nw_ref, w_ref, wff_ref, bff_ref, tri_ref = refs[:5]
    refs = refs[5:]
    if has_y:
        xres_ref = refs[0]
        refs = refs[1:]
    fq_ref, fk_ref, fv_ref, rq_ref, rk_ref, rv_ref, rg_ref, ct_ref, carry_sc = refs

    i = pl.program_id(0)
    x = x_ref[...]
    if has_y:
        rw = rw_ref[...]
        tm = x.shape[0]
        x = x + (rw[:, 0:1] * _tm_load(y0_ref, 0, tm) + rw[:, 1:2] * _tm_load(y1_ref, 0, tm))
        xres_ref[...] = x
    h = _rms(x, nw_ref[...]).astype(BF16)

    outs = (fq_ref, fk_ref, fv_ref, rq_ref, rk_ref, rv_ref, rg_ref)
    for j, o_ref in enumerate(outs):
        acc = _dot(h, w_ref[:, j * 512:(j + 1) * 512])
        if j == 0:
            acc = acc * (FOX_HEAD_DIM ** -0.5)
        o_ref[...] = acc.astype(BF16)

    z = _dot(h, wff_ref[...]) + bff_ref[...]
    lf = jnp.minimum(z, 0.0) - jnp.log1p(jnp.exp(-jnp.abs(z)))
    lane = lax.broadcasted_iota(jnp.int32, lf.shape, 1)
    lf = jnp.where(lane < FOX_HEADS, lf, 0.0)
    hi = lf.astype(BF16)
    r1 = lf - hi.astype(F32)
    mid = r1.astype(BF16)
    lo = (r1 - mid.astype(F32)).astype(BF16)
    tri = tri_ref[...]
    cs = _dot(tri, hi) + _dot(tri, mid) + _dot(tri, lo)

    @pl.when(i % tiles_per_seq == 0)
    def _():
        carry_sc[...] = jnp.zeros_like(carry_sc)

    c = cs + carry_sc[0:1, :]
    carry_sc[...] = jnp.broadcast_to(c[-1:, :], carry_sc.shape)
    ct_ref[0] = c.T[:8, :]


def _inproj(x, y, rw, nw, w_main, w_ff, b_ff, tri, seq):
    n = x.shape[0]
    tm = ROW_TILE
    nt = n // tm
    tps = seq // tm
    has_y = y is not None
    row_spec = pl.BlockSpec((tm, D_MODEL), lambda i: (i, 0))
    in_specs = [row_spec]
    args = [x]
    if has_y:
        in_specs += [pl.BlockSpec((tm * CHUNKS, LANES), lambda i: (i, 0)),
                     pl.BlockSpec((tm * CHUNKS, LANES), lambda i: (i + nt, 0)),
                     pl.BlockSpec((tm, LANES), lambda i: (i, 0))]
        args += [y, y, rw]
    in_specs += [
        pl.BlockSpec((1, D_MODEL), lambda i: (0, 0)),
        pl.BlockSpec((D_MODEL, N_MAIN), lambda i: (0, 0)),
        pl.BlockSpec((D_MODEL, LANES), lambda i: (0, 0)),
        pl.BlockSpec((1, LANES), lambda i: (0, 0)),
        pl.BlockSpec((tm, tm), lambda i: (0, 0)),
    ]
    args += [nw, w_main, w_ff, b_ff, tri]
    half_spec = pl.BlockSpec((tm, 512), lambda i: (i, 0))
    out_shape = []
    out_specs = []
    if has_y:
        out_shape.append(jax.ShapeDtypeStruct((n, D_MODEL), F32))
        out_specs.append(row_spec)
    out_shape += [jax.ShapeDtypeStruct((n, 512), BF16)] * 7
    out_specs += [half_spec] * 7
    out_shape.append(jax.ShapeDtypeStruct((n // seq, 8, seq), F32))
    out_specs.append(pl.BlockSpec((1, 8, tm), lambda i: (i // tps, 0, i % tps)))
    outs = pl.pallas_call(
        functools.partial(_inproj_kernel, has_y=has_y, tiles_per_seq=tps),
        grid=(nt,),
        in_specs=in_specs,
        out_specs=out_specs,
        out_shape=out_shape,
        scratch_shapes=[pltpu.VMEM((8, LANES), F32)],
        compiler_params=pltpu.CompilerParams(
            dimension_semantics=("arbitrary",), vmem_limit_bytes=VMEM_LIMIT),
        name="inproj_y" if has_y else "inproj",
    )(*args)
    if has_y:
        return outs[0], outs[1:]
    return x, outs


def _fox_kernel(q_ref, k_ref, v_ref, ct_ref, o_ref, m_sc, acc_sc, kmax_sc, *, tile, nq):
    hp = pl.program_id(1)
    qi = pl.program_id(2)
    q2 = q_ref[0]
    lane = lax.broadcasted_iota(jnp.int32, q2.shape, 1)
    first = lane < FOX_HEAD_DIM
    zero = jnp.zeros_like(q2)
    qh = (jnp.where(first, q2, zero), jnp.where(first, zero, q2))
    reps = tile // LANES

    def head_sqnorm_max(xf, h):
        sq = xf * xf
        sq = jnp.where(first, sq, 0.0) if h == 0 else jnp.where(first, 0.0, sq)
        return jnp.max(jnp.sum(sq, axis=1, keepdims=True), axis=0, keepdims=True)

    @pl.when(qi == 0)
    def _():
        for j in range(nq):
            kf = k_ref[0, j * tile:(j + 1) * tile, :].astype(F32)
            for h in range(2):
                kmax_sc[h, j:j + 1, :] = jnp.broadcast_to(head_sqnorm_max(kf, h), (1, LANES))

    def with_ones(v_blk, h):
        one = jnp.ones_like(v_blk)
        sel = lax.broadcasted_iota(jnp.int32, v_blk.shape, 1) < FOX_HEAD_DIM
        return jnp.where(sel, v_blk, one) if h == 0 else jnp.where(sel, one, v_blk)

    def head_step(h, kb, k_blk, v_blk, m_old, acc_old):
        s = _dot_nt(qh[h], k_blk) - ct_ref[0, 2 * hp + h, pl.ds(kb, 1), :]
        m_new = jnp.maximum(m_old, jnp.max(s, axis=1, keepdims=True))
        alpha = jnp.exp(m_old - m_new)
        p = jnp.exp(s - jnp.concatenate([m_new] * reps, axis=1))
        return m_new, alpha * acc_old + _dot(p.astype(BF16), with_ones(v_blk, h))

    half_t = tile // 2
    start = pl.multiple_of(qi * tile, tile)
    k_blk = k_ref[0, pl.ds(start, tile), :]
    v_blk = v_ref[0, pl.ds(start, tile), :]
    for h in range(2):
        for rows, keys in ((slice(0, half_t), half_t), (slice(half_t, tile), tile)):
            r_id = rows.start + lax.broadcasted_iota(jnp.int32, (half_t, keys), 0)
            c_id = lax.broadcasted_iota(jnp.int32, (half_t, keys), 1)
            c_row = ct_ref[0, 2 * hp + h, pl.ds(qi, 1), :keys]
            s = _dot_nt(qh[h][rows], k_blk[:keys]) - c_row
            s = jnp.where(c_id <= r_id, s, -jnp.inf)
            m_new = jnp.broadcast_to(jnp.max(s, axis=1, keepdims=True), (half_t, LANES))
            p = jnp.exp(s - jnp.concatenate([m_new] * (keys // LANES), axis=1))
            m_sc[h, rows] = m_new
            acc_sc[h, rows] = _dot(p.astype(BF16), with_ones(v_blk[:keys], h))

    def off_diagonal(kbs, heads=(0, 1)):
        state = {h: (m_sc[h], acc_sc[h]) for h in heads}
        for kb in kbs:
            start = pl.multiple_of(kb * tile, tile)
            k_blk = k_ref[0, pl.ds(start, tile), :]
            v_blk = v_ref[0, pl.ds(start, tile), :]
            state = {h: head_step(h, kb, k_blk, v_blk, *state[h]) for h in heads}
        for h in heads:
            m_sc[h] = state[h][0]
            acc_sc[h] = state[h][1]

    jrow = lax.broadcasted_iota(jnp.int32, (nq, LANES), 0)
    qf = q2.astype(F32)
    j_start = []
    for h in range(2):
        m_low = jnp.min(jnp.min(m_sc[h], axis=1, keepdims=True), axis=0, keepdims=True)
        c_last = ct_ref[0, 2 * hp + h, :, tile - 1:tile]
        bound = jnp.sqrt(head_sqnorm_max(qf, h) * kmax_sc[h]) - c_last
        need = (bound - m_low > -EXP_UNDERFLOW) & (jrow < qi)
        j_first = jnp.min(jnp.min(jnp.where(need, jrow, qi), axis=1, keepdims=True),
                          axis=0, keepdims=True)
        j_start.append(j_first[0, 0])
    j_both = jnp.maximum(j_start[0], j_start[1])
    count = qi - j_both

    def pair(j, carry):
        off_diagonal((j_both + 2 * j, j_both + 2 * j + 1))
        return carry

    lax.fori_loop(0, count // 2, pair, 0)

    @pl.when(count % 2 == 1)
    def _():
        off_diagonal((qi - 1,))

    for h in range(2):
        def single(j, carry, h=h):
            off_diagonal((j,), heads=(h,))
            return carry

        lax.fori_loop(j_start[h], j_both, single, 0)

    a0 = acc_sc[0]
    a1 = acc_sc[1]
    half = FOX_HEAD_DIM
    o = jnp.where(first, a0 / pltpu.roll(a0, half, 1), a1 / pltpu.roll(a1, half, 1))
    o_ref[0] = o.astype(o_ref.dtype)


def _fox_attention(fq, fk, fv, ct4, batch, seq):
    t = ATT_TILE
    nq = seq // t
    q3 = fq.reshape(batch, seq, FOX_WIDTH)
    k3 = fk.reshape(batch, seq, FOX_WIDTH)
    v3 = fv.reshape(batch, seq, FOX_WIDTH)
    out = pl.pallas_call(
        functools.partial(_fox_kernel, tile=t, nq=nq),
        grid=(batch, FOX_HEADS // 2, nq),
        in_specs=[
            pl.BlockSpec((1, t, LANES), lambda b, j, i: (b, i, j)),
            pl.BlockSpec((1, seq, LANES), lambda b, j, i: (b, 0, j)),
            pl.BlockSpec((1, seq, LANES), lambda b, j, i: (b, 0, j)),
            pl.BlockSpec((1, 8, nq, t), lambda b, j, i: (b, 0, 0, 0)),
        ],
        out_specs=pl.BlockSpec((1, t, LANES), lambda b, j, i: (b, i, j)),
        out_shape=jax.ShapeDtypeStruct((batch, seq, FOX_WIDTH), BF16),
        scratch_shapes=[pltpu.VMEM((2, t, LANES), F32), pltpu.VMEM((2, t, LANES), F32),
                        pltpu.VMEM((2, nq, LANES), F32)],
        compiler_params=pltpu.CompilerParams(
            dimension_semantics=("parallel", "parallel", "arbitrary"),
            vmem_limit_bytes=VMEM_LIMIT),
        name="fox_attention",
    )(q3, k3, v3, ct4)
    return out.reshape(batch * seq, FOX_WIDTH)


def _ret_kernel(q_ref, k_ref, v_ref, g_ref, cos_ref, sin_ref, dmat_ref, qdec_ref, kdec_ref,
                sdec_ref, o_ref, state_sc):
    si = pl.program_id(1)

    @pl.when(si == 0)
    def _():
        state_sc[...] = jnp.zeros_like(state_sc)

    cos2 = cos_ref[...]
    sin2 = sin_ref[...]
    dk = RET_HEAD_DIM

    def rot(xf):
        return xf * cos2 + pltpu.roll(xf, dk // 2, 1) * sin2

    for h in range(RET_HEADS):
        cols = slice(h * dk, (h + 1) * dk)
        q = rot(q_ref[0, :, cols].astype(F32))
        k = rot(k_ref[0, :, cols].astype(F32)) * (dk ** -0.5)
        v = v_ref[0, :, cols]
        scores = _dot_nt(q.astype(BF16), k.astype(BF16)) * dmat_ref[h]
        intra = _dot(scores.astype(BF16), v)
        state = state_sc[h]
        cross = _dot((q * qdec_ref[h]).astype(BF16), state.astype(BF16))
        out = intra + cross
        state_sc[h] = state * sdec_ref[h, 0:1, :] + _dot_tn((k * kdec_ref[h]).astype(BF16), v)

        y = out * lax.rsqrt(jnp.mean(out * out, axis=-1, keepdims=True) + RMS_EPS)
        g = g_ref[0, :, cols].astype(F32)
        o_ref[0, :, cols] = (y * (g * jax.nn.sigmoid(g))).astype(o_ref.dtype)


def _ret_tables(seq):
    half = RET_HEAD_DIM // 2
    inv_freq = 1.0 / (ROPE_BASE ** (jnp.arange(half, dtype=F32) / half))
    ang = jnp.arange(seq, dtype=F32)[:, None] * inv_freq[None, :]
    cos = jnp.cos(ang)
    sin = jnp.sin(ang)
    cos2 = jnp.concatenate([cos, cos], axis=1)
    sin2 = jnp.concatenate([-sin, sin], axis=1)
    lt = RET_TILE
    log_gamma = jnp.log(1.0 - 2.0 ** (-5.0 - jnp.arange(RET_HEADS, dtype=F32)))
    idx = jnp.arange(lt)
    t = idx[:, None]
    s = idx[None, :]
    same = (t // CHUNK) == (s // CHUNK)
    earlier = (s // CHUNK) < (t // CHUNK)
    dist = jnp.where(same, jnp.abs(t - s), t - s).astype(F32)
    dmat = jnp.where((same | earlier)[None], jnp.exp(log_gamma[:, None, None] * dist[None]), 0.0)
    idxf = idx.astype(F32)
    qdec = jnp.exp(log_gamma[:, None] * idxf[None, :])
    kdec = jnp.exp(log_gamma[:, None] * (lt - idxf)[None, :])
    sdec = jnp.exp(log_gamma * lt)
    qdec = jnp.broadcast_to(qdec[:, :, None], (RET_HEADS, lt, LANES))
    kdec = jnp.broadcast_to(kdec[:, :, None], (RET_HEADS, lt, LANES))
    sdec = jnp.broadcast_to(sdec[:, None, None], (RET_HEADS, 8, LANES))
    return cos2, sin2, dmat, qdec, kdec, sdec


def _retention(rq, rk, rv, rg, tables, batch, seq):
    lt = RET_TILE
    ns = seq // lt
    cos2, sin2, dmat, qdec, kdec, sdec = tables
    blk = pl.BlockSpec((1, lt, RET_WIDTH), lambda b, i: (b, i, 0))
    tab = pl.BlockSpec((lt, LANES), lambda b, i: (i, 0))
    args = [a.reshape(batch, seq, RET_WIDTH) for a in (rq, rk, rv, rg)]
    out = pl.pallas_call(
        _ret_kernel,
        grid=(batch, ns),
        in_specs=[blk, blk, blk, blk, tab, tab,
                  pl.BlockSpec((RET_HEADS, lt, lt), lambda b, i: (0, 0, 0)),
                  pl.BlockSpec((RET_HEADS, lt, LANES), lambda b, i: (0, 0, 0)),
                  pl.BlockSpec((RET_HEADS, lt, LANES), lambda b, i: (0, 0, 0)),
                  pl.BlockSpec((RET_HEADS, 8, LANES), lambda b, i: (0, 0, 0))],
        out_specs=blk,
        out_shape=jax.ShapeDtypeStruct((batch, seq, RET_WIDTH), BF16),
        scratch_shapes=[pltpu.VMEM((RET_HEADS, RET_HEAD_DIM, RET_HEAD_DIM), F32)],
        compiler_params=pltpu.CompilerParams(
            dimension_semantics=("parallel", "arbitrary"),
            vmem_limit_bytes=VMEM_LIMIT),
        name="retention",
    )(*args, cos2, sin2, dmat, qdec, kdec, sdec)
    return out.reshape(batch * seq, RET_WIDTH)


def _outproj_kernel(fox_ref, ret_ref, x_ref, wo_ref, nw_ref, wr_ref, br_ref, xo_ref, xg_ref,
                    eid_ref, rw_ref):
    mixed = jnp.concatenate([fox_ref[...], ret_ref[...]], axis=1)
    x = x_ref[...] + _dot(mixed, wo_ref[...])
    xo_ref[...] = x
    _tm_store(xg_ref, 0, x.shape[0], x)
    h = _rms(x, nw_ref[...]).astype(BF16)
    lt = (_dot(h, wr_ref[...]) + br_ref[...]).T
    tm = lt.shape[1]
    rowid = lax.broadcasted_iota(jnp.int32, (8, tm), 0)
    neg = -jnp.inf

    def top1(v):
        vmax = jnp.max(v, axis=0, keepdims=True)
        idx = jnp.min(jnp.where(v == vmax, rowid, 8), axis=0, keepdims=True)
        return vmax, idx

    gl = jnp.where(rowid < N_GROUPS, lt[0:8], neg)
    gmax, gidx = top1(gl)
    g_w = 1.0 / jnp.sum(jnp.exp(gl - gmax), axis=0, keepdims=True)
    e_in = jnp.zeros((8, tm), F32)
    for g in range(N_GROUPS):
        e_in = jnp.where(gidx == g, lt[8 + 8 * g:16 + 8 * g], e_in)
    v1, i1 = top1(e_in)
    rest = jnp.where(rowid == i1, neg, e_in)
    v2, i2 = top1(rest)
    t = jnp.exp(v2 - v1)
    w1 = g_w / (1.0 + t)
    eid_ref[0:1, :] = gidx * EXPERTS_PER_GROUP + i1
    eid_ref[1:2, :] = gidx * EXPERTS_PER_GROUP + i2
    wslab = jnp.concatenate([w1, w1 * t, jnp.zeros((LANES - TOP_K, tm), F32)], axis=0)
    rw_ref[...] = wslab.T


def _outproj(fox, ret, x, wo, nw, wr, br):
    n = x.shape[0]
    tm = ROW_TILE
    row = pl.BlockSpec((tm, D_MODEL), lambda i: (i, 0))
    half = pl.BlockSpec((tm, 512), lambda i: (i, 0))
    pair = pl.BlockSpec((TOP_K, tm), lambda i: (0, i))
    wts = pl.BlockSpec((tm, LANES), lambda i: (i, 0))
    return pl.pallas_call(
        _outproj_kernel,
        grid=(n // tm,),
        in_specs=[half, half, row,
                  pl.BlockSpec((D_MODEL, D_MODEL), lambda i: (0, 0)),
                  pl.BlockSpec((1, D_MODEL), lambda i: (0, 0)),
                  pl.BlockSpec((D_MODEL, LANES), lambda i: (0, 0)),
                  pl.BlockSpec((1, LANES), lambda i: (0, 0))],
        out_specs=[row, pl.BlockSpec((tm * CHUNKS, LANES), lambda i: (i, 0)), pair, wts],
        out_shape=[jax.ShapeDtypeStruct((n, D_MODEL), F32),
                   jax.ShapeDtypeStruct((n * CHUNKS, LANES), F32),
                   jax.ShapeDtypeStruct((TOP_K, n), jnp.int32),
                   jax.ShapeDtypeStruct((n, LANES), F32)],
        compiler_params=pltpu.CompilerParams(
            dimension_semantics=("parallel",), vmem_limit_bytes=VMEM_LIMIT),
        name="outproj",
    )(fox, ret, x, wo, nw, wr, br)


_FIRST, _LAST, _VALID, _NEWEXP = 1, 2, 4, 8


def _moe_kernel(blk_ref, exp_ref, flag_ref, starts_ref,
                tokc_ref, tokn_ref, dstp_ref, dstc_ref, x_hbm, nw_ref, wg_ref, wu_ref, wd_ref,
                y_hbm, xbuf, ybuf, hbuf, wgb, wub, wdb, gsem, ssem, *, tb, nb):
    w = pl.program_id(0)
    b = blk_ref[w]
    e = exp_ref[w]
    flags = flag_ref[w]
    slot = b % 2
    nslot = 1 - slot
    span = tb * CHUNKS

    def hbm_row(ref, idx):
        return ref.at[pl.ds(pl.multiple_of(idx * CHUNKS, CHUNKS), CHUNKS), :]

    def buf_row(buf, s, r):
        return buf.at[pl.ds(pl.multiple_of(s * span + r * CHUNKS, CHUNKS), CHUNKS), :]

    def start_gather(tok_ref, s):
        for r in range(tb):
            pltpu.make_async_copy(hbm_row(x_hbm, tok_ref[0, 0, r]), buf_row(xbuf, s, r),
                                  gsem.at[s]).start()

    def start_scatter(dst_ref, s):
        for r in range(tb):
            pltpu.make_async_copy(buf_row(ybuf, s, r), hbm_row(y_hbm, dst_ref[0, 0, r]),
                                  ssem.at[s]).start()

    def wait_rows(sem, s):
        whole = pl.ds(pl.multiple_of(s * span, span), span)
        pltpu.make_async_copy(xbuf.at[whole, :], ybuf.at[whole, :], sem.at[s]).wait()

    @pl.when((flags & _FIRST) != 0)
    def _():
        @pl.when(w == 0)
        def _():
            start_gather(tokc_ref, 0)

        wait_rows(gsem, slot)

        @pl.when(b >= 2)
        def _():
            wait_rows(ssem, slot)

        @pl.when(b + 1 < nb)
        def _():
            start_gather(tokn_ref, nslot)

        @pl.when(b >= 1)
        def _():
            start_scatter(dstp_ref, nslot)

        hbuf[...] = _rms(_tm_load(xbuf, slot * span, tb), nw_ref[...]).astype(BF16)
        ybuf[pl.ds(pl.multiple_of(slot * span, span), span), :] = jnp.zeros((span, LANES), F32)

    @pl.when((flags & _NEWEXP) != 0)
    def _():
        wgb[...] = wg_ref[0, 0].astype(BF16)
        wub[...] = wu_ref[0, 0].astype(BF16)
        wdb[...] = wd_ref[0, 0].astype(BF16)

    @pl.when((flags & _VALID) != 0)
    def _():
        h = hbuf[...]
        g = _dot(h, wgb[...])
        u = _dot(h, wub[...])
        a = (g * jax.nn.sigmoid(g) * u).astype(BF16)
        y = _dot(a, wdb[...])
        q = b * tb + lax.broadcasted_iota(jnp.int32, y.shape, 0)
        mine = (q >= starts_ref[e]) & (q < starts_ref[e + 1])
        _tm_store(ybuf, slot * span, tb, jnp.where(mine, y, _tm_load(ybuf, slot * span, tb)))

    @pl.when(((flags & _LAST) != 0) & (b == nb - 1))
    def _():
        start_scatter(dstc_ref, slot)
        if nb >= 2:
            wait_rows(ssem, nslot)
        wait_rows(ssem, slot)


def _moe(x, s_tok, s_dst, blk, exp, flags, starts, nw, wg, wu, wd, layer, tb=MOE_TILE):
    na = s_tok.shape[0]
    nb = na // tb
    d_model, d_expert = wg.shape[2], wg.shape[3]
    tok3 = s_tok.reshape(nb, 1, tb)
    dst3 = s_dst.reshape(nb, 1, tb)
    smem_blk = lambda f: pl.BlockSpec((1, 1, tb), f, memory_space=pltpu.SMEM)
    grid_spec = pltpu.PrefetchScalarGridSpec(
        num_scalar_prefetch=4,
        grid=(blk.shape[0],),
        in_specs=[
            smem_blk(lambda w, bl, ex, fl, st: (bl[w], 0, 0)),
            smem_blk(lambda w, bl, ex, fl, st: (jnp.minimum(bl[w] + 1, nb - 1), 0, 0)),
            smem_blk(lambda w, bl, ex, fl, st: (jnp.maximum(bl[w] - 1, 0), 0, 0)),
            smem_blk(lambda w, bl, ex, fl, st: (bl[w], 0, 0)),
            pl.BlockSpec(memory_space=pl.ANY),
            pl.BlockSpec((1, d_model), lambda w, bl, ex, fl, st: (0, 0)),
            pl.BlockSpec((1, 1, d_model, d_expert), lambda w, bl, ex, fl, st: (layer, ex[w], 0, 0)),
            pl.BlockSpec((1, 1, d_model, d_expert), lambda w, bl, ex, fl, st: (layer, ex[w], 0, 0)),
            pl.BlockSpec((1, 1, d_expert, d_model), lambda w, bl, ex, fl, st: (layer, ex[w], 0, 0)),
        ],
        out_specs=pl.BlockSpec(memory_space=pl.ANY),
        scratch_shapes=[pltpu.VMEM((2 * tb * CHUNKS, LANES), F32),
                        pltpu.VMEM((2 * tb * CHUNKS, LANES), F32),
                        pltpu.VMEM((tb, d_model), BF16),
                        pltpu.VMEM((d_model, d_expert), BF16), pltpu.VMEM((d_model, d_expert), BF16),
                        pltpu.VMEM((d_expert, d_model), BF16),
                        pltpu.SemaphoreType.DMA((2,)), pltpu.SemaphoreType.DMA((2,))],
    )
    return pl.pallas_call(
        functools.partial(_moe_kernel, tb=tb, nb=nb),
        grid_spec=grid_spec,
        out_shape=jax.ShapeDtypeStruct((na * CHUNKS, LANES), F32),
        compiler_params=pltpu.CompilerParams(
            dimension_semantics=("arbitrary",), vmem_limit_bytes=VMEM_LIMIT),
        name="moe_experts",
    )(blk, exp, flags, starts, tok3, tok3, dst3, dst3, x, nw, wg, wu, wd)


def _dispatch_plan(eid, n, tb=MOE_TILE, n_experts=N_EXPERTS):
    na = TOP_K * n
    nb = na // tb
    eid_flat = eid.reshape(na)
    _, s_a = lax.sort((eid_flat, jnp.arange(na, dtype=jnp.int32)), num_keys=1)
    s_tok = s_a % n
    counts = jnp.sum(eid_flat[None, :] == jnp.arange(n_experts, dtype=jnp.int32)[:, None], axis=1)
    starts = jnp.concatenate([jnp.zeros((1,), jnp.int32),
                              jnp.cumsum(counts).astype(jnp.int32)])
    lo, hi = starts[:-1], starts[1:]
    nonempty = hi > lo
    first_blk = lo // tb
    npass = jnp.where(nonempty, (hi - 1) // tb - first_blk + 1, 0)
    cum = jnp.cumsum(npass)
    total = cum[-1]
    n_pass = nb + n_experts
    w = jnp.arange(n_pass, dtype=jnp.int32)
    wc = jnp.minimum(w, total - 1)
    ex = jnp.sum(cum[None, :] <= wc[:, None], axis=1).astype(jnp.int32)
    sel = ex[:, None] == jnp.arange(n_experts, dtype=jnp.int32)[None, :]
    pick = lambda v: jnp.sum(jnp.where(sel, v[None, :], 0), axis=1)
    blk = (pick(first_blk) + (wc - pick(cum - npass))).astype(jnp.int32)
    valid = w < total
    prev_blk = jnp.concatenate([jnp.full((1,), -1, jnp.int32), blk[:-1]])
    next_blk = jnp.concatenate([blk[1:], jnp.full((1,), -1, jnp.int32)])
    first = valid & (blk != prev_blk)
    last = valid & ((blk != next_blk) | (w == total - 1))
    prev_ex = jnp.concatenate([jnp.full((1,), -1, jnp.int32), ex[:-1]])
    newexp = valid & (ex != prev_ex)
    flags = (first * _FIRST + last * _LAST + valid * _VALID + newexp * _NEWEXP).astype(jnp.int32)
    return s_tok, s_a, blk, ex, flags, starts


def _final_kernel(x_ref, y0_ref, y1_ref, rw_ref, nw_ref, o_ref):
    rw = rw_ref[...]
    tm = x_ref.shape[0]
    x = x_ref[...] + (rw[:, 0:1] * _tm_load(y0_ref, 0, tm) + rw[:, 1:2] * _tm_load(y1_ref, 0, tm))
    o_ref[...] = _rms(x, nw_ref[...])


def _final(x, y, rw, nw):
    n = x.shape[0]
    tm = ROW_TILE
    nt = n // tm
    row = pl.BlockSpec((tm, D_MODEL), lambda i: (i, 0))
    return pl.pallas_call(
        _final_kernel,
        grid=(nt,),
        in_specs=[row, pl.BlockSpec((tm * CHUNKS, LANES), lambda i: (i, 0)),
                  pl.BlockSpec((tm * CHUNKS, LANES), lambda i: (i + nt, 0)),
                  pl.BlockSpec((tm, LANES), lambda i: (i, 0)),
                  pl.BlockSpec((1, D_MODEL), lambda i: (0, 0))],
        out_specs=row,
        out_shape=jax.ShapeDtypeStruct((n, D_MODEL), F32),
        compiler_params=pltpu.CompilerParams(
            dimension_semantics=("parallel",), vmem_limit_bytes=VMEM_LIMIT),
        name="final_norm",
    )(x, y, y, rw, nw)


def kernel(x, norm_mix_w, w_in, fox_forget_b, w_out, norm_ffn_w, w_router_group, b_router_group,
           w_router_expert, b_router_expert, w_expert_gate, w_expert_up, w_expert_down,
           norm_final_w):
    batch, seq, d = x.shape
    n = batch * seq
    depth = w_in.shape[0]
    xf = x.reshape(n, d)
    tables = _ret_tables(seq)
    tri = jnp.tril(jnp.ones((ROW_TILE, ROW_TILE), F32)).astype(BF16)
    nq = seq // ATT_TILE

    y = rw = None
    for layer in range(depth):
        wl = w_in[layer]
        c0 = 3 * FOX_WIDTH
        w_main = jnp.concatenate([wl[:, :c0], wl[:, c0 + FOX_HEADS:]], axis=1).astype(BF16)
        w_ff = jnp.pad(wl[:, c0:c0 + FOX_HEADS], ((0, 0), (0, LANES - FOX_HEADS))).astype(BF16)
        b_ff = jnp.pad(fox_forget_b[layer], (0, LANES - FOX_HEADS)).reshape(1, LANES)
        xf, (fq, fk, fv, rq, rk, rv, rg, ct) = _inproj(
            xf, y, rw, norm_mix_w[layer].reshape(1, d), w_main, w_ff, b_ff, tri, seq)
        ct4 = ct.reshape(batch, 8, nq, ATT_TILE)
        fox = _fox_attention(fq, fk, fv, ct4, batch, seq)
        ret = _retention(rq, rk, rv, rg, tables, batch, seq)

        zpad = jnp.zeros((d, N_GROUPS), F32)
        w_r = jnp.concatenate([w_router_group[layer], zpad, w_router_expert[layer]], axis=1)
        nr = 2 * N_GROUPS + N_EXPERTS
        w_r = jnp.pad(w_r, ((0, 0), (0, LANES - nr))).astype(BF16)
        b_r = jnp.concatenate([b_router_group[layer], jnp.zeros((N_GROUPS,), F32),
                               b_router_expert[layer]])
        b_r = jnp.pad(b_r, (0, LANES - nr)).reshape(1, LANES)
        xf, xg, eid, rw = _outproj(fox, ret, xf, w_out[layer].astype(BF16),
                                   norm_ffn_w[layer].reshape(1, d), w_r, b_r)
        s_tok, s_dst, blk, ex, flags, starts = _dispatch_plan(eid, n)
        y = _moe(xg, s_tok, s_dst, blk, ex, flags, starts, norm_ffn_w[layer].reshape(1, d),
                 w_expert_gate, w_expert_up, w_expert_down, layer)
    out = _final(xf, y, rw, norm_final_w.reshape(1, d))
    return out.reshape(batch, seq, d)
```

```python
import functools

import jax
import jax.numpy as jnp
import numpy as np
from jax import lax
from jax.experimental import pallas as pl
from jax.experimental.pallas import tpu as pltpu

F32 = jnp.float32
BF16 = jnp.bfloat16

D_MODEL = 1024
FOX_HEADS = 8
FOX_HEAD_DIM = 64
FOX_WIDTH = 512
RET_HEADS = 4
RET_HEAD_DIM = 128
RET_WIDTH = 512
CHUNK = 64
ROPE_BASE = 10000.0
N_GROUPS = 4
EXPERTS_PER_GROUP = 8
N_EXPERTS = 32
TOP_K = 2
D_EXPERT = 512
RMS_EPS = 1e-6

LANES = 128
VMEM_LIMIT = 56 * 1024 * 1024

ROW_TILE = 512
ATT_TILE = 512
RET_TILE = 256
MOE_TILE = 256
N_MAIN = 7 * 512
EXP_UNDERFLOW = 110.0


def _rms(xf, w):
    return xf * lax.rsqrt(jnp.mean(xf * xf, axis=-1, keepdims=True) + RMS_EPS) * w


def _dot(a, b):
    return jnp.dot(a, b, preferred_element_type=F32)


def _dot_nt(a, b):
    return lax.dot_general(a, b, (((1,), (1,)), ((), ())), preferred_element_type=F32)


def _dot_tn(a, b):
    return lax.dot_general(a, b, (((0,), (0,)), ((), ())), preferred_element_type=F32)


CHUNKS = D_MODEL // LANES


def _tm_load(ref, base, rows):
    return jnp.concatenate([ref[pl.ds(base + c, rows, stride=CHUNKS), :] for c in range(CHUNKS)],
                           axis=1)


def _tm_store(ref, base, rows, val):
    for c in range(CHUNKS):
        ref[pl.ds(base + c, rows, stride=CHUNKS), :] = val[:, c * LANES:(c + 1) * LANES]


def _inproj_kernel(*refs, has_y, tiles_per_seq):
    if has_y:
        x_ref, y0_ref, y1_ref, rw_ref = refs[:4]
        refs = refs[4:]
    else:
        x_ref = refs[0]
        refs = refs[1:]
    nw_ref, w_ref, wff_ref, bff_ref, tri_ref = refs[:5]
    refs = refs[5:]
    if has_y:
        xres_ref = refs[0]
        refs = refs[1:]
    fq_ref, fk_ref, fv_ref, rq_ref, rk_ref, rv_ref, rg_ref, ct_ref, carry_sc = refs

    i = pl.program_id(0)
    x = x_ref[...]
    if has_y:
        rw = rw_ref[...]
        tm = x.shape[0]
        x = x + (rw[:, 0:1] * _tm_load(y0_ref, 0, tm) + rw[:, 1:2] * _tm_load(y1_ref, 0, tm))
        xres_ref[...] = x
    h = _rms(x, nw_ref[...]).astype(BF16)

    outs = (fq_ref, fk_ref, fv_ref, rq_ref, rk_ref, rv_ref, rg_ref)
    for j, o_ref in enumerate(outs):
        acc = _dot(h, w_ref[:, j * 512:(j + 1) * 512])
        if j == 0:
            acc = acc * (FOX_HEAD_DIM ** -0.5)
        o_ref[...] = acc.astype(BF16)

    z = _dot(h, wff_ref[...]) + bff_ref[...]
    lf = jnp.minimum(z, 0.0) - jnp.log1p(jnp.exp(-jnp.abs(z)))
    lane = lax.broadcasted_iota(jnp.int32, lf.shape, 1)
    lf = jnp.where(lane < FOX_HEADS, lf, 0.0)
    hi = lf.astype(BF16)
    r1 = lf - hi.astype(F32)
    mid = r1.astype(BF16)
    lo = (r1 - mid.astype(F32)).astype(BF16)
    tri = tri_ref[...]
    cs = _dot(tri, hi) + _dot(tri, mid) + _dot(tri, lo)

    @pl.when(i % tiles_per_seq == 0)
    def _():
        carry_sc[...] = jnp.zeros_like(carry_sc)

    c = cs + carry_sc[0:1, :]
    carry_sc[...] = jnp.broadcast_to(c[-1:, :], carry_sc.shape)
    ct_ref[0] = c.T[:8, :]


def _inproj(x, y, rw, nw, w_main, w_ff, b_ff, tri, seq):
    n = x.shape[0]
    tm = ROW_TILE
    nt = n // tm
    tps = seq // tm
    has_y = y is not None
    row_spec = pl.BlockSpec((tm, D_MODEL), lambda i: (i, 0))
    in_specs = [row_spec]
    args = [x]
    if has_y:
        in_specs += [pl.BlockSpec((tm * CHUNKS, LANES), lambda i: (i, 0)),
                     pl.BlockSpec((tm * CHUNKS, LANES), lambda i: (i + nt, 0)),
                     pl.BlockSpec((tm, LANES), lambda i: (i, 0))]
        args += [y, y, rw]
    in_specs += [
        pl.BlockSpec((1, D_MODEL), lambda i: (0, 0)),
        pl.BlockSpec((D_MODEL, N_MAIN), lambda i: (0, 0)),
        pl.BlockSpec((D_MODEL, LANES), lambda i: (0, 0)),
        pl.BlockSpec((1, LANES), lambda i: (0, 0)),
        pl.BlockSpec((tm, tm), lambda i: (0, 0)),
    ]
    args += [nw, w_main, w_ff, b_ff, tri]
    half_spec = pl.BlockSpec((tm, 512), lambda i: (i, 0))
    out_shape = []
    out_specs = []
    if has_y:
        out_shape.append(jax.ShapeDtypeStruct((n, D_MODEL), F32))
        out_specs.append(row_spec)
    out_shape += [jax.ShapeDtypeStruct((n, 512), BF16)] * 7
    out_specs += [half_spec] * 7
    out_shape.append(jax.ShapeDtypeStruct((n // seq, 8, seq), F32))
    out_specs.append(pl.BlockSpec((1, 8, tm), lambda i: (i // tps, 0, i % tps)))
    outs = pl.pallas_call(
        functools.partial(_inproj_kernel, has_y=has_y, tiles_per_seq=tps),
        grid=(nt,),
        in_specs=in_specs,
        out_specs=out_specs,
        out_shape=out_shape,
        scratch_shapes=[pltpu.VMEM((8, LANES), F32)],
        compiler_params=pltpu.CompilerParams(
            dimension_semantics=("arbitrary",), vmem_limit_bytes=VMEM_LIMIT),
        name="inproj_y" if has_y else "inproj",
    )(*args)
    if has_y:
        return outs[0], outs[1:]
    return x, outs


def _fox_kernel(q_ref, k_ref, v_ref, ct_ref, o_ref, m_sc, acc_sc, kmax_sc, *, tile, nq):
    hp = pl.program_id(1)
    qi = pl.program_id(2)
    q2 = q_ref[0]
    lane = lax.broadcasted_iota(jnp.int32, q2.shape, 1)
    first = lane < FOX_HEAD_DIM
    zero = jnp.zeros_like(q2)
    qh = (jnp.where(first, q2, zero), jnp.where(first, zero, q2))
    reps = tile // LANES

    def head_sqnorm_max(xf, h):
        sq = xf * xf
        sq = jnp.where(first, sq, 0.0) if h == 0 else jnp.where(first, 0.0, sq)
        return jnp.max(jnp.sum(sq, axis=1, keepdims=True), axis=0, keepdims=True)

    @pl.when(qi == 0)
    def _():
        for j in range(nq):
            kf = k_ref[0, j * tile:(j + 1) * tile, :].astype(F32)
            for h in range(2):
                kmax_sc[h, j:j + 1, :] = jnp.broadcast_to(head_sqnorm_max(kf, h), (1, LANES))

    def head_step(h, kb, k_blk, v_blk, mask, m_old, acc_old):
        one = jnp.ones_like(v_blk)
        va = jnp.where(first, v_blk, one) if h == 0 else jnp.where(first, one, v_blk)
        s = _dot_nt(qh[h], k_blk) - ct_ref[0, 2 * hp + h, pl.ds(kb, 1), :]
        if mask is not None:
            s = jnp.where(mask, s, -jnp.inf)
        m_cur = jnp.max(s, axis=1, keepdims=True)
        if m_old is None:
            m_new = jnp.broadcast_to(m_cur, (tile, LANES))
            p = jnp.exp(s - jnp.concatenate([m_new] * reps, axis=1))
            acc = _dot(p.astype(BF16), va)
        else:
            m_new = jnp.maximum(m_old, m_cur)
            alpha = jnp.exp(m_old - m_new)
            p = jnp.exp(s - jnp.concatenate([m_new] * reps, axis=1))
            acc = alpha * acc_old + _dot(p.astype(BF16), va)
        return m_new, acc

    row = lax.broadcasted_iota(jnp.int32, (tile, tile), 0)
    col = lax.broadcasted_iota(jnp.int32, (tile, tile), 1)
    start = pl.multiple_of(qi * tile, tile)
    k_blk = k_ref[0, pl.ds(start, tile), :]
    v_blk = v_ref[0, pl.ds(start, tile), :]
    for h in range(2):
        m_new, acc = head_step(h, qi, k_blk, v_blk, col <= row, None, None)
        m_sc[h] = m_new
        acc_sc[h] = acc

    def off_diagonal(kbs, heads=(0, 1)):
        state = {h: (m_sc[h], acc_sc[h]) for h in heads}
        for kb in kbs:
            start = pl.multiple_of(kb * tile, tile)
            k_blk = k_ref[0, pl.ds(start, tile), :]
            v_blk = v_ref[0, pl.ds(start, tile), :]
            state = {h: head_step(h, kb, k_blk, v_blk, None, *state[h]) for h in heads}
        for h in heads:
            m_sc[h] = state[h][0]
            acc_sc[h] = state[h][1]

    jrow = lax.broadcasted_iota(jnp.int32, (nq, LANES), 0)
    qf = q2.astype(F32)
    j_start = []
    for h in range(2):
        m_low = jnp.min(jnp.min(m_sc[h], axis=1, keepdims=True), axis=0, keepdims=True)
        c_last = ct_ref[0, 2 * hp + h, :, tile - 1:tile]
        bound = jnp.sqrt(head_sqnorm_max(qf, h) * kmax_sc[h]) - c_last
        need = (bound - m_low > -EXP_UNDERFLOW) & (jrow < qi)
        j_first = jnp.min(jnp.min(jnp.where(need, jrow, qi), axis=1, keepdims=True),
                          axis=0, keepdims=True)
        j_start.append(j_first[0, 0])
    j_both = jnp.maximum(j_start[0], j_start[1])
    count = qi - j_both

    def pair(j, carry):
        off_diagonal((j_both + 2 * j, j_both + 2 * j + 1))
        return carry

    lax.fori_loop(0, count // 2, pair, 0)

    @pl.when(count % 2 == 1)
    def _():
        off_diagonal((qi - 1,))

    for h in range(2):
        def single(j, carry, h=h):
            off_diagonal((j,), heads=(h,))
            return carry

        lax.fori_loop(j_start[h], j_both, single, 0)

    a0 = acc_sc[0]
    a1 = acc_sc[1]
    half = FOX_HEAD_DIM
    o = jnp.where(first, a0 / pltpu.roll(a0, half, 1), a1 / pltpu.roll(a1, half, 1))
    o_ref[0] = o.astype(o_ref.dtype)


def _fox_attention(fq, fk, fv, ct4, batch, seq):
    t = ATT_TILE
    nq = seq // t
    q3 = fq.reshape(batch, seq, FOX_WIDTH)
    k3 = fk.reshape(batch, seq, FOX_WIDTH)
    v3 = fv.reshape(batch, seq, FOX_WIDTH)
    out = pl.pallas_call(
        functools.partial(_fox_kernel, tile=t, nq=nq),
        grid=(batch, FOX_HEADS // 2, nq),
        in_specs=[
            pl.BlockSpec((1, t, LANES), lambda b, j, i: (b, i, j)),
            pl.BlockSpec((1, seq, LANES), lambda b, j, i: (b, 0, j)),
            pl.BlockSpec((1, seq, LANES), lambda b, j, i: (b, 0, j)),
            pl.BlockSpec((1, 8, nq, t), lambda b, j, i: (b, 0, 0, 0)),
        ],
        out_specs=pl.BlockSpec((1, t, LANES), lambda b, j, i: (b, i, j)),
        out_shape=jax.ShapeDtypeStruct((batch, seq, FOX_WIDTH), BF16),
        scratch_shapes=[pltpu.VMEM((2, t, LANES), F32), pltpu.VMEM((2, t, LANES), F32),
                        pltpu.VMEM((2, nq, LANES), F32)],
        compiler_params=pltpu.CompilerParams(
            dimension_semantics=("parallel", "parallel", "arbitrary"),
            vmem_limit_bytes=VMEM_LIMIT),
        name="fox_attention",
    )(q3, k3, v3, ct4)
    return out.reshape(batch * seq, FOX_WIDTH)


def _ret_kernel(q_ref, k_ref, v_ref, g_ref, cos_ref, sin_ref, dmat_ref, qdec_ref, kdec_ref,
                sdec_ref, o_ref, state_sc):
    si = pl.program_id(1)

    @pl.when(si == 0)
    def _():
        state_sc[...] = jnp.zeros_like(state_sc)

    cos2 = cos_ref[...]
    sin2 = sin_ref[...]
    dk = RET_HEAD_DIM

    def rot(xf):
        return xf * cos2 + pltpu.roll(xf, dk // 2, 1) * sin2

    for h in range(RET_HEADS):
        cols = slice(h * dk, (h + 1) * dk)
        q = rot(q_ref[0, :, cols].astype(F32))
        k = rot(k_ref[0, :, cols].astype(F32)) * (dk ** -0.5)
        v = v_ref[0, :, cols]
        scores = _dot_nt(q.astype(BF16), k.astype(BF16)) * dmat_ref[h]
        intra = _dot(scores.astype(BF16), v)
        state = state_sc[h]
        cross = _dot((q * qdec_ref[h]).astype(BF16), state.astype(BF16))
        out = intra + cross
        state_sc[h] = state * sdec_ref[h, 0:1, :] + _dot_tn((k * kdec_ref[h]).astype(BF16), v)

        y = out * lax.rsqrt(jnp.mean(out * out, axis=-1, keepdims=True) + RMS_EPS)
        g = g_ref[0, :, cols].astype(F32)
        o_ref[0, :, cols] = (y * (g * jax.nn.sigmoid(g))).astype(o_ref.dtype)


def _ret_tables(seq):
    half = RET_HEAD_DIM // 2
    inv_freq = 1.0 / (ROPE_BASE ** (jnp.arange(half, dtype=F32) / half))
    ang = jnp.arange(seq, dtype=F32)[:, None] * inv_freq[None, :]
    cos = jnp.cos(ang)
    sin = jnp.sin(ang)
    cos2 = jnp.concatenate([cos, cos], axis=1)
    sin2 = jnp.concatenate([-sin, sin], axis=1)
    lt = RET_TILE
    log_gamma = jnp.log(1.0 - 2.0 ** (-5.0 - jnp.arange(RET_HEADS, dtype=F32)))
    idx = jnp.arange(lt)
    t = idx[:, None]
    s = idx[None, :]
    same = (t // CHUNK) == (s // CHUNK)
    earlier = (s // CHUNK) < (t // CHUNK)
    dist = jnp.where(same, jnp.abs(t - s), t - s).astype(F32)
    dmat = jnp.where((same | earlier)[None], jnp.exp(log_gamma[:, None, None] * dist[None]), 0.0)
    idxf = idx.astype(F32)
    qdec = jnp.exp(log_gamma[:, None] * idxf[None, :])
    kdec = jnp.exp(log_gamma[:, None] * (lt - idxf)[None, :])
    sdec = jnp.exp(log_gamma * lt)
    qdec = jnp.broadcast_to(qdec[:, :, None], (RET_HEADS, lt, LANES))
    kdec = jnp.broadcast_to(kdec[:, :, None], (RET_HEADS, lt, LANES))
    sdec = jnp.broadcast_to(sdec[:, None, None], (RET_HEADS, 8, LANES))
    return cos2, sin2, dmat, qdec, kdec, sdec


def _retention(rq, rk, rv, rg, tables, batch, seq):
    lt = RET_TILE
    ns = seq // lt
    cos2, sin2, dmat, qdec, kdec, sdec = tables
    blk = pl.BlockSpec((1, lt, RET_WIDTH), lambda b, i: (b, i, 0))
    tab = pl.BlockSpec((lt, LANES), lambda b, i: (i, 0))
    args = [a.reshape(batch, seq, RET_WIDTH) for a in (rq, rk, rv, rg)]
    out = pl.pallas_call(
        _ret_kernel,
        grid=(batch, ns),
        in_specs=[blk, blk, blk, blk, tab, tab,
                  pl.BlockSpec((RET_HEADS, lt, lt), lambda b, i: (0, 0, 0)),
                  pl.BlockSpec((RET_HEADS, lt, LANES), lambda b, i: (0, 0, 0)),
                  pl.BlockSpec((RET_HEADS, lt, LANES), lambda b, i: (0, 0, 0)),
                  pl.BlockSpec((RET_HEADS, 8, LANES), lambda b, i: (0, 0, 0))],
        out_specs=blk,
        out_shape=jax.ShapeDtypeStruct((batch, seq, RET_WIDTH), BF16),
        scratch_shapes=[pltpu.VMEM((RET_HEADS, RET_HEAD_DIM, RET_HEAD_DIM), F32)],
        compiler_params=pltpu.CompilerParams(
            dimension_semantics=("parallel", "arbitrary"),
            vmem_limit_bytes=VMEM_LIMIT),
        name="retention",
    )(*args, cos2, sin2, dmat, qdec, kdec, sdec)
    return out.reshape(batch * seq, RET_WIDTH)


def _outproj_kernel(fox_ref, ret_ref, x_ref, wo_ref, nw_ref, wr_ref, br_ref, xo_ref, xg_ref,
                    eid_ref, rw_ref):
    mixed = jnp.concatenate([fox_ref[...], ret_ref[...]], axis=1)
    x = x_ref[...] + _dot(mixed, wo_ref[...])
    xo_ref[...] = x
    _tm_store(xg_ref, 0, x.shape[0], x)
    h = _rms(x, nw_ref[...]).astype(BF16)
    lt = (_dot(h, wr_ref[...]) + br_ref[...]).T
    tm = lt.shape[1]
    rowid = lax.broadcasted_iota(jnp.int32, (8, tm), 0)
    neg = -jnp.inf

    def top1(v):
        vmax = jnp.max(v, axis=0, keepdims=True)
        idx = jnp.min(jnp.where(v == vmax, rowid, 8), axis=0, keepdims=True)
        return vmax, idx

    gl = jnp.where(rowid < N_GROUPS, lt[0:8], neg)
    gmax, gidx = top1(gl)
    g_w = 1.0 / jnp.sum(jnp.exp(gl - gmax), axis=0, keepdims=True)
    e_in = jnp.zeros((8, tm), F32)
    for g in range(N_GROUPS):
        e_in = jnp.where(gidx == g, lt[8 + 8 * g:16 + 8 * g], e_in)
    v1, i1 = top1(e_in)
    rest = jnp.where(rowid == i1, neg, e_in)
    v2, i2 = top1(rest)
    t = jnp.exp(v2 - v1)
    w1 = g_w / (1.0 + t)
    eid_ref[0:1, :] = gidx * EXPERTS_PER_GROUP + i1
    eid_ref[1:2, :] = gidx * EXPERTS_PER_GROUP + i2
    wslab = jnp.concatenate([w1, w1 * t, jnp.zeros((LANES - TOP_K, tm), F32)], axis=0)
    rw_ref[...] = wslab.T


def _outproj(fox, ret, x, wo, nw, wr, br):
    n = x.shape[0]
    tm = ROW_TILE
    row = pl.BlockSpec((tm, D_MODEL), lambda i: (i, 0))
    half = pl.BlockSpec((tm, 512), lambda i: (i, 0))
    pair = pl.BlockSpec((TOP_K, tm), lambda i: (0, i))
    wts = pl.BlockSpec((tm, LANES), lambda i: (i, 0))
    return pl.pallas_call(
        _outproj_kernel,
        grid=(n // tm,),
        in_specs=[half, half, row,
                  pl.BlockSpec((D_MODEL, D_MODEL), lambda i: (0, 0)),
                  pl.BlockSpec((1, D_MODEL), lambda i: (0, 0)),
                  pl.BlockSpec((D_MODEL, LANES), lambda i: (0, 0)),
                  pl.BlockSpec((1, LANES), lambda i: (0, 0))],
        out_specs=[row, pl.BlockSpec((tm * CHUNKS, LANES), lambda i: (i, 0)), pair, wts],
        out_shape=[jax.ShapeDtypeStruct((n, D_MODEL), F32),
                   jax.ShapeDtypeStruct((n * CHUNKS, LANES), F32),
                   jax.ShapeDtypeStruct((TOP_K, n), jnp.int32),
                   jax.ShapeDtypeStruct((n, LANES), F32)],
        compiler_params=pltpu.CompilerParams(
            dimension_semantics=("parallel",), vmem_limit_bytes=VMEM_LIMIT),
        name="outproj",
    )(fox, ret, x, wo, nw, wr, br)


_FIRST, _LAST, _VALID, _NEWEXP = 1, 2, 4, 8


def _moe_kernel(blk_ref, exp_ref, flag_ref, starts_ref,
                tokc_ref, tokn_ref, dstp_ref, dstc_ref, x_hbm, nw_ref, wg_ref, wu_ref, wd_ref,
                y_hbm, xbuf, ybuf, hbuf, wgb, wub, wdb, gsem, ssem, *, tb, nb):
    w = pl.program_id(0)
    b = blk_ref[w]
    e = exp_ref[w]
    flags = flag_ref[w]
    slot = b % 2
    nslot = 1 - slot
    span = tb * CHUNKS

    def hbm_row(ref, idx):
        return ref.at[pl.ds(pl.multiple_of(idx * CHUNKS, CHUNKS), CHUNKS), :]

    def buf_row(buf, s, r):
        return buf.at[pl.ds(pl.multiple_of(s * span + r * CHUNKS, CHUNKS), CHUNKS), :]

    def start_gather(tok_ref, s):
        for r in range(tb):
            pltpu.make_async_copy(hbm_row(x_hbm, tok_ref[0, 0, r]), buf_row(xbuf, s, r),
                                  gsem.at[s]).start()

    def start_scatter(dst_ref, s):
        for r in range(tb):
            pltpu.make_async_copy(buf_row(ybuf, s, r), hbm_row(y_hbm, dst_ref[0, 0, r]),
                                  ssem.at[s]).start()

    def wait_rows(sem, s):
        whole = pl.ds(pl.multiple_of(s * span, span), span)
        pltpu.make_async_copy(xbuf.at[whole, :], ybuf.at[whole, :], sem.at[s]).wait()

    @pl.when((flags & _FIRST) != 0)
    def _():
        @pl.when(w == 0)
        def _():
            start_gather(tokc_ref, 0)

        wait_rows(gsem, slot)

        @pl.when(b >= 2)
        def _():
            wait_rows(ssem, slot)

        @pl.when(b + 1 < nb)
        def _():
            start_gather(tokn_ref, nslot)

        @pl.when(b >= 1)
        def _():
            start_scatter(dstp_ref, nslot)

        hbuf[...] = _rms(_tm_load(xbuf, slot * span, tb), nw_ref[...]).astype(BF16)
        ybuf[pl.ds(pl.multiple_of(slot * span, span), span), :] = jnp.zeros((span, LANES), F32)

    @pl.when((flags & _NEWEXP) != 0)
    def _():
        wgb[...] = wg_ref[0, 0].astype(BF16)
        wub[...] = wu_ref[0, 0].astype(BF16)
        wdb[...] = wd_ref[0, 0].astype(BF16)

    @pl.when((flags & _VALID) != 0)
    def _():
        h = hbuf[...]
        g = _dot(h, wgb[...])
        u = _dot(h, wub[...])
        a = (g * jax.nn.sigmoid(g) * u).astype(BF16)
        y = _dot(a, wdb[...])
        q = b * tb + lax.broadcasted_iota(jnp.int32, y.shape, 0)
        mine = (q >= starts_ref[e]) & (q < starts_ref[e + 1])
        _tm_store(ybuf, slot * span, tb, jnp.where(mine, y, _tm_load(ybuf, slot * span, tb)))

    @pl.when(((flags & _LAST) != 0) & (b == nb - 1))
    def _():
        start_scatter(dstc_ref, slot)
        if nb >= 2:
            wait_rows(ssem, nslot)
        wait_rows(ssem, slot)


def _moe(x, s_tok, s_dst, blk, exp, flags, starts, nw, wg, wu, wd, layer, tb=MOE_TILE):
    na = s_tok.shape[0]
    nb = na // tb
    d_model, d_expert = wg.shape[2], wg.shape[3]
    tok3 = s_tok.reshape(nb, 1, tb)
    dst3 = s_dst.reshape(nb, 1, tb)
    smem_blk = lambda f: pl.BlockSpec((1, 1, tb), f, memory_space=pltpu.SMEM)
    grid_spec = pltpu.PrefetchScalarGridSpec(
        num_scalar_prefetch=4,
        grid=(blk.shape[0],),
        in_specs=[
            smem_blk(lambda w, bl, ex, fl, st: (bl[w], 0, 0)),
            smem_blk(lambda w, bl, ex, fl, st: (jnp.minimum(bl[w] + 1, nb - 1), 0, 0)),
            smem_blk(lambda w, bl, ex, fl, st: (jnp.maximum(bl[w] - 1, 0), 0, 0)),
            smem_blk(lambda w, bl, ex, fl, st: (bl[w], 0, 0)),
            pl.BlockSpec(memory_space=pl.ANY),
            pl.BlockSpec((1, d_model), lambda w, bl, ex, fl, st: (0, 0)),
            pl.BlockSpec((1, 1, d_model, d_expert), lambda w, bl, ex, fl, st: (layer, ex[w], 0, 0)),
            pl.BlockSpec((1, 1, d_model, d_expert), lambda w, bl, ex, fl, st: (layer, ex[w], 0, 0)),
            pl.BlockSpec((1, 1, d_expert, d_model), lambda w, bl, ex, fl, st: (layer, ex[w], 0, 0)),
        ],
        out_specs=pl.BlockSpec(memory_space=pl.ANY),
        scratch_shapes=[pltpu.VMEM((2 * tb * CHUNKS, LANES), F32),
                        pltpu.VMEM((2 * tb * CHUNKS, LANES), F32),
                        pltpu.VMEM((tb, d_model), BF16),
                        pltpu.VMEM((d_model, d_expert), BF16), pltpu.VMEM((d_model, d_expert), BF16),
                        pltpu.VMEM((d_expert, d_model), BF16),
                        pltpu.SemaphoreType.DMA((2,)), pltpu.SemaphoreType.DMA((2,))],
    )
    return pl.pallas_call(
        functools.partial(_moe_kernel, tb=tb, nb=nb),
        grid_spec=grid_spec,
        out_shape=jax.ShapeDtypeStruct((na * CHUNKS, LANES), F32),
        compiler_params=pltpu.CompilerParams(
            dimension_semantics=("arbitrary",), vmem_limit_bytes=VMEM_LIMIT),
        name="moe_experts",
    )(blk, exp, flags, starts, tok3, tok3, dst3, dst3, x, nw, wg, wu, wd)


def _dispatch_plan(eid, n, tb=MOE_TILE, n_experts=N_EXPERTS):
    na = TOP_K * n
    nb = na // tb
    eid_flat = eid.reshape(na)
    _, s_a = lax.sort((eid_flat, jnp.arange(na, dtype=jnp.int32)), num_keys=1)
    s_tok = s_a % n
    counts = jnp.sum(eid_flat[None, :] == jnp.arange(n_experts, dtype=jnp.int32)[:, None], axis=1)
    starts = jnp.concatenate([jnp.zeros((1,), jnp.int32),
                              jnp.cumsum(counts).astype(jnp.int32)])
    lo, hi = starts[:-1], starts[1:]
    nonempty = hi > lo
    first_blk = lo // tb
    npass = jnp.where(nonempty, (hi - 1) // tb - first_blk + 1, 0)
    cum = jnp.cumsum(npass)
    total = cum[-1]
    n_pass = nb + n_experts
    w = jnp.arange(n_pass, dtype=jnp.int32)
    wc = jnp.minimum(w, total - 1)
    ex = jnp.sum(cum[None, :] <= wc[:, None], axis=1).astype(jnp.int32)
    sel = ex[:, None] == jnp.arange(n_experts, dtype=jnp.int32)[None, :]
    pick = lambda v: jnp.sum(jnp.where(sel, v[None, :], 0), axis=1)
    blk = (pick(first_blk) + (wc - pick(cum - npass))).astype(jnp.int32)
    valid = w < total
    prev_blk = jnp.concatenate([jnp.full((1,), -1, jnp.int32), blk[:-1]])
    next_blk = jnp.concatenate([blk[1:], jnp.full((1,), -1, jnp.int32)])
    first = valid & (blk != prev_blk)
    last = valid & ((blk != next_blk) | (w == total - 1))
    prev_ex = jnp.concatenate([jnp.full((1,), -1, jnp.int32), ex[:-1]])
    newexp = valid & (ex != prev_ex)
    flags = (first * _FIRST + last * _LAST + valid * _VALID + newexp * _NEWEXP).astype(jnp.int32)
    return s_tok, s_a, blk, ex, flags, starts


def _final_kernel(x_ref, y0_ref, y1_ref, rw_ref, nw_ref, o_ref):
    rw = rw_ref[...]
    tm = x_ref.shape[0]
    x = x_ref[...] + (rw[:, 0:1] * _tm_load(y0_ref, 0, tm) + rw[:, 1:2] * _tm_load(y1_ref, 0, tm))
    o_ref[...] = _rms(x, nw_ref[...])


def _final(x, y, rw, nw):
    n = x.shape[0]
    tm = ROW_TILE
    nt = n // tm
    row = pl.BlockSpec((tm, D_MODEL), lambda i: (i, 0))
    return pl.pallas_call(
        _final_kernel,
        grid=(nt,),
        in_specs=[row, pl.BlockSpec((tm * CHUNKS, LANES), lambda i: (i, 0)),
                  pl.BlockSpec((tm * CHUNKS, LANES), lambda i: (i + nt, 0)),
                  pl.BlockSpec((tm, LANES), lambda i: (i, 0)),
                  pl.BlockSpec((1, D_MODEL), lambda i: (0, 0))],
        out_specs=row,
        out_shape=jax.ShapeDtypeStruct((n, D_MODEL), F32),
        compiler_params=pltpu.CompilerParams(
            dimension_semantics=("parallel",), vmem_limit_bytes=VMEM_LIMIT),
        name="final_norm",
    )(x, y, y, rw, nw)


def kernel(x, norm_mix_w, w_in, fox_forget_b, w_out, norm_ffn_w, w_router_group, b_router_group,
           w_router_expert, b_router_expert, w_expert_gate, w_expert_up, w_expert_down,
           norm_final_w):
    batch, seq, d = x.shape
    n = batch * seq
    depth = w_in.shape[0]
    xf = x.reshape(n, d)
    tables = _ret_tables(seq)
    tri = jnp.tril(jnp.ones((ROW_TILE, ROW_TILE), F32)).astype(BF16)
    nq = seq // ATT_TILE

    y = rw = None
    for layer in range(depth):
        wl = w_in[layer]
        c0 = 3 * FOX_WIDTH
        w_main = jnp.concatenate([wl[:, :c0], wl[:, c0 + FOX_HEADS:]], axis=1).astype(BF16)
        w_ff = jnp.pad(wl[:, c0:c0 + FOX_HEADS], ((0, 0), (0, LANES - FOX_HEADS))).astype(BF16)
        b_ff = jnp.pad(fox_forget_b[layer], (0, LANES - FOX_HEADS)).reshape(1, LANES)
        xf, (fq, fk, fv, rq, rk, rv, rg, ct) = _inproj(
            xf, y, rw, norm_mix_w[layer].reshape(1, d), w_main, w_ff, b_ff, tri, seq)
        ct4 = ct.reshape(batch, 8, nq, ATT_TILE)
        fox = _fox_attention(fq, fk, fv, ct4, batch, seq)
        ret = _retention(rq, rk, rv, rg, tables, batch, seq)

        zpad = jnp.zeros((d, N_GROUPS), F32)
        w_r = jnp.concatenate([w_router_group[layer], zpad, w_router_expert[layer]], axis=1)
        nr = 2 * N_GROUPS + N_EXPERTS
        w_r = jnp.pad(w_r, ((0, 0), (0, LANES - nr))).astype(BF16)
        b_r = jnp.concatenate([b_router_group[layer], jnp.zeros((N_GROUPS,), F32),
                               b_router_expert[layer]])
        b_r = jnp.pad(b_r, (0, LANES - nr)).reshape(1, LANES)
        xf, xg, eid, rw = _outproj(fox, ret, xf, w_out[layer].astype(BF16),
                                   norm_ffn_w[layer].reshape(1, d), w_r, b_r)
        s_tok, s_dst, blk, ex, flags, starts = _dispatch_plan(eid, n)
        y = _moe(xg, s_tok, s_dst, blk, ex, flags, starts, norm_ffn_w[layer].reshape(1, d),
                 w_expert_gate, w_expert_up, w_expert_down, layer)
    out = _final(xf, y, rw, norm_final_w.reshape(1, d))
    return out.reshape(batch, seq, d)
```

```python
import functools

import jax
import jax.numpy as jnp
import numpy as np
from jax import lax
from jax.experimental import pallas as pl
from jax.experimental.pallas import tpu as pltpu

F32 = jnp.float32
BF16 = jnp.bfloat16

D_MODEL = 1024
FOX_HEADS = 8
FOX_HEAD_DIM = 64
FOX_WIDTH = 512
RET_HEADS = 4
RET_HEAD_DIM = 128
RET_WIDTH = 512
CHUNK = 64
ROPE_BASE = 10000.0
N_GROUPS = 4
EXPERTS_PER_GROUP = 8
N_EXPERTS = 32
TOP_K = 2
D_EXPERT = 512
RMS_EPS = 1e-6

LANES = 128
VMEM_LIMIT = 56 * 1024 * 1024

ROW_TILE = 512
ATT_TILE = 512
RET_TILE = 256
MOE_TILE = 256
N_MAIN = 7 * 512
EXP_UNDERFLOW = 110.0


def _rms(xf, w):
    return xf * lax.rsqrt(jnp.mean(xf * xf, axis=-1, keepdims=True) + RMS_EPS) * w


def _dot(a, b):
    return jnp.dot(a, b, preferred_element_type=F32)


def _dot_nt(a, b):
    return lax.dot_general(a, b, (((1,), (1,)), ((), ())), preferred_element_type=F32)


def _dot_tn(a, b):
    return lax.dot_general(a, b, (((0,), (0,)), ((), ())), preferred_element_type=F32)


CHUNKS = D_MODEL // LANES


def _tm_load(ref, base, rows):
    return jnp.concatenate([ref[pl.ds(base + c, rows, stride=CHUNKS), :] for c in range(CHUNKS)],
                           axis=1)


def _tm_store(ref, base, rows, val):
    for c in range(CHUNKS):
        ref[pl.ds(base + c, rows, stride=CHUNKS), :] = val[:, c * LANES:(c + 1) * LANES]


def _inproj_kernel(*refs, has_y, tiles_per_seq):
    if has_y:
        x_ref, y0_ref, y1_ref, rw_ref = refs[:4]
        refs = refs[4:]
    else:
        x_ref = refs[0]
        refs = refs[1:]
    nw_ref, w_ref, bff_ref, tri_ref = refs[:4]
    refs = refs[4:]
    if has_y:
        xres_ref = refs[0]
        refs = refs[1:]
    fq_ref, fk_ref, fv_ref, rq_ref, rk_ref, rv_ref, rg_ref, ct_ref, carry_sc = refs

    i = pl.program_id(0)
    x = x_ref[...]
    if has_y:
        rw = rw_ref[...]
        tm = x.shape[0]
        x = x + (rw[:, 0:1] * _tm_load(y0_ref, 0, tm) + rw[:, 1:2] * _tm_load(y1_ref, 0, tm))
        xres_ref[...] = x
    h = _rms(x, nw_ref[...]).astype(BF16)

    outs = (fq_ref, fk_ref, fv_ref, rq_ref, rk_ref, rv_ref)
    for j, o_ref in enumerate(outs):
        acc = _dot(h, w_ref[:, j * 512:(j + 1) * 512])
        if j == 0:
            acc = acc * (FOX_HEAD_DIM ** -0.5)
        o_ref[...] = acc.astype(BF16)
    acc = _dot(h, w_ref[:, 6 * 512:])
    rg_ref[...] = acc[:, :512].astype(BF16)

    z = acc[:, 512:] + bff_ref[...]
    lf = jnp.minimum(z, 0.0) - jnp.log1p(jnp.exp(-jnp.abs(z)))
    lft = lf.T[:8, :]
    hi = lft.astype(BF16).astype(F32)
    mid = (lft - hi).astype(BF16).astype(F32)
    lo = lft - hi - mid
    pieces = jnp.concatenate([hi, mid, lo, jnp.zeros_like(hi)], axis=0).astype(BF16)
    cs = _dot(pieces, tri_ref[...])
    cs = cs[0:8] + cs[8:16] + cs[16:24]

    @pl.when(i % tiles_per_seq == 0)
    def _():
        carry_sc[...] = jnp.zeros_like(carry_sc)

    c = cs + carry_sc[:, 0:1]
    carry_sc[...] = jnp.broadcast_to(c[:, -1:], carry_sc.shape)
    ct_ref[0] = c


def _inproj(x, y, rw, nw, w_main, b_ff, tri, seq):
    n = x.shape[0]
    tm = ROW_TILE
    nt = n // tm
    tps = seq // tm
    has_y = y is not None
    row_spec = pl.BlockSpec((tm, D_MODEL), lambda i: (i, 0))
    in_specs = [row_spec]
    args = [x]
    if has_y:
        in_specs += [pl.BlockSpec((tm * CHUNKS, LANES), lambda i: (i, 0)),
                     pl.BlockSpec((tm * CHUNKS, LANES), lambda i: (i + nt, 0)),
                     pl.BlockSpec((tm, LANES), lambda i: (i, 0))]
        args += [y, y, rw]
    in_specs += [
        pl.BlockSpec((1, D_MODEL), lambda i: (0, 0)),
        pl.BlockSpec((D_MODEL, N_MAIN + LANES), lambda i: (0, 0)),
        pl.BlockSpec((1, LANES), lambda i: (0, 0)),
        pl.BlockSpec((tm, tm), lambda i: (0, 0)),
    ]
    args += [nw, w_main, b_ff, tri]
    half_spec = pl.BlockSpec((tm, 512), lambda i: (i, 0))
    out_shape = []
    out_specs = []
    if has_y:
        out_shape.append(jax.ShapeDtypeStruct((n, D_MODEL), F32))
        out_specs.append(row_spec)
    out_shape += [jax.ShapeDtypeStruct((n, 512), BF16)] * 7
    out_specs += [half_spec] * 7
    out_shape.append(jax.ShapeDtypeStruct((n // seq, 8, seq), F32))
    out_specs.append(pl.BlockSpec((1, 8, tm), lambda i: (i // tps, 0, i % tps)))
    outs = pl.pallas_call(
        functools.partial(_inproj_kernel, has_y=has_y, tiles_per_seq=tps),
        grid=(nt,),
        in_specs=in_specs,
        out_specs=out_specs,
        out_shape=out_shape,
        scratch_shapes=[pltpu.VMEM((8, LANES), F32)],
        compiler_params=pltpu.CompilerParams(
            dimension_semantics=("arbitrary",), vmem_limit_bytes=VMEM_LIMIT),
        name="inproj_y" if has_y else "inproj",
    )(*args)
    if has_y:
        return outs[0], outs[1:]
    return x, outs


def _fox_kernel(q_ref, k_ref, v_ref, ct_ref, o_ref, m_sc, acc_sc, kmax_sc, *, tile, nq):
    hp = pl.program_id(1)
    qi = pl.program_id(2)
    q2 = q_ref[0]
    lane = lax.broadcasted_iota(jnp.int32, q2.shape, 1)
    first = lane < FOX_HEAD_DIM
    zero = jnp.zeros_like(q2)
    qh = (jnp.where(first, q2, zero), jnp.where(first, zero, q2))
    reps = tile // LANES

    def head_sqnorm_max(xf, h):
        sq = xf * xf
        sq = jnp.where(first, sq, 0.0) if h == 0 else jnp.where(first, 0.0, sq)
        return jnp.max(jnp.sum(sq, axis=1, keepdims=True), axis=0, keepdims=True)

    @pl.when(qi == 0)
    def _():
        for j in range(nq):
            kf = k_ref[0, j * tile:(j + 1) * tile, :].astype(F32)
            for h in range(2):
                kmax_sc[h, j:j + 1, :] = jnp.broadcast_to(head_sqnorm_max(kf, h), (1, LANES))

    def head_step(h, kb, k_blk, v_blk, mask, m_old, acc_old):
        one = jnp.ones_like(v_blk)
        va = jnp.where(first, v_blk, one) if h == 0 else jnp.where(first, one, v_blk)
        s = _dot_nt(qh[h], k_blk) - ct_ref[0, 2 * hp + h, pl.ds(kb, 1), :]
        if mask is not None:
            s = jnp.where(mask, s, -jnp.inf)
        m_cur = jnp.max(s, axis=1, keepdims=True)
        if m_old is None:
            m_new = jnp.broadcast_to(m_cur, (tile, LANES))
            p = jnp.exp(s - jnp.concatenate([m_new] * reps, axis=1))
            acc = _dot(p.astype(BF16), va)
        else:
            m_new = jnp.maximum(m_old, m_cur)
            alpha = jnp.exp(m_old - m_new)
            p = jnp.exp(s - jnp.concatenate([m_new] * reps, axis=1))
            acc = alpha * acc_old + _dot(p.astype(BF16), va)
        return m_new, acc

    row = lax.broadcasted_iota(jnp.int32, (tile, tile), 0)
    col = lax.broadcasted_iota(jnp.int32, (tile, tile), 1)
    start = pl.multiple_of(qi * tile, tile)
    k_blk = k_ref[0, pl.ds(start, tile), :]
    v_blk = v_ref[0, pl.ds(start, tile), :]
    for h in range(2):
        m_new, acc = head_step(h, qi, k_blk, v_blk, col <= row, None, None)
        m_sc[h] = m_new
        acc_sc[h] = acc

    def off_diagonal(kbs, heads=(0, 1)):
        state = {h: (m_sc[h], acc_sc[h]) for h in heads}
        for kb in kbs:
            start = pl.multiple_of(kb * tile, tile)
            k_blk = k_ref[0, pl.ds(start, tile), :]
            v_blk = v_ref[0, pl.ds(start, tile), :]
            state = {h: head_step(h, kb, k_blk, v_blk, None, *state[h]) for h in heads}
        for h in heads:
            m_sc[h] = state[h][0]
            acc_sc[h] = state[h][1]

    jrow = lax.broadcasted_iota(jnp.int32, (nq, LANES), 0)
    qf = q2.astype(F32)
    j_start = []
    for h in range(2):
        m_low = jnp.min(jnp.min(m_sc[h], axis=1, keepdims=True), axis=0, keepdims=True)
        c_last = ct_ref[0, 2 * hp + h, :, tile - 1:tile]
        bound = jnp.sqrt(head_sqnorm_max(qf, h) * kmax_sc[h]) - c_last
        need = (bound - m_low > -EXP_UNDERFLOW) & (jrow < qi)
        j_first = jnp.min(jnp.min(jnp.where(need, jrow, qi), axis=1, keepdims=True),
                          axis=0, keepdims=True)
        j_start.append(j_first[0, 0])
    j_both = jnp.maximum(j_start[0], j_start[1])
    count = qi - j_both

    def pair(j, carry):
        off_diagonal((j_both + 2 * j, j_both + 2 * j + 1))
        return carry

    lax.fori_loop(0, count // 2, pair, 0)

    @pl.when(count % 2 == 1)
    def _():
        off_diagonal((qi - 1,))

    for h in range(2):
        def single(j, carry, h=h):
            off_diagonal((j,), heads=(h,))
            return carry

        lax.fori_loop(j_start[h], j_both, single, 0)

    a0 = acc_sc[0]
    a1 = acc_sc[1]
    half = FOX_HEAD_DIM
    o = jnp.where(first, a0 / pltpu.roll(a0, half, 1), a1 / pltpu.roll(a1, half, 1))
    o_ref[0] = o.astype(o_ref.dtype)


def _fox_attention(fq, fk, fv, ct4, batch, seq):
    t = ATT_TILE
    nq = seq // t
    q3 = fq.reshape(batch, seq, FOX_WIDTH)
    k3 = fk.reshape(batch, seq, FOX_WIDTH)
    v3 = fv.reshape(batch, seq, FOX_WIDTH)
    out = pl.pallas_call(
        functools.partial(_fox_kernel, tile=t, nq=nq),
        grid=(batch, FOX_HEADS // 2, nq),
        in_specs=[
            pl.BlockSpec((1, t, LANES), lambda b, j, i: (b, i, j)),
            pl.BlockSpec((1, seq, LANES), lambda b, j, i: (b, 0, j)),
            pl.BlockSpec((1, seq, LANES), lambda b, j, i: (b, 0, j)),
            pl.BlockSpec((1, 8, nq, t), lambda b, j, i: (b, 0, 0, 0)),
        ],
        out_specs=pl.BlockSpec((1, t, LANES), lambda b, j, i: (b, i, j)),
        out_shape=jax.ShapeDtypeStruct((batch, seq, FOX_WIDTH), BF16),
        scratch_shapes=[pltpu.VMEM((2, t, LANES), F32), pltpu.VMEM((2, t, LANES), F32),
                        pltpu.VMEM((2, nq, LANES), F32)],
        compiler_params=pltpu.CompilerParams(
            dimension_semantics=("parallel", "parallel", "arbitrary"),
            vmem_limit_bytes=VMEM_LIMIT),
        name="fox_attention",
    )(q3, k3, v3, ct4)
    return out.reshape(batch * seq, FOX_WIDTH)


def _ret_kernel(q_ref, k_ref, v_ref, g_ref, cos_ref, sin_ref, dmat_ref, qdec_ref, kdec_ref,
                sdec_ref, o_ref, state_sc):
    si = pl.program_id(1)

    @pl.when(si == 0)
    def _():
        state_sc[...] = jnp.zeros_like(state_sc)

    cos2 = cos_ref[...]
    sin2 = sin_ref[...]
    dk = RET_HEAD_DIM

    def rot(xf):
        return xf * cos2 + pltpu.roll(xf, dk // 2, 1) * sin2

    for h in range(RET_HEADS):
        cols = slice(h * dk, (h + 1) * dk)
        q = rot(q_ref[0, :, cols].astype(F32))
        k = rot(k_ref[0, :, cols].astype(F32)) * (dk ** -0.5)
        v = v_ref[0, :, cols]
        scores = _dot_nt(q.astype(BF16), k.astype(BF16)) * dmat_ref[h]
        intra = _dot(scores.astype(BF16), v)
        state = state_sc[h]
        cross = _dot((q * qdec_ref[h]).astype(BF16), state.astype(BF16))
        out = intra + cross
        state_sc[h] = state * sdec_ref[h, 0:1, :] + _dot_tn((k * kdec_ref[h]).astype(BF16), v)

        y = out * lax.rsqrt(jnp.mean(out * out, axis=-1, keepdims=True) + RMS_EPS)
        g = g_ref[0, :, cols].astype(F32)
        o_ref[0, :, cols] = (y * (g * jax.nn.sigmoid(g))).astype(o_ref.dtype)


def _ret_tables(seq):
    half = RET_HEAD_DIM // 2
    inv_freq = 1.0 / (ROPE_BASE ** (jnp.arange(half, dtype=F32) / half))
    ang = jnp.arange(seq, dtype=F32)[:, None] * inv_freq[None, :]
    cos = jnp.cos(ang)
    sin = jnp.sin(ang)
    cos2 = jnp.concatenate([cos, cos], axis=1)
    sin2 = jnp.concatenate([-sin, sin], axis=1)
    lt = RET_TILE
    log_gamma = jnp.log(1.0 - 2.0 ** (-5.0 - jnp.arange(RET_HEADS, dtype=F32)))
    idx = jnp.arange(lt)
    t = idx[:, None]
    s = idx[None, :]
    same = (t // CHUNK) == (s // CHUNK)
    earlier = (s // CHUNK) < (t // CHUNK)
    dist = jnp.where(same, jnp.abs(t - s), t - s).astype(F32)
    dmat = jnp.where((same | earlier)[None], jnp.exp(log_gamma[:, None, None] * dist[None]), 0.0)
    idxf = idx.astype(F32)
    qdec = jnp.exp(log_gamma[:, None] * idxf[None, :])
    kdec = jnp.exp(log_gamma[:, None] * (lt - idxf)[None, :])
    sdec = jnp.exp(log_gamma * lt)
    qdec = jnp.broadcast_to(qdec[:, :, None], (RET_HEADS, lt, LANES))
    kdec = jnp.broadcast_to(kdec[:, :, None], (RET_HEADS, lt, LANES))
    sdec = jnp.broadcast_to(sdec[:, None, None], (RET_HEADS, 8, LANES))
    return cos2, sin2, dmat, qdec, kdec, sdec


def _retention(rq, rk, rv, rg, tables, batch, seq):
    lt = RET_TILE
    ns = seq // lt
    cos2, sin2, dmat, qdec, kdec, sdec = tables
    blk = pl.BlockSpec((1, lt, RET_WIDTH), lambda b, i: (b, i, 0))
    tab = pl.BlockSpec((lt, LANES), lambda b, i: (i, 0))
    args = [a.reshape(batch, seq, RET_WIDTH) for a in (rq, rk, rv, rg)]
    out = pl.pallas_call(
        _ret_kernel,
        grid=(batch, ns),
        in_specs=[blk, blk, blk, blk, tab, tab,
                  pl.BlockSpec((RET_HEADS, lt, lt), lambda b, i: (0, 0, 0)),
                  pl.BlockSpec((RET_HEADS, lt, LANES), lambda b, i: (0, 0, 0)),
                  pl.BlockSpec((RET_HEADS, lt, LANES), lambda b, i: (0, 0, 0)),
                  pl.BlockSpec((RET_HEADS, 8, LANES), lambda b, i: (0, 0, 0))],
        out_specs=blk,
        out_shape=jax.ShapeDtypeStruct((batch, seq, RET_WIDTH), BF16),
        scratch_shapes=[pltpu.VMEM((RET_HEADS, RET_HEAD_DIM, RET_HEAD_DIM), F32)],
        compiler_params=pltpu.CompilerParams(
            dimension_semantics=("parallel", "arbitrary"),
            vmem_limit_bytes=VMEM_LIMIT),
        name="retention",
    )(*args, cos2, sin2, dmat, qdec, kdec, sdec)
    return out.reshape(batch * seq, RET_WIDTH)


def _outproj_kernel(fox_ref, ret_ref, x_ref, wo_ref, nw_ref, wr_ref, br_ref, xo_ref, xg_ref,
                    eid_ref, rw_ref):
    mixed = jnp.concatenate([fox_ref[...], ret_ref[...]], axis=1)
    x = x_ref[...] + _dot(mixed, wo_ref[...])
    xo_ref[...] = x
    _tm_store(xg_ref, 0, x.shape[0], x)
    h = _rms(x, nw_ref[...]).astype(BF16)
    lt = (_dot(h, wr_ref[...]) + br_ref[...]).T
    tm = lt.shape[1]
    rowid = lax.broadcasted_iota(jnp.int32, (8, tm), 0)
    neg = -jnp.inf

    def top1(v):
        vmax = jnp.max(v, axis=0, keepdims=True)
        idx = jnp.min(jnp.where(v == vmax, rowid, 8), axis=0, keepdims=True)
        return vmax, idx

    gl = jnp.where(rowid < N_GROUPS, lt[0:8], neg)
    gmax, gidx = top1(gl)
    g_w = 1.0 / jnp.sum(jnp.exp(gl - gmax), axis=0, keepdims=True)
    e_in = jnp.zeros((8, tm), F32)
    for g in range(N_GROUPS):
        e_in = jnp.where(gidx == g, lt[8 + 8 * g:16 + 8 * g], e_in)
    v1, i1 = top1(e_in)
    rest = jnp.where(rowid == i1, neg, e_in)
    v2, i2 = top1(rest)
    t = jnp.exp(v2 - v1)
    w1 = g_w / (1.0 + t)
    eid_ref[0:1, :] = gidx * EXPERTS_PER_GROUP + i1
    eid_ref[1:2, :] = gidx * EXPERTS_PER_GROUP + i2
    wslab = jnp.concatenate([w1, w1 * t, jnp.zeros((LANES - TOP_K, tm), F32)], axis=0)
    rw_ref[...] = wslab.T


def _outproj(fox, ret, x, wo, nw, wr, br):
    n = x.shape[0]
    tm = ROW_TILE
    row = pl.BlockSpec((tm, D_MODEL), lambda i: (i, 0))
    half = pl.BlockSpec((tm, 512), lambda i: (i, 0))
    pair = pl.BlockSpec((TOP_K, tm), lambda i: (0, i))
    wts = pl.BlockSpec((tm, LANES), lambda i: (i, 0))
    return pl.pallas_call(
        _outproj_kernel,
        grid=(n // tm,),
        in_specs=[half, half, row,
                  pl.BlockSpec((D_MODEL, D_MODEL), lambda i: (0, 0)),
                  pl.BlockSpec((1, D_MODEL), lambda i: (0, 0)),
                  pl.BlockSpec((D_MODEL, LANES), lambda i: (0, 0)),
                  pl.BlockSpec((1, LANES), lambda i: (0, 0))],
        out_specs=[row, pl.BlockSpec((tm * CHUNKS, LANES), lambda i: (i, 0)), pair, wts],
        out_shape=[jax.ShapeDtypeStruct((n, D_MODEL), F32),
                   jax.ShapeDtypeStruct((n * CHUNKS, LANES), F32),
                   jax.ShapeDtypeStruct((TOP_K, n), jnp.int32),
                   jax.ShapeDtypeStruct((n, LANES), F32)],
        compiler_params=pltpu.CompilerParams(
            dimension_semantics=("parallel",), vmem_limit_bytes=VMEM_LIMIT),
        name="outproj",
    )(fox, ret, x, wo, nw, wr, br)


_FIRST, _LAST, _VALID, _NEWEXP = 1, 2, 4, 8


def _moe_kernel(blk_ref, exp_ref, flag_ref, starts_ref,
                tokc_ref, tokn_ref, dstp_ref, dstc_ref, x_hbm, nw_ref, wg_ref, wu_ref, wd_ref,
                y_hbm, xbuf, ybuf, hbuf, wgb, wub, wdb, gsem, ssem, *, tb, nb):
    w = pl.program_id(0)
    b = blk_ref[w]
    e = exp_ref[w]
    flags = flag_ref[w]
    slot = b % 2
    nslot = 1 - slot
    span = tb * CHUNKS

    def hbm_row(ref, idx):
        return ref.at[pl.ds(pl.multiple_of(idx * CHUNKS, CHUNKS), CHUNKS), :]

    def buf_row(buf, s, r):
        return buf.at[pl.ds(pl.multiple_of(s * span + r * CHUNKS, CHUNKS), CHUNKS), :]

    def start_gather(tok_ref, s):
        for r in range(tb):
            pltpu.make_async_copy(hbm_row(x_hbm, tok_ref[0, 0, r]), buf_row(xbuf, s, r),
                                  gsem.at[s]).start()

    def start_scatter(dst_ref, s):
        for r in range(tb):
            pltpu.make_async_copy(buf_row(ybuf, s, r), hbm_row(y_hbm, dst_ref[0, 0, r]),
                                  ssem.at[s]).start()

    def wait_rows(sem, s):
        whole = pl.ds(pl.multiple_of(s * span, span), span)
        pltpu.make_async_copy(xbuf.at[whole, :], ybuf.at[whole, :], sem.at[s]).wait()

    @pl.when((flags & _FIRST) != 0)
    def _():
        @pl.when(w == 0)
        def _():
            start_gather(tokc_ref, 0)

        wait_rows(gsem, slot)

        @pl.when(b >= 2)
        def _():
            wait_rows(ssem, slot)

        @pl.when(b + 1 < nb)
        def _():
            start_gather(tokn_ref, nslot)

        @pl.when(b >= 1)
        def _():
            start_scatter(dstp_ref, nslot)

        hbuf[...] = _rms(_tm_load(xbuf, slot * span, tb), nw_ref[...]).astype(BF16)

        @pl.when((flags & _LAST) == 0)
        def _():
            ybuf[pl.ds(pl.multiple_of(slot * span, span), span), :] = jnp.zeros((span, LANES), F32)

    @pl.when((flags & _NEWEXP) != 0)
    def _():
        wgb[...] = wg_ref[0, 0].astype(BF16)
        wub[...] = wu_ref[0, 0].astype(BF16)
        wdb[...] = wd_ref[0, 0].astype(BF16)

    @pl.when((flags & _VALID) != 0)
    def _():
        h = hbuf[...]
        g = _dot(h, wgb[...])
        u = _dot(h, wub[...])
        a = (g * jax.nn.sigmoid(g) * u).astype(BF16)
        y = _dot(a, wdb[...])
        whole_block = (flags & (_FIRST | _LAST)) == (_FIRST | _LAST)

        @pl.when(whole_block)
        def _():
            _tm_store(ybuf, slot * span, tb, y)

        @pl.when(jnp.logical_not(whole_block))
        def _():
            q = b * tb + lax.broadcasted_iota(jnp.int32, y.shape, 0)
            mine = (q >= starts_ref[e]) & (q < starts_ref[e + 1])
            _tm_store(ybuf, slot * span, tb, jnp.where(mine, y, _tm_load(ybuf, slot * span, tb)))

    @pl.when(((flags & _LAST) != 0) & (b == nb - 1))
    def _():
        start_scatter(dstc_ref, slot)
        if nb >= 2:
            wait_rows(ssem, nslot)
        wait_rows(ssem, slot)


def _moe(x, s_tok, s_dst, blk, exp, flags, starts, nw, wg, wu, wd, layer, tb=MOE_TILE):
    na = s_tok.shape[0]
    nb = na // tb
    d_model, d_expert = wg.shape[2], wg.shape[3]
    tok3 = s_tok.reshape(nb, 1, tb)
    dst3 = s_dst.reshape(nb, 1, tb)
    smem_blk = lambda f: pl.BlockSpec((1, 1, tb), f, memory_space=pltpu.SMEM)
    grid_spec = pltpu.PrefetchScalarGridSpec(
        num_scalar_prefetch=4,
        grid=(blk.shape[0],),
        in_specs=[
            smem_blk(lambda w, bl, ex, fl, st: (bl[w], 0, 0)),
            smem_blk(lambda w, bl, ex, fl, st: (jnp.minimum(bl[w] + 1, nb - 1), 0, 0)),
            smem_blk(lambda w, bl, ex, fl, st: (jnp.maximum(bl[w] - 1, 0), 0, 0)),
            smem_blk(lambda w, bl, ex, fl, st: (bl[w], 0, 0)),
            pl.BlockSpec(memory_space=pl.ANY),
            pl.BlockSpec((1, d_model), lambda w, bl, ex, fl, st: (0, 0)),
            pl.BlockSpec((1, 1, d_model, d_expert), lambda w, bl, ex, fl, st: (layer, ex[w], 0, 0)),
            pl.BlockSpec((1, 1, d_model, d_expert), lambda w, bl, ex, fl, st: (layer, ex[w], 0, 0)),
            pl.BlockSpec((1, 1, d_expert, d_model), lambda w, bl, ex, fl, st: (layer, ex[w], 0, 0)),
        ],
        out_specs=pl.BlockSpec(memory_space=pl.ANY),
        scratch_shapes=[pltpu.VMEM((2 * tb * CHUNKS, LANES), F32),
                        pltpu.VMEM((2 * tb * CHUNKS, LANES), F32),
                        pltpu.VMEM((tb, d_model), BF16),
                        pltpu.VMEM((d_model, d_expert), BF16), pltpu.VMEM((d_model, d_expert), BF16),
                        pltpu.VMEM((d_expert, d_model), BF16),
                        pltpu.SemaphoreType.DMA((2,)), pltpu.SemaphoreType.DMA((2,))],
    )
    return pl.pallas_call(
        functools.partial(_moe_kernel, tb=tb, nb=nb),
        grid_spec=grid_spec,
        out_shape=jax.ShapeDtypeStruct((na * CHUNKS, LANES), F32),
        compiler_params=pltpu.CompilerParams(
            dimension_semantics=("arbitrary",), vmem_limit_bytes=VMEM_LIMIT),
        name="moe_experts",
    )(blk, exp, flags, starts, tok3, tok3, dst3, dst3, x, nw, wg, wu, wd)


def _dispatch_plan(eid, n, tb=MOE_TILE, n_experts=N_EXPERTS):
    na = TOP_K * n
    nb = na // tb
    eid_flat = eid.reshape(na)
    _, s_a = lax.sort((eid_flat, jnp.arange(na, dtype=jnp.int32)), num_keys=1)
    s_tok = s_a % n
    counts = jnp.sum(eid_flat[None, :] == jnp.arange(n_experts, dtype=jnp.int32)[:, None], axis=1)
    starts = jnp.concatenate([jnp.zeros((1,), jnp.int32),
                              jnp.cumsum(counts).astype(jnp.int32)])
    lo, hi = starts[:-1], starts[1:]
    nonempty = hi > lo
    first_blk = lo // tb
    npass = jnp.where(nonempty, (hi - 1) // tb - first_blk + 1, 0)
    cum = jnp.cumsum(npass)
    total = cum[-1]
    n_pass = nb + n_experts
    w = jnp.arange(n_pass, dtype=jnp.int32)
    wc = jnp.minimum(w, total - 1)
    ex = jnp.sum(cum[None, :] <= wc[:, None], axis=1).astype(jnp.int32)
    sel = ex[:, None] == jnp.arange(n_experts, dtype=jnp.int32)[None, :]
    pick = lambda v: jnp.sum(jnp.where(sel, v[None, :], 0), axis=1)
    blk = (pick(first_blk) + (wc - pick(cum - npass))).astype(jnp.int32)
    valid = w < total
    prev_blk = jnp.concatenate([jnp.full((1,), -1, jnp.int32), blk[:-1]])
    next_blk = jnp.concatenate([blk[1:], jnp.full((1,), -1, jnp.int32)])
    first = valid & (blk != prev_blk)
    last = valid & ((blk != next_blk) | (w == total - 1))
    prev_ex = jnp.concatenate([jnp.full((1,), -1, jnp.int32), ex[:-1]])
    newexp = valid & (ex != prev_ex)
    flags = (first * _FIRST + last * _LAST + valid * _VALID + newexp * _NEWEXP).astype(jnp.int32)
    return s_tok, s_a, blk, ex, flags, starts


def _final_kernel(x_ref, y0_ref, y1_ref, rw_ref, nw_ref, o_ref):
    rw = rw_ref[...]
    tm = x_ref.shape[0]
    x = x_ref[...] + (rw[:, 0:1] * _tm_load(y0_ref, 0, tm) + rw[:, 1:2] * _tm_load(y1_ref, 0, tm))
    o_ref[...] = _rms(x, nw_ref[...])


def _final(x, y, rw, nw):
    n = x.shape[0]
    tm = ROW_TILE
    nt = n // tm
    row = pl.BlockSpec((tm, D_MODEL), lambda i: (i, 0))
    return pl.pallas_call(
        _final_kernel,
        grid=(nt,),
        in_specs=[row, pl.BlockSpec((tm * CHUNKS, LANES), lambda i: (i, 0)),
                  pl.BlockSpec((tm * CHUNKS, LANES), lambda i: (i + nt, 0)),
                  pl.BlockSpec((tm, LANES), lambda i: (i, 0)),
                  pl.BlockSpec((1, D_MODEL), lambda i: (0, 0))],
        out_specs=row,
        out_shape=jax.ShapeDtypeStruct((n, D_MODEL), F32),
        compiler_params=pltpu.CompilerParams(
            dimension_semantics=("parallel",), vmem_limit_bytes=VMEM_LIMIT),
        name="final_norm",
    )(x, y, y, rw, nw)


def kernel(x, norm_mix_w, w_in, fox_forget_b, w_out, norm_ffn_w, w_router_group, b_router_group,
           w_router_expert, b_router_expert, w_expert_gate, w_expert_up, w_expert_down,
           norm_final_w):
    batch, seq, d = x.shape
    n = batch * seq
    depth = w_in.shape[0]
    xf = x.reshape(n, d)
    tables = _ret_tables(seq)
    tri = jnp.triu(jnp.ones((ROW_TILE, ROW_TILE), F32)).astype(BF16)
    nq = seq // ATT_TILE

    y = rw = None
    for layer in range(depth):
        wl = w_in[layer]
        c0 = 3 * FOX_WIDTH
        w_ff = jnp.pad(wl[:, c0:c0 + FOX_HEADS], ((0, 0), (0, LANES - FOX_HEADS)))
        w_main = jnp.concatenate([wl[:, :c0], wl[:, c0 + FOX_HEADS:], w_ff], axis=1).astype(BF16)
        b_ff = jnp.pad(fox_forget_b[layer], (0, LANES - FOX_HEADS)).reshape(1, LANES)
        xf, (fq, fk, fv, rq, rk, rv, rg, ct) = _inproj(
            xf, y, rw, norm_mix_w[layer].reshape(1, d), w_main, b_ff, tri, seq)
        ct4 = ct.reshape(batch, 8, nq, ATT_TILE)
        fox = _fox_attention(fq, fk, fv, ct4, batch, seq)
        ret = _retention(rq, rk, rv, rg, tables, batch, seq)

        zpad = jnp.zeros((d, N_GROUPS), F32)
        w_r = jnp.concatenate([w_router_group[layer], zpad, w_router_expert[layer]], axis=1)
        nr = 2 * N_GROUPS + N_EXPERTS
        w_r = jnp.pad(w_r, ((0, 0), (0, LANES - nr))).astype(BF16)
        b_r = jnp.concatenate([b_router_group[layer], jnp.zeros((N_GROUPS,), F32),
                               b_router_expert[layer]])
        b_r = jnp.pad(b_r, (0, LANES - nr)).reshape(1, LANES)
        xf, xg, eid, rw = _outproj(fox, ret, xf, w_out[layer].astype(BF16),
                                   norm_ffn_w[layer].reshape(1, d), w_r, b_r)
        s_tok, s_dst, blk, ex, flags, starts = _dispatch_plan(eid, n)
        y = _moe(xg, s_tok, s_dst, blk, ex, flags, starts, norm_ffn_w[layer].reshape(1, d),
                 w_expert_gate, w_expert_up, w_expert_down, layer)
    out = _final(xf, y, rw, norm_final_w.reshape(1, d))
    return out.reshape(batch, seq, d)
```

```python
import functools

import jax
import jax.numpy as jnp
from jax import lax
from jax.experimental import pallas as pl
from jax.experimental.pallas import tpu as pltpu

F32 = jnp.float32
BF16 = jnp.bfloat16

D_MODEL = 1024
FOX_HEADS = 8
FOX_HEAD_DIM = 64
FOX_WIDTH = 512
RET_HEADS = 4
RET_HEAD_DIM = 128
RET_WIDTH = 512
CHUNK = 64
ROPE_BASE = 10000.0
N_GROUPS = 4
EXPERTS_PER_GROUP = 8
N_EXPERTS = 32
TOP_K = 2
D_EXPERT = 512
RMS_EPS = 1e-6

LANES = 128
VMEM_LIMIT = 56 * 1024 * 1024

ROW_TILE = 512
ATT_TILE = 512
RET_TILE = 256
MOE_TILE = 256
GROUP_W = FOX_WIDTH
N_MAIN = 7 * GROUP_W
EXP_UNDERFLOW = 110.0


def _rms(xf, w):
    return xf * lax.rsqrt(jnp.mean(xf * xf, axis=-1, keepdims=True) + RMS_EPS) * w


def _dot(a, b):
    return jnp.dot(a, b, preferred_element_type=F32)


def _dot_nt(a, b):
    return lax.dot_general(a, b, (((1,), (1,)), ((), ())), preferred_element_type=F32)


def _dot_tn(a, b):
    return lax.dot_general(a, b, (((0,), (0,)), ((), ())), preferred_element_type=F32)


CHUNKS = D_MODEL // LANES


def _tm_load(ref, base, rows):
    return jnp.concatenate([ref[pl.ds(base + c, rows, stride=CHUNKS), :] for c in range(CHUNKS)],
                           axis=1)


def _tm_store(ref, base, rows, val):
    for c in range(CHUNKS):
        ref[pl.ds(base + c, rows, stride=CHUNKS), :] = val[:, c * LANES:(c + 1) * LANES]


def _inproj_kernel(*refs, has_y, tiles_per_seq):
    if has_y:
        x_ref, y0_ref, y1_ref, rw_ref = refs[:4]
        refs = refs[4:]
    else:
        x_ref = refs[0]
        refs = refs[1:]
    nw_ref, w_ref, bff_ref, tri_ref = refs[:4]
    refs = refs[4:]
    if has_y:
        xres_ref = refs[0]
        refs = refs[1:]
    fq_ref, fk_ref, fv_ref, rq_ref, rk_ref, rv_ref, rg_ref, ct_ref, carry_sc = refs

    i = pl.program_id(0)
    x = x_ref[...]
    if has_y:
        rw = rw_ref[...]
        tm = x.shape[0]
        x = x + (rw[:, 0:1] * _tm_load(y0_ref, 0, tm) + rw[:, 1:2] * _tm_load(y1_ref, 0, tm))
        xres_ref[...] = x
    h = _rms(x, nw_ref[...]).astype(BF16)

    outs = (fq_ref, fk_ref, fv_ref, rq_ref, rk_ref, rv_ref)
    for j, o_ref in enumerate(outs):
        acc = _dot(h, w_ref[:, j * GROUP_W:(j + 1) * GROUP_W])
        if j == 0:
            acc = acc * (FOX_HEAD_DIM ** -0.5)
        o_ref[...] = acc.astype(BF16)
    acc = _dot(h, w_ref[:, 6 * GROUP_W:])
    rg_ref[...] = acc[:, :GROUP_W].astype(BF16)

    z = acc[:, GROUP_W:] + bff_ref[...]
    lf = jnp.minimum(z, 0.0) - jnp.log1p(jnp.exp(-jnp.abs(z)))
    lft = lf.T[:8, :]
    hi = lft.astype(BF16).astype(F32)
    mid = (lft - hi).astype(BF16).astype(F32)
    lo = lft - hi - mid
    pieces = jnp.concatenate([hi, mid, lo, jnp.zeros_like(hi)], axis=0).astype(BF16)
    cs = _dot(pieces, tri_ref[...])
    cs = cs[0:8] + cs[8:16] + cs[16:24]

    @pl.when(i % tiles_per_seq == 0)
    def _():
        carry_sc[...] = jnp.zeros_like(carry_sc)

    c = cs + carry_sc[:, 0:1]
    carry_sc[...] = jnp.broadcast_to(c[:, -1:], carry_sc.shape)
    ct_ref[0] = c


def _inproj(x, y, rw, nw, w_main, b_ff, tri, seq):
    n = x.shape[0]
    tm = ROW_TILE
    nt = n // tm
    tps = seq // tm
    has_y = y is not None
    row_spec = pl.BlockSpec((tm, D_MODEL), lambda i: (i, 0))
    in_specs = [row_spec]
    args = [x]
    if has_y:
        in_specs += [pl.BlockSpec((tm * CHUNKS, LANES), lambda i: (i, 0)),
                     pl.BlockSpec((tm * CHUNKS, LANES), lambda i: (i + nt, 0)),
                     pl.BlockSpec((tm, LANES), lambda i: (i, 0))]
        args += [y, y, rw]
    in_specs += [
        pl.BlockSpec((1, D_MODEL), lambda i: (0, 0)),
        pl.BlockSpec((D_MODEL, N_MAIN + LANES), lambda i: (0, 0)),
        pl.BlockSpec((1, LANES), lambda i: (0, 0)),
        pl.BlockSpec((tm, tm), lambda i: (0, 0)),
    ]
    args += [nw, w_main, b_ff, tri]
    half_spec = pl.BlockSpec((tm, GROUP_W), lambda i: (i, 0))
    out_shape = []
    out_specs = []
    if has_y:
        out_shape.append(jax.ShapeDtypeStruct((n, D_MODEL), F32))
        out_specs.append(row_spec)
    out_shape += [jax.ShapeDtypeStruct((n, GROUP_W), BF16)] * 7
    out_specs += [half_spec] * 7
    out_shape.append(jax.ShapeDtypeStruct((n // seq, 8, seq), F32))
    out_specs.append(pl.BlockSpec((1, 8, tm), lambda i: (i // tps, 0, i % tps)))
    outs = pl.pallas_call(
        functools.partial(_inproj_kernel, has_y=has_y, tiles_per_seq=tps),
        grid=(nt,),
        in_specs=in_specs,
        out_specs=out_specs,
        out_shape=out_shape,
        scratch_shapes=[pltpu.VMEM((8, LANES), F32)],
        compiler_params=pltpu.CompilerParams(
            dimension_semantics=("arbitrary",), vmem_limit_bytes=VMEM_LIMIT),
        name="inproj_y" if has_y else "inproj",
    )(*args)
    if has_y:
        return outs[0], outs[1:]
    return x, outs


def _fox_kernel(q_ref, k_ref, v_ref, ct_ref, o_ref, m_sc, acc_sc, kmax_sc, *, tile, nq):
    hp = pl.program_id(1)
    qi = pl.program_id(2)
    q2 = q_ref[0]
    lane = lax.broadcasted_iota(jnp.int32, q2.shape, 1)
    first = lane < FOX_HEAD_DIM
    zero = jnp.zeros_like(q2)
    qh = (jnp.where(first, q2, zero), jnp.where(first, zero, q2))
    reps = tile // LANES

    def head_sqnorm_max(xf, h):
        sq = xf * xf
        sq = jnp.where(first, sq, 0.0) if h == 0 else jnp.where(first, 0.0, sq)
        return jnp.max(jnp.sum(sq, axis=1, keepdims=True), axis=0, keepdims=True)

    @pl.when(qi == 0)
    def _():
        for j in range(nq):
            kf = k_ref[0, j * tile:(j + 1) * tile, :].astype(F32)
            for h in range(2):
                kmax_sc[h, j:j + 1, :] = jnp.broadcast_to(head_sqnorm_max(kf, h), (1, LANES))

    def head_step(h, kb, k_blk, v_blk, mask, m_old, acc_old):
        one = jnp.ones_like(v_blk)
        va = jnp.where(first, v_blk, one) if h == 0 else jnp.where(first, one, v_blk)
        s = _dot_nt(qh[h], k_blk) - ct_ref[0, 2 * hp + h, pl.ds(kb, 1), :]
        if mask is not None:
            s = jnp.where(mask, s, -jnp.inf)
        m_cur = jnp.max(s, axis=1, keepdims=True)
        if m_old is None:
            m_new = jnp.broadcast_to(m_cur, (tile, LANES))
            p = jnp.exp(s - jnp.concatenate([m_new] * reps, axis=1))
            acc = _dot(p.astype(BF16), va)
        else:
            m_new = jnp.maximum(m_old, m_cur)
            alpha = jnp.exp(m_old - m_new)
            p = jnp.exp(s - jnp.concatenate([m_new] * reps, axis=1))
            acc = alpha * acc_old + _dot(p.astype(BF16), va)
        return m_new, acc

    def diagonal(with_previous):
        row = lax.broadcasted_iota(jnp.int32, (tile, tile), 0)
        col = lax.broadcasted_iota(jnp.int32, (tile, tile), 1)
        start = pl.multiple_of(qi * tile, tile)
        k_blk = k_ref[0, pl.ds(start, tile), :]
        v_blk = v_ref[0, pl.ds(start, tile), :]
        state = [head_step(h, qi, k_blk, v_blk, col <= row, None, None) for h in range(2)]
        if with_previous:
            start = pl.multiple_of((qi - 1) * tile, tile)
            k_blk = k_ref[0, pl.ds(start, tile), :]
            v_blk = v_ref[0, pl.ds(start, tile), :]
            state = [head_step(h, qi - 1, k_blk, v_blk, None, *state[h]) for h in range(2)]
        for h in range(2):
            m_sc[h] = state[h][0]
            acc_sc[h] = state[h][1]

    @pl.when(qi == 0)
    def _():
        diagonal(False)

    @pl.when(qi > 0)
    def _():
        diagonal(True)

    top = jnp.maximum(qi - 1, 0)

    def off_diagonal(kbs, heads=(0, 1)):
        state = {h: (m_sc[h], acc_sc[h]) for h in heads}
        for kb in kbs:
            start = pl.multiple_of(kb * tile, tile)
            k_blk = k_ref[0, pl.ds(start, tile), :]
            v_blk = v_ref[0, pl.ds(start, tile), :]
            state = {h: head_step(h, kb, k_blk, v_blk, None, *state[h]) for h in heads}
        for h in heads:
            m_sc[h] = state[h][0]
            acc_sc[h] = state[h][1]

    jrow = lax.broadcasted_iota(jnp.int32, (nq, LANES), 0)
    qf = q2.astype(F32)
    j_start = []
    for h in range(2):
        m_low = jnp.min(jnp.min(m_sc[h], axis=1, keepdims=True), axis=0, keepdims=True)
        c_last = ct_ref[0, 2 * hp + h, :, tile - 1:tile]
        bound = jnp.sqrt(head_sqnorm_max(qf, h) * kmax_sc[h]) - c_last
        need = (bound - m_low > -EXP_UNDERFLOW) & (jrow < top)
        j_first = jnp.min(jnp.min(jnp.where(need, jrow, top), axis=1, keepdims=True),
                          axis=0, keepdims=True)
        j_start.append(j_first[0, 0])
    j_both = jnp.maximum(j_start[0], j_start[1])
    count = top - j_both

    def pair(j, carry):
        off_diagonal((j_both + 2 * j, j_both + 2 * j + 1))
        return carry

    lax.fori_loop(0, count // 2, pair, 0)

    @pl.when(count % 2 == 1)
    def _():
        off_diagonal((top - 1,))

    for h in range(2):
        def single(j, carry, h=h):
            off_diagonal((j,), heads=(h,))
            return carry

        lax.fori_loop(j_start[h], j_both, single, 0)

    a0 = acc_sc[0]
    a1 = acc_sc[1]
    half = FOX_HEAD_DIM
    o = jnp.where(first, a0 / pltpu.roll(a0, half, 1), a1 / pltpu.roll(a1, half, 1))
    o_ref[0] = o.astype(o_ref.dtype)


def _fox_attention(fq, fk, fv, ct4, batch, seq):
    t = ATT_TILE
    nq = seq // t
    q3 = fq.reshape(batch, seq, FOX_WIDTH)
    k3 = fk.reshape(batch, seq, FOX_WIDTH)
    v3 = fv.reshape(batch, seq, FOX_WIDTH)
    out = pl.pallas_call(
        functools.partial(_fox_kernel, tile=t, nq=nq),
        grid=(batch, FOX_HEADS // 2, nq),
        in_specs=[
            pl.BlockSpec((1, t, LANES), lambda b, j, i: (b, i, j)),
            pl.BlockSpec((1, seq, LANES), lambda b, j, i: (b, 0, j)),
            pl.BlockSpec((1, seq, LANES), lambda b, j, i: (b, 0, j)),
            pl.BlockSpec((1, 8, nq, t), lambda b, j, i: (b, 0, 0, 0)),
        ],
        out_specs=pl.BlockSpec((1, t, LANES), lambda b, j, i: (b, i, j)),
        out_shape=jax.ShapeDtypeStruct((batch, seq, FOX_WIDTH), BF16),
        scratch_shapes=[pltpu.VMEM((2, t, LANES), F32), pltpu.VMEM((2, t, LANES), F32),
                        pltpu.VMEM((2, nq, LANES), F32)],
        compiler_params=pltpu.CompilerParams(
            dimension_semantics=("parallel", "parallel", "arbitrary"),
            vmem_limit_bytes=VMEM_LIMIT),
        name="fox_attention",
    )(q3, k3, v3, ct4)
    return out.reshape(batch * seq, FOX_WIDTH)


def _ret_kernel(q_ref, k_ref, v_ref, g_ref, cos_ref, sin_ref, dmat_ref, qdec_ref, kdec_ref,
                sdec_ref, o_ref, state_sc):
    si = pl.program_id(1)

    @pl.when(si == 0)
    def _():
        state_sc[...] = jnp.zeros_like(state_sc)

    cos2 = cos_ref[...]
    sin2 = sin_ref[...]
    dk = RET_HEAD_DIM

    def rot(xf):
        return xf * cos2 + pltpu.roll(xf, dk // 2, 1) * sin2

    for h in range(RET_HEADS):
        cols = slice(h * dk, (h + 1) * dk)
        q = rot(q_ref[0, :, cols].astype(F32))
        k = rot(k_ref[0, :, cols].astype(F32)) * (dk ** -0.5)
        v = v_ref[0, :, cols]
        scores = _dot_nt(q.astype(BF16), k.astype(BF16)) * dmat_ref[h]
        intra = _dot(scores.astype(BF16), v)
        state = state_sc[h]
        cross = _dot((q * qdec_ref[h]).astype(BF16), state.astype(BF16))
        out = intra + cross
        state_sc[h] = state * sdec_ref[h, 0:1, :] + _dot_tn((k * kdec_ref[h]).astype(BF16), v)

        y = out * lax.rsqrt(jnp.mean(out * out, axis=-1, keepdims=True) + RMS_EPS)
        g = g_ref[0, :, cols].astype(F32)
        o_ref[0, :, cols] = (y * (g * jax.nn.sigmoid(g))).astype(o_ref.dtype)


def _ret_tables(seq):
    half = RET_HEAD_DIM // 2
    inv_freq = 1.0 / (ROPE_BASE ** (jnp.arange(half, dtype=F32) / half))
    ang = jnp.arange(seq, dtype=F32)[:, None] * inv_freq[None, :]
    cos = jnp.cos(ang)
    sin = jnp.sin(ang)
    cos2 = jnp.concatenate([cos, cos], axis=1)
    sin2 = jnp.concatenate([-sin, sin], axis=1)
    lt = RET_TILE
    log_gamma = jnp.log(1.0 - 2.0 ** (-5.0 - jnp.arange(RET_HEADS, dtype=F32)))
    idx = jnp.arange(lt)
    t = idx[:, None]
    s = idx[None, :]
    same = (t // CHUNK) == (s // CHUNK)
    earlier = (s // CHUNK) < (t // CHUNK)
    dist = jnp.where(same, jnp.abs(t - s), t - s).astype(F32)
    dmat = jnp.where((same | earlier)[None], jnp.exp(log_gamma[:, None, None] * dist[None]), 0.0)
    idxf = idx.astype(F32)
    qdec = jnp.exp(log_gamma[:, None] * idxf[None, :])
    kdec = jnp.exp(log_gamma[:, None] * (lt - idxf)[None, :])
    sdec = jnp.exp(log_gamma * lt)
    qdec = jnp.broadcast_to(qdec[:, :, None], (RET_HEADS, lt, LANES))
    kdec = jnp.broadcast_to(kdec[:, :, None], (RET_HEADS, lt, LANES))
    sdec = jnp.broadcast_to(sdec[:, None, None], (RET_HEADS, 8, LANES))
    return cos2, sin2, dmat, qdec, kdec, sdec


def _retention(rq, rk, rv, rg, tables, batch, seq):
    lt = RET_TILE
    ns = seq // lt
    cos2, sin2, dmat, qdec, kdec, sdec = tables
    blk = pl.BlockSpec((1, lt, RET_WIDTH), lambda b, i: (b, i, 0))
    tab = pl.BlockSpec((lt, LANES), lambda b, i: (i, 0))
    args = [a.reshape(batch, seq, RET_WIDTH) for a in (rq, rk, rv, rg)]
    out = pl.pallas_call(
        _ret_kernel,
        grid=(batch, ns),
        in_specs=[blk, blk, blk, blk, tab, tab,
                  pl.BlockSpec((RET_HEADS, lt, lt), lambda b, i: (0, 0, 0)),
                  pl.BlockSpec((RET_HEADS, lt, LANES), lambda b, i: (0, 0, 0)),
                  pl.BlockSpec((RET_HEADS, lt, LANES), lambda b, i: (0, 0, 0)),
                  pl.BlockSpec((RET_HEADS, 8, LANES), lambda b, i: (0, 0, 0))],
        out_specs=blk,
        out_shape=jax.ShapeDtypeStruct((batch, seq, RET_WIDTH), BF16),
        scratch_shapes=[pltpu.VMEM((RET_HEADS, RET_HEAD_DIM, RET_HEAD_DIM), F32)],
        compiler_params=pltpu.CompilerParams(
            dimension_semantics=("parallel", "arbitrary"),
            vmem_limit_bytes=VMEM_LIMIT),
        name="retention",
    )(*args, cos2, sin2, dmat, qdec, kdec, sdec)
    return out.reshape(batch * seq, RET_WIDTH)


def _outproj_kernel(fox_ref, ret_ref, x_ref, wo_ref, nw_ref, wr_ref, br_ref, xo_ref, xg_ref,
                    eid_ref, rw_ref):
    mixed = jnp.concatenate([fox_ref[...], ret_ref[...]], axis=1)
    x = x_ref[...] + _dot(mixed, wo_ref[...])
    xo_ref[...] = x
    _tm_store(xg_ref, 0, x.shape[0], x)
    h = _rms(x, nw_ref[...]).astype(BF16)
    lt = (_dot(h, wr_ref[...]) + br_ref[...]).T
    tm = lt.shape[1]
    rowid = lax.broadcasted_iota(jnp.int32, (8, tm), 0)
    neg = -jnp.inf

    def top1(v):
        vmax = jnp.max(v, axis=0, keepdims=True)
        idx = jnp.min(jnp.where(v == vmax, rowid, 8), axis=0, keepdims=True)
        return vmax, idx

    gl = jnp.where(rowid < N_GROUPS, lt[0:8], neg)
    gmax, gidx = top1(gl)
    g_w = 1.0 / jnp.sum(jnp.exp(gl - gmax), axis=0, keepdims=True)
    e_in = jnp.zeros((8, tm), F32)
    for g in range(N_GROUPS):
        e_in = jnp.where(gidx == g, lt[8 + 8 * g:16 + 8 * g], e_in)
    v1, i1 = top1(e_in)
    rest = jnp.where(rowid == i1, neg, e_in)
    v2, i2 = top1(rest)
    t = jnp.exp(v2 - v1)
    w1 = g_w / (1.0 + t)
    eid_ref[0:1, :] = gidx * EXPERTS_PER_GROUP + i1
    eid_ref[1:2, :] = gidx * EXPERTS_PER_GROUP + i2
    wslab = jnp.concatenate([w1, w1 * t, jnp.zeros((LANES - TOP_K, tm), F32)], axis=0)
    rw_ref[...] = wslab.T


def _outproj(fox, ret, x, wo, nw, wr, br):
    n = x.shape[0]
    tm = ROW_TILE
    row = pl.BlockSpec((tm, D_MODEL), lambda i: (i, 0))
    half = pl.BlockSpec((tm, GROUP_W), lambda i: (i, 0))
    pair = pl.BlockSpec((TOP_K, tm), lambda i: (0, i))
    wts = pl.BlockSpec((tm, LANES), lambda i: (i, 0))
    return pl.pallas_call(
        _outproj_kernel,
        grid=(n // tm,),
        in_specs=[half, half, row,
                  pl.BlockSpec((D_MODEL, D_MODEL), lambda i: (0, 0)),
                  pl.BlockSpec((1, D_MODEL), lambda i: (0, 0)),
                  pl.BlockSpec((D_MODEL, LANES), lambda i: (0, 0)),
                  pl.BlockSpec((1, LANES), lambda i: (0, 0))],
        out_specs=[row, pl.BlockSpec((tm * CHUNKS, LANES), lambda i: (i, 0)), pair, wts],
        out_shape=[jax.ShapeDtypeStruct((n, D_MODEL), F32),
                   jax.ShapeDtypeStruct((n * CHUNKS, LANES), F32),
                   jax.ShapeDtypeStruct((TOP_K, n), jnp.int32),
                   jax.ShapeDtypeStruct((n, LANES), F32)],
        compiler_params=pltpu.CompilerParams(
            dimension_semantics=("parallel",), vmem_limit_bytes=VMEM_LIMIT),
        name="outproj",
    )(fox, ret, x, wo, nw, wr, br)


_FIRST, _LAST, _VALID, _NEWEXP = 1, 2, 4, 8


def _moe_kernel(blk_ref, exp_ref, flag_ref, starts_ref,
                tokc_ref, tokn_ref, dstp_ref, dstc_ref, x_hbm, nw_ref, wg_ref, wu_ref, wd_ref,
                y_hbm, xbuf, ybuf, hbuf, wgb, wub, wdb, gsem, ssem, *, tb, nb):
    w = pl.program_id(0)
    b = blk_ref[w]
    e = exp_ref[w]
    flags = flag_ref[w]
    slot = b % 2
    nslot = 1 - slot
    span = tb * CHUNKS

    def hbm_row(ref, idx):
        return ref.at[pl.ds(pl.multiple_of(idx * CHUNKS, CHUNKS), CHUNKS), :]

    def buf_row(buf, s, r):
        return buf.at[pl.ds(pl.multiple_of(s * span + r * CHUNKS, CHUNKS), CHUNKS), :]

    def start_gather(tok_ref, s):
        for r in range(tb):
            pltpu.make_async_copy(hbm_row(x_hbm, tok_ref[0, 0, r]), buf_row(xbuf, s, r),
                                  gsem.at[s]).start()

    def start_scatter(dst_ref, s):
        for r in range(tb):
            pltpu.make_async_copy(buf_row(ybuf, s, r), hbm_row(y_hbm, dst_ref[0, 0, r]),
                                  ssem.at[s]).start()

    def wait_rows(sem, s):
        whole = pl.ds(pl.multiple_of(s * span, span), span)
        pltpu.make_async_copy(xbuf.at[whole, :], ybuf.at[whole, :], sem.at[s]).wait()

    @pl.when((flags & _FIRST) != 0)
    def _():
        @pl.when(w == 0)
        def _():
            start_gather(tokc_ref, 0)

        wait_rows(gsem, slot)

        @pl.when(b >= 2)
        def _():
            wait_rows(ssem, slot)

        @pl.when(b + 1 < nb)
        def _():
            start_gather(tokn_ref, nslot)

        @pl.when(b >= 1)
        def _():
            start_scatter(dstp_ref, nslot)

        hbuf[...] = _rms(_tm_load(xbuf, slot * span, tb), nw_ref[...]).astype(BF16)

        @pl.when((flags & _LAST) == 0)
        def _():
            ybuf[pl.ds(pl.multiple_of(slot * span, span), span), :] = jnp.zeros((span, LANES), F32)

    @pl.when((flags & _NEWEXP) != 0)
    def _():
        wgb[...] = wg_ref[0, 0].astype(BF16)
        wub[...] = wu_ref[0, 0].astype(BF16)
        wdb[...] = wd_ref[0, 0].astype(BF16)

    @pl.when((flags & _VALID) != 0)
    def _():
        h = hbuf[...]
        g = _dot(h, wgb[...])
        u = _dot(h, wub[...])
        a = (g * jax.nn.sigmoid(g) * u).astype(BF16)
        y = _dot(a, wdb[...])
        whole_block = (flags & (_FIRST | _LAST)) == (_FIRST | _LAST)

        @pl.when(whole_block)
        def _():
            _tm_store(ybuf, slot * span, tb, y)

        @pl.when(jnp.logical_not(whole_block))
        def _():
            q = b * tb + lax.broadcasted_iota(jnp.int32, y.shape, 0)
            mine = (q >= starts_ref[e]) & (q < starts_ref[e + 1])
            _tm_store(ybuf, slot * span, tb, jnp.where(mine, y, _tm_load(ybuf, slot * span, tb)))

    @pl.when(((flags & _LAST) != 0) & (b == nb - 1))
    def _():
        start_scatter(dstc_ref, slot)
        if nb >= 2:
            wait_rows(ssem, nslot)
        wait_rows(ssem, slot)


def _moe(x, s_tok, s_dst, blk, exp, flags, starts, nw, wg, wu, wd, layer, tb=MOE_TILE):
    na = s_tok.shape[0]
    nb = na // tb
    d_model, d_expert = wg.shape[2], wg.shape[3]
    tok3 = s_tok.reshape(nb, 1, tb)
    dst3 = s_dst.reshape(nb, 1, tb)
    smem_blk = lambda f: pl.BlockSpec((1, 1, tb), f, memory_space=pltpu.SMEM)
    grid_spec = pltpu.PrefetchScalarGridSpec(
        num_scalar_prefetch=4,
        grid=(blk.shape[0],),
        in_specs=[
            smem_blk(lambda w, bl, ex, fl, st: (bl[w], 0, 0)),
            smem_blk(lambda w, bl, ex, fl, st: (jnp.minimum(bl[w] + 1, nb - 1), 0, 0)),
            smem_blk(lambda w, bl, ex, fl, st: (jnp.maximum(bl[w] - 1, 0), 0, 0)),
            smem_blk(lambda w, bl, ex, fl, st: (bl[w], 0, 0)),
            pl.BlockSpec(memory_space=pl.ANY),
            pl.BlockSpec((1, d_model), lambda w, bl, ex, fl, st: (0, 0)),
            pl.BlockSpec((1, 1, d_model, d_expert), lambda w, bl, ex, fl, st: (layer, ex[w], 0, 0)),
            pl.BlockSpec((1, 1, d_model, d_expert), lambda w, bl, ex, fl, st: (layer, ex[w], 0, 0)),
            pl.BlockSpec((1, 1, d_expert, d_model), lambda w, bl, ex, fl, st: (layer, ex[w], 0, 0)),
        ],
        out_specs=pl.BlockSpec(memory_space=pl.ANY),
        scratch_shapes=[pltpu.VMEM((2 * tb * CHUNKS, LANES), F32),
                        pltpu.VMEM((2 * tb * CHUNKS, LANES), F32),
                        pltpu.VMEM((tb, d_model), BF16),
                        pltpu.VMEM((d_model, d_expert), BF16), pltpu.VMEM((d_model, d_expert), BF16),
                        pltpu.VMEM((d_expert, d_model), BF16),
                        pltpu.SemaphoreType.DMA((2,)), pltpu.SemaphoreType.DMA((2,))],
    )
    return pl.pallas_call(
        functools.partial(_moe_kernel, tb=tb, nb=nb),
        grid_spec=grid_spec,
        out_shape=jax.ShapeDtypeStruct((na * CHUNKS, LANES), F32),
        compiler_params=pltpu.CompilerParams(
            dimension_semantics=("arbitrary",), vmem_limit_bytes=VMEM_LIMIT),
        name="moe_experts",
    )(blk, exp, flags, starts, tok3, tok3, dst3, dst3, x, nw, wg, wu, wd)


def _dispatch_plan(eid, n, tb=MOE_TILE, n_experts=N_EXPERTS):
    na = TOP_K * n
    nb = na // tb
    eid_flat = eid.reshape(na)
    _, s_a = lax.sort((eid_flat, jnp.arange(na, dtype=jnp.int32)), num_keys=1)
    s_tok = s_a % n
    counts = jnp.sum(eid_flat[None, :] == jnp.arange(n_experts, dtype=jnp.int32)[:, None], axis=1)
    starts = jnp.concatenate([jnp.zeros((1,), jnp.int32),
                              jnp.cumsum(counts).astype(jnp.int32)])
    lo, hi = starts[:-1], starts[1:]
    nonempty = hi > lo
    first_blk = lo // tb
    npass = jnp.where(nonempty, (hi - 1) // tb - first_blk + 1, 0)
    cum = jnp.cumsum(npass)
    total = cum[-1]
    n_pass = nb + n_experts
    w = jnp.arange(n_pass, dtype=jnp.int32)
    wc = jnp.minimum(w, total - 1)
    ex = jnp.sum(cum[None, :] <= wc[:, None], axis=1).astype(jnp.int32)
    sel = ex[:, None] == jnp.arange(n_experts, dtype=jnp.int32)[None, :]
    pick = lambda v: jnp.sum(jnp.where(sel, v[None, :], 0), axis=1)
    blk = (pick(first_blk) + (wc - pick(cum - npass))).astype(jnp.int32)
    valid = w < total
    prev_blk = jnp.concatenate([jnp.full((1,), -1, jnp.int32), blk[:-1]])
    next_blk = jnp.concatenate([blk[1:], jnp.full((1,), -1, jnp.int32)])
    first = valid & (blk != prev_blk)
    last = valid & ((blk != next_blk) | (w == total - 1))
    prev_ex = jnp.concatenate([jnp.full((1,), -1, jnp.int32), ex[:-1]])
    newexp = valid & (ex != prev_ex)
    flags = (first * _FIRST + last * _LAST + valid * _VALID + newexp * _NEWEXP).astype(jnp.int32)
    return s_tok, s_a, blk, ex, flags, starts


def _final_kernel(x_ref, y0_ref, y1_ref, rw_ref, nw_ref, o_ref):
    rw = rw_ref[...]
    tm = x_ref.shape[0]
    x = x_ref[...] + (rw[:, 0:1] * _tm_load(y0_ref, 0, tm) + rw[:, 1:2] * _tm_load(y1_ref, 0, tm))
    o_ref[...] = _rms(x, nw_ref[...])


def _final(x, y, rw, nw):
    n = x.shape[0]
    tm = ROW_TILE
    nt = n // tm
    row = pl.BlockSpec((tm, D_MODEL), lambda i: (i, 0))
    return pl.pallas_call(
        _final_kernel,
        grid=(nt,),
        in_specs=[row, pl.BlockSpec((tm * CHUNKS, LANES), lambda i: (i, 0)),
                  pl.BlockSpec((tm * CHUNKS, LANES), lambda i: (i + nt, 0)),
                  pl.BlockSpec((tm, LANES), lambda i: (i, 0)),
                  pl.BlockSpec((1, D_MODEL), lambda i: (0, 0))],
        out_specs=row,
        out_shape=jax.ShapeDtypeStruct((n, D_MODEL), F32),
        compiler_params=pltpu.CompilerParams(
            dimension_semantics=("parallel",), vmem_limit_bytes=VMEM_LIMIT),
        name="final_norm",
    )(x, y, y, rw, nw)


def kernel(x, norm_mix_w, w_in, fox_forget_b, w_out, norm_ffn_w, w_router_group, b_router_group,
           w_router_expert, b_router_expert, w_expert_gate, w_expert_up, w_expert_down,
           norm_final_w):
    batch, seq, d = x.shape
    n = batch * seq
    depth = w_in.shape[0]
    xf = x.reshape(n, d)
    tables = _ret_tables(seq)
    tri = jnp.triu(jnp.ones((ROW_TILE, ROW_TILE), F32)).astype(BF16)
    nq = seq // ATT_TILE

    y = rw = None
    for layer in range(depth):
        wl = w_in[layer]
        c0 = 3 * FOX_WIDTH
        w_ff = jnp.pad(wl[:, c0:c0 + FOX_HEADS], ((0, 0), (0, LANES - FOX_HEADS)))
        w_main = jnp.concatenate([wl[:, :c0], wl[:, c0 + FOX_HEADS:], w_ff], axis=1).astype(BF16)
        b_ff = jnp.pad(fox_forget_b[layer], (0, LANES - FOX_HEADS)).reshape(1, LANES)
        xf, (fq, fk, fv, rq, rk, rv, rg, ct) = _inproj(
            xf, y, rw, norm_mix_w[layer].reshape(1, d), w_main, b_ff, tri, seq)
        ct4 = ct.reshape(batch, 8, nq, ATT_TILE)
        fox = _fox_attention(fq, fk, fv, ct4, batch, seq)
        ret = _retention(rq, rk, rv, rg, tables, batch, seq)

        zpad = jnp.zeros((d, N_GROUPS), F32)
        w_r = jnp.concatenate([w_router_group[layer], zpad, w_router_expert[layer]], axis=1)
        nr = 2 * N_GROUPS + N_EXPERTS
        w_r = jnp.pad(w_r, ((0, 0), (0, LANES - nr))).astype(BF16)
        b_r = jnp.concatenate([b_router_group[layer], jnp.zeros((N_GROUPS,), F32),
                               b_router_expert[layer]])
        b_r = jnp.pad(b_r, (0, LANES - nr)).reshape(1, LANES)
        xf, xg, eid, rw = _outproj(fox, ret, xf, w_out[layer].astype(BF16),
                                   norm_ffn_w[layer].reshape(1, d), w_r, b_r)
        s_tok, s_dst, blk, ex, flags, starts = _dispatch_plan(eid, n)
        y = _moe(xg, s_tok, s_dst, blk, ex, flags, starts, norm_ffn_w[layer].reshape(1, d),
                 w_expert_gate, w_expert_up, w_expert_down, layer)
    out = _final(xf, y, rw, norm_final_w.reshape(1, d))
    return out.reshape(batch, seq, d)
```

```python
import functools

import jax
import jax.numpy as jnp
from jax import lax
from jax.experimental import pallas as pl
from jax.experimental.pallas import tpu as pltpu

F32 = jnp.float32
BF16 = jnp.bfloat16

D_MODEL = 1024
FOX_HEADS = 8
FOX_HEAD_DIM = 64
FOX_WIDTH = 512
RET_HEADS = 4
RET_HEAD_DIM = 128
RET_WIDTH = 512
CHUNK = 64
ROPE_BASE = 10000.0
N_GROUPS = 4
EXPERTS_PER_GROUP = 8
N_EXPERTS = 32
TOP_K = 2
D_EXPERT = 512
RMS_EPS = 1e-6

LANES = 128
VMEM_LIMIT = 56 * 1024 * 1024

ROW_TILE = 512
WIDE_TILE = 1024
ATT_TILE = 512
RET_TILE = 512
MOE_TILE = 256
GROUP_W = FOX_WIDTH
N_MAIN = 7 * GROUP_W
EXP_UNDERFLOW = 110.0


def _rms(xf, w):
    return xf * lax.rsqrt(jnp.mean(xf * xf, axis=-1, keepdims=True) + RMS_EPS) * w


def _dot(a, b):
    return jnp.dot(a, b, preferred_element_type=F32)


def _dot_nt(a, b):
    return lax.dot_general(a, b, (((1,), (1,)), ((), ())), preferred_element_type=F32)


def _dot_tn(a, b):
    return lax.dot_general(a, b, (((0,), (0,)), ((), ())), preferred_element_type=F32)


CHUNKS = D_MODEL // LANES


def _tm_load(ref, base, rows):
    return jnp.concatenate([ref[pl.ds(base + c, rows, stride=CHUNKS), :] for c in range(CHUNKS)],
                           axis=1)


def _tm_store(ref, base, rows, val):
    for c in range(CHUNKS):
        ref[pl.ds(base + c, rows, stride=CHUNKS), :] = val[:, c * LANES:(c + 1) * LANES]


def _inproj_kernel(*refs, has_y, tiles_per_seq):
    if has_y:
        x_ref, y0_ref, y1_ref, rw_ref = refs[:4]
        refs = refs[4:]
    else:
        x_ref = refs[0]
        refs = refs[1:]
    nw_ref, w_ref, bff_ref, tri_ref = refs[:4]
    refs = refs[4:]
    if has_y:
        xres_ref = refs[0]
        refs = refs[1:]
    fq_ref, fk_ref, fv_ref, rq_ref, rk_ref, rv_ref, rg_ref, ct_ref, carry_sc = refs

    i = pl.program_id(0)
    x = x_ref[...]
    if has_y:
        rw = rw_ref[...]
        tm = x.shape[0]
        x = x + (rw[:, 0:1] * _tm_load(y0_ref, 0, tm) + rw[:, 1:2] * _tm_load(y1_ref, 0, tm))
        xres_ref[...] = x
    h = _rms(x, nw_ref[...]).astype(BF16)

    outs = (fq_ref, fk_ref, fv_ref, rq_ref, rk_ref, rv_ref)
    for j, o_ref in enumerate(outs):
        acc = _dot(h, w_ref[:, j * GROUP_W:(j + 1) * GROUP_W])
        if j == 0:
            acc = acc * (FOX_HEAD_DIM ** -0.5)
        o_ref[...] = acc.astype(BF16)
    acc = _dot(h, w_ref[:, 6 * GROUP_W:])
    rg_ref[...] = acc[:, :GROUP_W].astype(BF16)

    z = acc[:, GROUP_W:] + bff_ref[...]
    lf = jnp.minimum(z, 0.0) - jnp.log1p(jnp.exp(-jnp.abs(z)))
    lft = lf.T[:8, :]
    hi = lft.astype(BF16).astype(F32)
    mid = (lft - hi).astype(BF16).astype(F32)
    lo = lft - hi - mid
    pieces = jnp.concatenate([hi, mid, lo, jnp.zeros_like(hi)], axis=0).astype(BF16)
    cs = _dot(pieces, tri_ref[...])
    cs = cs[0:8] + cs[8:16] + cs[16:24]

    @pl.when(i % tiles_per_seq == 0)
    def _():
        carry_sc[...] = jnp.zeros_like(carry_sc)

    c = cs + carry_sc[:, 0:1]
    carry_sc[...] = jnp.broadcast_to(c[:, -1:], carry_sc.shape)
    ct_ref[0] = c


def _inproj(x, y, rw, nw, w_main, b_ff, seq):
    n = x.shape[0]
    tm = ROW_TILE
    tri = jnp.triu(jnp.ones((tm, tm), F32)).astype(BF16)
    nt = n // tm
    tps = seq // tm
    has_y = y is not None
    row_spec = pl.BlockSpec((tm, D_MODEL), lambda i: (i, 0))
    in_specs = [row_spec]
    args = [x]
    if has_y:
        in_specs += [pl.BlockSpec((tm * CHUNKS, LANES), lambda i: (i, 0)),
                     pl.BlockSpec((tm * CHUNKS, LANES), lambda i: (i + nt, 0)),
                     pl.BlockSpec((tm, LANES), lambda i: (i, 0))]
        args += [y, y, rw]
    in_specs += [
        pl.BlockSpec((1, D_MODEL), lambda i: (0, 0)),
        pl.BlockSpec((D_MODEL, N_MAIN + LANES), lambda i: (0, 0)),
        pl.BlockSpec((1, LANES), lambda i: (0, 0)),
        pl.BlockSpec((tm, tm), lambda i: (0, 0)),
    ]
    args += [nw, w_main, b_ff, tri]
    half_spec = pl.BlockSpec((tm, GROUP_W), lambda i: (i, 0))
    out_shape = []
    out_specs = []
    if has_y:
        out_shape.append(jax.ShapeDtypeStruct((n, D_MODEL), F32))
        out_specs.append(row_spec)
    out_shape += [jax.ShapeDtypeStruct((n, GROUP_W), BF16)] * 7
    out_specs += [half_spec] * 7
    out_shape.append(jax.ShapeDtypeStruct((n // seq, 8, seq), F32))
    out_specs.append(pl.BlockSpec((1, 8, tm), lambda i: (i // tps, 0, i % tps)))
    outs = pl.pallas_call(
        functools.partial(_inproj_kernel, has_y=has_y, tiles_per_seq=tps),
        grid=(nt,),
        in_specs=in_specs,
        out_specs=out_specs,
        out_shape=out_shape,
        scratch_shapes=[pltpu.VMEM((8, LANES), F32)],
        compiler_params=pltpu.CompilerParams(
            dimension_semantics=("arbitrary",), vmem_limit_bytes=VMEM_LIMIT),
        name="inproj_y" if has_y else "inproj",
    )(*args)
    if has_y:
        return outs[0], outs[1:]
    return x, outs


def _fox_kernel(q_ref, k_ref, v_ref, ct_ref, o_ref, m_sc, acc_sc, kmax_sc, *, tile, nq):
    hp = pl.program_id(1)
    qi = pl.program_id(2)
    q2 = q_ref[0]
    lane = lax.broadcasted_iota(jnp.int32, q2.shape, 1)
    first = lane < FOX_HEAD_DIM
    zero = jnp.zeros_like(q2)
    qh = (jnp.where(first, q2, zero), jnp.where(first, zero, q2))
    reps = tile // LANES

    def head_sqnorm_max(xf, h):
        sq = xf * xf
        sq = jnp.where(first, sq, 0.0) if h == 0 else jnp.where(first, 0.0, sq)
        return jnp.max(jnp.sum(sq, axis=1, keepdims=True), axis=0, keepdims=True)

    @pl.when(qi == 0)
    def _():
        for j in range(nq):
            kf = k_ref[0, j * tile:(j + 1) * tile, :].astype(F32)
            for h in range(2):
                kmax_sc[h, j:j + 1, :] = jnp.broadcast_to(head_sqnorm_max(kf, h), (1, LANES))

    def head_step(h, kb, k_blk, v_blk, mask, m_old, acc_old):
        one = jnp.ones_like(v_blk)
        va = jnp.where(first, v_blk, one) if h == 0 else jnp.where(first, one, v_blk)
        s = _dot_nt(qh[h], k_blk) - ct_ref[0, 2 * hp + h, pl.ds(kb, 1), :]
        if mask is not None:
            s = jnp.where(mask, s, -jnp.inf)
        m_cur = jnp.max(s, axis=1, keepdims=True)
        if m_old is None:
            m_new = jnp.broadcast_to(m_cur, (tile, LANES))
            p = jnp.exp(s - jnp.concatenate([m_new] * reps, axis=1))
            acc = _dot(p.astype(BF16), va)
        else:
            m_new = jnp.maximum(m_old, m_cur)
            alpha = jnp.exp(m_old - m_new)
            p = jnp.exp(s - jnp.concatenate([m_new] * reps, axis=1))
            acc = alpha * acc_old + _dot(p.astype(BF16), va)
        return m_new, acc

    def diagonal(with_previous):
        row = lax.broadcasted_iota(jnp.int32, (tile, tile), 0)
        col = lax.broadcasted_iota(jnp.int32, (tile, tile), 1)
        start = pl.multiple_of(qi * tile, tile)
        k_blk = k_ref[0, pl.ds(start, tile), :]
        v_blk = v_ref[0, pl.ds(start, tile), :]
        state = [head_step(h, qi, k_blk, v_blk, col <= row, None, None) for h in range(2)]
        if with_previous:
            start = pl.multiple_of((qi - 1) * tile, tile)
            k_blk = k_ref[0, pl.ds(start, tile), :]
            v_blk = v_ref[0, pl.ds(start, tile), :]
            state = [head_step(h, qi - 1, k_blk, v_blk, None, *state[h]) for h in range(2)]
        for h in range(2):
            m_sc[h] = state[h][0]
            acc_sc[h] = state[h][1]

    @pl.when(qi == 0)
    def _():
        diagonal(False)

    @pl.when(qi > 0)
    def _():
        diagonal(True)

    top = jnp.maximum(qi - 1, 0)

    def off_diagonal(kbs, heads=(0, 1)):
        state = {h: (m_sc[h], acc_sc[h]) for h in heads}
        for kb in kbs:
            start = pl.multiple_of(kb * tile, tile)
            k_blk = k_ref[0, pl.ds(start, tile), :]
            v_blk = v_ref[0, pl.ds(start, tile), :]
            state = {h: head_step(h, kb, k_blk, v_blk, None, *state[h]) for h in heads}
        for h in heads:
            m_sc[h] = state[h][0]
            acc_sc[h] = state[h][1]

    jrow = lax.broadcasted_iota(jnp.int32, (nq, LANES), 0)
    qf = q2.astype(F32)
    j_start = []
    for h in range(2):
        m_low = jnp.min(jnp.min(m_sc[h], axis=1, keepdims=True), axis=0, keepdims=True)
        c_last = ct_ref[0, 2 * hp + h, :, tile - 1:tile]
        bound = jnp.sqrt(head_sqnorm_max(qf, h) * kmax_sc[h]) - c_last
        need = (bound - m_low > -EXP_UNDERFLOW) & (jrow < top)
        j_first = jnp.min(jnp.min(jnp.where(need, jrow, top), axis=1, keepdims=True),
                          axis=0, keepdims=True)
        j_start.append(j_first[0, 0])
    j_both = jnp.maximum(j_start[0], j_start[1])
    count = top - j_both

    def pair(j, carry):
        off_diagonal((j_both + 2 * j, j_both + 2 * j + 1))
        return carry

    lax.fori_loop(0, count // 2, pair, 0)

    @pl.when(count % 2 == 1)
    def _():
        off_diagonal((top - 1,))

    for h in range(2):
        def single(j, carry, h=h):
            off_diagonal((j,), heads=(h,))
            return carry

        lax.fori_loop(j_start[h], j_both, single, 0)

    a0 = acc_sc[0]
    a1 = acc_sc[1]
    half = FOX_HEAD_DIM
    o = jnp.where(first, a0 / pltpu.roll(a0, half, 1), a1 / pltpu.roll(a1, half, 1))
    o_ref[0] = o.astype(o_ref.dtype)


def _fox_attention(fq, fk, fv, ct4, batch, seq):
    t = ATT_TILE
    nq = seq // t
    q3 = fq.reshape(batch, seq, FOX_WIDTH)
    k3 = fk.reshape(batch, seq, FOX_WIDTH)
    v3 = fv.reshape(batch, seq, FOX_WIDTH)
    out = pl.pallas_call(
        functools.partial(_fox_kernel, tile=t, nq=nq),
        grid=(batch, FOX_HEADS // 2, nq),
        in_specs=[
            pl.BlockSpec((1, t, LANES), lambda b, j, i: (b, i, j)),
            pl.BlockSpec((1, seq, LANES), lambda b, j, i: (b, 0, j)),
            pl.BlockSpec((1, seq, LANES), lambda b, j, i: (b, 0, j)),
            pl.BlockSpec((1, 8, nq, t), lambda b, j, i: (b, 0, 0, 0)),
        ],
        out_specs=pl.BlockSpec((1, t, LANES), lambda b, j, i: (b, i, j)),
        out_shape=jax.ShapeDtypeStruct((batch, seq, FOX_WIDTH), BF16),
        scratch_shapes=[pltpu.VMEM((2, t, LANES), F32), pltpu.VMEM((2, t, LANES), F32),
                        pltpu.VMEM((2, nq, LANES), F32)],
        compiler_params=pltpu.CompilerParams(
            dimension_semantics=("parallel", "parallel", "arbitrary"),
            vmem_limit_bytes=VMEM_LIMIT),
        name="fox_attention",
    )(q3, k3, v3, ct4)
    return out.reshape(batch * seq, FOX_WIDTH)


def _ret_kernel(q_ref, k_ref, v_ref, g_ref, cos_ref, sin_ref, dmat_ref, qdec_ref, kdec_ref,
                sdec_ref, o_ref, state_sc):
    si = pl.program_id(1)

    @pl.when(si == 0)
    def _():
        state_sc[...] = jnp.zeros_like(state_sc)

    cos2 = cos_ref[...]
    sin2 = sin_ref[...]
    dk = RET_HEAD_DIM

    def rot(xf):
        return xf * cos2 + pltpu.roll(xf, dk // 2, 1) * sin2

    for h in range(RET_HEADS):
        cols = slice(h * dk, (h + 1) * dk)
        q = rot(q_ref[0, :, cols].astype(F32))
        k = rot(k_ref[0, :, cols].astype(F32)) * (dk ** -0.5)
        v = v_ref[0, :, cols]
        scores = _dot_nt(q.astype(BF16), k.astype(BF16)) * dmat_ref[h]
        intra = _dot(scores.astype(BF16), v)
        state = state_sc[h]
        cross = _dot((q * qdec_ref[h]).astype(BF16), state.astype(BF16))
        out = intra + cross
        state_sc[h] = state * sdec_ref[h, 0:1, :] + _dot_tn((k * kdec_ref[h]).astype(BF16), v)

        y = out * lax.rsqrt(jnp.mean(out * out, axis=-1, keepdims=True) + RMS_EPS)
        g = g_ref[0, :, cols].astype(F32)
        o_ref[0, :, cols] = (y * (g * jax.nn.sigmoid(g))).astype(o_ref.dtype)


def _ret_tables(seq):
    half = RET_HEAD_DIM // 2
    inv_freq = 1.0 / (ROPE_BASE ** (jnp.arange(half, dtype=F32) / half))
    ang = jnp.arange(seq, dtype=F32)[:, None] * inv_freq[None, :]
    cos = jnp.cos(ang)
    sin = jnp.sin(ang)
    cos2 = jnp.concatenate([cos, cos], axis=1)
    sin2 = jnp.concatenate([-sin, sin], axis=1)
    lt = RET_TILE
    log_gamma = jnp.log(1.0 - 2.0 ** (-5.0 - jnp.arange(RET_HEADS, dtype=F32)))
    idx = jnp.arange(lt)
    t = idx[:, None]
    s = idx[None, :]
    same = (t // CHUNK) == (s // CHUNK)
    earlier = (s // CHUNK) < (t // CHUNK)
    dist = jnp.where(same, jnp.abs(t - s), t - s).astype(F32)
    dmat = jnp.where((same | earlier)[None], jnp.exp(log_gamma[:, None, None] * dist[None]), 0.0)
    idxf = idx.astype(F32)
    qdec = jnp.exp(log_gamma[:, None] * idxf[None, :])
    kdec = jnp.exp(log_gamma[:, None] * (lt - idxf)[None, :])
    sdec = jnp.exp(log_gamma * lt)
    qdec = jnp.broadcast_to(qdec[:, :, None], (RET_HEADS, lt, LANES))
    kdec = jnp.broadcast_to(kdec[:, :, None], (RET_HEADS, lt, LANES))
    sdec = jnp.broadcast_to(sdec[:, None, None], (RET_HEADS, 8, LANES))
    return cos2, sin2, dmat, qdec, kdec, sdec


def _retention(rq, rk, rv, rg, tables, batch, seq):
    lt = RET_TILE
    ns = seq // lt
    cos2, sin2, dmat, qdec, kdec, sdec = tables
    blk = pl.BlockSpec((1, lt, RET_WIDTH), lambda b, i: (b, i, 0))
    tab = pl.BlockSpec((lt, LANES), lambda b, i: (i, 0))
    args = [a.reshape(batch, seq, RET_WIDTH) for a in (rq, rk, rv, rg)]
    out = pl.pallas_call(
        _ret_kernel,
        grid=(batch, ns),
        in_specs=[blk, blk, blk, blk, tab, tab,
                  pl.BlockSpec((RET_HEADS, lt, lt), lambda b, i: (0, 0, 0)),
                  pl.BlockSpec((RET_HEADS, lt, LANES), lambda b, i: (0, 0, 0)),
                  pl.BlockSpec((RET_HEADS, lt, LANES), lambda b, i: (0, 0, 0)),
                  pl.BlockSpec((RET_HEADS, 8, LANES), lambda b, i: (0, 0, 0))],
        out_specs=blk,
        out_shape=jax.ShapeDtypeStruct((batch, seq, RET_WIDTH), BF16),
        scratch_shapes=[pltpu.VMEM((RET_HEADS, RET_HEAD_DIM, RET_HEAD_DIM), F32)],
        compiler_params=pltpu.CompilerParams(
            dimension_semantics=("parallel", "arbitrary"),
            vmem_limit_bytes=VMEM_LIMIT),
        name="retention",
    )(*args, cos2, sin2, dmat, qdec, kdec, sdec)
    return out.reshape(batch * seq, RET_WIDTH)


def _outproj_kernel(fox_ref, ret_ref, x_ref, wo_ref, nw_ref, wr_ref, br_ref, xo_ref, xg_ref,
                    eid_ref, rw_ref):
    mixed = jnp.concatenate([fox_ref[...], ret_ref[...]], axis=1)
    x = x_ref[...] + _dot(mixed, wo_ref[...])
    xo_ref[...] = x
    _tm_store(xg_ref, 0, x.shape[0], x)
    h = _rms(x, nw_ref[...]).astype(BF16)
    lt = (_dot(h, wr_ref[...]) + br_ref[...]).T
    tm = lt.shape[1]
    rowid = lax.broadcasted_iota(jnp.int32, (8, tm), 0)
    neg = -jnp.inf

    def top1(v):
        vmax = jnp.max(v, axis=0, keepdims=True)
        idx = jnp.min(jnp.where(v == vmax, rowid, 8), axis=0, keepdims=True)
        return vmax, idx

    gl = jnp.where(rowid < N_GROUPS, lt[0:8], neg)
    gmax, gidx = top1(gl)
    g_w = 1.0 / jnp.sum(jnp.exp(gl - gmax), axis=0, keepdims=True)
    e_in = jnp.zeros((8, tm), F32)
    for g in range(N_GROUPS):
        e_in = jnp.where(gidx == g, lt[8 + 8 * g:16 + 8 * g], e_in)
    v1, i1 = top1(e_in)
    rest = jnp.where(rowid == i1, neg, e_in)
    v2, i2 = top1(rest)
    t = jnp.exp(v2 - v1)
    w1 = g_w / (1.0 + t)
    eid_ref[0:1, :] = gidx * EXPERTS_PER_GROUP + i1
    eid_ref[1:2, :] = gidx * EXPERTS_PER_GROUP + i2
    wslab = jnp.concatenate([w1, w1 * t, jnp.zeros((LANES - TOP_K, tm), F32)], axis=0)
    rw_ref[...] = wslab.T


def _outproj(fox, ret, x, wo, nw, wr, br):
    n = x.shape[0]
    tm = WIDE_TILE
    row = pl.BlockSpec((tm, D_MODEL), lambda i: (i, 0))
    half = pl.BlockSpec((tm, GROUP_W), lambda i: (i, 0))
    pair = pl.BlockSpec((TOP_K, tm), lambda i: (0, i))
    wts = pl.BlockSpec((tm, LANES), lambda i: (i, 0))
    return pl.pallas_call(
        _outproj_kernel,
        grid=(n // tm,),
        in_specs=[half, half, row,
                  pl.BlockSpec((D_MODEL, D_MODEL), lambda i: (0, 0)),
                  pl.BlockSpec((1, D_MODEL), lambda i: (0, 0)),
                  pl.BlockSpec((D_MODEL, LANES), lambda i: (0, 0)),
                  pl.BlockSpec((1, LANES), lambda i: (0, 0))],
        out_specs=[row, pl.BlockSpec((tm * CHUNKS, LANES), lambda i: (i, 0)), pair, wts],
        out_shape=[jax.ShapeDtypeStruct((n, D_MODEL), F32),
                   jax.ShapeDtypeStruct((n * CHUNKS, LANES), F32),
                   jax.ShapeDtypeStruct((TOP_K, n), jnp.int32),
                   jax.ShapeDtypeStruct((n, LANES), F32)],
        compiler_params=pltpu.CompilerParams(
            dimension_semantics=("parallel",), vmem_limit_bytes=VMEM_LIMIT),
        name="outproj",
    )(fox, ret, x, wo, nw, wr, br)


_FIRST, _LAST, _VALID, _NEWEXP = 1, 2, 4, 8


def _moe_kernel(blk_ref, exp_ref, flag_ref, starts_ref,
                tokc_ref, tokn_ref, dstp_ref, dstc_ref, x_hbm, nw_ref, wg_ref, wu_ref, wd_ref,
                y_hbm, xbuf, ybuf, hbuf, wgb, wub, wdb, gsem, ssem, *, tb, nb):
    w = pl.program_id(0)
    b = blk_ref[w]
    e = exp_ref[w]
    flags = flag_ref[w]
    slot = b % 2
    nslot = 1 - slot
    span = tb * CHUNKS

    def hbm_row(ref, idx):
        return ref.at[pl.ds(pl.multiple_of(idx * CHUNKS, CHUNKS), CHUNKS), :]

    def buf_row(buf, s, r):
        return buf.at[pl.ds(pl.multiple_of(s * span + r * CHUNKS, CHUNKS), CHUNKS), :]

    def start_gather(tok_ref, s):
        for r in range(tb):
            pltpu.make_async_copy(hbm_row(x_hbm, tok_ref[0, 0, r]), buf_row(xbuf, s, r),
                                  gsem.at[s]).start()

    def start_scatter(dst_ref, s):
        for r in range(tb):
            pltpu.make_async_copy(buf_row(ybuf, s, r), hbm_row(y_hbm, dst_ref[0, 0, r]),
                                  ssem.at[s]).start()

    def wait_rows(sem, s):
        whole = pl.ds(pl.multiple_of(s * span, span), span)
        pltpu.make_async_copy(xbuf.at[whole, :], ybuf.at[whole, :], sem.at[s]).wait()

    @pl.when((flags & _FIRST) != 0)
    def _():
        @pl.when(w == 0)
        def _():
            start_gather(tokc_ref, 0)

        wait_rows(gsem, slot)

        @pl.when(b >= 2)
        def _():
            wait_rows(ssem, slot)

        @pl.when(b + 1 < nb)
        def _():
            start_gather(tokn_ref, nslot)

        @pl.when(b >= 1)
        def _():
            start_scatter(dstp_ref, nslot)

        hbuf[...] = _rms(_tm_load(xbuf, slot * span, tb), nw_ref[...]).astype(BF16)

        @pl.when((flags & _LAST) == 0)
        def _():
            ybuf[pl.ds(pl.multiple_of(slot * span, span), span), :] = jnp.zeros((span, LANES), F32)

    @pl.when((flags & _NEWEXP) != 0)
    def _():
        wgb[...] = wg_ref[0, 0].astype(BF16)
        wub[...] = wu_ref[0, 0].astype(BF16)
        wdb[...] = wd_ref[0, 0].astype(BF16)

    @pl.when((flags & _VALID) != 0)
    def _():
        h = hbuf[...]
        g = _dot(h, wgb[...])
        u = _dot(h, wub[...])
        a = (g * jax.nn.sigmoid(g) * u).astype(BF16)
        y = _dot(a, wdb[...])
        whole_block = (flags & (_FIRST | _LAST)) == (_FIRST | _LAST)

        @pl.when(whole_block)
        def _():
            _tm_store(ybuf, slot * span, tb, y)

        @pl.when(jnp.logical_not(whole_block))
        def _():
            q = b * tb + lax.broadcasted_iota(jnp.int32, y.shape, 0)
            mine = (q >= starts_ref[e]) & (q < starts_ref[e + 1])
            _tm_store(ybuf, slot * span, tb, jnp.where(mine, y, _tm_load(ybuf, slot * span, tb)))

    @pl.when(((flags & _LAST) != 0) & (b == nb - 1))
    def _():
        start_scatter(dstc_ref, slot)
        if nb >= 2:
            wait_rows(ssem, nslot)
        wait_rows(ssem, slot)


def _moe(x, s_tok, s_dst, blk, exp, flags, starts, nw, wg, wu, wd, layer, tb=MOE_TILE):
    na = s_tok.shape[0]
    nb = na // tb
    d_model, d_expert = wg.shape[2], wg.shape[3]
    tok3 = s_tok.reshape(nb, 1, tb)
    dst3 = s_dst.reshape(nb, 1, tb)
    smem_blk = lambda f: pl.BlockSpec((1, 1, tb), f, memory_space=pltpu.SMEM)
    grid_spec = pltpu.PrefetchScalarGridSpec(
        num_scalar_prefetch=4,
        grid=(blk.shape[0],),
        in_specs=[
            smem_blk(lambda w, bl, ex, fl, st: (bl[w], 0, 0)),
            smem_blk(lambda w, bl, ex, fl, st: (jnp.minimum(bl[w] + 1, nb - 1), 0, 0)),
            smem_blk(lambda w, bl, ex, fl, st: (jnp.maximum(bl[w] - 1, 0), 0, 0)),
            smem_blk(lambda w, bl, ex, fl, st: (bl[w], 0, 0)),
            pl.BlockSpec(memory_space=pl.ANY),
            pl.BlockSpec((1, d_model), lambda w, bl, ex, fl, st: (0, 0)),
            pl.BlockSpec((1, 1, d_model, d_expert), lambda w, bl, ex, fl, st: (layer, ex[w], 0, 0)),
            pl.BlockSpec((1, 1, d_model, d_expert), lambda w, bl, ex, fl, st: (layer, ex[w], 0, 0)),
            pl.BlockSpec((1, 1, d_expert, d_model), lambda w, bl, ex, fl, st: (layer, ex[w], 0, 0)),
        ],
        out_specs=pl.BlockSpec(memory_space=pl.ANY),
        scratch_shapes=[pltpu.VMEM((2 * tb * CHUNKS, LANES), F32),
                        pltpu.VMEM((2 * tb * CHUNKS, LANES), F32),
                        pltpu.VMEM((tb, d_model), BF16),
                        pltpu.VMEM((d_model, d_expert), BF16), pltpu.VMEM((d_model, d_expert), BF16),
                        pltpu.VMEM((d_expert, d_model), BF16),
                        pltpu.SemaphoreType.DMA((2,)), pltpu.SemaphoreType.DMA((2,))],
    )
    return pl.pallas_call(
        functools.partial(_moe_kernel, tb=tb, nb=nb),
        grid_spec=grid_spec,
        out_shape=jax.ShapeDtypeStruct((na * CHUNKS, LANES), F32),
        compiler_params=pltpu.CompilerParams(
            dimension_semantics=("arbitrary",), vmem_limit_bytes=VMEM_LIMIT),
        name="moe_experts",
    )(blk, exp, flags, starts, tok3, tok3, dst3, dst3, x, nw, wg, wu, wd)


def _dispatch_plan(eid, n, tb=MOE_TILE, n_experts=N_EXPERTS):
    na = TOP_K * n
    nb = na // tb
    eid_flat = eid.reshape(na)
    _, s_a = lax.sort((eid_flat, jnp.arange(na, dtype=jnp.int32)), num_keys=1)
    s_tok = s_a % n
    counts = jnp.sum(eid_flat[None, :] == jnp.arange(n_experts, dtype=jnp.int32)[:, None], axis=1)
    starts = jnp.concatenate([jnp.zeros((1,), jnp.int32),
                              jnp.cumsum(counts).astype(jnp.int32)])
    lo, hi = starts[:-1], starts[1:]
    nonempty = hi > lo
    first_blk = lo // tb
    npass = jnp.where(nonempty, (hi - 1) // tb - first_blk + 1, 0)
    cum = jnp.cumsum(npass)
    total = cum[-1]
    n_pass = nb + n_experts
    w = jnp.arange(n_pass, dtype=jnp.int32)
    wc = jnp.minimum(w, total - 1)
    ex = jnp.sum(cum[None, :] <= wc[:, None], axis=1).astype(jnp.int32)
    sel = ex[:, None] == jnp.arange(n_experts, dtype=jnp.int32)[None, :]
    pick = lambda v: jnp.sum(jnp.where(sel, v[None, :], 0), axis=1)
    blk = (pick(first_blk) + (wc - pick(cum - npass))).astype(jnp.int32)
    valid = w < total
    prev_blk = jnp.concatenate([jnp.full((1,), -1, jnp.int32), blk[:-1]])
    next_blk = jnp.concatenate([blk[1:], jnp.full((1,), -1, jnp.int32)])
    first = valid & (blk != prev_blk)
    last = valid & ((blk != next_blk) | (w == total - 1))
    prev_ex = jnp.concatenate([jnp.full((1,), -1, jnp.int32), ex[:-1]])
    newexp = valid & (ex != prev_ex)
    flags = (first * _FIRST + last * _LAST + valid * _VALID + newexp * _NEWEXP).astype(jnp.int32)
    return s_tok, s_a, blk, ex, flags, starts


def _final_kernel(x_ref, y0_ref, y1_ref, rw_ref, nw_ref, o_ref):
    rw = rw_ref[...]
    tm = x_ref.shape[0]
    x = x_ref[...] + (rw[:, 0:1] * _tm_load(y0_ref, 0, tm) + rw[:, 1:2] * _tm_load(y1_ref, 0, tm))
    o_ref[...] = _rms(x, nw_ref[...])


def _final(x, y, rw, nw):
    n = x.shape[0]
    tm = WIDE_TILE
    nt = n // tm
    row = pl.BlockSpec((tm, D_MODEL), lambda i: (i, 0))
    return pl.pallas_call(
        _final_kernel,
        grid=(nt,),
        in_specs=[row, pl.BlockSpec((tm * CHUNKS, LANES), lambda i: (i, 0)),
                  pl.BlockSpec((tm * CHUNKS, LANES), lambda i: (i + nt, 0)),
                  pl.BlockSpec((tm, LANES), lambda i: (i, 0)),
                  pl.BlockSpec((1, D_MODEL), lambda i: (0, 0))],
        out_specs=row,
        out_shape=jax.ShapeDtypeStruct((n, D_MODEL), F32),
        compiler_params=pltpu.CompilerParams(
            dimension_semantics=("parallel",), vmem_limit_bytes=VMEM_LIMIT),
        name="final_norm",
    )(x, y, y, rw, nw)


def kernel(x, norm_mix_w, w_in, fox_forget_b, w_out, norm_ffn_w, w_router_group, b_router_group,
           w_router_expert, b_router_expert, w_expert_gate, w_expert_up, w_expert_down,
           norm_final_w):
    batch, seq, d = x.shape
    n = batch * seq
    depth = w_in.shape[0]
    xf = x.reshape(n, d)
    tables = _ret_tables(seq)
    nq = seq // ATT_TILE

    y = rw = None
    for layer in range(depth):
        wl = w_in[layer]
        c0 = 3 * FOX_WIDTH
        w_ff = jnp.pad(wl[:, c0:c0 + FOX_HEADS], ((0, 0), (0, LANES - FOX_HEADS)))
        w_main = jnp.concatenate([wl[:, :c0], wl[:, c0 + FOX_HEADS:], w_ff], axis=1).astype(BF16)
        b_ff = jnp.pad(fox_forget_b[layer], (0, LANES - FOX_HEADS)).reshape(1, LANES)
        xf, (fq, fk, fv, rq, rk, rv, rg, ct) = _inproj(
            xf, y, rw, norm_mix_w[layer].reshape(1, d), w_main, b_ff, seq)
        ct4 = ct.reshape(batch, 8, nq, ATT_TILE)
        fox = _fox_attention(fq, fk, fv, ct4, batch, seq)
        ret = _retention(rq, rk, rv, rg, tables, batch, seq)

        zpad = jnp.zeros((d, N_GROUPS), F32)
        w_r = jnp.concatenate([w_router_group[layer], zpad, w_router_expert[layer]], axis=1)
        nr = 2 * N_GROUPS + N_EXPERTS
        w_r = jnp.pad(w_r, ((0, 0), (0, LANES - nr))).astype(BF16)
        b_r = jnp.concatenate([b_router_group[layer], jnp.zeros((N_GROUPS,), F32),
                               b_router_expert[layer]])
        b_r = jnp.pad(b_r, (0, LANES - nr)).reshape(1, LANES)
        xf, xg, eid, rw = _outproj(fox, ret, xf, w_out[layer].astype(BF16),
                                   norm_ffn_w[layer].reshape(1, d), w_r, b_r)
        s_tok, s_dst, blk, ex, flags, starts = _dispatch_plan(eid, n)
        y = _moe(xg, s_tok, s_dst, blk, ex, flags, starts, norm_ffn_w[layer].reshape(1, d),
                 w_expert_gate, w_expert_up, w_expert_down, layer)
    out = _final(xf, y, rw, norm_final_w.reshape(1, d))
    return out.reshape(batch, seq, d)
```

```python
import functools

import jax
import jax.numpy as jnp
from jax import lax
from jax.experimental import pallas as pl
from jax.experimental.pallas import tpu as pltpu

F32 = jnp.float32
BF16 = jnp.bfloat16

D_MODEL = 1024
FOX_HEADS = 8
FOX_HEAD_DIM = 64
FOX_WIDTH = 512
RET_HEADS = 4
RET_HEAD_DIM = 128
RET_WIDTH = 512
CHUNK = 64
ROPE_BASE = 10000.0
N_GROUPS = 4
EXPERTS_PER_GROUP = 8
N_EXPERTS = 32
TOP_K = 2
D_EXPERT = 512
RMS_EPS = 1e-6

LANES = 128
VMEM_LIMIT = 56 * 1024 * 1024

ROW_TILE = 512
WIDE_TILE = 1024
ATT_TILE = 512
RET_TILE = 512
MOE_TILE = 256
GROUP_W = FOX_WIDTH
N_MAIN = 7 * GROUP_W
EXP_UNDERFLOW = 110.0


def _rms(xf, w):
    return xf * lax.rsqrt(jnp.mean(xf * xf, axis=-1, keepdims=True) + RMS_EPS) * w


def _dot(a, b):
    return jnp.dot(a, b, preferred_element_type=F32)


def _dot_nt(a, b):
    return lax.dot_general(a, b, (((1,), (1,)), ((), ())), preferred_element_type=F32)


def _dot_tn(a, b):
    return lax.dot_general(a, b, (((0,), (0,)), ((), ())), preferred_element_type=F32)


CHUNKS = D_MODEL // LANES


def _tm_load(ref, base, rows):
    return jnp.concatenate([ref[pl.ds(base + c, rows, stride=CHUNKS), :] for c in range(CHUNKS)],
                           axis=1)


def _tm_store(ref, base, rows, val):
    for c in range(CHUNKS):
        ref[pl.ds(base + c, rows, stride=CHUNKS), :] = val[:, c * LANES:(c + 1) * LANES]


def _inproj_kernel(*refs, has_y, tiles_per_seq):
    if has_y:
        x_ref, y0_ref, y1_ref, rw_ref = refs[:4]
        refs = refs[4:]
    else:
        x_ref = refs[0]
        refs = refs[1:]
    nw_ref, w_ref, bff_ref, tri_ref = refs[:4]
    refs = refs[4:]
    if has_y:
        xres_ref = refs[0]
        refs = refs[1:]
    fq_ref, fk_ref, fv_ref, rq_ref, rk_ref, rv_ref, rg_ref, ct_ref, carry_sc = refs

    i = pl.program_id(0)
    x = x_ref[...]
    if has_y:
        rw = rw_ref[...]
        tm = x.shape[0]
        x = x + (rw[:, 0:1] * _tm_load(y0_ref, 0, tm) + rw[:, 1:2] * _tm_load(y1_ref, 0, tm))
        xres_ref[...] = x
    h = _rms(x, nw_ref[...]).astype(BF16)

    outs = (fq_ref, fk_ref, fv_ref, rq_ref, rk_ref, rv_ref)
    for j, o_ref in enumerate(outs):
        acc = _dot(h, w_ref[:, j * GROUP_W:(j + 1) * GROUP_W])
        if j == 0:
            acc = acc * (FOX_HEAD_DIM ** -0.5)
        o_ref[...] = acc.astype(BF16)
    acc = _dot(h, w_ref[:, 6 * GROUP_W:])
    rg_ref[...] = acc[:, :GROUP_W].astype(BF16)

    z = acc[:, GROUP_W:] + bff_ref[...]
    lf = jnp.minimum(z, 0.0) - jnp.log1p(jnp.exp(-jnp.abs(z)))
    lft = lf.T[:8, :]
    hi = lft.astype(BF16).astype(F32)
    mid = (lft - hi).astype(BF16).astype(F32)
    lo = lft - hi - mid
    pieces = jnp.concatenate([hi, mid, lo, jnp.zeros_like(hi)], axis=0).astype(BF16)
    cs = _dot(pieces, tri_ref[...])
    cs = cs[0:8] + cs[8:16] + cs[16:24]

    @pl.when(i % tiles_per_seq == 0)
    def _():
        carry_sc[...] = jnp.zeros_like(carry_sc)

    c = cs + carry_sc[:, 0:1]
    carry_sc[...] = jnp.broadcast_to(c[:, -1:], carry_sc.shape)
    ct_ref[0] = c


def _inproj(x, y, rw, nw, w_main, b_ff, seq):
    n = x.shape[0]
    tm = ROW_TILE
    tri = jnp.triu(jnp.ones((tm, tm), F32)).astype(BF16)
    nt = n // tm
    tps = seq // tm
    has_y = y is not None
    row_spec = pl.BlockSpec((tm, D_MODEL), lambda i: (i, 0))
    in_specs = [row_spec]
    args = [x]
    if has_y:
        in_specs += [pl.BlockSpec((tm * CHUNKS, LANES), lambda i: (i, 0)),
                     pl.BlockSpec((tm * CHUNKS, LANES), lambda i: (i + nt, 0)),
                     pl.BlockSpec((tm, LANES), lambda i: (i, 0))]
        args += [y, y, rw]
    in_specs += [
        pl.BlockSpec((1, D_MODEL), lambda i: (0, 0)),
        pl.BlockSpec((D_MODEL, N_MAIN + LANES), lambda i: (0, 0)),
        pl.BlockSpec((1, LANES), lambda i: (0, 0)),
        pl.BlockSpec((tm, tm), lambda i: (0, 0)),
    ]
    args += [nw, w_main, b_ff, tri]
    half_spec = pl.BlockSpec((tm, GROUP_W), lambda i: (i, 0))
    out_shape = []
    out_specs = []
    if has_y:
        out_shape.append(jax.ShapeDtypeStruct((n, D_MODEL), F32))
        out_specs.append(row_spec)
    out_shape += [jax.ShapeDtypeStruct((n, GROUP_W), BF16)] * 7
    out_specs += [half_spec] * 7
    out_shape.append(jax.ShapeDtypeStruct((n // seq, 8, seq), F32))
    out_specs.append(pl.BlockSpec((1, 8, tm), lambda i: (i // tps, 0, i % tps)))
    outs = pl.pallas_call(
        functools.partial(_inproj_kernel, has_y=has_y, tiles_per_seq=tps),
        grid=(nt,),
        in_specs=in_specs,
        out_specs=out_specs,
        out_shape=out_shape,
        scratch_shapes=[pltpu.VMEM((8, LANES), F32)],
        compiler_params=pltpu.CompilerParams(
            dimension_semantics=("arbitrary",), vmem_limit_bytes=VMEM_LIMIT),
        name="inproj_y" if has_y else "inproj",
    )(*args)
    if has_y:
        return outs[0], outs[1:]
    return x, outs


def _fox_kernel(q_ref, k_ref, v_ref, ct_ref, o_ref, m_sc, acc_sc, kmax_sc, *, tile, nq):
    hp = pl.program_id(1)
    qi = pl.program_id(2)
    q2 = q_ref[0]
    lane = lax.broadcasted_iota(jnp.int32, q2.shape, 1)
    first = lane < FOX_HEAD_DIM
    zero = jnp.zeros_like(q2)
    qh = (jnp.where(first, q2, zero), jnp.where(first, zero, q2))
    reps = tile // LANES

    def head_sqnorm_max(xf, h):
        sq = xf * xf
        sq = jnp.where(first, sq, 0.0) if h == 0 else jnp.where(first, 0.0, sq)
        return jnp.max(jnp.sum(sq, axis=1, keepdims=True), axis=0, keepdims=True)

    @pl.when(qi == 0)
    def _():
        for j in range(nq):
            kf = k_ref[0, j * tile:(j + 1) * tile, :].astype(F32)
            for h in range(2):
                kmax_sc[h, j:j + 1, :] = jnp.broadcast_to(head_sqnorm_max(kf, h), (1, LANES))

    def head_step(h, kb, k_blk, v_blk, mask, m_old, acc_old):
        one = jnp.ones_like(v_blk)
        va = jnp.where(first, v_blk, one) if h == 0 else jnp.where(first, one, v_blk)
        s = _dot_nt(qh[h], k_blk) - ct_ref[0, 2 * hp + h, pl.ds(kb, 1), :]
        if mask is not None:
            s = jnp.where(mask, s, -jnp.inf)
        m_cur = jnp.max(s, axis=1, keepdims=True)
        if m_old is None:
            m_new = jnp.broadcast_to(m_cur, (tile, LANES))
            p = jnp.exp(s - jnp.concatenate([m_new] * reps, axis=1))
            acc = _dot(p.astype(BF16), va)
        else:
            m_new = jnp.maximum(m_old, m_cur)
            alpha = jnp.exp(m_old - m_new)
            p = jnp.exp(s - jnp.concatenate([m_new] * reps, axis=1))
            acc = alpha * acc_old + _dot(p.astype(BF16), va)
        return m_new, acc

    def diagonal(with_previous):
        row = lax.broadcasted_iota(jnp.int32, (tile, tile), 0)
        col = lax.broadcasted_iota(jnp.int32, (tile, tile), 1)
        start = pl.multiple_of(qi * tile, tile)
        k_blk = k_ref[0, pl.ds(start, tile), :]
        v_blk = v_ref[0, pl.ds(start, tile), :]
        state = [head_step(h, qi, k_blk, v_blk, col <= row, None, None) for h in range(2)]
        if with_previous:
            start = pl.multiple_of((qi - 1) * tile, tile)
            k_blk = k_ref[0, pl.ds(start, tile), :]
            v_blk = v_ref[0, pl.ds(start, tile), :]
            state = [head_step(h, qi - 1, k_blk, v_blk, None, *state[h]) for h in range(2)]
        for h in range(2):
            m_sc[h] = state[h][0]
            acc_sc[h] = state[h][1]

    @pl.when(qi == 0)
    def _():
        diagonal(False)

    @pl.when(qi > 0)
    def _():
        diagonal(True)

    top = jnp.maximum(qi - 1, 0)

    def off_diagonal(kbs, heads=(0, 1)):
        state = {h: (m_sc[h], acc_sc[h]) for h in heads}
        for kb in kbs:
            start = pl.multiple_of(kb * tile, tile)
            k_blk = k_ref[0, pl.ds(start, tile), :]
            v_blk = v_ref[0, pl.ds(start, tile), :]
            state = {h: head_step(h, kb, k_blk, v_blk, None, *state[h]) for h in heads}
        for h in heads:
            m_sc[h] = state[h][0]
            acc_sc[h] = state[h][1]

    jrow = lax.broadcasted_iota(jnp.int32, (nq, LANES), 0)
    qf = q2.astype(F32)
    j_start = []
    for h in range(2):
        m_low = jnp.min(jnp.min(m_sc[h], axis=1, keepdims=True), axis=0, keepdims=True)
        c_last = ct_ref[0, 2 * hp + h, :, tile - 1:tile]
        bound = jnp.sqrt(head_sqnorm_max(qf, h) * kmax_sc[h]) - c_last
        need = (bound - m_low > -EXP_UNDERFLOW) & (jrow < top)
        j_first = jnp.min(jnp.min(jnp.where(need, jrow, top), axis=1, keepdims=True),
                          axis=0, keepdims=True)
        j_start.append(j_first[0, 0])
    j_both = jnp.maximum(j_start[0], j_start[1])
    count = top - j_both

    def pair(j, carry):
        off_diagonal((j_both + 2 * j, j_both + 2 * j + 1))
        return carry

    lax.fori_loop(0, count // 2, pair, 0)

    @pl.when(count % 2 == 1)
    def _():
        off_diagonal((top - 1,))

    for h in range(2):
        lone = j_both - j_start[h]

        def single_pair(j, carry, h=h):
            off_diagonal((j_start[h] + 2 * j, j_start[h] + 2 * j + 1), heads=(h,))
            return carry

        lax.fori_loop(0, lone // 2, single_pair, 0)

        @pl.when(lone % 2 == 1)
        def _(h=h):
            off_diagonal((j_both - 1,), heads=(h,))

    a0 = acc_sc[0]
    a1 = acc_sc[1]
    half = FOX_HEAD_DIM
    o = jnp.where(first, a0 / pltpu.roll(a0, half, 1), a1 / pltpu.roll(a1, half, 1))
    o_ref[0] = o.astype(o_ref.dtype)


def _fox_attention(fq, fk, fv, ct4, batch, seq):
    t = ATT_TILE
    nq = seq // t
    q3 = fq.reshape(batch, seq, FOX_WIDTH)
    k3 = fk.reshape(batch, seq, FOX_WIDTH)
    v3 = fv.reshape(batch, seq, FOX_WIDTH)
    out = pl.pallas_call(
        functools.partial(_fox_kernel, tile=t, nq=nq),
        grid=(batch, FOX_HEADS // 2, nq),
        in_specs=[
            pl.BlockSpec((1, t, LANES), lambda b, j, i: (b, i, j)),
            pl.BlockSpec((1, seq, LANES), lambda b, j, i: (b, 0, j)),
            pl.BlockSpec((1, seq, LANES), lambda b, j, i: (b, 0, j)),
            pl.BlockSpec((1, 8, nq, t), lambda b, j, i: (b, 0, 0, 0)),
        ],
        out_specs=pl.BlockSpec((1, t, LANES), lambda b, j, i: (b, i, j)),
        out_shape=jax.ShapeDtypeStruct((batch, seq, FOX_WIDTH), BF16),
        scratch_shapes=[pltpu.VMEM((2, t, LANES), F32), pltpu.VMEM((2, t, LANES), F32),
                        pltpu.VMEM((2, nq, LANES), F32)],
        compiler_params=pltpu.CompilerParams(
            dimension_semantics=("parallel", "parallel", "arbitrary"),
            vmem_limit_bytes=VMEM_LIMIT),
        name="fox_attention",
    )(q3, k3, v3, ct4)
    return out.reshape(batch * seq, FOX_WIDTH)


def _ret_kernel(q_ref, k_ref, v_ref, g_ref, cos_ref, sin_ref, dmat_ref, qdec_ref, kdec_ref,
                sdec_ref, o_ref, state_sc):
    si = pl.program_id(1)

    @pl.when(si == 0)
    def _():
        state_sc[...] = jnp.zeros_like(state_sc)

    cos2 = cos_ref[...]
    sin2 = sin_ref[...]
    dk = RET_HEAD_DIM

    def rot(xf):
        return xf * cos2 + pltpu.roll(xf, dk // 2, 1) * sin2

    for h in range(RET_HEADS):
        cols = slice(h * dk, (h + 1) * dk)
        q = rot(q_ref[0, :, cols].astype(F32))
        k = rot(k_ref[0, :, cols].astype(F32)) * (dk ** -0.5)
        v = v_ref[0, :, cols]
        scores = _dot_nt(q.astype(BF16), k.astype(BF16)) * dmat_ref[h]
        intra = _dot(scores.astype(BF16), v)
        state = state_sc[h]
        cross = _dot((q * qdec_ref[h]).astype(BF16), state.astype(BF16))
        out = intra + cross
        state_sc[h] = state * sdec_ref[h, 0:1, :] + _dot_tn((k * kdec_ref[h]).astype(BF16), v)

        y = out * lax.rsqrt(jnp.mean(out * out, axis=-1, keepdims=True) + RMS_EPS)
        g = g_ref[0, :, cols].astype(F32)
        o_ref[0, :, cols] = (y * (g * jax.nn.sigmoid(g))).astype(o_ref.dtype)


def _ret_tables(seq):
    half = RET_HEAD_DIM // 2
    inv_freq = 1.0 / (ROPE_BASE ** (jnp.arange(half, dtype=F32) / half))
    ang = jnp.arange(seq, dtype=F32)[:, None] * inv_freq[None, :]
    cos = jnp.cos(ang)
    sin = jnp.sin(ang)
    cos2 = jnp.concatenate([cos, cos], axis=1)
    sin2 = jnp.concatenate([-sin, sin], axis=1)
    lt = RET_TILE
    log_gamma = jnp.log(1.0 - 2.0 ** (-5.0 - jnp.arange(RET_HEADS, dtype=F32)))
    idx = jnp.arange(lt)
    t = idx[:, None]
    s = idx[None, :]
    same = (t // CHUNK) == (s // CHUNK)
    earlier = (s // CHUNK) < (t // CHUNK)
    dist = jnp.where(same, jnp.abs(t - s), t - s).astype(F32)
    dmat = jnp.where((same | earlier)[None], jnp.exp(log_gamma[:, None, None] * dist[None]), 0.0)
    idxf = idx.astype(F32)
    qdec = jnp.exp(log_gamma[:, None] * idxf[None, :])
    kdec = jnp.exp(log_gamma[:, None] * (lt - idxf)[None, :])
    sdec = jnp.exp(log_gamma * lt)
    qdec = jnp.broadcast_to(qdec[:, :, None], (RET_HEADS, lt, LANES))
    kdec = jnp.broadcast_to(kdec[:, :, None], (RET_HEADS, lt, LANES))
    sdec = jnp.broadcast_to(sdec[:, None, None], (RET_HEADS, 8, LANES))
    return cos2, sin2, dmat, qdec, kdec, sdec


def _retention(rq, rk, rv, rg, tables, batch, seq):
    lt = RET_TILE
    ns = seq // lt
    cos2, sin2, dmat, qdec, kdec, sdec = tables
    blk = pl.BlockSpec((1, lt, RET_WIDTH), lambda b, i: (b, i, 0))
    tab = pl.BlockSpec((lt, LANES), lambda b, i: (i, 0))
    args = [a.reshape(batch, seq, RET_WIDTH) for a in (rq, rk, rv, rg)]
    out = pl.pallas_call(
        _ret_kernel,
        grid=(batch, ns),
        in_specs=[blk, blk, blk, blk, tab, tab,
                  pl.BlockSpec((RET_HEADS, lt, lt), lambda b, i: (0, 0, 0)),
                  pl.BlockSpec((RET_HEADS, lt, LANES), lambda b, i: (0, 0, 0)),
                  pl.BlockSpec((RET_HEADS, lt, LANES), lambda b, i: (0, 0, 0)),
                  pl.BlockSpec((RET_HEADS, 8, LANES), lambda b, i: (0, 0, 0))],
        out_specs=blk,
        out_shape=jax.ShapeDtypeStruct((batch, seq, RET_WIDTH), BF16),
        scratch_shapes=[pltpu.VMEM((RET_HEADS, RET_HEAD_DIM, RET_HEAD_DIM), F32)],
        compiler_params=pltpu.CompilerParams(
            dimension_semantics=("parallel", "arbitrary"),
            vmem_limit_bytes=VMEM_LIMIT),
        name="retention",
    )(*args, cos2, sin2, dmat, qdec, kdec, sdec)
    return out.reshape(batch * seq, RET_WIDTH)


def _outproj_kernel(fox_ref, ret_ref, x_ref, wo_ref, nw_ref, wr_ref, br_ref, xo_ref, xg_ref,
                    eid_ref, rw_ref):
    mixed = jnp.concatenate([fox_ref[...], ret_ref[...]], axis=1)
    x = x_ref[...] + _dot(mixed, wo_ref[...])
    xo_ref[...] = x
    _tm_store(xg_ref, 0, x.shape[0], x)
    h = _rms(x, nw_ref[...]).astype(BF16)
    lt = (_dot(h, wr_ref[...]) + br_ref[...]).T
    tm = lt.shape[1]
    rowid = lax.broadcasted_iota(jnp.int32, (8, tm), 0)
    neg = -jnp.inf

    def top1(v):
        vmax = jnp.max(v, axis=0, keepdims=True)
        idx = jnp.min(jnp.where(v == vmax, rowid, 8), axis=0, keepdims=True)
        return vmax, idx

    gl = jnp.where(rowid < N_GROUPS, lt[0:8], neg)
    gmax, gidx = top1(gl)
    g_w = 1.0 / jnp.sum(jnp.exp(gl - gmax), axis=0, keepdims=True)
    e_in = jnp.zeros((8, tm), F32)
    for g in range(N_GROUPS):
        e_in = jnp.where(gidx == g, lt[8 + 8 * g:16 + 8 * g], e_in)
    v1, i1 = top1(e_in)
    rest = jnp.where(rowid == i1, neg, e_in)
    v2, i2 = top1(rest)
    t = jnp.exp(v2 - v1)
    w1 = g_w / (1.0 + t)
    eid_ref[0:1, :] = gidx * EXPERTS_PER_GROUP + i1
    eid_ref[1:2, :] = gidx * EXPERTS_PER_GROUP + i2
    wslab = jnp.concatenate([w1, w1 * t, jnp.zeros((LANES - TOP_K, tm), F32)], axis=0)
    rw_ref[...] = wslab.T


def _outproj(fox, ret, x, wo, nw, wr, br):
    n = x.shape[0]
    tm = WIDE_TILE
    row = pl.BlockSpec((tm, D_MODEL), lambda i: (i, 0))
    half = pl.BlockSpec((tm, GROUP_W), lambda i: (i, 0))
    pair = pl.BlockSpec((TOP_K, tm), lambda i: (0, i))
    wts = pl.BlockSpec((tm, LANES), lambda i: (i, 0))
    return pl.pallas_call(
        _outproj_kernel,
        grid=(n // tm,),
        in_specs=[half, half, row,
                  pl.BlockSpec((D_MODEL, D_MODEL), lambda i: (0, 0)),
                  pl.BlockSpec((1, D_MODEL), lambda i: (0, 0)),
                  pl.BlockSpec((D_MODEL, LANES), lambda i: (0, 0)),
                  pl.BlockSpec((1, LANES), lambda i: (0, 0))],
        out_specs=[row, pl.BlockSpec((tm * CHUNKS, LANES), lambda i: (i, 0)), pair, wts],
        out_shape=[jax.ShapeDtypeStruct((n, D_MODEL), F32),
                   jax.ShapeDtypeStruct((n * CHUNKS, LANES), F32),
                   jax.ShapeDtypeStruct((TOP_K, n), jnp.int32),
                   jax.ShapeDtypeStruct((n, LANES), F32)],
        compiler_params=pltpu.CompilerParams(
            dimension_semantics=("parallel",), vmem_limit_bytes=VMEM_LIMIT),
        name="outproj",
    )(fox, ret, x, wo, nw, wr, br)


_FIRST, _LAST, _VALID, _NEWEXP = 1, 2, 4, 8


def _moe_kernel(blk_ref, exp_ref, flag_ref, starts_ref,
                tokc_ref, tokn_ref, dstp_ref, dstc_ref, x_hbm, nw_ref, wg_ref, wu_ref, wd_ref,
                y_hbm, xbuf, ybuf, hbuf, wgb, wub, wdb, gsem, ssem, *, tb, nb):
    w = pl.program_id(0)
    b = blk_ref[w]
    e = exp_ref[w]
    flags = flag_ref[w]
    slot = b % 2
    nslot = 1 - slot
    span = tb * CHUNKS

    def hbm_row(ref, idx):
        return ref.at[pl.ds(pl.multiple_of(idx * CHUNKS, CHUNKS), CHUNKS), :]

    def buf_row(buf, s, r):
        return buf.at[pl.ds(pl.multiple_of(s * span + r * CHUNKS, CHUNKS), CHUNKS), :]

    def start_gather(tok_ref, s):
        for r in range(tb):
            pltpu.make_async_copy(hbm_row(x_hbm, tok_ref[0, 0, r]), buf_row(xbuf, s, r),
                                  gsem.at[s]).start()

    def start_scatter(dst_ref, s):
        for r in range(tb):
            pltpu.make_async_copy(buf_row(ybuf, s, r), hbm_row(y_hbm, dst_ref[0, 0, r]),
                                  ssem.at[s]).start()

    def wait_rows(sem, s):
        whole = pl.ds(pl.multiple_of(s * span, span), span)
        pltpu.make_async_copy(xbuf.at[whole, :], ybuf.at[whole, :], sem.at[s]).wait()

    @pl.when((flags & _FIRST) != 0)
    def _():
        @pl.when(w == 0)
        def _():
            start_gather(tokc_ref, 0)

        wait_rows(gsem, slot)

        @pl.when(b >= 2)
        def _():
            wait_rows(ssem, slot)

        @pl.when(b + 1 < nb)
        def _():
            start_gather(tokn_ref, nslot)

        @pl.when(b >= 1)
        def _():
            start_scatter(dstp_ref, nslot)

        hbuf[...] = _rms(_tm_load(xbuf, slot * span, tb), nw_ref[...]).astype(BF16)

        @pl.when((flags & _LAST) == 0)
        def _():
            ybuf[pl.ds(pl.multiple_of(slot * span, span), span), :] = jnp.zeros((span, LANES), F32)

    @pl.when((flags & _NEWEXP) != 0)
    def _():
        wgb[...] = wg_ref[0, 0].astype(BF16)
        wub[...] = wu_ref[0, 0].astype(BF16)
        wdb[...] = wd_ref[0, 0].astype(BF16)

    @pl.when((flags & _VALID) != 0)
    def _():
        h = hbuf[...]
        g = _dot(h, wgb[...])
        u = _dot(h, wub[...])
        a = (g * jax.nn.sigmoid(g) * u).astype(BF16)
        y = _dot(a, wdb[...])
        whole_block = (flags & (_FIRST | _LAST)) == (_FIRST | _LAST)

        @pl.when(whole_block)
        def _():
            _tm_store(ybuf, slot * span, tb, y)

        @pl.when(jnp.logical_not(whole_block))
        def _():
            q = b * tb + lax.broadcasted_iota(jnp.int32, y.shape, 0)
            mine = (q >= starts_ref[e]) & (q < starts_ref[e + 1])
            _tm_store(ybuf, slot * span, tb, jnp.where(mine, y, _tm_load(ybuf, slot * span, tb)))

    @pl.when(((flags & _LAST) != 0) & (b == nb - 1))
    def _():
        start_scatter(dstc_ref, slot)
        if nb >= 2:
            wait_rows(ssem, nslot)
        wait_rows(ssem, slot)


def _moe(x, s_tok, s_dst, blk, exp, flags, starts, nw, wg, wu, wd, layer, tb=MOE_TILE):
    na = s_tok.shape[0]
    nb = na // tb
    d_model, d_expert = wg.shape[2], wg.shape[3]
    tok3 = s_tok.reshape(nb, 1, tb)
    dst3 = s_dst.reshape(nb, 1, tb)
    smem_blk = lambda f: pl.BlockSpec((1, 1, tb), f, memory_space=pltpu.SMEM)
    grid_spec = pltpu.PrefetchScalarGridSpec(
        num_scalar_prefetch=4,
        grid=(blk.shape[0],),
        in_specs=[
            smem_blk(lambda w, bl, ex, fl, st: (bl[w], 0, 0)),
            smem_blk(lambda w, bl, ex, fl, st: (jnp.minimum(bl[w] + 1, nb - 1), 0, 0)),
            smem_blk(lambda w, bl, ex, fl, st: (jnp.maximum(bl[w] - 1, 0), 0, 0)),
            smem_blk(lambda w, bl, ex, fl, st: (bl[w], 0, 0)),
            pl.BlockSpec(memory_space=pl.ANY),
            pl.BlockSpec((1, d_model), lambda w, bl, ex, fl, st: (0, 0)),
            pl.BlockSpec((1, 1, d_model, d_expert), lambda w, bl, ex, fl, st: (layer, ex[w], 0, 0)),
            pl.BlockSpec((1, 1, d_model, d_expert), lambda w, bl, ex, fl, st: (layer, ex[w], 0, 0)),
            pl.BlockSpec((1, 1, d_expert, d_model), lambda w, bl, ex, fl, st: (layer, ex[w], 0, 0)),
        ],
        out_specs=pl.BlockSpec(memory_space=pl.ANY),
        scratch_shapes=[pltpu.VMEM((2 * tb * CHUNKS, LANES), F32),
                        pltpu.VMEM((2 * tb * CHUNKS, LANES), F32),
                        pltpu.VMEM((tb, d_model), BF16),
                        pltpu.VMEM((d_model, d_expert), BF16), pltpu.VMEM((d_model, d_expert), BF16),
                        pltpu.VMEM((d_expert, d_model), BF16),
                        pltpu.SemaphoreType.DMA((2,)), pltpu.SemaphoreType.DMA((2,))],
    )
    return pl.pallas_call(
        functools.partial(_moe_kernel, tb=tb, nb=nb),
        grid_spec=grid_spec,
        out_shape=jax.ShapeDtypeStruct((na * CHUNKS, LANES), F32),
        compiler_params=pltpu.CompilerParams(
            dimension_semantics=("arbitrary",), vmem_limit_bytes=VMEM_LIMIT),
        name="moe_experts",
    )(blk, exp, flags, starts, tok3, tok3, dst3, dst3, x, nw, wg, wu, wd)


def _dispatch_plan(eid, n, tb=MOE_TILE, n_experts=N_EXPERTS):
    na = TOP_K * n
    nb = na // tb
    eid_flat = eid.reshape(na)
    _, s_a = lax.sort((eid_flat, jnp.arange(na, dtype=jnp.int32)), num_keys=1)
    s_tok = s_a % n
    counts = jnp.sum(eid_flat[None, :] == jnp.arange(n_experts, dtype=jnp.int32)[:, None], axis=1)
    starts = jnp.concatenate([jnp.zeros((1,), jnp.int32),
                              jnp.cumsum(counts).astype(jnp.int32)])
    lo, hi = starts[:-1], starts[1:]
    nonempty = hi > lo
    first_blk = lo // tb
    npass = jnp.where(nonempty, (hi - 1) // tb - first_blk + 1, 0)
    cum = jnp.cumsum(npass)
    total = cum[-1]
    n_pass = nb + n_experts
    w = jnp.arange(n_pass, dtype=jnp.int32)
    wc = jnp.minimum(w, total - 1)
    ex = jnp.sum(cum[None, :] <= wc[:, None], axis=1).astype(jnp.int32)
    sel = ex[:, None] == jnp.arange(n_experts, dtype=jnp.int32)[None, :]
    pick = lambda v: jnp.sum(jnp.where(sel, v[None, :], 0), axis=1)
    blk = (pick(first_blk) + (wc - pick(cum - npass))).astype(jnp.int32)
    valid = w < total
    prev_blk = jnp.concatenate([jnp.full((1,), -1, jnp.int32), blk[:-1]])
    next_blk = jnp.concatenate([blk[1:], jnp.full((1,), -1, jnp.int32)])
    first = valid & (blk != prev_blk)
    last = valid & ((blk != next_blk) | (w == total - 1))
    prev_ex = jnp.concatenate([jnp.full((1,), -1, jnp.int32), ex[:-1]])
    newexp = valid & (ex != prev_ex)
    flags = (first * _FIRST + last * _LAST + valid * _VALID + newexp * _NEWEXP).astype(jnp.int32)
    return s_tok, s_a, blk, ex, flags, starts


def _final_kernel(x_ref, y0_ref, y1_ref, rw_ref, nw_ref, o_ref):
    rw = rw_ref[...]
    tm = x_ref.shape[0]
    x = x_ref[...] + (rw[:, 0:1] * _tm_load(y0_ref, 0, tm) + rw[:, 1:2] * _tm_load(y1_ref, 0, tm))
    o_ref[...] = _rms(x, nw_ref[...])


def _final(x, y, rw, nw):
    n = x.shape[0]
    tm = WIDE_TILE
    nt = n // tm
    row = pl.BlockSpec((tm, D_MODEL), lambda i: (i, 0))
    return pl.pallas_call(
        _final_kernel,
        grid=(nt,),
        in_specs=[row, pl.BlockSpec((tm * CHUNKS, LANES), lambda i: (i, 0)),
                  pl.BlockSpec((tm * CHUNKS, LANES), lambda i: (i + nt, 0)),
                  pl.BlockSpec((tm, LANES), lambda i: (i, 0)),
                  pl.BlockSpec((1, D_MODEL), lambda i: (0, 0))],
        out_specs=row,
        out_shape=jax.ShapeDtypeStruct((n, D_MODEL), F32),
        compiler_params=pltpu.CompilerParams(
            dimension_semantics=("parallel",), vmem_limit_bytes=VMEM_LIMIT),
        name="final_norm",
    )(x, y, y, rw, nw)


def kernel(x, norm_mix_w, w_in, fox_forget_b, w_out, norm_ffn_w, w_router_group, b_router_group,
           w_router_expert, b_router_expert, w_expert_gate, w_expert_up, w_expert_down,
           norm_final_w):
    batch, seq, d = x.shape
    n = batch * seq
    depth = w_in.shape[0]
    xf = x.reshape(n, d)
    tables = _ret_tables(seq)
    nq = seq // ATT_TILE

    y = rw = None
    for layer in range(depth):
        wl = w_in[layer]
        c0 = 3 * FOX_WIDTH
        w_ff = jnp.pad(wl[:, c0:c0 + FOX_HEADS], ((0, 0), (0, LANES - FOX_HEADS)))
        w_main = jnp.concatenate([wl[:, :c0], wl[:, c0 + FOX_HEADS:], w_ff], axis=1).astype(BF16)
        b_ff = jnp.pad(fox_forget_b[layer], (0, LANES - FOX_HEADS)).reshape(1, LANES)
        xf, (fq, fk, fv, rq, rk, rv, rg, ct) = _inproj(
            xf, y, rw, norm_mix_w[layer].reshape(1, d), w_main, b_ff, seq)
        ct4 = ct.reshape(batch, 8, nq, ATT_TILE)
        fox = _fox_attention(fq, fk, fv, ct4, batch, seq)
        ret = _retention(rq, rk, rv, rg, tables, batch, seq)

        zpad = jnp.zeros((d, N_GROUPS), F32)
        w_r = jnp.concatenate([w_router_group[layer], zpad, w_router_expert[layer]], axis=1)
        nr = 2 * N_GROUPS + N_EXPERTS
        w_r = jnp.pad(w_r, ((0, 0), (0, LANES - nr))).astype(BF16)
        b_r = jnp.concatenate([b_router_group[layer], jnp.zeros((N_GROUPS,), F32),
                               b_router_expert[layer]])
        b_r = jnp.pad(b_r, (0, LANES - nr)).reshape(1, LANES)
        xf, xg, eid, rw = _outproj(fox, ret, xf, w_out[layer].astype(BF16),
                                   norm_ffn_w[layer].reshape(1, d), w_r, b_r)
        s_tok, s_dst, blk, ex, flags, starts = _dispatch_plan(eid, n)
        y = _moe(xg, s_tok, s_dst, blk, ex, flags, starts, norm_ffn_w[layer].reshape(1, d),
                 w_expert_gate, w_expert_up, w_expert_down, layer)
    out = _final(xf, y, rw, norm_final_w.reshape(1, d))
    return out.reshape(batch, seq, d)
```

```python
import functools

import jax
import jax.numpy as jnp
from jax import lax
from jax.experimental import pallas as pl
from jax.experimental.pallas import tpu as pltpu

F32 = jnp.float32
BF16 = jnp.bfloat16

D_MODEL = 1024
FOX_HEADS = 8
FOX_HEAD_DIM = 64
FOX_WIDTH = 512
RET_HEADS = 4
RET_HEAD_DIM = 128
RET_WIDTH = 512
CHUNK = 64
ROPE_BASE = 10000.0
N_GROUPS = 4
EXPERTS_PER_GROUP = 8
N_EXPERTS = 32
TOP_K = 2
D_EXPERT = 512
RMS_EPS = 1e-6

LANES = 128
VMEM_LIMIT = 56 * 1024 * 1024

ROW_TILE = 512
WIDE_TILE = 1024
ATT_TILE = 512
RET_TILE = 512
MOE_TILE = 256
GROUP_W = FOX_WIDTH
SLAB = EXPERTS_PER_GROUP
N_MAIN = 7 * GROUP_W
EXP_UNDERFLOW = 110.0


def _rms(xf, w):
    return xf * lax.rsqrt(jnp.mean(xf * xf, axis=-1, keepdims=True) + RMS_EPS) * w


def _dot(a, b):
    return jnp.dot(a, b, preferred_element_type=F32)


def _dot_nt(a, b):
    return lax.dot_general(a, b, (((1,), (1,)), ((), ())), preferred_element_type=F32)


def _dot_tn(a, b):
    return lax.dot_general(a, b, (((0,), (0,)), ((), ())), preferred_element_type=F32)


CHUNKS = D_MODEL // LANES


def _tm_load(ref, base, rows):
    return jnp.concatenate([ref[pl.ds(base + c, rows, stride=CHUNKS), :] for c in range(CHUNKS)],
                           axis=1)


def _tm_store(ref, base, rows, val):
    for c in range(CHUNKS):
        ref[pl.ds(base + c, rows, stride=CHUNKS), :] = val[:, c * LANES:(c + 1) * LANES]


def _inproj_kernel(*refs, has_y, tiles_per_seq):
    if has_y:
        x_ref, y0_ref, y1_ref, rw_ref = refs[:4]
        refs = refs[4:]
    else:
        x_ref = refs[0]
        refs = refs[1:]
    nw_ref, w_ref, bff_ref, tri_ref = refs[:4]
    refs = refs[4:]
    if has_y:
        xres_ref = refs[0]
        refs = refs[1:]
    fq_ref, fk_ref, fv_ref, rq_ref, rk_ref, rv_ref, rg_ref, ct_ref, carry_sc = refs

    i = pl.program_id(0)
    x = x_ref[...]
    if has_y:
        rw = rw_ref[...]
        tm = x.shape[0]
        x = x + (rw[:, 0:1] * _tm_load(y0_ref, 0, tm) + rw[:, 1:2] * _tm_load(y1_ref, 0, tm))
        xres_ref[...] = x
    h = _rms(x, nw_ref[...]).astype(BF16)

    outs = (fq_ref, fk_ref, fv_ref, rq_ref, rk_ref, rv_ref)
    for j, o_ref in enumerate(outs):
        acc = _dot(h, w_ref[:, j * GROUP_W:(j + 1) * GROUP_W])
        if j == 0:
            acc = acc * (FOX_HEAD_DIM ** -0.5)
        o_ref[...] = acc.astype(BF16)
    acc = _dot(h, w_ref[:, 6 * GROUP_W:])
    rg_ref[...] = acc[:, :GROUP_W].astype(BF16)

    z = acc[:, GROUP_W:] + bff_ref[...]
    lf = jnp.minimum(z, 0.0) - jnp.log1p(jnp.exp(-jnp.abs(z)))
    lft = lf.T[:FOX_HEADS, :]
    hi = lft.astype(BF16).astype(F32)
    mid = (lft - hi).astype(BF16).astype(F32)
    lo = lft - hi - mid
    pieces = jnp.concatenate([hi, mid, lo, jnp.zeros_like(hi)], axis=0).astype(BF16)
    cs = _dot(pieces, tri_ref[...])
    cs = cs[0:FOX_HEADS] + cs[FOX_HEADS:2 * FOX_HEADS] + cs[2 * FOX_HEADS:3 * FOX_HEADS]

    @pl.when(i % tiles_per_seq == 0)
    def _():
        carry_sc[...] = jnp.zeros_like(carry_sc)

    c = cs + carry_sc[:, 0:1]
    carry_sc[...] = jnp.broadcast_to(c[:, -1:], carry_sc.shape)
    ct_ref[0] = c


def _inproj(x, y, rw, nw, w_main, b_ff, seq):
    n = x.shape[0]
    tm = ROW_TILE
    tri = jnp.triu(jnp.ones((tm, tm), F32)).astype(BF16)
    nt = n // tm
    tps = seq // tm
    has_y = y is not None
    row_spec = pl.BlockSpec((tm, D_MODEL), lambda i: (i, 0))
    in_specs = [row_spec]
    args = [x]
    if has_y:
        in_specs += [pl.BlockSpec((tm * CHUNKS, LANES), lambda i: (i, 0)),
                     pl.BlockSpec((tm * CHUNKS, LANES), lambda i: (i + nt, 0)),
                     pl.BlockSpec((tm, LANES), lambda i: (i, 0))]
        args += [y, y, rw]
    in_specs += [
        pl.BlockSpec((1, D_MODEL), lambda i: (0, 0)),
        pl.BlockSpec((D_MODEL, N_MAIN + LANES), lambda i: (0, 0)),
        pl.BlockSpec((1, LANES), lambda i: (0, 0)),
        pl.BlockSpec((tm, tm), lambda i: (0, 0)),
    ]
    args += [nw, w_main, b_ff, tri]
    half_spec = pl.BlockSpec((tm, GROUP_W), lambda i: (i, 0))
    out_shape = []
    out_specs = []
    if has_y:
        out_shape.append(jax.ShapeDtypeStruct((n, D_MODEL), F32))
        out_specs.append(row_spec)
    out_shape += [jax.ShapeDtypeStruct((n, GROUP_W), BF16)] * 7
    out_specs += [half_spec] * 7
    out_shape.append(jax.ShapeDtypeStruct((n // seq, 8, seq), F32))
    out_specs.append(pl.BlockSpec((1, 8, tm), lambda i: (i // tps, 0, i % tps)))
    outs = pl.pallas_call(
        functools.partial(_inproj_kernel, has_y=has_y, tiles_per_seq=tps),
        grid=(nt,),
        in_specs=in_specs,
        out_specs=out_specs,
        out_shape=out_shape,
        scratch_shapes=[pltpu.VMEM((8, LANES), F32)],
        compiler_params=pltpu.CompilerParams(
            dimension_semantics=("arbitrary",), vmem_limit_bytes=VMEM_LIMIT),
        name="inproj_y" if has_y else "inproj",
    )(*args)
    if has_y:
        return outs[0], outs[1:]
    return x, outs


def _fox_kernel(q_ref, k_ref, v_ref, ct_ref, o_ref, m_sc, acc_sc, kmax_sc, *, tile, nq):
    hp = pl.program_id(1)
    qi = pl.program_id(2)
    q2 = q_ref[0]
    lane = lax.broadcasted_iota(jnp.int32, q2.shape, 1)
    first = lane < FOX_HEAD_DIM
    zero = jnp.zeros_like(q2)
    qh = (jnp.where(first, q2, zero), jnp.where(first, zero, q2))
    reps = tile // LANES

    def head_sqnorm_max(xf, h):
        sq = xf * xf
        sq = jnp.where(first, sq, 0.0) if h == 0 else jnp.where(first, 0.0, sq)
        return jnp.max(jnp.sum(sq, axis=1, keepdims=True), axis=0, keepdims=True)

    @pl.when(qi == 0)
    def _():
        for j in range(nq):
            kf = k_ref[0, j * tile:(j + 1) * tile, :].astype(F32)
            for h in range(2):
                kmax_sc[h, j:j + 1, :] = jnp.broadcast_to(head_sqnorm_max(kf, h), (1, LANES))

    def head_step(h, kb, k_blk, v_blk, mask, m_old, acc_old):
        one = jnp.ones_like(v_blk)
        va = jnp.where(first, v_blk, one) if h == 0 else jnp.where(first, one, v_blk)
        s = _dot_nt(qh[h], k_blk) - ct_ref[0, 2 * hp + h, pl.ds(kb, 1), :]
        if mask is not None:
            s = jnp.where(mask, s, -jnp.inf)
        m_cur = jnp.max(s, axis=1, keepdims=True)
        if m_old is None:
            m_new = jnp.broadcast_to(m_cur, (tile, LANES))
            p = jnp.exp(s - jnp.concatenate([m_new] * reps, axis=1))
            acc = _dot(p.astype(BF16), va)
        else:
            m_new = jnp.maximum(m_old, m_cur)
            alpha = jnp.exp(m_old - m_new)
            p = jnp.exp(s - jnp.concatenate([m_new] * reps, axis=1))
            acc = alpha * acc_old + _dot(p.astype(BF16), va)
        return m_new, acc

    def diagonal(with_previous):
        row = lax.broadcasted_iota(jnp.int32, (tile, tile), 0)
        col = lax.broadcasted_iota(jnp.int32, (tile, tile), 1)
        start = pl.multiple_of(qi * tile, tile)
        k_blk = k_ref[0, pl.ds(start, tile), :]
        v_blk = v_ref[0, pl.ds(start, tile), :]
        state = [head_step(h, qi, k_blk, v_blk, col <= row, None, None) for h in range(2)]
        if with_previous:
            start = pl.multiple_of((qi - 1) * tile, tile)
            k_blk = k_ref[0, pl.ds(start, tile), :]
            v_blk = v_ref[0, pl.ds(start, tile), :]
            state = [head_step(h, qi - 1, k_blk, v_blk, None, *state[h]) for h in range(2)]
        for h in range(2):
            m_sc[h] = state[h][0]
            acc_sc[h] = state[h][1]

    @pl.when(qi == 0)
    def _():
        diagonal(False)

    @pl.when(qi > 0)
    def _():
        diagonal(True)

    top = jnp.maximum(qi - 1, 0)

    def off_diagonal(kbs, heads=(0, 1)):
        state = {h: (m_sc[h], acc_sc[h]) for h in heads}
        for kb in kbs:
            start = pl.multiple_of(kb * tile, tile)
            k_blk = k_ref[0, pl.ds(start, tile), :]
            v_blk = v_ref[0, pl.ds(start, tile), :]
            state = {h: head_step(h, kb, k_blk, v_blk, None, *state[h]) for h in heads}
        for h in heads:
            m_sc[h] = state[h][0]
            acc_sc[h] = state[h][1]

    jrow = lax.broadcasted_iota(jnp.int32, (nq, LANES), 0)
    qf = q2.astype(F32)
    j_start = []
    for h in range(2):
        m_low = jnp.min(jnp.min(m_sc[h], axis=1, keepdims=True), axis=0, keepdims=True)
        c_last = ct_ref[0, 2 * hp + h, :, tile - 1:tile]
        bound = jnp.sqrt(head_sqnorm_max(qf, h) * kmax_sc[h]) - c_last
        need = (bound - m_low > -EXP_UNDERFLOW) & (jrow < top)
        j_first = jnp.min(jnp.min(jnp.where(need, jrow, top), axis=1, keepdims=True),
                          axis=0, keepdims=True)
        j_start.append(j_first[0, 0])
    j_both = jnp.maximum(j_start[0], j_start[1])
    count = top - j_both

    def pair(j, carry):
        off_diagonal((j_both + 2 * j, j_both + 2 * j + 1))
        return carry

    lax.fori_loop(0, count // 2, pair, 0)

    @pl.when(count % 2 == 1)
    def _():
        off_diagonal((top - 1,))

    for h in range(2):
        lone = j_both - j_start[h]

        def single_pair(j, carry, h=h):
            off_diagonal((j_start[h] + 2 * j, j_start[h] + 2 * j + 1), heads=(h,))
            return carry

        lax.fori_loop(0, lone // 2, single_pair, 0)

        @pl.when(lone % 2 == 1)
        def _(h=h):
            off_diagonal((j_both - 1,), heads=(h,))

    a0 = acc_sc[0]
    a1 = acc_sc[1]
    half = FOX_HEAD_DIM
    o = jnp.where(first, a0 / pltpu.roll(a0, half, 1), a1 / pltpu.roll(a1, half, 1))
    o_ref[0] = o.astype(o_ref.dtype)


def _fox_attention(fq, fk, fv, ct4, batch, seq):
    t = ATT_TILE
    nq = seq // t
    q3 = fq.reshape(batch, seq, FOX_WIDTH)
    k3 = fk.reshape(batch, seq, FOX_WIDTH)
    v3 = fv.reshape(batch, seq, FOX_WIDTH)
    out = pl.pallas_call(
        functools.partial(_fox_kernel, tile=t, nq=nq),
        grid=(batch, FOX_HEADS // 2, nq),
        in_specs=[
            pl.BlockSpec((1, t, LANES), lambda b, j, i: (b, i, j)),
            pl.BlockSpec((1, seq, LANES), lambda b, j, i: (b, 0, j)),
            pl.BlockSpec((1, seq, LANES), lambda b, j, i: (b, 0, j)),
            pl.BlockSpec((1, 8, nq, t), lambda b, j, i: (b, 0, 0, 0)),
        ],
        out_specs=pl.BlockSpec((1, t, LANES), lambda b, j, i: (b, i, j)),
        out_shape=jax.ShapeDtypeStruct((batch, seq, FOX_WIDTH), BF16),
        scratch_shapes=[pltpu.VMEM((2, t, LANES), F32), pltpu.VMEM((2, t, LANES), F32),
                        pltpu.VMEM((2, nq, LANES), F32)],
        compiler_params=pltpu.CompilerParams(
            dimension_semantics=("parallel", "parallel", "arbitrary"),
            vmem_limit_bytes=VMEM_LIMIT),
        name="fox_attention",
    )(q3, k3, v3, ct4)
    return out.reshape(batch * seq, FOX_WIDTH)


def _ret_kernel(q_ref, k_ref, v_ref, g_ref, cos_ref, sin_ref, dmat_ref, qdec_ref, kdec_ref,
                sdec_ref, o_ref, state_sc):
    si = pl.program_id(1)

    @pl.when(si == 0)
    def _():
        state_sc[...] = jnp.zeros_like(state_sc)

    cos2 = cos_ref[...]
    sin2 = sin_ref[...]
    dk = RET_HEAD_DIM

    def rot(xf):
        return xf * cos2 + pltpu.roll(xf, dk // 2, 1) * sin2

    for h in range(RET_HEADS):
        cols = slice(h * dk, (h + 1) * dk)
        q = rot(q_ref[0, :, cols].astype(F32))
        k = rot(k_ref[0, :, cols].astype(F32)) * (dk ** -0.5)
        v = v_ref[0, :, cols]
        scores = _dot_nt(q.astype(BF16), k.astype(BF16)) * dmat_ref[h]
        intra = _dot(scores.astype(BF16), v)
        state = state_sc[h]
        cross = _dot((q * qdec_ref[h]).astype(BF16), state.astype(BF16))
        out = intra + cross
        state_sc[h] = state * sdec_ref[h, 0:1, :] + _dot_tn((k * kdec_ref[h]).astype(BF16), v)

        y = out * lax.rsqrt(jnp.mean(out * out, axis=-1, keepdims=True) + RMS_EPS)
        g = g_ref[0, :, cols].astype(F32)
        o_ref[0, :, cols] = (y * (g * jax.nn.sigmoid(g))).astype(o_ref.dtype)


def _ret_tables(seq):
    half = RET_HEAD_DIM // 2
    inv_freq = 1.0 / (ROPE_BASE ** (jnp.arange(half, dtype=F32) / half))
    ang = jnp.arange(seq, dtype=F32)[:, None] * inv_freq[None, :]
    cos = jnp.cos(ang)
    sin = jnp.sin(ang)
    cos2 = jnp.concatenate([cos, cos], axis=1)
    sin2 = jnp.concatenate([-sin, sin], axis=1)
    lt = RET_TILE
    log_gamma = jnp.log(1.0 - 2.0 ** (-5.0 - jnp.arange(RET_HEADS, dtype=F32)))
    idx = jnp.arange(lt)
    t = idx[:, None]
    s = idx[None, :]
    same = (t // CHUNK) == (s // CHUNK)
    earlier = (s // CHUNK) < (t // CHUNK)
    dist = jnp.where(same, jnp.abs(t - s), t - s).astype(F32)
    dmat = jnp.where((same | earlier)[None], jnp.exp(log_gamma[:, None, None] * dist[None]), 0.0)
    idxf = idx.astype(F32)
    qdec = jnp.exp(log_gamma[:, None] * idxf[None, :])
    kdec = jnp.exp(log_gamma[:, None] * (lt - idxf)[None, :])
    sdec = jnp.exp(log_gamma * lt)
    qdec = jnp.broadcast_to(qdec[:, :, None], (RET_HEADS, lt, LANES))
    kdec = jnp.broadcast_to(kdec[:, :, None], (RET_HEADS, lt, LANES))
    sdec = jnp.broadcast_to(sdec[:, None, None], (RET_HEADS, 8, LANES))
    return cos2, sin2, dmat, qdec, kdec, sdec


def _retention(rq, rk, rv, rg, tables, batch, seq):
    lt = RET_TILE
    ns = seq // lt
    cos2, sin2, dmat, qdec, kdec, sdec = tables
    blk = pl.BlockSpec((1, lt, RET_WIDTH), lambda b, i: (b, i, 0))
    tab = pl.BlockSpec((lt, LANES), lambda b, i: (i, 0))
    args = [a.reshape(batch, seq, RET_WIDTH) for a in (rq, rk, rv, rg)]
    out = pl.pallas_call(
        _ret_kernel,
        grid=(batch, ns),
        in_specs=[blk, blk, blk, blk, tab, tab,
                  pl.BlockSpec((RET_HEADS, lt, lt), lambda b, i: (0, 0, 0)),
                  pl.BlockSpec((RET_HEADS, lt, LANES), lambda b, i: (0, 0, 0)),
                  pl.BlockSpec((RET_HEADS, lt, LANES), lambda b, i: (0, 0, 0)),
                  pl.BlockSpec((RET_HEADS, 8, LANES), lambda b, i: (0, 0, 0))],
        out_specs=blk,
        out_shape=jax.ShapeDtypeStruct((batch, seq, RET_WIDTH), BF16),
        scratch_shapes=[pltpu.VMEM((RET_HEADS, RET_HEAD_DIM, RET_HEAD_DIM), F32)],
        compiler_params=pltpu.CompilerParams(
            dimension_semantics=("parallel", "arbitrary"),
            vmem_limit_bytes=VMEM_LIMIT),
        name="retention",
    )(*args, cos2, sin2, dmat, qdec, kdec, sdec)
    return out.reshape(batch * seq, RET_WIDTH)


def _outproj_kernel(fox_ref, ret_ref, x_ref, wo_ref, nw_ref, wr_ref, br_ref, xo_ref, xg_ref,
                    eid_ref, rw_ref):
    mixed = jnp.concatenate([fox_ref[...], ret_ref[...]], axis=1)
    x = x_ref[...] + _dot(mixed, wo_ref[...])
    xo_ref[...] = x
    hf = _rms(x, nw_ref[...])
    _tm_store(xg_ref, 0, x.shape[0], hf)
    h = hf.astype(BF16)
    lt = (_dot(h, wr_ref[...]) + br_ref[...]).T
    tm = lt.shape[1]
    rowid = lax.broadcasted_iota(jnp.int32, (SLAB, tm), 0)
    neg = -jnp.inf

    def top1(v):
        vmax = jnp.max(v, axis=0, keepdims=True)
        idx = jnp.min(jnp.where(v == vmax, rowid, SLAB), axis=0, keepdims=True)
        return vmax, idx

    gl = jnp.where(rowid < N_GROUPS, lt[0:SLAB], neg)
    gmax, gidx = top1(gl)
    g_w = 1.0 / jnp.sum(jnp.exp(gl - gmax), axis=0, keepdims=True)
    e_in = jnp.zeros((SLAB, tm), F32)
    for g in range(N_GROUPS):
        e_in = jnp.where(gidx == g, lt[SLAB * (g + 1):SLAB * (g + 2)], e_in)
    v1, i1 = top1(e_in)
    rest = jnp.where(rowid == i1, neg, e_in)
    v2, i2 = top1(rest)
    t = jnp.exp(v2 - v1)
    w1 = g_w / (1.0 + t)
    eid_ref[0:1, :] = gidx * EXPERTS_PER_GROUP + i1
    eid_ref[1:2, :] = gidx * EXPERTS_PER_GROUP + i2
    wslab = jnp.concatenate([w1, w1 * t, jnp.zeros((LANES - TOP_K, tm), F32)], axis=0)
    rw_ref[...] = wslab.T


def _outproj(fox, ret, x, wo, nw, wr, br):
    n = x.shape[0]
    tm = WIDE_TILE
    row = pl.BlockSpec((tm, D_MODEL), lambda i: (i, 0))
    half = pl.BlockSpec((tm, GROUP_W), lambda i: (i, 0))
    pair = pl.BlockSpec((TOP_K, tm), lambda i: (0, i))
    wts = pl.BlockSpec((tm, LANES), lambda i: (i, 0))
    return pl.pallas_call(
        _outproj_kernel,
        grid=(n // tm,),
        in_specs=[half, half, row,
                  pl.BlockSpec((D_MODEL, D_MODEL), lambda i: (0, 0)),
                  pl.BlockSpec((1, D_MODEL), lambda i: (0, 0)),
                  pl.BlockSpec((D_MODEL, LANES), lambda i: (0, 0)),
                  pl.BlockSpec((1, LANES), lambda i: (0, 0))],
        out_specs=[row, pl.BlockSpec((tm * CHUNKS, LANES), lambda i: (i, 0)), pair, wts],
        out_shape=[jax.ShapeDtypeStruct((n, D_MODEL), F32),
                   jax.ShapeDtypeStruct((n * CHUNKS, LANES), F32),
                   jax.ShapeDtypeStruct((TOP_K, n), jnp.int32),
                   jax.ShapeDtypeStruct((n, LANES), F32)],
        compiler_params=pltpu.CompilerParams(
            dimension_semantics=("parallel",), vmem_limit_bytes=VMEM_LIMIT),
        name="outproj",
    )(fox, ret, x, wo, nw, wr, br)


_FIRST, _LAST, _VALID, _NEWEXP = 1, 2, 4, 8


def _moe_kernel(blk_ref, exp_ref, flag_ref, starts_ref,
                tokc_ref, tokn_ref, dstp_ref, dstc_ref, x_hbm, wg_ref, wu_ref, wd_ref,
                y_hbm, xbuf, ybuf, hbuf, wgb, wub, wdb, gsem, ssem, *, tb, nb):
    w = pl.program_id(0)
    b = blk_ref[w]
    e = exp_ref[w]
    flags = flag_ref[w]
    slot = b % 2
    nslot = 1 - slot
    span = tb * CHUNKS

    def hbm_row(ref, idx):
        return ref.at[pl.ds(pl.multiple_of(idx * CHUNKS, CHUNKS), CHUNKS), :]

    def buf_row(buf, s, r):
        return buf.at[pl.ds(pl.multiple_of(s * span + r * CHUNKS, CHUNKS), CHUNKS), :]

    def start_gather(tok_ref, s):
        for r in range(tb):
            pltpu.make_async_copy(hbm_row(x_hbm, tok_ref[0, 0, r]), buf_row(xbuf, s, r),
                                  gsem.at[s]).start()

    def start_scatter(dst_ref, s):
        for r in range(tb):
            pltpu.make_async_copy(buf_row(ybuf, s, r), hbm_row(y_hbm, dst_ref[0, 0, r]),
                                  ssem.at[s]).start()

    def wait_rows(sem, s):
        whole = pl.ds(pl.multiple_of(s * span, span), span)
        pltpu.make_async_copy(xbuf.at[whole, :], ybuf.at[whole, :], sem.at[s]).wait()

    @pl.when((flags & _FIRST) != 0)
    def _():
        @pl.when(w == 0)
        def _():
            start_gather(tokc_ref, 0)

        wait_rows(gsem, slot)

        @pl.when(b >= 2)
        def _():
            wait_rows(ssem, slot)

        @pl.when(b + 1 < nb)
        def _():
            start_gather(tokn_ref, nslot)

        @pl.when(b >= 1)
        def _():
            start_scatter(dstp_ref, nslot)

        hbuf[...] = _tm_load(xbuf, slot * span, tb).astype(BF16)

        @pl.when((flags & _LAST) == 0)
        def _():
            ybuf[pl.ds(pl.multiple_of(slot * span, span), span), :] = jnp.zeros((span, LANES), F32)

    @pl.when((flags & _NEWEXP) != 0)
    def _():
        wgb[...] = wg_ref[0, 0].astype(BF16)
        wub[...] = wu_ref[0, 0].astype(BF16)
        wdb[...] = wd_ref[0, 0].astype(BF16)

    @pl.when((flags & _VALID) != 0)
    def _():
        h = hbuf[...]
        g = _dot(h, wgb[...])
        u = _dot(h, wub[...])
        a = (g * jax.nn.sigmoid(g) * u).astype(BF16)
        y = _dot(a, wdb[...])
        whole_block = (flags & (_FIRST | _LAST)) == (_FIRST | _LAST)

        @pl.when(whole_block)
        def _():
            _tm_store(ybuf, slot * span, tb, y)

        @pl.when(jnp.logical_not(whole_block))
        def _():
            q = b * tb + lax.broadcasted_iota(jnp.int32, y.shape, 0)
            mine = (q >= starts_ref[e]) & (q < starts_ref[e + 1])
            _tm_store(ybuf, slot * span, tb, jnp.where(mine, y, _tm_load(ybuf, slot * span, tb)))

    @pl.when(((flags & _LAST) != 0) & (b == nb - 1))
    def _():
        start_scatter(dstc_ref, slot)
        if nb >= 2:
            wait_rows(ssem, nslot)
        wait_rows(ssem, slot)


def _moe(x, s_tok, s_dst, blk, exp, flags, starts, wg, wu, wd, layer, tb=MOE_TILE):
    na = s_tok.shape[0]
    nb = na // tb
    d_model, d_expert = wg.shape[2], wg.shape[3]
    tok3 = s_tok.reshape(nb, 1, tb)
    dst3 = s_dst.reshape(nb, 1, tb)
    smem_blk = lambda f: pl.BlockSpec((1, 1, tb), f, memory_space=pltpu.SMEM)
    grid_spec = pltpu.PrefetchScalarGridSpec(
        num_scalar_prefetch=4,
        grid=(blk.shape[0],),
        in_specs=[
            smem_blk(lambda w, bl, ex, fl, st: (bl[w], 0, 0)),
            smem_blk(lambda w, bl, ex, fl, st: (jnp.minimum(bl[w] + 1, nb - 1), 0, 0)),
            smem_blk(lambda w, bl, ex, fl, st: (jnp.maximum(bl[w] - 1, 0), 0, 0)),
            smem_blk(lambda w, bl, ex, fl, st: (bl[w], 0, 0)),
            pl.BlockSpec(memory_space=pl.ANY),
            pl.BlockSpec((1, 1, d_model, d_expert), lambda w, bl, ex, fl, st: (layer, ex[w], 0, 0)),
            pl.BlockSpec((1, 1, d_model, d_expert), lambda w, bl, ex, fl, st: (layer, ex[w], 0, 0)),
            pl.BlockSpec((1, 1, d_expert, d_model), lambda w, bl, ex, fl, st: (layer, ex[w], 0, 0)),
        ],
        out_specs=pl.BlockSpec(memory_space=pl.ANY),
        scratch_shapes=[pltpu.VMEM((2 * tb * CHUNKS, LANES), F32),
                        pltpu.VMEM((2 * tb * CHUNKS, LANES), F32),
                        pltpu.VMEM((tb, d_model), BF16),
                        pltpu.VMEM((d_model, d_expert), BF16), pltpu.VMEM((d_model, d_expert), BF16),
                        pltpu.VMEM((d_expert, d_model), BF16),
                        pltpu.SemaphoreType.DMA((2,)), pltpu.SemaphoreType.DMA((2,))],
    )
    return pl.pallas_call(
        functools.partial(_moe_kernel, tb=tb, nb=nb),
        grid_spec=grid_spec,
        out_shape=jax.ShapeDtypeStruct((na * CHUNKS, LANES), F32),
        compiler_params=pltpu.CompilerParams(
            dimension_semantics=("arbitrary",), vmem_limit_bytes=VMEM_LIMIT),
        name="moe_experts",
    )(blk, exp, flags, starts, tok3, tok3, dst3, dst3, x, wg, wu, wd)


def _dispatch_plan(eid, n, tb=MOE_TILE, n_experts=N_EXPERTS):
    na = TOP_K * n
    nb = na // tb
    eid_flat = eid.reshape(na)
    _, s_a = lax.sort((eid_flat, jnp.arange(na, dtype=jnp.int32)), num_keys=1)
    s_tok = s_a % n
    counts = jnp.sum(eid_flat[None, :] == jnp.arange(n_experts, dtype=jnp.int32)[:, None], axis=1)
    starts = jnp.concatenate([jnp.zeros((1,), jnp.int32),
                              jnp.cumsum(counts).astype(jnp.int32)])
    lo, hi = starts[:-1], starts[1:]
    nonempty = hi > lo
    first_blk = lo // tb
    npass = jnp.where(nonempty, (hi - 1) // tb - first_blk + 1, 0)
    cum = jnp.cumsum(npass)
    total = cum[-1]
    n_pass = nb + n_experts
    w = jnp.arange(n_pass, dtype=jnp.int32)
    wc = jnp.minimum(w, total - 1)
    ex = jnp.sum(cum[None, :] <= wc[:, None], axis=1).astype(jnp.int32)
    sel = ex[:, None] == jnp.arange(n_experts, dtype=jnp.int32)[None, :]
    pick = lambda v: jnp.sum(jnp.where(sel, v[None, :], 0), axis=1)
    blk = (pick(first_blk) + (wc - pick(cum - npass))).astype(jnp.int32)
    valid = w < total
    prev_blk = jnp.concatenate([jnp.full((1,), -1, jnp.int32), blk[:-1]])
    next_blk = jnp.concatenate([blk[1:], jnp.full((1,), -1, jnp.int32)])
    first = valid & (blk != prev_blk)
    last = valid & ((blk != next_blk) | (w == total - 1))
    prev_ex = jnp.concatenate([jnp.full((1,), -1, jnp.int32), ex[:-1]])
    newexp = valid & (ex != prev_ex)
    flags = (first * _FIRST + last * _LAST + valid * _VALID + newexp * _NEWEXP).astype(jnp.int32)
    return s_tok, s_a, blk, ex, flags, starts


def _final_kernel(x_ref, y0_ref, y1_ref, rw_ref, nw_ref, o_ref):
    rw = rw_ref[...]
    tm = x_ref.shape[0]
    x = x_ref[...] + (rw[:, 0:1] * _tm_load(y0_ref, 0, tm) + rw[:, 1:2] * _tm_load(y1_ref, 0, tm))
    o_ref[...] = _rms(x, nw_ref[...])


def _final(x, y, rw, nw):
    n = x.shape[0]
    tm = WIDE_TILE
    nt = n // tm
    row = pl.BlockSpec((tm, D_MODEL), lambda i: (i, 0))
    return pl.pallas_call(
        _final_kernel,
        grid=(nt,),
        in_specs=[row, pl.BlockSpec((tm * CHUNKS, LANES), lambda i: (i, 0)),
                  pl.BlockSpec((tm * CHUNKS, LANES), lambda i: (i + nt, 0)),
                  pl.BlockSpec((tm, LANES), lambda i: (i, 0)),
                  pl.BlockSpec((1, D_MODEL), lambda i: (0, 0))],
        out_specs=row,
        out_shape=jax.ShapeDtypeStruct((n, D_MODEL), F32),
        compiler_params=pltpu.CompilerParams(
            dimension_semantics=("parallel",), vmem_limit_bytes=VMEM_LIMIT),
        name="final_norm",
    )(x, y, y, rw, nw)


def kernel(x, norm_mix_w, w_in, fox_forget_b, w_out, norm_ffn_w, w_router_group, b_router_group,
           w_router_expert, b_router_expert, w_expert_gate, w_expert_up, w_expert_down,
           norm_final_w):
    batch, seq, d = x.shape
    n = batch * seq
    depth = w_in.shape[0]
    xf = x.reshape(n, d)
    tables = _ret_tables(seq)
    nq = seq // ATT_TILE

    y = rw = None
    for layer in range(depth):
        wl = w_in[layer]
        c0 = 3 * FOX_WIDTH
        w_ff = jnp.pad(wl[:, c0:c0 + FOX_HEADS], ((0, 0), (0, LANES - FOX_HEADS)))
        w_main = jnp.concatenate([wl[:, :c0], wl[:, c0 + FOX_HEADS:], w_ff], axis=1).astype(BF16)
        b_ff = jnp.pad(fox_forget_b[layer], (0, LANES - FOX_HEADS)).reshape(1, LANES)
        xf, (fq, fk, fv, rq, rk, rv, rg, ct) = _inproj(
            xf, y, rw, norm_mix_w[layer].reshape(1, d), w_main, b_ff, seq)
        ct4 = ct.reshape(batch, 8, nq, ATT_TILE)
        fox = _fox_attention(fq, fk, fv, ct4, batch, seq)
        ret = _retention(rq, rk, rv, rg, tables, batch, seq)

        zpad = jnp.zeros((d, SLAB - N_GROUPS), F32)
        w_r = jnp.concatenate([w_router_group[layer], zpad, w_router_expert[layer]], axis=1)
        nr = SLAB + N_EXPERTS
        w_r = jnp.pad(w_r, ((0, 0), (0, LANES - nr))).astype(BF16)
        b_r = jnp.concatenate([b_router_group[layer], jnp.zeros((SLAB - N_GROUPS,), F32),
                               b_router_expert[layer]])
        b_r = jnp.pad(b_r, (0, LANES - nr)).reshape(1, LANES)
        xf, xg, eid, rw = _outproj(fox, ret, xf, w_out[layer].astype(BF16),
                                   norm_ffn_w[layer].reshape(1, d), w_r, b_r)
        s_tok, s_dst, blk, ex, flags, starts = _dispatch_plan(eid, n)
        y = _moe(xg, s_tok, s_dst, blk, ex, flags, starts,
                 w_expert_gate, w_expert_up, w_expert_down, layer)
    out = _final(xf, y, rw, norm_final_w.reshape(1, d))
    return out.reshape(batch, seq, d)
```

```python
import functools

import jax
import jax.numpy as jnp
from jax import lax
from jax.experimental import pallas as pl
from jax.experimental.pallas import tpu as pltpu

F32 = jnp.float32
BF16 = jnp.bfloat16

D_MODEL = 1024
FOX_HEADS = 8
FOX_HEAD_DIM = 64
FOX_WIDTH = 512
RET_HEADS = 4
RET_HEAD_DIM = 128
RET_WIDTH = 512
CHUNK = 64
ROPE_BASE = 10000.0
N_GROUPS = 4
EXPERTS_PER_GROUP = 8
N_EXPERTS = 32
TOP_K = 2
D_EXPERT = 512
RMS_EPS = 1e-6

LANES = 128
VMEM_LIMIT = 56 * 1024 * 1024

ROW_TILE = 512
ROW_PARTS = 2
WIDE_TILE = 1024
ATT_TILE = 512
RET_TILE = 512
MOE_TILE = 256
GROUP_W = FOX_WIDTH
SLAB = EXPERTS_PER_GROUP
N_MAIN = 7 * GROUP_W
EXP_UNDERFLOW = 110.0


def _rms(xf, w):
    return xf * lax.rsqrt(jnp.mean(xf * xf, axis=-1, keepdims=True) + RMS_EPS) * w


def _dot(a, b):
    return jnp.dot(a, b, preferred_element_type=F32)


def _dot_nt(a, b):
    return lax.dot_general(a, b, (((1,), (1,)), ((), ())), preferred_element_type=F32)


def _dot_tn(a, b):
    return lax.dot_general(a, b, (((0,), (0,)), ((), ())), preferred_element_type=F32)


CHUNKS = D_MODEL // LANES


def _tm_load(ref, base, rows):
    return jnp.concatenate([ref[pl.ds(base + c, rows, stride=CHUNKS), :] for c in range(CHUNKS)],
                           axis=1)


def _tm_store(ref, base, rows, val):
    for c in range(CHUNKS):
        ref[pl.ds(base + c, rows, stride=CHUNKS), :] = val[:, c * LANES:(c + 1) * LANES]


def _inproj_kernel(*refs, has_y, tiles_per_seq):
    if has_y:
        x_ref, y0_ref, y1_ref, rw_ref = refs[:4]
        refs = refs[4:]
    else:
        x_ref = refs[0]
        refs = refs[1:]
    nw_ref, w_ref, bff_ref, tri_ref = refs[:4]
    refs = refs[4:]
    if has_y:
        xres_ref = refs[0]
        refs = refs[1:]
    fq_ref, fk_ref, fv_ref, rq_ref, rk_ref, rv_ref, rg_ref, ct_ref, carry_sc = refs

    i = pl.program_id(0)
    tm = x_ref.shape[0]
    part_rows = tm // ROW_PARTS
    outs = (fq_ref, fk_ref, fv_ref, rq_ref, rk_ref, rv_ref)
    z_parts = []
    for part in range(ROW_PARTS):
        rows = slice(part * part_rows, (part + 1) * part_rows)
        x = x_ref[rows, :]
        if has_y:
            rw = rw_ref[rows, :]
            base = part * part_rows * CHUNKS
            x = x + (rw[:, 0:1] * _tm_load(y0_ref, base, part_rows)
                     + rw[:, 1:2] * _tm_load(y1_ref, base, part_rows))
            xres_ref[rows, :] = x
        h = _rms(x, nw_ref[...]).astype(BF16)
        for j, o_ref in enumerate(outs):
            acc = _dot(h, w_ref[:, j * GROUP_W:(j + 1) * GROUP_W])
            if j == 0:
                acc = acc * (FOX_HEAD_DIM ** -0.5)
            o_ref[rows, :] = acc.astype(BF16)
        acc = _dot(h, w_ref[:, 6 * GROUP_W:])
        rg_ref[rows, :] = acc[:, :GROUP_W].astype(BF16)
        z_parts.append(acc[:, GROUP_W:])

    z = jnp.concatenate(z_parts, axis=0) + bff_ref[...]
    lf = jnp.minimum(z, 0.0) - jnp.log1p(jnp.exp(-jnp.abs(z)))
    lft = lf.T[:FOX_HEADS, :]
    hi = lft.astype(BF16).astype(F32)
    mid = (lft - hi).astype(BF16).astype(F32)
    lo = lft - hi - mid
    pieces = jnp.concatenate([hi, mid, lo, jnp.zeros_like(hi)], axis=0).astype(BF16)
    cs = _dot(pieces, tri_ref[...])
    cs = cs[0:FOX_HEADS] + cs[FOX_HEADS:2 * FOX_HEADS] + cs[2 * FOX_HEADS:3 * FOX_HEADS]

    @pl.when(i % tiles_per_seq == 0)
    def _():
        carry_sc[...] = jnp.zeros_like(carry_sc)

    c = cs + carry_sc[:, 0:1]
    carry_sc[...] = jnp.broadcast_to(c[:, -1:], carry_sc.shape)
    ct_ref[0] = c


def _inproj(x, y, rw, nw, w_main, b_ff, seq):
    n = x.shape[0]
    tm = ROW_TILE
    tri = jnp.triu(jnp.ones((tm, tm), F32)).astype(BF16)
    nt = n // tm
    tps = seq // tm
    has_y = y is not None
    row_spec = pl.BlockSpec((tm, D_MODEL), lambda i: (i, 0))
    in_specs = [row_spec]
    args = [x]
    if has_y:
        in_specs += [pl.BlockSpec((tm * CHUNKS, LANES), lambda i: (i, 0)),
                     pl.BlockSpec((tm * CHUNKS, LANES), lambda i: (i + nt, 0)),
                     pl.BlockSpec((tm, LANES), lambda i: (i, 0))]
        args += [y, y, rw]
    in_specs += [
        pl.BlockSpec((1, D_MODEL), lambda i: (0, 0)),
        pl.BlockSpec((D_MODEL, N_MAIN + LANES), lambda i: (0, 0)),
        pl.BlockSpec((1, LANES), lambda i: (0, 0)),
        pl.BlockSpec((tm, tm), lambda i: (0, 0)),
    ]
    args += [nw, w_main, b_ff, tri]
    half_spec = pl.BlockSpec((tm, GROUP_W), lambda i: (i, 0))
    out_shape = []
    out_specs = []
    if has_y:
        out_shape.append(jax.ShapeDtypeStruct((n, D_MODEL), F32))
        out_specs.append(row_spec)
    out_shape += [jax.ShapeDtypeStruct((n, GROUP_W), BF16)] * 7
    out_specs += [half_spec] * 7
    out_shape.append(jax.ShapeDtypeStruct((n // seq, 8, seq), F32))
    out_specs.append(pl.BlockSpec((1, 8, tm), lambda i: (i // tps, 0, i % tps)))
    outs = pl.pallas_call(
        functools.partial(_inproj_kernel, has_y=has_y, tiles_per_seq=tps),
        grid=(nt,),
        in_specs=in_specs,
        out_specs=out_specs,
        out_shape=out_shape,
        scratch_shapes=[pltpu.VMEM((8, LANES), F32)],
        compiler_params=pltpu.CompilerParams(
            dimension_semantics=("arbitrary",), vmem_limit_bytes=VMEM_LIMIT),
        name="inproj_y" if has_y else "inproj",
    )(*args)
    if has_y:
        return outs[0], outs[1:]
    return x, outs


def _fox_kernel(q_ref, k_ref, v_ref, ct_ref, o_ref, m_sc, acc_sc, kmax_sc, *, tile, nq):
    hp = pl.program_id(1)
    qi = pl.program_id(2)
    q2 = q_ref[0]
    lane = lax.broadcasted_iota(jnp.int32, q2.shape, 1)
    first = lane < FOX_HEAD_DIM
    zero = jnp.zeros_like(q2)
    qh = (jnp.where(first, q2, zero), jnp.where(first, zero, q2))
    reps = tile // LANES

    def head_sqnorm_max(xf, h):
        sq = xf * xf
        sq = jnp.where(first, sq, 0.0) if h == 0 else jnp.where(first, 0.0, sq)
        return jnp.max(jnp.sum(sq, axis=1, keepdims=True), axis=0, keepdims=True)

    @pl.when(qi == 0)
    def _():
        for j in range(nq):
            kf = k_ref[0, j * tile:(j + 1) * tile, :].astype(F32)
            for h in range(2):
                kmax_sc[h, j:j + 1, :] = jnp.broadcast_to(head_sqnorm_max(kf, h), (1, LANES))

    def head_step(h, kb, k_blk, v_blk, mask, m_old, acc_old):
        one = jnp.ones_like(v_blk)
        va = jnp.where(first, v_blk, one) if h == 0 else jnp.where(first, one, v_blk)
        s = _dot_nt(qh[h], k_blk) - ct_ref[0, 2 * hp + h, pl.ds(kb, 1), :]
        if mask is not None:
            s = jnp.where(mask, s, -jnp.inf)
        m_cur = jnp.max(s, axis=1, keepdims=True)
        if m_old is None:
            m_new = jnp.broadcast_to(m_cur, (tile, LANES))
            p = jnp.exp(s - jnp.concatenate([m_new] * reps, axis=1))
            acc = _dot(p.astype(BF16), va)
        else:
            m_new = jnp.maximum(m_old, m_cur)
            alpha = jnp.exp(m_old - m_new)
            p = jnp.exp(s - jnp.concatenate([m_new] * reps, axis=1))
            acc = alpha * acc_old + _dot(p.astype(BF16), va)
        return m_new, acc

    def diagonal(with_previous):
        row = lax.broadcasted_iota(jnp.int32, (tile, tile), 0)
        col = lax.broadcasted_iota(jnp.int32, (tile, tile), 1)
        start = pl.multiple_of(qi * tile, tile)
        k_blk = k_ref[0, pl.ds(start, tile), :]
        v_blk = v_ref[0, pl.ds(start, tile), :]
        state = [head_step(h, qi, k_blk, v_blk, col <= row, None, None) for h in range(2)]
        if with_previous:
            start = pl.multiple_of((qi - 1) * tile, tile)
            k_blk = k_ref[0, pl.ds(start, tile), :]
            v_blk = v_ref[0, pl.ds(start, tile), :]
            state = [head_step(h, qi - 1, k_blk, v_blk, None, *state[h]) for h in range(2)]
        for h in range(2):
            m_sc[h] = state[h][0]
            acc_sc[h] = state[h][1]

    @pl.when(qi == 0)
    def _():
        diagonal(False)

    @pl.when(qi > 0)
    def _():
        diagonal(True)

    top = jnp.maximum(qi - 1, 0)

    def off_diagonal(kbs, heads=(0, 1)):
        state = {h: (m_sc[h], acc_sc[h]) for h in heads}
        for kb in kbs:
            start = pl.multiple_of(kb * tile, tile)
            k_blk = k_ref[0, pl.ds(start, tile), :]
            v_blk = v_ref[0, pl.ds(start, tile), :]
            state = {h: head_step(h, kb, k_blk, v_blk, None, *state[h]) for h in heads}
        for h in heads:
            m_sc[h] = state[h][0]
            acc_sc[h] = state[h][1]

    jrow = lax.broadcasted_iota(jnp.int32, (nq, LANES), 0)
    qf = q2.astype(F32)
    j_start = []
    for h in range(2):
        m_low = jnp.min(jnp.min(m_sc[h], axis=1, keepdims=True), axis=0, keepdims=True)
        c_last = ct_ref[0, 2 * hp + h, :, tile - 1:tile]
        bound = jnp.sqrt(head_sqnorm_max(qf, h) * kmax_sc[h]) - c_last
        need = (bound - m_low > -EXP_UNDERFLOW) & (jrow < top)
        j_first = jnp.min(jnp.min(jnp.where(need, jrow, top), axis=1, keepdims=True),
                          axis=0, keepdims=True)
        j_start.append(j_first[0, 0])
    j_both = jnp.maximum(j_start[0], j_start[1])
    count = top - j_both

    def pair(j, carry):
        off_diagonal((j_both + 2 * j, j_both + 2 * j + 1))
        return carry

    lax.fori_loop(0, count // 2, pair, 0)

    @pl.when(count % 2 == 1)
    def _():
        off_diagonal((top - 1,))

    for h in range(2):
        lone = j_both - j_start[h]

        def single_pair(j, carry, h=h):
            off_diagonal((j_start[h] + 2 * j, j_start[h] + 2 * j + 1), heads=(h,))
            return carry

        lax.fori_loop(0, lone // 2, single_pair, 0)

        @pl.when(lone % 2 == 1)
        def _(h=h):
            off_diagonal((j_both - 1,), heads=(h,))

    a0 = acc_sc[0]
    a1 = acc_sc[1]
    half = FOX_HEAD_DIM
    o = jnp.where(first, a0 / pltpu.roll(a0, half, 1), a1 / pltpu.roll(a1, half, 1))
    o_ref[0] = o.astype(o_ref.dtype)


def _fox_attention(fq, fk, fv, ct4, batch, seq):
    t = ATT_TILE
    nq = seq // t
    q3 = fq.reshape(batch, seq, FOX_WIDTH)
    k3 = fk.reshape(batch, seq, FOX_WIDTH)
    v3 = fv.reshape(batch, seq, FOX_WIDTH)
    out = pl.pallas_call(
        functools.partial(_fox_kernel, tile=t, nq=nq),
        grid=(batch, FOX_HEADS // 2, nq),
        in_specs=[
            pl.BlockSpec((1, t, LANES), lambda b, j, i: (b, i, j)),
            pl.BlockSpec((1, seq, LANES), lambda b, j, i: (b, 0, j)),
            pl.BlockSpec((1, seq, LANES), lambda b, j, i: (b, 0, j)),
            pl.BlockSpec((1, 8, nq, t), lambda b, j, i: (b, 0, 0, 0)),
        ],
        out_specs=pl.BlockSpec((1, t, LANES), lambda b, j, i: (b, i, j)),
        out_shape=jax.ShapeDtypeStruct((batch, seq, FOX_WIDTH), BF16),
        scratch_shapes=[pltpu.VMEM((2, t, LANES), F32), pltpu.VMEM((2, t, LANES), F32),
                        pltpu.VMEM((2, nq, LANES), F32)],
        compiler_params=pltpu.CompilerParams(
            dimension_semantics=("parallel", "parallel", "arbitrary"),
            vmem_limit_bytes=VMEM_LIMIT),
        name="fox_attention",
    )(q3, k3, v3, ct4)
    return out.reshape(batch * seq, FOX_WIDTH)


def _ret_kernel(q_ref, k_ref, v_ref, g_ref, cos_ref, sin_ref, dmat_ref, qdec_ref, kdec_ref,
                sdec_ref, o_ref, state_sc):
    si = pl.program_id(1)

    @pl.when(si == 0)
    def _():
        state_sc[...] = jnp.zeros_like(state_sc)

    cos2 = cos_ref[...]
    sin2 = sin_ref[...]
    dk = RET_HEAD_DIM

    def rot(xf):
        return xf * cos2 + pltpu.roll(xf, dk // 2, 1) * sin2

    for h in range(RET_HEADS):
        cols = slice(h * dk, (h + 1) * dk)
        q = rot(q_ref[0, :, cols].astype(F32))
        k = rot(k_ref[0, :, cols].astype(F32)) * (dk ** -0.5)
        v = v_ref[0, :, cols]
        scores = _dot_nt(q.astype(BF16), k.astype(BF16)) * dmat_ref[h]
        intra = _dot(scores.astype(BF16), v)
        state = state_sc[h]
        cross = _dot((q * qdec_ref[h]).astype(BF16), state.astype(BF16))
        out = intra + cross
        state_sc[h] = state * sdec_ref[h, 0:1, :] + _dot_tn((k * kdec_ref[h]).astype(BF16), v)

        y = out * lax.rsqrt(jnp.mean(out * out, axis=-1, keepdims=True) + RMS_EPS)
        g = g_ref[0, :, cols].astype(F32)
        o_ref[0, :, cols] = (y * (g * jax.nn.sigmoid(g))).astype(o_ref.dtype)


def _ret_tables(seq):
    half = RET_HEAD_DIM // 2
    inv_freq = 1.0 / (ROPE_BASE ** (jnp.arange(half, dtype=F32) / half))
    ang = jnp.arange(seq, dtype=F32)[:, None] * inv_freq[None, :]
    cos = jnp.cos(ang)
    sin = jnp.sin(ang)
    cos2 = jnp.concatenate([cos, cos], axis=1)
    sin2 = jnp.concatenate([-sin, sin], axis=1)
    lt = RET_TILE
    log_gamma = jnp.log(1.0 - 2.0 ** (-5.0 - jnp.arange(RET_HEADS, dtype=F32)))
    idx = jnp.arange(lt)
    t = idx[:, None]
    s = idx[None, :]
    same = (t // CHUNK) == (s // CHUNK)
    earlier = (s // CHUNK) < (t // CHUNK)
    dist = jnp.where(same, jnp.abs(t - s), t - s).astype(F32)
    dmat = jnp.where((same | earlier)[None], jnp.exp(log_gamma[:, None, None] * dist[None]), 0.0)
    idxf = idx.astype(F32)
    qdec = jnp.exp(log_gamma[:, None] * idxf[None, :])
    kdec = jnp.exp(log_gamma[:, None] * (lt - idxf)[None, :])
    sdec = jnp.exp(log_gamma * lt)
    qdec = jnp.broadcast_to(qdec[:, :, None], (RET_HEADS, lt, LANES))
    kdec = jnp.broadcast_to(kdec[:, :, None], (RET_HEADS, lt, LANES))
    sdec = jnp.broadcast_to(sdec[:, None, None], (RET_HEADS, 8, LANES))
    return cos2, sin2, dmat, qdec, kdec, sdec


def _retention(rq, rk, rv, rg, tables, batch, seq):
    lt = RET_TILE
    ns = seq // lt
    cos2, sin2, dmat, qdec, kdec, sdec = tables
    blk = pl.BlockSpec((1, lt, RET_WIDTH), lambda b, i: (b, i, 0))
    tab = pl.BlockSpec((lt, LANES), lambda b, i: (i, 0))
    args = [a.reshape(batch, seq, RET_WIDTH) for a in (rq, rk, rv, rg)]
    out = pl.pallas_call(
        _ret_kernel,
        grid=(batch, ns),
        in_specs=[blk, blk, blk, blk, tab, tab,
                  pl.BlockSpec((RET_HEADS, lt, lt), lambda b, i: (0, 0, 0)),
                  pl.BlockSpec((RET_HEADS, lt, LANES), lambda b, i: (0, 0, 0)),
                  pl.BlockSpec((RET_HEADS, lt, LANES), lambda b, i: (0, 0, 0)),
                  pl.BlockSpec((RET_HEADS, 8, LANES), lambda b, i: (0, 0, 0))],
        out_specs=blk,
        out_shape=jax.ShapeDtypeStruct((batch, seq, RET_WIDTH), BF16),
        scratch_shapes=[pltpu.VMEM((RET_HEADS, RET_HEAD_DIM, RET_HEAD_DIM), F32)],
        compiler_params=pltpu.CompilerParams(
            dimension_semantics=("parallel", "arbitrary"),
            vmem_limit_bytes=VMEM_LIMIT),
        name="retention",
    )(*args, cos2, sin2, dmat, qdec, kdec, sdec)
    return out.reshape(batch * seq, RET_WIDTH)


def _outproj_kernel(fox_ref, ret_ref, x_ref, wo_ref, nw_ref, wr_ref, br_ref, xo_ref, xg_ref,
                    eid_ref, rw_ref):
    mixed = jnp.concatenate([fox_ref[...], ret_ref[...]], axis=1)
    x = x_ref[...] + _dot(mixed, wo_ref[...])
    xo_ref[...] = x
    hf = _rms(x, nw_ref[...])
    _tm_store(xg_ref, 0, x.shape[0], hf)
    h = hf.astype(BF16)
    lt = (_dot(h, wr_ref[...]) + br_ref[...]).T
    tm = lt.shape[1]
    rowid = lax.broadcasted_iota(jnp.int32, (SLAB, tm), 0)
    neg = -jnp.inf

    def top1(v):
        vmax = jnp.max(v, axis=0, keepdims=True)
        idx = jnp.min(jnp.where(v == vmax, rowid, SLAB), axis=0, keepdims=True)
        return vmax, idx

    gl = jnp.where(rowid < N_GROUPS, lt[0:SLAB], neg)
    gmax, gidx = top1(gl)
    g_w = 1.0 / jnp.sum(jnp.exp(gl - gmax), axis=0, keepdims=True)
    e_in = jnp.zeros((SLAB, tm), F32)
    for g in range(N_GROUPS):
        e_in = jnp.where(gidx == g, lt[SLAB * (g + 1):SLAB * (g + 2)], e_in)
    v1, i1 = top1(e_in)
    rest = jnp.where(rowid == i1, neg, e_in)
    v2, i2 = top1(rest)
    t = jnp.exp(v2 - v1)
    w1 = g_w / (1.0 + t)
    eid_ref[0:1, :] = gidx * EXPERTS_PER_GROUP + i1
    eid_ref[1:2, :] = gidx * EXPERTS_PER_GROUP + i2
    wslab = jnp.concatenate([w1, w1 * t, jnp.zeros((LANES - TOP_K, tm), F32)], axis=0)
    rw_ref[...] = wslab.T


def _outproj(fox, ret, x, wo, nw, wr, br):
    n = x.shape[0]
    tm = WIDE_TILE
    row = pl.BlockSpec((tm, D_MODEL), lambda i: (i, 0))
    half = pl.BlockSpec((tm, GROUP_W), lambda i: (i, 0))
    pair = pl.BlockSpec((TOP_K, tm), lambda i: (0, i))
    wts = pl.BlockSpec((tm, LANES), lambda i: (i, 0))
    return pl.pallas_call(
        _outproj_kernel,
        grid=(n // tm,),
        in_specs=[half, half, row,
                  pl.BlockSpec((D_MODEL, D_MODEL), lambda i: (0, 0)),
                  pl.BlockSpec((1, D_MODEL), lambda i: (0, 0)),
                  pl.BlockSpec((D_MODEL, LANES), lambda i: (0, 0)),
                  pl.BlockSpec((1, LANES), lambda i: (0, 0))],
        out_specs=[row, pl.BlockSpec((tm * CHUNKS, LANES), lambda i: (i, 0)), pair, wts],
        out_shape=[jax.ShapeDtypeStruct((n, D_MODEL), F32),
                   jax.ShapeDtypeStruct((n * CHUNKS, LANES), F32),
                   jax.ShapeDtypeStruct((TOP_K, n), jnp.int32),
                   jax.ShapeDtypeStruct((n, LANES), F32)],
        compiler_params=pltpu.CompilerParams(
            dimension_semantics=("parallel",), vmem_limit_bytes=VMEM_LIMIT),
        name="outproj",
    )(fox, ret, x, wo, nw, wr, br)


_FIRST, _LAST, _VALID, _NEWEXP = 1, 2, 4, 8


def _moe_kernel(blk_ref, exp_ref, flag_ref, starts_ref,
                tokc_ref, tokn_ref, dstp_ref, dstc_ref, x_hbm, wg_ref, wu_ref, wd_ref,
                y_hbm, xbuf, ybuf, hbuf, wgb, wub, wdb, gsem, ssem, *, tb, nb):
    w = pl.program_id(0)
    b = blk_ref[w]
    e = exp_ref[w]
    flags = flag_ref[w]
    slot = b % 2
    nslot = 1 - slot
    span = tb * CHUNKS

    def hbm_row(ref, idx):
        return ref.at[pl.ds(pl.multiple_of(idx * CHUNKS, CHUNKS), CHUNKS), :]

    def buf_row(buf, s, r):
        return buf.at[pl.ds(pl.multiple_of(s * span + r * CHUNKS, CHUNKS), CHUNKS), :]

    def start_gather(tok_ref, s):
        for r in range(tb):
            pltpu.make_async_copy(hbm_row(x_hbm, tok_ref[0, 0, r]), buf_row(xbuf, s, r),
                                  gsem.at[s]).start()

    def start_scatter(dst_ref, s):
        for r in range(tb):
            pltpu.make_async_copy(buf_row(ybuf, s, r), hbm_row(y_hbm, dst_ref[0, 0, r]),
                                  ssem.at[s]).start()

    def wait_rows(sem, s):
        whole = pl.ds(pl.multiple_of(s * span, span), span)
        pltpu.make_async_copy(xbuf.at[whole, :], ybuf.at[whole, :], sem.at[s]).wait()

    @pl.when((flags & _FIRST) != 0)
    def _():
        @pl.when(w == 0)
        def _():
            start_gather(tokc_ref, 0)

        wait_rows(gsem, slot)

        @pl.when(b >= 2)
        def _():
            wait_rows(ssem, slot)

        @pl.when(b + 1 < nb)
        def _():
            start_gather(tokn_ref, nslot)

        @pl.when(b >= 1)
        def _():
            start_scatter(dstp_ref, nslot)

        hbuf[...] = _tm_load(xbuf, slot * span, tb).astype(BF16)

        @pl.when((flags & _LAST) == 0)
        def _():
            ybuf[pl.ds(pl.multiple_of(slot * span, span), span), :] = jnp.zeros((span, LANES), F32)

    @pl.when((flags & _NEWEXP) != 0)
    def _():
        wgb[...] = wg_ref[0, 0].astype(BF16)
        wub[...] = wu_ref[0, 0].astype(BF16)
        wdb[...] = wd_ref[0, 0].astype(BF16)

    @pl.when((flags & _VALID) != 0)
    def _():
        h = hbuf[...]
        g = _dot(h, wgb[...])
        u = _dot(h, wub[...])
        a = (g * jax.nn.sigmoid(g) * u).astype(BF16)
        y = _dot(a, wdb[...])
        whole_block = (flags & (_FIRST | _LAST)) == (_FIRST | _LAST)

        @pl.when(whole_block)
        def _():
            _tm_store(ybuf, slot * span, tb, y)

        @pl.when(jnp.logical_not(whole_block))
        def _():
            q = b * tb + lax.broadcasted_iota(jnp.int32, y.shape, 0)
            mine = (q >= starts_ref[e]) & (q < starts_ref[e + 1])
            _tm_store(ybuf, slot * span, tb, jnp.where(mine, y, _tm_load(ybuf, slot * span, tb)))

    @pl.when(((flags & _LAST) != 0) & (b == nb - 1))
    def _():
        start_scatter(dstc_ref, slot)
        if nb >= 2:
            wait_rows(ssem, nslot)
        wait_rows(ssem, slot)


def _moe(x, s_tok, s_dst, blk, exp, flags, starts, wg, wu, wd, layer, tb=MOE_TILE):
    na = s_tok.shape[0]
    nb = na // tb
    d_model, d_expert = wg.shape[2], wg.shape[3]
    tok3 = s_tok.reshape(nb, 1, tb)
    dst3 = s_dst.reshape(nb, 1, tb)
    smem_blk = lambda f: pl.BlockSpec((1, 1, tb), f, memory_space=pltpu.SMEM)
    grid_spec = pltpu.PrefetchScalarGridSpec(
        num_scalar_prefetch=4,
        grid=(blk.shape[0],),
        in_specs=[
            smem_blk(lambda w, bl, ex, fl, st: (bl[w], 0, 0)),
            smem_blk(lambda w, bl, ex, fl, st: (jnp.minimum(bl[w] + 1, nb - 1), 0, 0)),
            smem_blk(lambda w, bl, ex, fl, st: (jnp.maximum(bl[w] - 1, 0), 0, 0)),
            smem_blk(lambda w, bl, ex, fl, st: (bl[w], 0, 0)),
            pl.BlockSpec(memory_space=pl.ANY),
            pl.BlockSpec((1, 1, d_model, d_expert), lambda w, bl, ex, fl, st: (layer, ex[w], 0, 0)),
            pl.BlockSpec((1, 1, d_model, d_expert), lambda w, bl, ex, fl, st: (layer, ex[w], 0, 0)),
            pl.BlockSpec((1, 1, d_expert, d_model), lambda w, bl, ex, fl, st: (layer, ex[w], 0, 0)),
        ],
        out_specs=pl.BlockSpec(memory_space=pl.ANY),
        scratch_shapes=[pltpu.VMEM((2 * tb * CHUNKS, LANES), F32),
                        pltpu.VMEM((2 * tb * CHUNKS, LANES), F32),
                        pltpu.VMEM((tb, d_model), BF16),
                        pltpu.VMEM((d_model, d_expert), BF16), pltpu.VMEM((d_model, d_expert), BF16),
                        pltpu.VMEM((d_expert, d_model), BF16),
                        pltpu.SemaphoreType.DMA((2,)), pltpu.SemaphoreType.DMA((2,))],
    )
    return pl.pallas_call(
        functools.partial(_moe_kernel, tb=tb, nb=nb),
        grid_spec=grid_spec,
        out_shape=jax.ShapeDtypeStruct((na * CHUNKS, LANES), F32),
        compiler_params=pltpu.CompilerParams(
            dimension_semantics=("arbitrary",), vmem_limit_bytes=VMEM_LIMIT),
        name="moe_experts",
    )(blk, exp, flags, starts, tok3, tok3, dst3, dst3, x, wg, wu, wd)


def _dispatch_plan(eid, n, tb=MOE_TILE, n_experts=N_EXPERTS):
    na = TOP_K * n
    nb = na // tb
    eid_flat = eid.reshape(na)
    _, s_a = lax.sort((eid_flat, jnp.arange(na, dtype=jnp.int32)), num_keys=1)
    s_tok = s_a % n
    counts = jnp.sum(eid_flat[None, :] == jnp.arange(n_experts, dtype=jnp.int32)[:, None], axis=1)
    starts = jnp.concatenate([jnp.zeros((1,), jnp.int32),
                              jnp.cumsum(counts).astype(jnp.int32)])
    lo, hi = starts[:-1], starts[1:]
    nonempty = hi > lo
    first_blk = lo // tb
    npass = jnp.where(nonempty, (hi - 1) // tb - first_blk + 1, 0)
    cum = jnp.cumsum(npass)
    total = cum[-1]
    n_pass = nb + n_experts
    w = jnp.arange(n_pass, dtype=jnp.int32)
    wc = jnp.minimum(w, total - 1)
    ex = jnp.sum(cum[None, :] <= wc[:, None], axis=1).astype(jnp.int32)
    sel = ex[:, None] == jnp.arange(n_experts, dtype=jnp.int32)[None, :]
    pick = lambda v: jnp.sum(jnp.where(sel, v[None, :], 0), axis=1)
    blk = (pick(first_blk) + (wc - pick(cum - npass))).astype(jnp.int32)
    valid = w < total
    prev_blk = jnp.concatenate([jnp.full((1,), -1, jnp.int32), blk[:-1]])
    next_blk = jnp.concatenate([blk[1:], jnp.full((1,), -1, jnp.int32)])
    first = valid & (blk != prev_blk)
    last = valid & ((blk != next_blk) | (w == total - 1))
    prev_ex = jnp.concatenate([jnp.full((1,), -1, jnp.int32), ex[:-1]])
    newexp = valid & (ex != prev_ex)
    flags = (first * _FIRST + last * _LAST + valid * _VALID + newexp * _NEWEXP).astype(jnp.int32)
    return s_tok, s_a, blk, ex, flags, starts


def _final_kernel(x_ref, y0_ref, y1_ref, rw_ref, nw_ref, o_ref):
    rw = rw_ref[...]
    tm = x_ref.shape[0]
    x = x_ref[...] + (rw[:, 0:1] * _tm_load(y0_ref, 0, tm) + rw[:, 1:2] * _tm_load(y1_ref, 0, tm))
    o_ref[...] = _rms(x, nw_ref[...])


def _final(x, y, rw, nw):
    n = x.shape[0]
    tm = WIDE_TILE
    nt = n // tm
    row = pl.BlockSpec((tm, D_MODEL), lambda i: (i, 0))
    return pl.pallas_call(
        _final_kernel,
        grid=(nt,),
        in_specs=[row, pl.BlockSpec((tm * CHUNKS, LANES), lambda i: (i, 0)),
                  pl.BlockSpec((tm * CHUNKS, LANES), lambda i: (i + nt, 0)),
                  pl.BlockSpec((tm, LANES), lambda i: (i, 0)),
                  pl.BlockSpec((1, D_MODEL), lambda i: (0, 0))],
        out_specs=row,
        out_shape=jax.ShapeDtypeStruct((n, D_MODEL), F32),
        compiler_params=pltpu.CompilerParams(
            dimension_semantics=("parallel",), vmem_limit_bytes=VMEM_LIMIT),
        name="final_norm",
    )(x, y, y, rw, nw)


def kernel(x, norm_mix_w, w_in, fox_forget_b, w_out, norm_ffn_w, w_router_group, b_router_group,
           w_router_expert, b_router_expert, w_expert_gate, w_expert_up, w_expert_down,
           norm_final_w):
    batch, seq, d = x.shape
    n = batch * seq
    depth = w_in.shape[0]
    xf = x.reshape(n, d)
    tables = _ret_tables(seq)
    nq = seq // ATT_TILE

    y = rw = None
    for layer in range(depth):
        wl = w_in[layer]
        c0 = 3 * FOX_WIDTH
        w_ff = jnp.pad(wl[:, c0:c0 + FOX_HEADS], ((0, 0), (0, LANES - FOX_HEADS)))
        w_main = jnp.concatenate([wl[:, :c0], wl[:, c0 + FOX_HEADS:], w_ff], axis=1).astype(BF16)
        b_ff = jnp.pad(fox_forget_b[layer], (0, LANES - FOX_HEADS)).reshape(1, LANES)
        xf, (fq, fk, fv, rq, rk, rv, rg, ct) = _inproj(
            xf, y, rw, norm_mix_w[layer].reshape(1, d), w_main, b_ff, seq)
        ct4 = ct.reshape(batch, 8, nq, ATT_TILE)
        fox = _fox_attention(fq, fk, fv, ct4, batch, seq)
        ret = _retention(rq, rk, rv, rg, tables, batch, seq)

        zpad = jnp.zeros((d, SLAB - N_GROUPS), F32)
        w_r = jnp.concatenate([w_router_group[layer], zpad, w_router_expert[layer]], axis=1)
        nr = SLAB + N_EXPERTS
        w_r = jnp.pad(w_r, ((0, 0), (0, LANES - nr))).astype(BF16)
        b_r = jnp.concatenate([b_router_group[layer], jnp.zeros((SLAB - N_GROUPS,), F32),
                               b_router_expert[layer]])
        b_r = jnp.pad(b_r, (0, LANES - nr)).reshape(1, LANES)
        xf, xg, eid, rw = _outproj(fox, ret, xf, w_out[layer].astype(BF16),
                                   norm_ffn_w[layer].reshape(1, d), w_r, b_r)
        s_tok, s_dst, blk, ex, flags, starts = _dispatch_plan(eid, n)
        y = _moe(xg, s_tok, s_dst, blk, ex, flags, starts,
                 w_expert_gate, w_expert_up, w_expert_down, layer)
    out = _final(xf, y, rw, norm_final_w.reshape(1, d))
    return out.reshape(batch, seq, d)
```

```python
import functools

import jax
import jax.numpy as jnp
from jax import lax
from jax.experimental import pallas as pl
from jax.experimental.pallas import tpu as pltpu

F32 = jnp.float32
BF16 = jnp.bfloat16

D_MODEL = 1024
FOX_HEADS = 8
FOX_HEAD_DIM = 64
FOX_WIDTH = 512
RET_HEADS = 4
RET_HEAD_DIM = 128
RET_WIDTH = 512
CHUNK = 64
ROPE_BASE = 10000.0
N_GROUPS = 4
EXPERTS_PER_GROUP = 8
N_EXPERTS = 32
TOP_K = 2
D_EXPERT = 512
RMS_EPS = 1e-6

LANES = 128
VMEM_LIMIT = 56 * 1024 * 1024

ROW_TILE = 512
ROW_PARTS = 2
WIDE_TILE = 1024
ATT_TILE = 512
RET_TILE = 512
MOE_TILE = 256
GROUP_W = FOX_WIDTH
SLAB = EXPERTS_PER_GROUP
N_MAIN = 7 * GROUP_W
EXP_UNDERFLOW = 110.0


def _rms(xf, w):
    return xf * lax.rsqrt(jnp.mean(xf * xf, axis=-1, keepdims=True) + RMS_EPS) * w


def _dot(a, b):
    return jnp.dot(a, b, preferred_element_type=F32)


def _dot_nt(a, b):
    return lax.dot_general(a, b, (((1,), (1,)), ((), ())), preferred_element_type=F32)


def _dot_tn(a, b):
    return lax.dot_general(a, b, (((0,), (0,)), ((), ())), preferred_element_type=F32)


CHUNKS = D_MODEL // LANES


def _tm_load(ref, base, rows):
    return jnp.concatenate([ref[pl.ds(base + c, rows, stride=CHUNKS), :] for c in range(CHUNKS)],
                           axis=1)


def _tm_store(ref, base, rows, val):
    for c in range(CHUNKS):
        ref[pl.ds(base + c, rows, stride=CHUNKS), :] = val[:, c * LANES:(c + 1) * LANES]


def _inproj_kernel(*refs, has_y, tiles_per_seq):
    if has_y:
        x_ref, y0_ref, y1_ref, rw_ref = refs[:4]
        refs = refs[4:]
    else:
        x_ref = refs[0]
        refs = refs[1:]
    nw_ref, w_ref, bff_ref, tri_ref = refs[:4]
    refs = refs[4:]
    if has_y:
        xres_ref = refs[0]
        refs = refs[1:]
    fq_ref, fk_ref, fv_ref, rq_ref, rk_ref, rv_ref, rg_ref, ct_ref, carry_sc = refs

    i = pl.program_id(0)
    tm = x_ref.shape[0]
    part_rows = tm // ROW_PARTS
    outs = (fq_ref, fk_ref, fv_ref, rq_ref, rk_ref, rv_ref)
    z_parts = []
    for part in range(ROW_PARTS):
        rows = slice(part * part_rows, (part + 1) * part_rows)
        x = x_ref[rows, :]
        if has_y:
            rw = rw_ref[rows, :]
            base = part * part_rows * CHUNKS
            x = x + (rw[:, 0:1] * _tm_load(y0_ref, base, part_rows)
                     + rw[:, 1:2] * _tm_load(y1_ref, base, part_rows))
            xres_ref[rows, :] = x
        h = _rms(x, nw_ref[...]).astype(BF16)
        for j, o_ref in enumerate(outs):
            acc = _dot(h, w_ref[:, j * GROUP_W:(j + 1) * GROUP_W])
            if j == 0:
                acc = acc * (FOX_HEAD_DIM ** -0.5)
            o_ref[rows, :] = acc.astype(BF16)
        acc = _dot(h, w_ref[:, 6 * GROUP_W:])
        rg_ref[rows, :] = acc[:, :GROUP_W].astype(BF16)
        z_parts.append(acc[:, GROUP_W:])

    z = jnp.concatenate(z_parts, axis=0) + bff_ref[...]
    lf = jnp.minimum(z, 0.0) - jnp.log1p(jnp.exp(-jnp.abs(z)))
    lft = lf.T[:FOX_HEADS, :]
    hi = lft.astype(BF16).astype(F32)
    mid = (lft - hi).astype(BF16).astype(F32)
    lo = lft - hi - mid
    pieces = jnp.concatenate([hi, mid, lo, jnp.zeros_like(hi)], axis=0).astype(BF16)
    cs = _dot(pieces, tri_ref[...])
    cs = cs[0:FOX_HEADS] + cs[FOX_HEADS:2 * FOX_HEADS] + cs[2 * FOX_HEADS:3 * FOX_HEADS]

    @pl.when(i % tiles_per_seq == 0)
    def _():
        carry_sc[...] = jnp.zeros_like(carry_sc)

    c = cs + carry_sc[:, 0:1]
    carry_sc[...] = jnp.broadcast_to(c[:, -1:], carry_sc.shape)
    ct_ref[0] = c


def _inproj(x, y, rw, nw, w_main, b_ff, seq):
    n = x.shape[0]
    tm = ROW_TILE
    tri = jnp.triu(jnp.ones((tm, tm), F32)).astype(BF16)
    nt = n // tm
    tps = seq // tm
    has_y = y is not None
    row_spec = pl.BlockSpec((tm, D_MODEL), lambda i: (i, 0))
    in_specs = [row_spec]
    args = [x]
    if has_y:
        in_specs += [pl.BlockSpec((tm * CHUNKS, LANES), lambda i: (i, 0)),
                     pl.BlockSpec((tm * CHUNKS, LANES), lambda i: (i + nt, 0)),
                     pl.BlockSpec((tm, LANES), lambda i: (i, 0))]
        args += [y, y, rw]
    in_specs += [
        pl.BlockSpec((1, D_MODEL), lambda i: (0, 0)),
        pl.BlockSpec((D_MODEL, N_MAIN + LANES), lambda i: (0, 0)),
        pl.BlockSpec((1, LANES), lambda i: (0, 0)),
        pl.BlockSpec((tm, tm), lambda i: (0, 0)),
    ]
    args += [nw, w_main, b_ff, tri]
    half_spec = pl.BlockSpec((tm, GROUP_W), lambda i: (i, 0))
    out_shape = []
    out_specs = []
    if has_y:
        out_shape.append(jax.ShapeDtypeStruct((n, D_MODEL), F32))
        out_specs.append(row_spec)
    out_shape += [jax.ShapeDtypeStruct((n, GROUP_W), BF16)] * 7
    out_specs += [half_spec] * 7
    out_shape.append(jax.ShapeDtypeStruct((n // seq, 8, seq), F32))
    out_specs.append(pl.BlockSpec((1, 8, tm), lambda i: (i // tps, 0, i % tps)))
    outs = pl.pallas_call(
        functools.partial(_inproj_kernel, has_y=has_y, tiles_per_seq=tps),
        grid=(nt,),
        in_specs=in_specs,
        out_specs=out_specs,
        out_shape=out_shape,
        scratch_shapes=[pltpu.VMEM((8, LANES), F32)],
        compiler_params=pltpu.CompilerParams(
            dimension_semantics=("arbitrary",), vmem_limit_bytes=VMEM_LIMIT),
        name="inproj_y" if has_y else "inproj",
    )(*args)
    if has_y:
        return outs[0], outs[1:]
    return x, outs


def _fox_kernel(q_ref, k_ref, v_ref, ct_ref, o_ref, m_sc, acc_sc, kmax_sc, *, tile, nq):
    hp = pl.program_id(1)
    qi = pl.program_id(2)
    q2 = q_ref[0]
    lane = lax.broadcasted_iota(jnp.int32, q2.shape, 1)
    first = lane < FOX_HEAD_DIM
    zero = jnp.zeros_like(q2)
    qh = (jnp.where(first, q2, zero), jnp.where(first, zero, q2))
    reps = tile // LANES

    def head_sqnorm_max(xf, h):
        sq = xf * xf
        sq = jnp.where(first, sq, 0.0) if h == 0 else jnp.where(first, 0.0, sq)
        return jnp.max(jnp.sum(sq, axis=1, keepdims=True), axis=0, keepdims=True)

    @pl.when(qi == 0)
    def _():
        for j in range(nq):
            kf = k_ref[0, j * tile:(j + 1) * tile, :].astype(F32)
            for h in range(2):
                kmax_sc[h, j:j + 1, :] = jnp.broadcast_to(head_sqnorm_max(kf, h), (1, LANES))

    def head_step(h, kb, k_blk, v_blk, mask, m_old, acc_old):
        one = jnp.ones_like(v_blk)
        va = jnp.where(first, v_blk, one) if h == 0 else jnp.where(first, one, v_blk)
        s = _dot_nt(qh[h], k_blk) - ct_ref[0, 2 * hp + h, pl.ds(kb, 1), :]
        if mask is not None:
            s = jnp.where(mask, s, -jnp.inf)
        m_cur = jnp.max(s, axis=1, keepdims=True)
        if m_old is None:
            m_new = jnp.broadcast_to(m_cur, (tile, LANES))
            p = jnp.exp(s - jnp.concatenate([m_new] * reps, axis=1))
            acc = _dot(p.astype(BF16), va)
        else:
            m_new = jnp.maximum(m_old, m_cur)
            alpha = jnp.exp(m_old - m_new)
            p = jnp.exp(s - jnp.concatenate([m_new] * reps, axis=1))
            acc = alpha * acc_old + _dot(p.astype(BF16), va)
        return m_new, acc

    def diagonal(with_previous):
        row = lax.broadcasted_iota(jnp.int32, (tile, tile), 0)
        col = lax.broadcasted_iota(jnp.int32, (tile, tile), 1)
        start = pl.multiple_of(qi * tile, tile)
        k_blk = k_ref[0, pl.ds(start, tile), :]
        v_blk = v_ref[0, pl.ds(start, tile), :]
        state = [head_step(h, qi, k_blk, v_blk, col <= row, None, None) for h in range(2)]
        if with_previous:
            start = pl.multiple_of((qi - 1) * tile, tile)
            k_blk = k_ref[0, pl.ds(start, tile), :]
            v_blk = v_ref[0, pl.ds(start, tile), :]
            state = [head_step(h, qi - 1, k_blk, v_blk, None, *state[h]) for h in range(2)]
        for h in range(2):
            m_sc[h] = state[h][0]
            acc_sc[h] = state[h][1]

    @pl.when(qi == 0)
    def _():
        diagonal(False)

    @pl.when(qi > 0)
    def _():
        diagonal(True)

    top = jnp.maximum(qi - 1, 0)

    def off_diagonal(kbs, heads=(0, 1)):
        state = {h: (m_sc[h], acc_sc[h]) for h in heads}
        for kb in kbs:
            start = pl.multiple_of(kb * tile, tile)
            k_blk = k_ref[0, pl.ds(start, tile), :]
            v_blk = v_ref[0, pl.ds(start, tile), :]
            state = {h: head_step(h, kb, k_blk, v_blk, None, *state[h]) for h in heads}
        for h in heads:
            m_sc[h] = state[h][0]
            acc_sc[h] = state[h][1]

    jrow = lax.broadcasted_iota(jnp.int32, (nq, LANES), 0)
    qf = q2.astype(F32)
    j_start = []
    for h in range(2):
        m_low = jnp.min(jnp.min(m_sc[h], axis=1, keepdims=True), axis=0, keepdims=True)
        c_last = ct_ref[0, 2 * hp + h, :, tile - 1:tile]
        bound = jnp.sqrt(head_sqnorm_max(qf, h) * kmax_sc[h]) - c_last
        need = (bound - m_low > -EXP_UNDERFLOW) & (jrow < top)
        j_first = jnp.min(jnp.min(jnp.where(need, jrow, top), axis=1, keepdims=True),
                          axis=0, keepdims=True)
        j_start.append(j_first[0, 0])
    j_both = jnp.maximum(j_start[0], j_start[1])
    count = top - j_both

    def pair(j, carry):
        off_diagonal((j_both + 2 * j, j_both + 2 * j + 1))
        return carry

    lax.fori_loop(0, count // 2, pair, 0)

    @pl.when(count % 2 == 1)
    def _():
        off_diagonal((top - 1,))

    for h in range(2):
        lone = j_both - j_start[h]

        def single_pair(j, carry, h=h):
            off_diagonal((j_start[h] + 2 * j, j_start[h] + 2 * j + 1), heads=(h,))
            return carry

        lax.fori_loop(0, lone // 2, single_pair, 0)

        @pl.when(lone % 2 == 1)
        def _(h=h):
            off_diagonal((j_both - 1,), heads=(h,))

    a0 = acc_sc[0]
    a1 = acc_sc[1]
    half = FOX_HEAD_DIM
    o = jnp.where(first, a0 / pltpu.roll(a0, half, 1), a1 / pltpu.roll(a1, half, 1))
    o_ref[0] = o.astype(o_ref.dtype)


def _fox_attention(fq, fk, fv, ct4, batch, seq):
    t = ATT_TILE
    nq = seq // t
    q3 = fq.reshape(batch, seq, FOX_WIDTH)
    k3 = fk.reshape(batch, seq, FOX_WIDTH)
    v3 = fv.reshape(batch, seq, FOX_WIDTH)
    out = pl.pallas_call(
        functools.partial(_fox_kernel, tile=t, nq=nq),
        grid=(batch, FOX_HEADS // 2, nq),
        in_specs=[
            pl.BlockSpec((1, t, LANES), lambda b, j, i: (b, i, j)),
            pl.BlockSpec((1, seq, LANES), lambda b, j, i: (b, 0, j)),
            pl.BlockSpec((1, seq, LANES), lambda b, j, i: (b, 0, j)),
            pl.BlockSpec((1, 8, nq, t), lambda b, j, i: (b, 0, 0, 0)),
        ],
        out_specs=pl.BlockSpec((1, t, LANES), lambda b, j, i: (b, i, j)),
        out_shape=jax.ShapeDtypeStruct((batch, seq, FOX_WIDTH), BF16),
        scratch_shapes=[pltpu.VMEM((2, t, LANES), F32), pltpu.VMEM((2, t, LANES), F32),
                        pltpu.VMEM((2, nq, LANES), F32)],
        compiler_params=pltpu.CompilerParams(
            dimension_semantics=("parallel", "parallel", "arbitrary"),
            vmem_limit_bytes=VMEM_LIMIT),
        name="fox_attention",
    )(q3, k3, v3, ct4)
    return out.reshape(batch * seq, FOX_WIDTH)


def _ret_kernel(q_ref, k_ref, v_ref, g_ref, cos_ref, sin_ref, dmat_ref, qdec_ref, kdec_ref,
                sdec_ref, o_ref, state_sc):
    si = pl.program_id(1)

    @pl.when(si == 0)
    def _():
        state_sc[...] = jnp.zeros_like(state_sc)

    cos2 = cos_ref[...]
    sin2 = sin_ref[...]
    dk = RET_HEAD_DIM

    def rot(xf):
        return xf * cos2 + pltpu.roll(xf, dk // 2, 1) * sin2

    for h in range(RET_HEADS):
        cols = slice(h * dk, (h + 1) * dk)
        q = rot(q_ref[0, :, cols].astype(F32))
        k = rot(k_ref[0, :, cols].astype(F32)) * (dk ** -0.5)
        v = v_ref[0, :, cols]
        scores = _dot_nt(q.astype(BF16), k.astype(BF16)) * dmat_ref[h]
        intra = _dot(scores.astype(BF16), v)
        state = state_sc[h]
        cross = _dot((q * qdec_ref[h]).astype(BF16), state.astype(BF16))
        out = intra + cross
        state_sc[h] = state * sdec_ref[h, 0:1, :] + _dot_tn((k * kdec_ref[h]).astype(BF16), v)

        y = out * lax.rsqrt(jnp.mean(out * out, axis=-1, keepdims=True) + RMS_EPS)
        g = g_ref[0, :, cols].astype(F32)
        o_ref[0, :, cols] = (y * (g * jax.nn.sigmoid(g))).astype(o_ref.dtype)


def _ret_tables(seq):
    half = RET_HEAD_DIM // 2
    inv_freq = 1.0 / (ROPE_BASE ** (jnp.arange(half, dtype=F32) / half))
    ang = jnp.arange(seq, dtype=F32)[:, None] * inv_freq[None, :]
    cos = jnp.cos(ang)
    sin = jnp.sin(ang)
    cos2 = jnp.concatenate([cos, cos], axis=1)
    sin2 = jnp.concatenate([-sin, sin], axis=1)
    lt = RET_TILE
    log_gamma = jnp.log(1.0 - 2.0 ** (-5.0 - jnp.arange(RET_HEADS, dtype=F32)))
    idx = jnp.arange(lt)
    t = idx[:, None]
    s = idx[None, :]
    same = (t // CHUNK) == (s // CHUNK)
    earlier = (s // CHUNK) < (t // CHUNK)
    dist = jnp.where(same, jnp.abs(t - s), t - s).astype(F32)
    dmat = jnp.where((same | earlier)[None], jnp.exp(log_gamma[:, None, None] * dist[None]), 0.0)
    idxf = idx.astype(F32)
    qdec = jnp.exp(log_gamma[:, None] * idxf[None, :])
    kdec = jnp.exp(log_gamma[:, None] * (lt - idxf)[None, :])
    sdec = jnp.exp(log_gamma * lt)
    qdec = jnp.broadcast_to(qdec[:, :, None], (RET_HEADS, lt, LANES))
    kdec = jnp.broadcast_to(kdec[:, :, None], (RET_HEADS, lt, LANES))
    sdec = jnp.broadcast_to(sdec[:, None, None], (RET_HEADS, 8, LANES))
    return cos2, sin2, dmat, qdec, kdec, sdec


def _retention(rq, rk, rv, rg, tables, batch, seq):
    lt = RET_TILE
    ns = seq // lt
    cos2, sin2, dmat, qdec, kdec, sdec = tables
    blk = pl.BlockSpec((1, lt, RET_WIDTH), lambda b, i: (b, i, 0))
    tab = pl.BlockSpec((lt, LANES), lambda b, i: (i, 0))
    args = [a.reshape(batch, seq, RET_WIDTH) for a in (rq, rk, rv, rg)]
    out = pl.pallas_call(
        _ret_kernel,
        grid=(batch, ns),
        in_specs=[blk, blk, blk, blk, tab, tab,
                  pl.BlockSpec((RET_HEADS, lt, lt), lambda b, i: (0, 0, 0)),
                  pl.BlockSpec((RET_HEADS, lt, LANES), lambda b, i: (0, 0, 0)),
                  pl.BlockSpec((RET_HEADS, lt, LANES), lambda b, i: (0, 0, 0)),
                  pl.BlockSpec((RET_HEADS, 8, LANES), lambda b, i: (0, 0, 0))],
        out_specs=blk,
        out_shape=jax.ShapeDtypeStruct((batch, seq, RET_WIDTH), BF16),
        scratch_shapes=[pltpu.VMEM((RET_HEADS, RET_HEAD_DIM, RET_HEAD_DIM), F32)],
        compiler_params=pltpu.CompilerParams(
            dimension_semantics=("parallel", "arbitrary"),
            vmem_limit_bytes=VMEM_LIMIT),
        name="retention",
    )(*args, cos2, sin2, dmat, qdec, kdec, sdec)
    return out.reshape(batch * seq, RET_WIDTH)


def _outproj_kernel(fox_ref, ret_ref, x_ref, wo_ref, nw_ref, wr_ref, br_ref, xo_ref, xg_ref,
                    eid_ref, rw_ref):
    mixed = jnp.concatenate([fox_ref[...], ret_ref[...]], axis=1)
    x = x_ref[...] + _dot(mixed, wo_ref[...])
    xo_ref[...] = x
    hf = _rms(x, nw_ref[...])
    _tm_store(xg_ref, 0, x.shape[0], hf)
    h = hf.astype(BF16)
    lt = (_dot(h, wr_ref[...]) + br_ref[...]).T
    tm = lt.shape[1]
    rowid = lax.broadcasted_iota(jnp.int32, (SLAB, tm), 0)
    neg = -jnp.inf

    def top1(v):
        vmax = jnp.max(v, axis=0, keepdims=True)
        idx = jnp.min(jnp.where(v == vmax, rowid, SLAB), axis=0, keepdims=True)
        return vmax, idx

    gl = jnp.where(rowid < N_GROUPS, lt[0:SLAB], neg)
    gmax, gidx = top1(gl)
    g_w = 1.0 / jnp.sum(jnp.exp(gl - gmax), axis=0, keepdims=True)
    e_in = jnp.zeros((SLAB, tm), F32)
    for g in range(N_GROUPS):
        e_in = jnp.where(gidx == g, lt[SLAB * (g + 1):SLAB * (g + 2)], e_in)
    v1, i1 = top1(e_in)
    rest = jnp.where(rowid == i1, neg, e_in)
    v2, i2 = top1(rest)
    t = jnp.exp(v2 - v1)
    w1 = g_w / (1.0 + t)
    eid_ref[0:1, :] = gidx * EXPERTS_PER_GROUP + i1
    eid_ref[1:2, :] = gidx * EXPERTS_PER_GROUP + i2
    wslab = jnp.concatenate([w1, w1 * t, jnp.zeros((LANES - TOP_K, tm), F32)], axis=0)
    rw_ref[...] = wslab.T


def _outproj(fox, ret, x, wo, nw, wr, br):
    n = x.shape[0]
    tm = WIDE_TILE
    row = pl.BlockSpec((tm, D_MODEL), lambda i: (i, 0))
    half = pl.BlockSpec((tm, GROUP_W), lambda i: (i, 0))
    pair = pl.BlockSpec((TOP_K, tm), lambda i: (0, i))
    wts = pl.BlockSpec((tm, LANES), lambda i: (i, 0))
    return pl.pallas_call(
        _outproj_kernel,
        grid=(n // tm,),
        in_specs=[half, half, row,
                  pl.BlockSpec((D_MODEL, D_MODEL), lambda i: (0, 0)),
                  pl.BlockSpec((1, D_MODEL), lambda i: (0, 0)),
                  pl.BlockSpec((D_MODEL, LANES), lambda i: (0, 0)),
                  pl.BlockSpec((1, LANES), lambda i: (0, 0))],
        out_specs=[row, pl.BlockSpec((tm * CHUNKS, LANES), lambda i: (i, 0)), pair, wts],
        out_shape=[jax.ShapeDtypeStruct((n, D_MODEL), F32),
                   jax.ShapeDtypeStruct((n * CHUNKS, LANES), F32),
                   jax.ShapeDtypeStruct((TOP_K, n), jnp.int32),
                   jax.ShapeDtypeStruct((n, LANES), F32)],
        compiler_params=pltpu.CompilerParams(
            dimension_semantics=("parallel",), vmem_limit_bytes=VMEM_LIMIT),
        name="outproj",
    )(fox, ret, x, wo, nw, wr, br)


_FIRST, _LAST, _VALID, _NEWEXP = 1, 2, 4, 8


def _moe_kernel(blk_ref, exp_ref, flag_ref, starts_ref,
                tokc_ref, tokn_ref, dstp_ref, dstc_ref, x_hbm, wg_ref, wu_ref, wd_ref,
                y_hbm, xbuf, ybuf, hbuf, wgb, wub, wdb, gsem, ssem, *, tb, nb):
    w = pl.program_id(0)
    b = blk_ref[w]
    e = exp_ref[w]
    flags = flag_ref[w]
    slot = b % 2
    nslot = 1 - slot
    span = tb * CHUNKS

    def hbm_row(ref, idx):
        return ref.at[pl.ds(pl.multiple_of(idx * CHUNKS, CHUNKS), CHUNKS), :]

    def buf_row(buf, s, r):
        return buf.at[pl.ds(pl.multiple_of(s * span + r * CHUNKS, CHUNKS), CHUNKS), :]

    def start_gather(tok_ref, s):
        for r in range(tb):
            pltpu.make_async_copy(hbm_row(x_hbm, tok_ref[0, 0, r]), buf_row(xbuf, s, r),
                                  gsem.at[s]).start()

    def start_scatter(dst_ref, s):
        for r in range(tb):
            pltpu.make_async_copy(buf_row(ybuf, s, r), hbm_row(y_hbm, dst_ref[0, 0, r]),
                                  ssem.at[s]).start()

    def wait_rows(sem, s):
        whole = pl.ds(pl.multiple_of(s * span, span), span)
        pltpu.make_async_copy(xbuf.at[whole, :], ybuf.at[whole, :], sem.at[s]).wait()

    @pl.when((flags & _FIRST) != 0)
    def _():
        @pl.when(w == 0)
        def _():
            start_gather(tokc_ref, 0)

        wait_rows(gsem, slot)

        @pl.when(b >= 2)
        def _():
            wait_rows(ssem, slot)

        @pl.when(b + 1 < nb)
        def _():
            start_gather(tokn_ref, nslot)

        @pl.when(b >= 1)
        def _():
            start_scatter(dstp_ref, nslot)

        hbuf[...] = _tm_load(xbuf, slot * span, tb).astype(BF16)

        @pl.when((flags & _LAST) == 0)
        def _():
            ybuf[pl.ds(pl.multiple_of(slot * span, span), span), :] = jnp.zeros((span, LANES), F32)

    @pl.when((flags & _NEWEXP) != 0)
    def _():
        wgb[...] = wg_ref[0, 0].astype(BF16)
        wub[...] = wu_ref[0, 0].astype(BF16)
        wdb[...] = wd_ref[0, 0].astype(BF16)

    @pl.when((flags & _VALID) != 0)
    def _():
        h = hbuf[...]
        g = _dot(h, wgb[...])
        u = _dot(h, wub[...])
        a = (g * jax.nn.sigmoid(g) * u).astype(BF16)
        y = _dot(a, wdb[...])
        whole_block = (flags & (_FIRST | _LAST)) == (_FIRST | _LAST)

        @pl.when(whole_block)
        def _():
            _tm_store(ybuf, slot * span, tb, y)

        @pl.when(jnp.logical_not(whole_block))
        def _():
            q = b * tb + lax.broadcasted_iota(jnp.int32, y.shape, 0)
            mine = (q >= starts_ref[e]) & (q < starts_ref[e + 1])
            _tm_store(ybuf, slot * span, tb, jnp.where(mine, y, _tm_load(ybuf, slot * span, tb)))

    @pl.when(((flags & _LAST) != 0) & (b == nb - 1))
    def _():
        start_scatter(dstc_ref, slot)
        if nb >= 2:
            wait_rows(ssem, nslot)
        wait_rows(ssem, slot)


def _moe(x, s_tok, s_dst, blk, exp, flags, starts, wg, wu, wd, layer, tb=MOE_TILE):
    na = s_tok.shape[0]
    nb = na // tb
    d_model, d_expert = wg.shape[2], wg.shape[3]
    tok3 = s_tok.reshape(nb, 1, tb)
    dst3 = s_dst.reshape(nb, 1, tb)
    smem_blk = lambda f: pl.BlockSpec((1, 1, tb), f, memory_space=pltpu.SMEM)
    grid_spec = pltpu.PrefetchScalarGridSpec(
        num_scalar_prefetch=4,
        grid=(blk.shape[0],),
        in_specs=[
            smem_blk(lambda w, bl, ex, fl, st: (bl[w], 0, 0)),
            smem_blk(lambda w, bl, ex, fl, st: (jnp.minimum(bl[w] + 1, nb - 1), 0, 0)),
            smem_blk(lambda w, bl, ex, fl, st: (jnp.maximum(bl[w] - 1, 0), 0, 0)),
            smem_blk(lambda w, bl, ex, fl, st: (bl[w], 0, 0)),
            pl.BlockSpec(memory_space=pl.ANY),
            pl.BlockSpec((1, 1, d_model, d_expert), lambda w, bl, ex, fl, st: (layer, ex[w], 0, 0)),
            pl.BlockSpec((1, 1, d_model, d_expert), lambda w, bl, ex, fl, st: (layer, ex[w], 0, 0)),
            pl.BlockSpec((1, 1, d_expert, d_model), lambda w, bl, ex, fl, st: (layer, ex[w], 0, 0)),
        ],
        out_specs=pl.BlockSpec(memory_space=pl.ANY),
        scratch_shapes=[pltpu.VMEM((2 * tb * CHUNKS, LANES), F32),
                        pltpu.VMEM((2 * tb * CHUNKS, LANES), F32),
                        pltpu.VMEM((tb, d_model), BF16),
                        pltpu.VMEM((d_model, d_expert), BF16), pltpu.VMEM((d_model, d_expert), BF16),
                        pltpu.VMEM((d_expert, d_model), BF16),
                        pltpu.SemaphoreType.DMA((2,)), pltpu.SemaphoreType.DMA((2,))],
    )
    return pl.pallas_call(
        functools.partial(_moe_kernel, tb=tb, nb=nb),
        grid_spec=grid_spec,
        out_shape=jax.ShapeDtypeStruct((na * CHUNKS, LANES), F32),
        compiler_params=pltpu.CompilerParams(
            dimension_semantics=("arbitrary",), vmem_limit_bytes=VMEM_LIMIT),
        name="moe_experts",
    )(blk, exp, flags, starts, tok3, tok3, dst3, dst3, x, wg, wu, wd)


def _dispatch_plan(eid, n, tb=MOE_TILE, n_experts=N_EXPERTS):
    na = TOP_K * n
    nb = na // tb
    eid_flat = eid.reshape(na)
    assert n_experts * na < 2 ** 31
    s_a = lax.sort(eid_flat * na + jnp.arange(na, dtype=jnp.int32), is_stable=False) % na
    s_tok = s_a % n
    counts = jnp.sum(eid_flat[None, :] == jnp.arange(n_experts, dtype=jnp.int32)[:, None], axis=1)
    starts = jnp.concatenate([jnp.zeros((1,), jnp.int32),
                              jnp.cumsum(counts).astype(jnp.int32)])
    lo, hi = starts[:-1], starts[1:]
    nonempty = hi > lo
    first_blk = lo // tb
    npass = jnp.where(nonempty, (hi - 1) // tb - first_blk + 1, 0)
    cum = jnp.cumsum(npass)
    total = cum[-1]
    n_pass = nb + n_experts
    w = jnp.arange(n_pass, dtype=jnp.int32)
    wc = jnp.minimum(w, total - 1)
    ex = jnp.sum(cum[None, :] <= wc[:, None], axis=1).astype(jnp.int32)
    sel = ex[:, None] == jnp.arange(n_experts, dtype=jnp.int32)[None, :]
    pick = lambda v: jnp.sum(jnp.where(sel, v[None, :], 0), axis=1)
    blk = (pick(first_blk) + (wc - pick(cum - npass))).astype(jnp.int32)
    valid = w < total
    prev_blk = jnp.concatenate([jnp.full((1,), -1, jnp.int32), blk[:-1]])
    next_blk = jnp.concatenate([blk[1:], jnp.full((1,), -1, jnp.int32)])
    first = valid & (blk != prev_blk)
    last = valid & ((blk != next_blk) | (w == total - 1))
    prev_ex = jnp.concatenate([jnp.full((1,), -1, jnp.int32), ex[:-1]])
    newexp = valid & (ex != prev_ex)
    flags = (first * _FIRST + last * _LAST + valid * _VALID + newexp * _NEWEXP).astype(jnp.int32)
    return s_tok, s_a, blk, ex, flags, starts


def _final_kernel(x_ref, y0_ref, y1_ref, rw_ref, nw_ref, o_ref):
    rw = rw_ref[...]
    tm = x_ref.shape[0]
    x = x_ref[...] + (rw[:, 0:1] * _tm_load(y0_ref, 0, tm) + rw[:, 1:2] * _tm_load(y1_ref, 0, tm))
    o_ref[...] = _rms(x, nw_ref[...])


def _final(x, y, rw, nw):
    n = x.shape[0]
    tm = WIDE_TILE
    nt = n // tm
    row = pl.BlockSpec((tm, D_MODEL), lambda i: (i, 0))
    return pl.pallas_call(
        _final_kernel,
        grid=(nt,),
        in_specs=[row, pl.BlockSpec((tm * CHUNKS, LANES), lambda i: (i, 0)),
                  pl.BlockSpec((tm * CHUNKS, LANES), lambda i: (i + nt, 0)),
                  pl.BlockSpec((tm, LANES), lambda i: (i, 0)),
                  pl.BlockSpec((1, D_MODEL), lambda i: (0, 0))],
        out_specs=row,
        out_shape=jax.ShapeDtypeStruct((n, D_MODEL), F32),
        compiler_params=pltpu.CompilerParams(
            dimension_semantics=("parallel",), vmem_limit_bytes=VMEM_LIMIT),
        name="final_norm",
    )(x, y, y, rw, nw)


def kernel(x, norm_mix_w, w_in, fox_forget_b, w_out, norm_ffn_w, w_router_group, b_router_group,
           w_router_expert, b_router_expert, w_expert_gate, w_expert_up, w_expert_down,
           norm_final_w):
    batch, seq, d = x.shape
    n = batch * seq
    depth = w_in.shape[0]
    xf = x.reshape(n, d)
    tables = _ret_tables(seq)
    nq = seq // ATT_TILE

    y = rw = None
    for layer in range(depth):
        wl = w_in[layer]
        c0 = 3 * FOX_WIDTH
        w_ff = jnp.pad(wl[:, c0:c0 + FOX_HEADS], ((0, 0), (0, LANES - FOX_HEADS)))
        w_main = jnp.concatenate([wl[:, :c0], wl[:, c0 + FOX_HEADS:], w_ff], axis=1).astype(BF16)
        b_ff = jnp.pad(fox_forget_b[layer], (0, LANES - FOX_HEADS)).reshape(1, LANES)
        xf, (fq, fk, fv, rq, rk, rv, rg, ct) = _inproj(
            xf, y, rw, norm_mix_w[layer].reshape(1, d), w_main, b_ff, seq)
        ct4 = ct.reshape(batch, 8, nq, ATT_TILE)
        fox = _fox_attention(fq, fk, fv, ct4, batch, seq)
        ret = _retention(rq, rk, rv, rg, tables, batch, seq)

        zpad = jnp.zeros((d, SLAB - N_GROUPS), F32)
        w_r = jnp.concatenate([w_router_group[layer], zpad, w_router_expert[layer]], axis=1)
        nr = SLAB + N_EXPERTS
        w_r = jnp.pad(w_r, ((0, 0), (0, LANES - nr))).astype(BF16)
        b_r = jnp.concatenate([b_router_group[layer], jnp.zeros((SLAB - N_GROUPS,), F32),
                               b_router_expert[layer]])
        b_r = jnp.pad(b_r, (0, LANES - nr)).reshape(1, LANES)
        xf, xg, eid, rw = _outproj(fox, ret, xf, w_out[layer].astype(BF16),
                                   norm_ffn_w[layer].reshape(1, d), w_r, b_r)
        s_tok, s_dst, blk, ex, flags, starts = _dispatch_plan(eid, n)
        y = _moe(xg, s_tok, s_dst, blk, ex, flags, starts,
                 w_expert_gate, w_expert_up, w_expert_down, layer)
    out = _final(xf, y, rw, norm_final_w.reshape(1, d))
    return out.reshape(batch, seq, d)
```

```python
import functools

import jax
import jax.numpy as jnp
from jax import lax
from jax.experimental import pallas as pl
from jax.experimental.pallas import tpu as pltpu

F32 = jnp.float32
BF16 = jnp.bfloat16

D_MODEL = 1024
FOX_HEADS = 8
FOX_HEAD_DIM = 64
FOX_WIDTH = 512
RET_HEADS = 4
RET_HEAD_DIM = 128
RET_WIDTH = 512
CHUNK = 64
ROPE_BASE = 10000.0
N_GROUPS = 4
EXPERTS_PER_GROUP = 8
N_EXPERTS = 32
TOP_K = 2
D_EXPERT = 512
RMS_EPS = 1e-6

LANES = 128
VMEM_LIMIT = 56 * 1024 * 1024

ROW_TILE = 512
ROW_PARTS = 2
WIDE_TILE = 1024
ATT_TILE = 512
RET_TILE = 512
MOE_TILE = 256
GROUP_W = FOX_WIDTH
SLAB = EXPERTS_PER_GROUP
N_MAIN = 7 * GROUP_W
EXP_UNDERFLOW = 110.0


def _rms(xf, w):
    return xf * lax.rsqrt(jnp.mean(xf * xf, axis=-1, keepdims=True) + RMS_EPS) * w


def _dot(a, b):
    return jnp.dot(a, b, preferred_element_type=F32)


def _dot_nt(a, b):
    return lax.dot_general(a, b, (((1,), (1,)), ((), ())), preferred_element_type=F32)


def _dot_tn(a, b):
    return lax.dot_general(a, b, (((0,), (0,)), ((), ())), preferred_element_type=F32)


CHUNKS = D_MODEL // LANES


def _tm_load(ref, base, rows):
    return jnp.concatenate([ref[pl.ds(base + c, rows, stride=CHUNKS), :] for c in range(CHUNKS)],
                           axis=1)


def _tm_store(ref, base, rows, val):
    for c in range(CHUNKS):
        ref[pl.ds(base + c, rows, stride=CHUNKS), :] = val[:, c * LANES:(c + 1) * LANES]


def _inproj_kernel(*refs, has_y, tiles_per_seq):
    if has_y:
        x_ref, y0_ref, y1_ref, rw_ref = refs[:4]
        refs = refs[4:]
    else:
        x_ref = refs[0]
        refs = refs[1:]
    nw_ref, w_ref, bff_ref, tri_ref = refs[:4]
    refs = refs[4:]
    if has_y:
        xres_ref = refs[0]
        refs = refs[1:]
    fq_ref, fk_ref, fv_ref, rq_ref, rk_ref, rv_ref, rg_ref, ct_ref, carry_sc = refs

    i = pl.program_id(0)
    tm = x_ref.shape[0]
    part_rows = tm // ROW_PARTS
    outs = (fq_ref, fk_ref, fv_ref, rq_ref, rk_ref, rv_ref)
    z_parts = []
    for part in range(ROW_PARTS):
        rows = slice(part * part_rows, (part + 1) * part_rows)
        x = x_ref[rows, :]
        if has_y:
            rw = rw_ref[rows, :]
            base = part * part_rows * CHUNKS
            x = x + (rw[:, 0:1] * _tm_load(y0_ref, base, part_rows)
                     + rw[:, 1:2] * _tm_load(y1_ref, base, part_rows))
            xres_ref[rows, :] = x
        h = _rms(x, nw_ref[...]).astype(BF16)
        for j, o_ref in enumerate(outs):
            acc = _dot(h, w_ref[:, j * GROUP_W:(j + 1) * GROUP_W])
            if j == 0:
                acc = acc * (FOX_HEAD_DIM ** -0.5)
            o_ref[rows, :] = acc.astype(BF16)
        acc = _dot(h, w_ref[:, 6 * GROUP_W:])
        rg_ref[rows, :] = acc[:, :GROUP_W].astype(BF16)
        z_parts.append(acc[:, GROUP_W:])

    z = jnp.concatenate(z_parts, axis=0) + bff_ref[...]
    lf = jnp.minimum(z, 0.0) - jnp.log1p(jnp.exp(-jnp.abs(z)))
    lft = lf.T[:FOX_HEADS, :]
    hi = lft.astype(BF16).astype(F32)
    mid = (lft - hi).astype(BF16).astype(F32)
    lo = lft - hi - mid
    pieces = jnp.concatenate([hi, mid, lo, jnp.zeros_like(hi)], axis=0).astype(BF16)
    cs = _dot(pieces, tri_ref[...])
    cs = cs[0:FOX_HEADS] + cs[FOX_HEADS:2 * FOX_HEADS] + cs[2 * FOX_HEADS:3 * FOX_HEADS]

    @pl.when(i % tiles_per_seq == 0)
    def _():
        carry_sc[...] = jnp.zeros_like(carry_sc)

    c = cs + carry_sc[:, 0:1]
    carry_sc[...] = jnp.broadcast_to(c[:, -1:], carry_sc.shape)
    ct_ref[0] = c


def _inproj(x, y, rw, nw, w_main, b_ff, seq):
    n = x.shape[0]
    tm = ROW_TILE
    tri = jnp.triu(jnp.ones((tm, tm), F32)).astype(BF16)
    nt = n // tm
    tps = seq // tm
    has_y = y is not None
    row_spec = pl.BlockSpec((tm, D_MODEL), lambda i: (i, 0))
    in_specs = [row_spec]
    args = [x]
    if has_y:
        in_specs += [pl.BlockSpec((tm * CHUNKS, LANES), lambda i: (i, 0)),
                     pl.BlockSpec((tm * CHUNKS, LANES), lambda i: (i + nt, 0)),
                     pl.BlockSpec((tm, LANES), lambda i: (i, 0))]
        args += [y, y, rw]
    in_specs += [
        pl.BlockSpec((1, D_MODEL), lambda i: (0, 0)),
        pl.BlockSpec((D_MODEL, N_MAIN + LANES), lambda i: (0, 0)),
        pl.BlockSpec((1, LANES), lambda i: (0, 0)),
        pl.BlockSpec((tm, tm), lambda i: (0, 0)),
    ]
    args += [nw, w_main, b_ff, tri]
    half_spec = pl.BlockSpec((tm, GROUP_W), lambda i: (i, 0))
    out_shape = []
    out_specs = []
    if has_y:
        out_shape.append(jax.ShapeDtypeStruct((n, D_MODEL), F32))
        out_specs.append(row_spec)
    out_shape += [jax.ShapeDtypeStruct((n, GROUP_W), BF16)] * 7
    out_specs += [half_spec] * 7
    out_shape.append(jax.ShapeDtypeStruct((n // seq, 8, seq), F32))
    out_specs.append(pl.BlockSpec((1, 8, tm), lambda i: (i // tps, 0, i % tps)))
    outs = pl.pallas_call(
        functools.partial(_inproj_kernel, has_y=has_y, tiles_per_seq=tps),
        grid=(nt,),
        in_specs=in_specs,
        out_specs=out_specs,
        out_shape=out_shape,
        scratch_shapes=[pltpu.VMEM((8, LANES), F32)],
        compiler_params=pltpu.CompilerParams(
            dimension_semantics=("arbitrary",), vmem_limit_bytes=VMEM_LIMIT),
        name="inproj_y" if has_y else "inproj",
    )(*args)
    if has_y:
        return outs[0], outs[1:]
    return x, outs


def _fox_kernel(q_ref, k_ref, v_ref, ct_ref, o_ref, m_sc, acc_sc, kmax_sc, *, tile, nq):
    hp = pl.program_id(1)
    qi = pl.program_id(2)
    q2 = q_ref[0]
    lane = lax.broadcasted_iota(jnp.int32, q2.shape, 1)
    first = lane < FOX_HEAD_DIM
    zero = jnp.zeros_like(q2)
    qh = (jnp.where(first, q2, zero), jnp.where(first, zero, q2))
    reps = tile // LANES

    def head_sqnorm_max(xf, h):
        sq = xf * xf
        sq = jnp.where(first, sq, 0.0) if h == 0 else jnp.where(first, 0.0, sq)
        return jnp.max(jnp.sum(sq, axis=1, keepdims=True), axis=0, keepdims=True)

    @pl.when(qi == 0)
    def _():
        for j in range(nq):
            kf = k_ref[0, j * tile:(j + 1) * tile, :].astype(F32)
            for h in range(2):
                kmax_sc[h, j:j + 1, :] = jnp.broadcast_to(head_sqnorm_max(kf, h), (1, LANES))

    def head_step(h, kb, k_blk, v_blk, mask, m_old, acc_old):
        one = jnp.ones_like(v_blk)
        va = jnp.where(first, v_blk, one) if h == 0 else jnp.where(first, one, v_blk)
        s = _dot_nt(qh[h], k_blk) - ct_ref[0, 2 * hp + h, pl.ds(kb, 1), :]
        if mask is not None:
            s = jnp.where(mask, s, -jnp.inf)
        m_cur = jnp.max(s, axis=1, keepdims=True)
        if m_old is None:
            m_new = jnp.broadcast_to(m_cur, (tile, LANES))
            p = jnp.exp(s - jnp.concatenate([m_new] * reps, axis=1))
            acc = _dot(p.astype(BF16), va)
        else:
            m_new = jnp.maximum(m_old, m_cur)
            alpha = jnp.exp(m_old - m_new)
            p = jnp.exp(s - jnp.concatenate([m_new] * reps, axis=1))
            acc = alpha * acc_old + _dot(p.astype(BF16), va)
        return m_new, acc

    def diagonal(with_previous):
        row = lax.broadcasted_iota(jnp.int32, (tile, tile), 0)
        col = lax.broadcasted_iota(jnp.int32, (tile, tile), 1)
        start = pl.multiple_of(qi * tile, tile)
        k_blk = k_ref[0, pl.ds(start, tile), :]
        v_blk = v_ref[0, pl.ds(start, tile), :]
        state = [head_step(h, qi, k_blk, v_blk, col <= row, None, None) for h in range(2)]
        if with_previous:
            start = pl.multiple_of((qi - 1) * tile, tile)
            k_blk = k_ref[0, pl.ds(start, tile), :]
            v_blk = v_ref[0, pl.ds(start, tile), :]
            state = [head_step(h, qi - 1, k_blk, v_blk, None, *state[h]) for h in range(2)]
        for h in range(2):
            m_sc[h] = state[h][0]
            acc_sc[h] = state[h][1]

    @pl.when(qi == 0)
    def _():
        diagonal(False)

    @pl.when(qi > 0)
    def _():
        diagonal(True)

    top = jnp.maximum(qi - 1, 0)

    def off_diagonal(kbs, heads=(0, 1)):
        state = {h: (m_sc[h], acc_sc[h]) for h in heads}
        for kb in kbs:
            start = pl.multiple_of(kb * tile, tile)
            k_blk = k_ref[0, pl.ds(start, tile), :]
            v_blk = v_ref[0, pl.ds(start, tile), :]
            state = {h: head_step(h, kb, k_blk, v_blk, None, *state[h]) for h in heads}
        for h in heads:
            m_sc[h] = state[h][0]
            acc_sc[h] = state[h][1]

    jrow = lax.broadcasted_iota(jnp.int32, (nq, LANES), 0)
    qf = q2.astype(F32)
    j_start = []
    for h in range(2):
        m_low = jnp.min(jnp.min(m_sc[h], axis=1, keepdims=True), axis=0, keepdims=True)
        c_last = ct_ref[0, 2 * hp + h, :, tile - 1:tile]
        bound = jnp.sqrt(head_sqnorm_max(qf, h) * kmax_sc[h]) - c_last
        need = (bound - m_low > -EXP_UNDERFLOW) & (jrow < top)
        j_first = jnp.min(jnp.min(jnp.where(need, jrow, top), axis=1, keepdims=True),
                          axis=0, keepdims=True)
        j_start.append(j_first[0, 0])
    j_both = jnp.maximum(j_start[0], j_start[1])
    count = top - j_both

    def pair(j, carry):
        off_diagonal((j_both + 2 * j, j_both + 2 * j + 1))
        return carry

    lax.fori_loop(0, count // 2, pair, 0)

    @pl.when(count % 2 == 1)
    def _():
        off_diagonal((top - 1,))

    for h in range(2):
        lone = j_both - j_start[h]

        def single_pair(j, carry, h=h):
            off_diagonal((j_start[h] + 2 * j, j_start[h] + 2 * j + 1), heads=(h,))
            return carry

        lax.fori_loop(0, lone // 2, single_pair, 0)

        @pl.when(lone % 2 == 1)
        def _(h=h):
            off_diagonal((j_both - 1,), heads=(h,))

    a0 = acc_sc[0]
    a1 = acc_sc[1]
    half = FOX_HEAD_DIM
    o = jnp.where(first, a0 / pltpu.roll(a0, half, 1), a1 / pltpu.roll(a1, half, 1))
    o_ref[0] = o.astype(o_ref.dtype)


def _fox_attention(fq, fk, fv, ct4, batch, seq):
    t = ATT_TILE
    nq = seq // t
    q3 = fq.reshape(batch, seq, FOX_WIDTH)
    k3 = fk.reshape(batch, seq, FOX_WIDTH)
    v3 = fv.reshape(batch, seq, FOX_WIDTH)
    out = pl.pallas_call(
        functools.partial(_fox_kernel, tile=t, nq=nq),
        grid=(batch, FOX_HEADS // 2, nq),
        in_specs=[
            pl.BlockSpec((1, t, LANES), lambda b, j, i: (b, i, j)),
            pl.BlockSpec((1, seq, LANES), lambda b, j, i: (b, 0, j)),
            pl.BlockSpec((1, seq, LANES), lambda b, j, i: (b, 0, j)),
            pl.BlockSpec((1, 8, nq, t), lambda b, j, i: (b, 0, 0, 0)),
        ],
        out_specs=pl.BlockSpec((1, t, LANES), lambda b, j, i: (b, i, j)),
        out_shape=jax.ShapeDtypeStruct((batch, seq, FOX_WIDTH), BF16),
        scratch_shapes=[pltpu.VMEM((2, t, LANES), F32), pltpu.VMEM((2, t, LANES), F32),
                        pltpu.VMEM((2, nq, LANES), F32)],
        compiler_params=pltpu.CompilerParams(
            dimension_semantics=("parallel", "parallel", "arbitrary"),
            vmem_limit_bytes=VMEM_LIMIT),
        name="fox_attention",
    )(q3, k3, v3, ct4)
    return out.reshape(batch * seq, FOX_WIDTH)


def _ret_kernel(q_ref, k_ref, v_ref, g_ref, cos_ref, sin_ref, dmat_ref, qdec_ref, kdec_ref,
                sdec_ref, o_ref, state_sc):
    si = pl.program_id(1)

    @pl.when(si == 0)
    def _():
        state_sc[...] = jnp.zeros_like(state_sc)

    cos2 = cos_ref[...]
    sin2 = sin_ref[...]
    dk = RET_HEAD_DIM

    def rot(xf):
        return xf * cos2 + pltpu.roll(xf, dk // 2, 1) * sin2

    for h in range(RET_HEADS):
        cols = slice(h * dk, (h + 1) * dk)
        q = rot(q_ref[0, :, cols].astype(F32))
        k = rot(k_ref[0, :, cols].astype(F32)) * (dk ** -0.5)
        v = v_ref[0, :, cols]
        scores = _dot_nt(q.astype(BF16), k.astype(BF16)) * dmat_ref[h]
        intra = _dot(scores.astype(BF16), v)
        state = state_sc[h]
        cross = _dot((q * qdec_ref[h]).astype(BF16), state.astype(BF16))
        out = intra + cross
        state_sc[h] = state * sdec_ref[h, 0:1, :] + _dot_tn((k * kdec_ref[h]).astype(BF16), v)

        y = out * lax.rsqrt(jnp.mean(out * out, axis=-1, keepdims=True) + RMS_EPS)
        g = g_ref[0, :, cols].astype(F32)
        o_ref[0, :, cols] = (y * (g * jax.nn.sigmoid(g))).astype(o_ref.dtype)


def _ret_tables(seq):
    half = RET_HEAD_DIM // 2
    inv_freq = 1.0 / (ROPE_BASE ** (jnp.arange(half, dtype=F32) / half))
    ang = jnp.arange(seq, dtype=F32)[:, None] * inv_freq[None, :]
    cos = jnp.cos(ang)
    sin = jnp.sin(ang)
    cos2 = jnp.concatenate([cos, cos], axis=1)
    sin2 = jnp.concatenate([-sin, sin], axis=1)
    lt = RET_TILE
    log_gamma = jnp.log(1.0 - 2.0 ** (-5.0 - jnp.arange(RET_HEADS, dtype=F32)))
    idx = jnp.arange(lt)
    t = idx[:, None]
    s = idx[None, :]
    same = (t // CHUNK) == (s // CHUNK)
    earlier = (s // CHUNK) < (t // CHUNK)
    dist = jnp.where(same, jnp.abs(t - s), t - s).astype(F32)
    dmat = jnp.where((same | earlier)[None], jnp.exp(log_gamma[:, None, None] * dist[None]), 0.0)
    idxf = idx.astype(F32)
    qdec = jnp.exp(log_gamma[:, None] * idxf[None, :])
    kdec = jnp.exp(log_gamma[:, None] * (lt - idxf)[None, :])
    sdec = jnp.exp(log_gamma * lt)
    qdec = jnp.broadcast_to(qdec[:, :, None], (RET_HEADS, lt, LANES))
    kdec = jnp.broadcast_to(kdec[:, :, None], (RET_HEADS, lt, LANES))
    sdec = jnp.broadcast_to(sdec[:, None, None], (RET_HEADS, 8, LANES))
    return cos2, sin2, dmat, qdec, kdec, sdec


def _retention(rq, rk, rv, rg, tables, batch, seq):
    lt = RET_TILE
    ns = seq // lt
    cos2, sin2, dmat, qdec, kdec, sdec = tables
    blk = pl.BlockSpec((1, lt, RET_WIDTH), lambda b, i: (b, i, 0))
    tab = pl.BlockSpec((lt, LANES), lambda b, i: (i, 0))
    args = [a.reshape(batch, seq, RET_WIDTH) for a in (rq, rk, rv, rg)]
    out = pl.pallas_call(
        _ret_kernel,
        grid=(batch, ns),
        in_specs=[blk, blk, blk, blk, tab, tab,
                  pl.BlockSpec((RET_HEADS, lt, lt), lambda b, i: (0, 0, 0)),
                  pl.BlockSpec((RET_HEADS, lt, LANES), lambda b, i: (0, 0, 0)),
                  pl.BlockSpec((RET_HEADS, lt, LANES), lambda b, i: (0, 0, 0)),
                  pl.BlockSpec((RET_HEADS, 8, LANES), lambda b, i: (0, 0, 0))],
        out_specs=blk,
        out_shape=jax.ShapeDtypeStruct((batch, seq, RET_WIDTH), BF16),
        scratch_shapes=[pltpu.VMEM((RET_HEADS, RET_HEAD_DIM, RET_HEAD_DIM), F32)],
        compiler_params=pltpu.CompilerParams(
            dimension_semantics=("parallel", "arbitrary"),
            vmem_limit_bytes=VMEM_LIMIT),
        name="retention",
    )(*args, cos2, sin2, dmat, qdec, kdec, sdec)
    return out.reshape(batch * seq, RET_WIDTH)


def _outproj_kernel(fox_ref, ret_ref, x_ref, wo_ref, nw_ref, wr_ref, br_ref, xo_ref, xg_ref,
                    eid_ref, rw_ref):
    mixed = jnp.concatenate([fox_ref[...], ret_ref[...]], axis=1)
    x = x_ref[...] + _dot(mixed, wo_ref[...])
    xo_ref[...] = x
    hf = _rms(x, nw_ref[...])
    _tm_store(xg_ref, 0, x.shape[0], hf)
    h = hf.astype(BF16)
    lt = (_dot(h, wr_ref[...]) + br_ref[...]).T
    tm = lt.shape[1]
    rowid = lax.broadcasted_iota(jnp.int32, (SLAB, tm), 0)
    neg = -jnp.inf

    def top1(v):
        vmax = jnp.max(v, axis=0, keepdims=True)
        idx = jnp.min(jnp.where(v == vmax, rowid, SLAB), axis=0, keepdims=True)
        return vmax, idx

    gl = jnp.where(rowid < N_GROUPS, lt[0:SLAB], neg)
    gmax, gidx = top1(gl)
    g_w = 1.0 / jnp.sum(jnp.exp(gl - gmax), axis=0, keepdims=True)
    e_in = jnp.zeros((SLAB, tm), F32)
    for g in range(N_GROUPS):
        e_in = jnp.where(gidx == g, lt[SLAB * (g + 1):SLAB * (g + 2)], e_in)
    v1, i1 = top1(e_in)
    rest = jnp.where(rowid == i1, neg, e_in)
    v2, i2 = top1(rest)
    t = jnp.exp(v2 - v1)
    w1 = g_w / (1.0 + t)
    eid_ref[0:1, :] = gidx * EXPERTS_PER_GROUP + i1
    eid_ref[1:2, :] = gidx * EXPERTS_PER_GROUP + i2
    wslab = jnp.concatenate([w1, w1 * t, jnp.zeros((LANES - TOP_K, tm), F32)], axis=0)
    rw_ref[...] = wslab.T


def _outproj(fox, ret, x, wo, nw, wr, br):
    n = x.shape[0]
    tm = WIDE_TILE
    row = pl.BlockSpec((tm, D_MODEL), lambda i: (i, 0))
    half = pl.BlockSpec((tm, GROUP_W), lambda i: (i, 0))
    pair = pl.BlockSpec((TOP_K, tm), lambda i: (0, i))
    wts = pl.BlockSpec((tm, LANES), lambda i: (i, 0))
    return pl.pallas_call(
        _outproj_kernel,
        grid=(n // tm,),
        in_specs=[half, half, row,
                  pl.BlockSpec((D_MODEL, D_MODEL), lambda i: (0, 0)),
                  pl.BlockSpec((1, D_MODEL), lambda i: (0, 0)),
                  pl.BlockSpec((D_MODEL, LANES), lambda i: (0, 0)),
                  pl.BlockSpec((1, LANES), lambda i: (0, 0))],
        out_specs=[row, pl.BlockSpec((tm * CHUNKS, LANES), lambda i: (i, 0)), pair, wts],
        out_shape=[jax.ShapeDtypeStruct((n, D_MODEL), F32),
                   jax.ShapeDtypeStruct((n * CHUNKS, LANES), F32),
                   jax.ShapeDtypeStruct((TOP_K, n), jnp.int32),
                   jax.ShapeDtypeStruct((n, LANES), F32)],
        compiler_params=pltpu.CompilerParams(
            dimension_semantics=("parallel",), vmem_limit_bytes=VMEM_LIMIT),
        name="outproj",
    )(fox, ret, x, wo, nw, wr, br)


_FIRST, _LAST, _VALID, _NEWEXP = 1, 2, 4, 8


def _moe_kernel(blk_ref, exp_ref, flag_ref, starts_ref,
                tokc_ref, tokn_ref, dstp_ref, dstc_ref, x_hbm, wg_ref, wu_ref, wd_ref,
                y_hbm, xbuf, ybuf, hbuf, wgb, wub, wdb, gsem, ssem, *, tb, nb):
    w = pl.program_id(0)
    b = blk_ref[w]
    e = exp_ref[w]
    flags = flag_ref[w]
    slot = b % 2
    nslot = 1 - slot
    span = tb * CHUNKS

    def hbm_row(ref, idx):
        return ref.at[pl.ds(pl.multiple_of(idx * CHUNKS, CHUNKS), CHUNKS), :]

    def buf_row(buf, s, r):
        return buf.at[pl.ds(pl.multiple_of(s * span + r * CHUNKS, CHUNKS), CHUNKS), :]

    def start_gather(tok_ref, s):
        for r in range(tb):
            pltpu.make_async_copy(hbm_row(x_hbm, tok_ref[0, 0, r]), buf_row(xbuf, s, r),
                                  gsem.at[s]).start()

    def start_scatter(dst_ref, s):
        for r in range(tb):
            pltpu.make_async_copy(buf_row(ybuf, s, r), hbm_row(y_hbm, dst_ref[0, 0, r]),
                                  ssem.at[s]).start()

    def wait_rows(sem, s):
        whole = pl.ds(pl.multiple_of(s * span, span), span)
        pltpu.make_async_copy(xbuf.at[whole, :], ybuf.at[whole, :], sem.at[s]).wait()

    @pl.when((flags & _FIRST) != 0)
    def _():
        @pl.when(w == 0)
        def _():
            start_gather(tokc_ref, 0)

        wait_rows(gsem, slot)

        @pl.when(b >= 2)
        def _():
            wait_rows(ssem, slot)

        hbuf[...] = _tm_load(xbuf, slot * span, tb).astype(BF16)

        @pl.when((flags & _LAST) == 0)
        def _():
            ybuf[pl.ds(pl.multiple_of(slot * span, span), span), :] = jnp.zeros((span, LANES), F32)

        @pl.when(b + 1 < nb)
        def _():
            start_gather(tokn_ref, nslot)

        @pl.when(b >= 1)
        def _():
            start_scatter(dstp_ref, nslot)

    @pl.when((flags & _NEWEXP) != 0)
    def _():
        wgb[...] = wg_ref[0, 0].astype(BF16)
        wub[...] = wu_ref[0, 0].astype(BF16)
        wdb[...] = wd_ref[0, 0].astype(BF16)

    @pl.when((flags & _VALID) != 0)
    def _():
        h = hbuf[...]
        g = _dot(h, wgb[...])
        u = _dot(h, wub[...])
        a = (g * jax.nn.sigmoid(g) * u).astype(BF16)
        y = _dot(a, wdb[...])
        whole_block = (flags & (_FIRST | _LAST)) == (_FIRST | _LAST)

        @pl.when(whole_block)
        def _():
            _tm_store(ybuf, slot * span, tb, y)

        @pl.when(jnp.logical_not(whole_block))
        def _():
            q = b * tb + lax.broadcasted_iota(jnp.int32, y.shape, 0)
            mine = (q >= starts_ref[e]) & (q < starts_ref[e + 1])
            _tm_store(ybuf, slot * span, tb, jnp.where(mine, y, _tm_load(ybuf, slot * span, tb)))

    @pl.when(((flags & _LAST) != 0) & (b == nb - 1))
    def _():
        start_scatter(dstc_ref, slot)
        if nb >= 2:
            wait_rows(ssem, nslot)
        wait_rows(ssem, slot)


def _moe(x, s_tok, s_dst, blk, exp, flags, starts, wg, wu, wd, layer, tb=MOE_TILE):
    na = s_tok.shape[0]
    nb = na // tb
    d_model, d_expert = wg.shape[2], wg.shape[3]
    tok3 = s_tok.reshape(nb, 1, tb)
    dst3 = s_dst.reshape(nb, 1, tb)
    smem_blk = lambda f: pl.BlockSpec((1, 1, tb), f, memory_space=pltpu.SMEM)
    grid_spec = pltpu.PrefetchScalarGridSpec(
        num_scalar_prefetch=4,
        grid=(blk.shape[0],),
        in_specs=[
            smem_blk(lambda w, bl, ex, fl, st: (bl[w], 0, 0)),
            smem_blk(lambda w, bl, ex, fl, st: (jnp.minimum(bl[w] + 1, nb - 1), 0, 0)),
            smem_blk(lambda w, bl, ex, fl, st: (jnp.maximum(bl[w] - 1, 0), 0, 0)),
            smem_blk(lambda w, bl, ex, fl, st: (bl[w], 0, 0)),
            pl.BlockSpec(memory_space=pl.ANY),
            pl.BlockSpec((1, 1, d_model, d_expert), lambda w, bl, ex, fl, st: (layer, ex[w], 0, 0)),
            pl.BlockSpec((1, 1, d_model, d_expert), lambda w, bl, ex, fl, st: (layer, ex[w], 0, 0)),
            pl.BlockSpec((1, 1, d_expert, d_model), lambda w, bl, ex, fl, st: (layer, ex[w], 0, 0)),
        ],
        out_specs=pl.BlockSpec(memory_space=pl.ANY),
        scratch_shapes=[pltpu.VMEM((2 * tb * CHUNKS, LANES), F32),
                        pltpu.VMEM((2 * tb * CHUNKS, LANES), F32),
                        pltpu.VMEM((tb, d_model), BF16),
                        pltpu.VMEM((d_model, d_expert), BF16), pltpu.VMEM((d_model, d_expert), BF16),
                        pltpu.VMEM((d_expert, d_model), BF16),
                        pltpu.SemaphoreType.DMA((2,)), pltpu.SemaphoreType.DMA((2,))],
    )
    return pl.pallas_call(
        functools.partial(_moe_kernel, tb=tb, nb=nb),
        grid_spec=grid_spec,
        out_shape=jax.ShapeDtypeStruct((na * CHUNKS, LANES), F32),
        compiler_params=pltpu.CompilerParams(
            dimension_semantics=("arbitrary",), vmem_limit_bytes=VMEM_LIMIT),
        name="moe_experts",
    )(blk, exp, flags, starts, tok3, tok3, dst3, dst3, x, wg, wu, wd)


def _dispatch_plan(eid, n, tb=MOE_TILE, n_experts=N_EXPERTS):
    na = TOP_K * n
    nb = na // tb
    eid_flat = eid.reshape(na)
    assert n_experts * na < 2 ** 31
    s_a = lax.sort(eid_flat * na + jnp.arange(na, dtype=jnp.int32), is_stable=False) % na
    s_tok = s_a % n
    counts = jnp.sum(eid_flat[None, :] == jnp.arange(n_experts, dtype=jnp.int32)[:, None], axis=1)
    starts = jnp.concatenate([jnp.zeros((1,), jnp.int32),
                              jnp.cumsum(counts).astype(jnp.int32)])
    lo, hi = starts[:-1], starts[1:]
    nonempty = hi > lo
    first_blk = lo // tb
    npass = jnp.where(nonempty, (hi - 1) // tb - first_blk + 1, 0)
    cum = jnp.cumsum(npass)
    total = cum[-1]
    n_pass = nb + n_experts
    w = jnp.arange(n_pass, dtype=jnp.int32)
    wc = jnp.minimum(w, total - 1)
    ex = jnp.sum(cum[None, :] <= wc[:, None], axis=1).astype(jnp.int32)
    sel = ex[:, None] == jnp.arange(n_experts, dtype=jnp.int32)[None, :]
    pick = lambda v: jnp.sum(jnp.where(sel, v[None, :], 0), axis=1)
    blk = (pick(first_blk) + (wc - pick(cum - npass))).astype(jnp.int32)
    valid = w < total
    prev_blk = jnp.concatenate([jnp.full((1,), -1, jnp.int32), blk[:-1]])
    next_blk = jnp.concatenate([blk[1:], jnp.full((1,), -1, jnp.int32)])
    first = valid & (blk != prev_blk)
    last = valid & ((blk != next_blk) | (w == total - 1))
    prev_ex = jnp.concatenate([jnp.full((1,), -1, jnp.int32), ex[:-1]])
    newexp = valid & (ex != prev_ex)
    flags = (first * _FIRST + last * _LAST + valid * _VALID + newexp * _NEWEXP).astype(jnp.int32)
    return s_tok, s_a, blk, ex, flags, starts


def _final_kernel(x_ref, y0_ref, y1_ref, rw_ref, nw_ref, o_ref):
    rw = rw_ref[...]
    tm = x_ref.shape[0]
    x = x_ref[...] + (rw[:, 0:1] * _tm_load(y0_ref, 0, tm) + rw[:, 1:2] * _tm_load(y1_ref, 0, tm))
    o_ref[...] = _rms(x, nw_ref[...])


def _final(x, y, rw, nw):
    n = x.shape[0]
    tm = WIDE_TILE
    nt = n // tm
    row = pl.BlockSpec((tm, D_MODEL), lambda i: (i, 0))
    return pl.pallas_call(
        _final_kernel,
        grid=(nt,),
        in_specs=[row, pl.BlockSpec((tm * CHUNKS, LANES), lambda i: (i, 0)),
                  pl.BlockSpec((tm * CHUNKS, LANES), lambda i: (i + nt, 0)),
                  pl.BlockSpec((tm, LANES), lambda i: (i, 0)),
                  pl.BlockSpec((1, D_MODEL), lambda i: (0, 0))],
        out_specs=row,
        out_shape=jax.ShapeDtypeStruct((n, D_MODEL), F32),
        compiler_params=pltpu.CompilerParams(
            dimension_semantics=("parallel",), vmem_limit_bytes=VMEM_LIMIT),
        name="final_norm",
    )(x, y, y, rw, nw)


def kernel(x, norm_mix_w, w_in, fox_forget_b, w_out, norm_ffn_w, w_router_group, b_router_group,
           w_router_expert, b_router_expert, w_expert_gate, w_expert_up, w_expert_down,
           norm_final_w):
    batch, seq, d = x.shape
    n = batch * seq
    depth = w_in.shape[0]
    xf = x.reshape(n, d)
    tables = _ret_tables(seq)
    nq = seq // ATT_TILE

    y = rw = None
    for layer in range(depth):
        wl = w_in[layer]
        c0 = 3 * FOX_WIDTH
        w_ff = jnp.pad(wl[:, c0:c0 + FOX_HEADS], ((0, 0), (0, LANES - FOX_HEADS)))
        w_main = jnp.concatenate([wl[:, :c0], wl[:, c0 + FOX_HEADS:], w_ff], axis=1).astype(BF16)
        b_ff = jnp.pad(fox_forget_b[layer], (0, LANES - FOX_HEADS)).reshape(1, LANES)
        xf, (fq, fk, fv, rq, rk, rv, rg, ct) = _inproj(
            xf, y, rw, norm_mix_w[layer].reshape(1, d), w_main, b_ff, seq)
        ct4 = ct.reshape(batch, 8, nq, ATT_TILE)
        fox = _fox_attention(fq, fk, fv, ct4, batch, seq)
        ret = _retention(rq, rk, rv, rg, tables, batch, seq)

        zpad = jnp.zeros((d, SLAB - N_GROUPS), F32)
        w_r = jnp.concatenate([w_router_group[layer], zpad, w_router_expert[layer]], axis=1)
        nr = SLAB + N_EXPERTS
        w_r = jnp.pad(w_r, ((0, 0), (0, LANES - nr))).astype(BF16)
        b_r = jnp.concatenate([b_router_group[layer], jnp.zeros((SLAB - N_GROUPS,), F32),
                               b_router_expert[layer]])
        b_r = jnp.pad(b_r, (0, LANES - nr)).reshape(1, LANES)
        xf, xg, eid, rw = _outproj(fox, ret, xf, w_out[layer].astype(BF16),
                                   norm_ffn_w[layer].reshape(1, d), w_r, b_r)
        s_tok, s_dst, blk, ex, flags, starts = _dispatch_plan(eid, n)
        y = _moe(xg, s_tok, s_dst, blk, ex, flags, starts,
                 w_expert_gate, w_expert_up, w_expert_down, layer)
    out = _final(xf, y, rw, norm_final_w.reshape(1, d))
    return out.reshape(batch, seq, d)
```
